```python
import math
import jax
import jax.numpy as jnp
from jax import lax
import numpy as np

D_MODEL = 1024
BATCH = 4
SEQ = 4096
DEPTH = 2

CTX_LEN = 256
GRID_W = 64

LRU_WIDTH = 384
LRU_BLOCKS = 6
LRU_BLOCK = LRU_WIDTH // LRU_BLOCKS
LRU_CONV = 4
LRU_C = 8.0
HY_WIDTH = 256
HY_ORDER = 2
HY_SHORT = 3
HY_BANDS = 16
HY_EMB = 2 * HY_BANDS + 1
HY_FILT_HID = 64
HY_MAX_DECAY = math.log(1e-2) / 0.3
HY_MIN_DECAY = math.log(1e-2) / 1.5
NA_HEADS = 6
NA_HEAD_DIM = 64
NA_WIDTH = NA_HEADS * NA_HEAD_DIM
NA_WIN_R = 8
NA_WIN_C = 16
S5_WIDTH = 256
S5_GROUP = 16
S5_GROUPS = S5_WIDTH // S5_GROUP
S5_STATE = 64
N_BRANCH = 4
OFF_A = 0
OFF_B = OFF_A + 2 * LRU_WIDTH
OFF_C = OFF_B + 3 * HY_WIDTH
OFF_D = OFF_C + 3 * NA_WIDTH
OFF_G = OFF_D + S5_WIDTH
IN_COLS = OFF_G + N_BRANCH * D_MODEL
FF_DENSE = 2816
N_EXPERTS = 8
TOP_K = 2
FF_EXPERT = 3584
LN_EPS = 1e-5
F32 = jnp.float32

kernel_name = 'hybrid_gated_mixers_diffusion_block'


def layer_norm(x, g=None, b=None):
    xf = x.astype(F32)
    mu = jnp.mean(xf, axis=-1, keepdims=True)
    var = jnp.mean(jnp.square(xf - mu), axis=-1, keepdims=True)
    y = (xf - mu) * lax.rsqrt(var + LN_EPS)
    if g is not None:
        y = y * g.astype(F32) + b.astype(F32)
    return y.astype(x.dtype)


def modulate(x, shift, scale):
    return layer_norm(x) * (1.0 + scale) + shift


def dw_conv(x, w, b):
    k_w = w.shape[0]
    left = k_w // 2
    L = x.shape[1]
    xp = jnp.pad(x, ((0, 0), (left, k_w - 1 - left), (0, 0)))
    y = xp[:, 0:L] * w[0]
    for k in range(1, k_w):
        y = y + xp[:, k:k + L] * w[k]
    return y + b


def linear_scan(a, b, h0, reverse):
    def combine(e1, e2):
        a1, b1 = e1
        a2, b2 = e2
        return a1 * a2, a2 * b1 + b2
    a_cum, b_cum = lax.associative_scan(combine, (a, b), reverse=reverse, axis=0)
    return b_cum + a_cum * h0


def complex_scan(a_re, a_im, b_re, b_im, h0_re, h0_im, reverse):
    def combine(e1, e2):
        ar1, ai1, br1, bi1 = e1
        ar2, ai2, br2, bi2 = e2
        return (ar1 * ar2 - ai1 * ai2, ar1 * ai2 + ai1 * ar2,
                ar2 * br1 - ai2 * bi1 + br2, ar2 * bi1 + ai2 * br1 + bi2)
    ar, ai, br, bi = lax.associative_scan(combine, (a_re, a_im, b_re, b_im), reverse=reverse, axis=0)
    return br + ar * h0_re - ai * h0_im, bi + ar * h0_im + ai * h0_re


def rglru_scan(xc, w_r, b_r, w_i, b_i, lam, h0, reverse):
    bsz, L, _ = xc.shape
    xb = xc.reshape(bsz, L, LRU_BLOCKS, LRU_BLOCK)
    gate_r = jax.nn.sigmoid(jnp.einsum('blgi,gij->blgj', xb, w_r.astype(F32)).reshape(bsz, L, LRU_WIDTH) + b_r)
    gate_i = jax.nn.sigmoid(jnp.einsum('blgi,gij->blgj', xb, w_i.astype(F32)).reshape(bsz, L, LRU_WIDTH) + b_i)
    log_a = -LRU_C * gate_r * jax.nn.softplus(-lam.astype(F32))
    a = jnp.exp(log_a)
    b = jnp.sqrt(-jnp.expm1(2.0 * log_a)) * gate_i * xc
    h = linear_scan(jnp.swapaxes(a, 0, 1), jnp.swapaxes(b, 0, 1), h0, reverse)
    return jnp.swapaxes(h, 0, 1)


def rglru_mixer(pa_c, pa_l, lp, with_ctx):
    def conv_in(pa):
        return dw_conv(pa[..., LRU_WIDTH:], lp['lru_conv_w'], lp['lru_conv_b']).astype(F32)

    def run(xc, d, h0, reverse):
        return rglru_scan(xc, lp['lru_w_r'][d], lp['lru_b_r'][d], lp['lru_w_i'][d], lp['lru_b_i'][d],
                          lp['lru_lambda'][d], h0, reverse)

    xc_c, xc_l = conv_in(pa_c), conv_in(pa_l)
    hc_f = run(xc_c, 0, 0.0, False)
    hc_b = run(xc_c, 1, 0.0, True)
    hl_f = run(xc_l, 0, hc_f[:, -1], False)
    hl_b = run(xc_l, 1, hc_b[:, 0], True)
    y_l = (jax.nn.gelu(pa_l[..., :LRU_WIDTH].astype(F32)) * (hl_f + hl_b)).astype(pa_l.dtype)
    y_c = None
    if with_ctx:
        y_c = (jax.nn.gelu(pa_c[..., :LRU_WIDTH].astype(F32)) * (hc_f + hc_b)).astype(pa_c.dtype)
    return y_c, y_l


def hyena_filters(L, w1, b1, w2, b2, w3, b3, freq):
    t = jnp.arange(L, dtype=F32)
    t_norm = t / L
    bands = jnp.arange(1, HY_BANDS + 1, dtype=F32)
    ang = (2.0 * math.pi / L) * t[:, None] * bands[None, :]
    feat = jnp.concatenate([t_norm[:, None], jnp.cos(ang), jnp.sin(ang)], axis=-1)
    h = jnp.sin(freq * (feat @ w1 + b1))
    h = jnp.sin(freq * (h @ w2 + b2))
    h = (h @ w3 + b3).reshape(L, 2, HY_ORDER, HY_WIDTH)
    deltas = jnp.abs(jnp.linspace(HY_MIN_DECAY, HY_MAX_DECAY, HY_WIDTH, dtype=F32))
    window = jnp.exp(-t_norm[:, None] * deltas[None, :])
    return h * window[:, None, None, :]


def bidir_kernel(h_f, h_b):
    k = jnp.concatenate([h_f, jnp.zeros_like(h_f[:1]), h_b[:0:-1]], axis=0)
    return k / (jnp.sum(jnp.abs(k), axis=0, keepdims=True) + 1e-6)


def long_conv(z, k, bias):
    L = z.shape[1]
    kf = jnp.fft.rfft(k, axis=0)
    zf = jnp.fft.rfft(z, n=2 * L, axis=1)
    y = jnp.fft.irfft(zf * kf[None], n=2 * L, axis=1)[:, :L]
    return y + z * bias


def hyena_sequence(pb, lp):
    L = pb.shape[1]
    z = dw_conv(pb, lp['hy_conv_w'], lp['hy_conv_b']).astype(F32)
    v, x1, x2 = jnp.split(z, 3, axis=-1)
    h = hyena_filters(L, lp['hy_w1'], lp['hy_b1'], lp['hy_w2'], lp['hy_b2'], lp['hy_w3'], lp['hy_b3'], lp['hy_freq'])
    y = v
    for o, gate in enumerate((x1, x2)):
        y = gate * long_conv(y, bidir_kernel(h[:, 0, o], h[:, 1, o]), lp['hy_bias'][o])
    return y.astype(pb.dtype)


def natten_mixer(pc_c, pc_l, rpb, with_ctx):
    bsz, L, _ = pc_l.shape
    rows = L // GRID_W
    kr = min(NA_WIN_R, rows)
    nw = kr * NA_WIN_C
    scale = NA_HEAD_DIM ** -0.5

    def split_heads(p):
        return [p[..., i * NA_WIDTH:(i + 1) * NA_WIDTH].reshape(p.shape[0], p.shape[1], NA_HEADS, NA_HEAD_DIM)
                for i in range(3)]

    q_l, k_l, v_l = split_heads(pc_l)
    q_c, k_c, v_c = split_heads(pc_c)
    kg = k_l.reshape(bsz, rows, GRID_W, NA_HEADS, NA_HEAD_DIM)
    vg = v_l.reshape(bsz, rows, GRID_W, NA_HEADS, NA_HEAD_DIM)
    qg = jnp.moveaxis((q_l * scale).reshape(bsz, rows, GRID_W, NA_HEADS, NA_HEAD_DIM), 1, 0)
    row_ids = jnp.arange(rows)
    row_start = jnp.clip(row_ids - kr // 2, 0, rows - kr)
    cols = jnp.arange(GRID_W)
    col_idx = jnp.clip(cols - NA_WIN_C // 2, 0, GRID_W - NA_WIN_C)[:, None] + jnp.arange(NA_WIN_C)[None, :]
    dc_idx = col_idx - cols[:, None] + (NA_WIN_C - 1)

    def row_block(args):
        q_r, r, rs = args
        k_win = lax.dynamic_slice_in_dim(kg, rs, kr, axis=1)[:, :, col_idx]
        v_win = lax.dynamic_slice_in_dim(vg, rs, kr, axis=1)[:, :, col_idx]
        dr_idx = rs + jnp.arange(kr) - r + (NA_WIN_R - 1)
        bias = rpb[:, dr_idx[:, None, None], dc_idx[None, :, :]]
        s_win = jnp.einsum('bwhd,biwjhd->bhwij', q_r, k_win) + jnp.transpose(bias, (0, 2, 1, 3))[None]
        s_ctx = jnp.einsum('bwhd,bnhd->bhwn', q_r, k_c)
        s = jnp.concatenate([s_win.reshape(bsz, NA_HEADS, GRID_W, nw), s_ctx], axis=-1)
        pr = jax.nn.softmax(s.astype(F32), axis=-1).astype(v_l.dtype)
        p_win = pr[..., :nw].reshape(bsz, NA_HEADS, GRID_W, kr, NA_WIN_C)
        return (jnp.einsum('bhwij,biwjhd->bwhd', p_win, v_win)
                + jnp.einsum('bhwn,bnhd->bwhd', pr[..., nw:], v_c))

    o_l = lax.map(row_block, (qg, row_ids, row_start))
    y_l = jnp.moveaxis(o_l, 0, 1).reshape(bsz, L, NA_WIDTH)
    y_c = None
    if with_ctx:
        s = jnp.einsum('bqhd,bkhd->bhqk', q_c * scale, k_c)
        pr = jax.nn.softmax(s.astype(F32), axis=-1).astype(v_c.dtype)
        y_c = jnp.einsum('bhqk,bkhd->bqhd', pr, v_c).reshape(bsz, q_c.shape[1], NA_WIDTH)
    return y_c, y_l


def s5_discretise(a_re, a_im, log_dt, b_re, b_im):
    a_re, a_im = a_re.astype(F32), a_im.astype(F32)
    dt = jnp.exp(log_dt.astype(F32))[:, None]
    mag = jnp.exp(dt * a_re)
    ab_re = mag * jnp.cos(dt * a_im)
    ab_im = mag * jnp.sin(dt * a_im)
    den = jnp.square(a_re) + jnp.square(a_im)
    f_re = ((ab_re - 1.0) * a_re + ab_im * a_im) / den
    f_im = (ab_im * a_re - (ab_re - 1.0) * a_im) / den
    b_re, b_im = b_re.astype(F32), b_im.astype(F32)
    bb_re = f_re[..., None] * b_re - f_im[..., None] * b_im
    bb_im = f_re[..., None] * b_im + f_im[..., None] * b_re
    return ab_re, ab_im, bb_re, bb_im


def s5_states(u, disc, h0_re, h0_im, reverse):
    ab_re, ab_im, bb_re, bb_im = disc
    L = u.shape[1]
    bu_re = jnp.einsum('blgi,gpi->lbgp', u, bb_re)
    bu_im = jnp.einsum('blgi,gpi->lbgp', u, bb_im)
    a_re = jnp.broadcast_to(ab_re[None, None], (L, 1) + ab_re.shape)
    a_im = jnp.broadcast_to(ab_im[None, None], (L, 1) + ab_im.shape)
    return complex_scan(a_re, a_im, bu_re, bu_im, h0_re, h0_im, reverse)


def s5_readout(h_re, h_im, c_re, c_im):
    return (jnp.einsum('lbgp,gop->blgo', h_re, c_re.astype(F32))
            - jnp.einsum('lbgp,gop->blgo', h_im, c_im.astype(F32)))


def s5_mixer(pd_c, pd_l, lp, with_ctx):
    def groups(pd):
        return pd.astype(F32).reshape(pd.shape[0], pd.shape[1], S5_GROUPS, S5_GROUP)

    discs = [s5_discretise(lp['s5_a_re'][d], lp['s5_a_im'][d], lp['s5_log_dt'][d],
                           lp['s5_b_re'][d], lp['s5_b_im'][d]) for d in range(2)]
    u_c, u_l = groups(pd_c), groups(pd_l)
    hc_f = s5_states(u_c, discs[0], 0.0, 0.0, False)
    hc_b = s5_states(u_c, discs[1], 0.0, 0.0, True)
    hl_f = s5_states(u_l, discs[0], hc_f[0][-1], hc_f[1][-1], False)
    hl_b = s5_states(u_l, discs[1], hc_b[0][0], hc_b[1][0], True)

    def out(h_f, h_b, pd):
        y = (s5_readout(h_f[0], h_f[1], lp['s5_c_re'][0], lp['s5_c_im'][0])
             + s5_readout(h_b[0], h_b[1], lp['s5_c_re'][1], lp['s5_c_im'][1]))
        y = y.reshape(pd.shape) + lp['s5_d'] * pd.astype(F32)
        g = jax.nn.gelu(y)
        return (g * jax.nn.sigmoid(g @ lp['s5_w_glu'].astype(F32) + lp['s5_b_glu'])).astype(pd.dtype)

    y_l = out(hl_f, hl_b, pd_l)
    y_c = out(hc_f, hc_b, pd_c) if with_ctx else None
    return y_c, y_l


def merge_branches(pg, ys, lp):
    gates = jax.nn.sigmoid(pg.astype(F32)).reshape(pg.shape[:-1] + (N_BRANCH, D_MODEL)).astype(pg.dtype)
    projs = (lp['w_br_a'], lp['w_br_b'], lp['w_br_c'], lp['w_br_d'])
    merged = gates[..., 0, :] * (ys[0] @ projs[0])
    for i in range(1, N_BRANCH):
        merged = merged + gates[..., i, :] * (ys[i] @ projs[i])
    return merged @ lp['w_out']


def token_mixer(u_c, u_l, lp, with_ctx):
    p_l = u_l @ lp['w_in']
    p_c = u_c @ (lp['w_in'] if with_ctx else lp['w_in'][:, :OFF_G])
    ya_c, ya_l = rglru_mixer(p_c[..., OFF_A:OFF_B], p_l[..., OFF_A:OFF_B], lp, with_ctx)
    yb_l = hyena_sequence(p_l[..., OFF_B:OFF_C], lp)
    yc_c, yc_l = natten_mixer(p_c[..., OFF_C:OFF_D], p_l[..., OFF_C:OFF_D], lp['na_rpb'], with_ctx)
    yd_c, yd_l = s5_mixer(p_c[..., OFF_D:OFF_G], p_l[..., OFF_D:OFF_G], lp, with_ctx)
    m_l = merge_branches(p_l[..., OFF_G:], (ya_l, yb_l, yc_l, yd_l), lp)
    if not with_ctx:
        return None, m_l
    yb_c = hyena_sequence(p_c[..., OFF_B:OFF_C], lp)
    m_c = merge_branches(p_c[..., OFF_G:], (ya_c, yb_c, yc_c, yd_c), lp)
    return m_c, m_l


def swiglu(u, w_gate, w_up, w_down):
    return (jax.nn.silu(u @ w_gate) * (u @ w_up)) @ w_down


def moe_swiglu(u, w_router, w_gate, w_up, w_down):
    logits = (u @ w_router).astype(F32)
    top_v, top_i = lax.top_k(logits, TOP_K)
    weights = jax.nn.softmax(top_v, axis=-1)
    gate = jnp.sum(jax.nn.one_hot(top_i, N_EXPERTS, dtype=F32) * weights[..., None], axis=-2).astype(u.dtype)
    out = gate[..., 0:1] * swiglu(u, w_gate[0], w_up[0], w_down[0])
    for e in range(1, N_EXPERTS):
        out = out + gate[..., e:e + 1] * swiglu(u, w_gate[e], w_up[e], w_down[e])
    return out


def setup_inputs(seed: int = 0) -> dict:
    key = jax.random.key(seed)
    keys = iter(jax.random.split(key, 64))

    def nrm(shape, std):
        return std * jax.random.normal(next(keys), shape, jnp.float32)

    D = D_MODEL
    beta = (8.0 * DEPTH) ** -0.25
    n_dense = (DEPTH + 1) // 2
    n_moe = DEPTH // 2
    a_target = jax.random.uniform(next(keys), (DEPTH, 2, LRU_WIDTH), jnp.float32, 0.9, 0.999)
    s_root = a_target ** (1.0 / LRU_C)
    state_n = jnp.arange(S5_STATE, dtype=jnp.float32)
    return {
        'x': nrm((BATCH, SEQ, D), 1.0),
        'c': nrm((BATCH, D), 1.0),
        'ctx': nrm((BATCH, CTX_LEN, D), 1.0),
        'c_ctx': nrm((D,), 1.0),
        'w_mod': nrm((DEPTH, D, 6 * D), 0.5 * D ** -0.5),
        'b_mod': nrm((DEPTH, 6 * D), 0.01),
        'w_in': nrm((DEPTH, D, IN_COLS), D ** -0.5),
        'lru_conv_w': nrm((DEPTH, LRU_CONV, LRU_WIDTH), LRU_CONV ** -0.5),
        'lru_conv_b': nrm((DEPTH, LRU_WIDTH), 0.01),
        'lru_w_r': nrm((DEPTH, 2, LRU_BLOCKS, LRU_BLOCK, LRU_BLOCK), LRU_BLOCK ** -0.5),
        'lru_b_r': nrm((DEPTH, 2, LRU_WIDTH), 0.01),
        'lru_w_i': nrm((DEPTH, 2, LRU_BLOCKS, LRU_BLOCK, LRU_BLOCK), LRU_BLOCK ** -0.5),
        'lru_b_i': nrm((DEPTH, 2, LRU_WIDTH), 0.01),
        'lru_lambda': jnp.log(s_root) - jnp.log1p(-s_root),
        'hy_conv_w': nrm((DEPTH, HY_SHORT, 3 * HY_WIDTH), HY_SHORT ** -0.5),
        'hy_conv_b': nrm((DEPTH, 3 * HY_WIDTH), 0.01),
        'hy_w1': nrm((DEPTH, HY_EMB, HY_FILT_HID), HY_EMB ** -0.5),
        'hy_b1': nrm((DEPTH, HY_FILT_HID), 0.1),
        'hy_w2': nrm((DEPTH, HY_FILT_HID, HY_FILT_HID), HY_FILT_HID ** -0.5),
        'hy_b2': nrm((DEPTH, HY_FILT_HID), 0.1),
        'hy_w3': nrm((DEPTH, HY_FILT_HID, 2 * HY_ORDER * HY_WIDTH), HY_FILT_HID ** -0.5),
        'hy_b3': nrm((DEPTH, 2 * HY_ORDER * HY_WIDTH), 0.01),
        'hy_freq': 1.0 + nrm((DEPTH, HY_FILT_HID), 0.01),
        'hy_bias': nrm((DEPTH, HY_ORDER, HY_WIDTH), 0.5),
        'na_rpb': nrm((DEPTH, NA_HEADS, 2 * NA_WIN_R - 1, 2 * NA_WIN_C - 1), 0.1),
        's5_a_re': -0.5 + nrm((DEPTH, 2, S5_GROUPS, S5_STATE), 0.01),
        's5_a_im': math.pi * state_n + nrm((DEPTH, 2, S5_GROUPS, S5_STATE), 0.01),
        's5_log_dt': jax.random.uniform(next(keys), (DEPTH, 2, S5_GROUPS), jnp.float32, math.log(1e-3), math.log(1e-1)),
        's5_b_re': nrm((DEPTH, 2, S5_GROUPS, S5_STATE, S5_GROUP), (2 * S5_GROUP) ** -0.5),
        's5_b_im': nrm((DEPTH, 2, S5_GROUPS, S5_STATE, S5_GROUP), (2 * S5_GROUP) ** -0.5),
        's5_c_re': nrm((DEPTH, 2, S5_GROUPS, S5_GROUP, S5_STATE), (2 * S5_STATE) ** -0.5),
        's5_c_im': nrm((DEPTH, 2, S5_GROUPS, S5_GROUP, S5_STATE), (2 * S5_STATE) ** -0.5),
        's5_d': nrm((DEPTH, S5_WIDTH), 1.0),
        's5_w_glu': nrm((DEPTH, S5_WIDTH, S5_WIDTH), S5_WIDTH ** -0.5),
        's5_b_glu': nrm((DEPTH, S5_WIDTH), 0.01),
        'w_br_a': nrm((DEPTH, LRU_WIDTH, D), beta * LRU_WIDTH ** -0.5),
        'w_br_b': nrm((DEPTH, HY_WIDTH, D), beta * HY_WIDTH ** -0.5),
        'w_br_c': nrm((DEPTH, NA_WIDTH, D), beta * NA_WIDTH ** -0.5),
        'w_br_d': nrm((DEPTH, S5_WIDTH, D), beta * S5_WIDTH ** -0.5),
        'w_out': nrm((DEPTH, D, D), beta * D ** -0.5),
        'ln1_g': 1.0 + nrm((DEPTH, D), 0.01),
        'ln1_b': nrm((DEPTH, D), 0.01),
        'ln2_g': 1.0 + nrm((DEPTH, D), 0.01),
        'ln2_b': nrm((DEPTH, D), 0.01),
        'ff_w_gate': nrm((n_dense, D, FF_DENSE), D ** -0.5),
        'ff_w_up': nrm((n_dense, D, FF_DENSE), D ** -0.5),
        'ff_w_down': nrm((n_dense, FF_DENSE, D), beta * FF_DENSE ** -0.5),
        'moe_router': nrm((n_moe, D, N_EXPERTS), D ** -0.5),
        'moe_w_gate': nrm((n_moe, N_EXPERTS, D, FF_EXPERT), D ** -0.5),
        'moe_w_up': nrm((n_moe, N_EXPERTS, D, FF_EXPERT), D ** -0.5),
        'moe_w_down': nrm((n_moe, N_EXPERTS, FF_EXPERT, D), beta * FF_EXPERT ** -0.5),
    }


def reference(x, c, ctx, c_ctx, w_mod, b_mod, w_in, lru_conv_w, lru_conv_b, lru_w_r, lru_b_r, lru_w_i, lru_b_i,
              lru_lambda, hy_conv_w, hy_conv_b, hy_w1, hy_b1, hy_w2, hy_b2, hy_w3, hy_b3, hy_freq, hy_bias, na_rpb,
              s5_a_re, s5_a_im, s5_log_dt, s5_b_re, s5_b_im, s5_c_re, s5_c_im, s5_d, s5_w_glu, s5_b_glu,
              w_br_a, w_br_b, w_br_c, w_br_d, w_out, ln1_g, ln1_b, ln2_g, ln2_b,
              ff_w_gate, ff_w_up, ff_w_down, moe_router, moe_w_gate, moe_w_up, moe_w_down):
    alpha = (2.0 * DEPTH) ** 0.25
    for l in range(DEPTH):
        with_ctx = l < DEPTH - 1
        lp = {
            'w_in': w_in[l], 'lru_conv_w': lru_conv_w[l], 'lru_conv_b': lru_conv_b[l],
            'lru_w_r': lru_w_r[l], 'lru_b_r': lru_b_r[l], 'lru_w_i': lru_w_i[l], 'lru_b_i': lru_b_i[l],
            'lru_lambda': lru_lambda[l], 'hy_conv_w': hy_conv_w[l], 'hy_conv_b': hy_conv_b[l],
            'hy_w1': hy_w1[l], 'hy_b1': hy_b1[l], 'hy_w2': hy_w2[l], 'hy_b2': hy_b2[l], 'hy_w3': hy_w3[l],
            'hy_b3': hy_b3[l], 'hy_freq': hy_freq[l], 'hy_bias': hy_bias[l], 'na_rpb': na_rpb[l],
            's5_a_re': s5_a_re[l], 's5_a_im': s5_a_im[l], 's5_log_dt': s5_log_dt[l], 's5_b_re': s5_b_re[l],
            's5_b_im': s5_b_im[l], 's5_c_re': s5_c_re[l], 's5_c_im': s5_c_im[l], 's5_d': s5_d[l],
            's5_w_glu': s5_w_glu[l], 's5_b_glu': s5_b_glu[l], 'w_br_a': w_br_a[l], 'w_br_b': w_br_b[l],
            'w_br_c': w_br_c[l], 'w_br_d': w_br_d[l], 'w_out': w_out[l],
        }
        mod_l = (jax.nn.silu(c) @ w_mod[l] + b_mod[l])[:, None, :]
        mod_c = jax.nn.silu(c_ctx) @ w_mod[l] + b_mod[l]
        sh1, sc1, g1, sh2, sc2, g2 = jnp.split(mod_l, 6, axis=-1)
        csh1, csc1, cg1, csh2, csc2, cg2 = jnp.split(mod_c, 6, axis=-1)
        m_c, m_l = token_mixer(modulate(ctx, csh1, csc1), modulate(x, sh1, sc1), lp, with_ctx)
        x = layer_norm(alpha * x + g1 * m_l, ln1_g[l], ln1_b[l])
        if with_ctx:
            ctx = layer_norm(alpha * ctx + cg1 * m_c, ln1_g[l], ln1_b[l])

        def ffn(u):
            if l % 2 == 0:
                return swiglu(u, ff_w_gate[l // 2], ff_w_up[l // 2], ff_w_down[l // 2])
            return moe_swiglu(u, moe_router[l // 2], moe_w_gate[l // 2], moe_w_up[l // 2], moe_w_down[l // 2])

        x = layer_norm(alpha * x + g2 * ffn(modulate(x, sh2, sc2)), ln2_g[l], ln2_b[l])
        if with_ctx:
            ctx = layer_norm(alpha * ctx + cg2 * ffn(modulate(ctx, csh2, csc2)), ln2_g[l], ln2_b[l])
    return x
```

```python
import functools
import math

import numpy as np
import jax
import jax.numpy as jnp
from jax import lax
from jax.experimental import pallas as pl
from jax.experimental.pallas import tpu as pltpu

F32 = jnp.float32
BF16 = jnp.bfloat16
HIGHEST = lax.Precision.HIGHEST

LRU_WIDTH = 384
LRU_BLOCK = 64
LRU_CONV = 4
LRU_C = 8.0
HY_WIDTH = 256
HY_ORDER = 2
HY_SHORT = 3
HY_BANDS = 16
HY_FILT_HID = 64
HY_MAX_DECAY = math.log(1e-2) / 0.3
HY_MIN_DECAY = math.log(1e-2) / 1.5
NA_HEADS = 6
NA_HEAD_DIM = 64
NA_WIDTH = NA_HEADS * NA_HEAD_DIM
NA_WIN_R = 8
NA_WIN_C = 16
GRID_W = 64
S5_WIDTH = 256
S5_GROUP = 16
S5_GROUPS = 16
S5_STATE = 64
N_BRANCH = 4
OFF_A = 0
OFF_B = OFF_A + 2 * LRU_WIDTH
OFF_C = OFF_B + 3 * HY_WIDTH
OFF_D = OFF_C + 3 * NA_WIDTH
OFF_G = OFF_D + S5_WIDTH
N_EXPERTS = 8
TOP_K = 2
LN_EPS = 1e-5
MASK_VALUE = -1e30

LANES = 128
SUBLANES = 8
VMEM_LIMIT = 56 * 1024 * 1024


def _cp(sem, vmem=VMEM_LIMIT):
    return pltpu.CompilerParams(dimension_semantics=sem, vmem_limit_bytes=vmem)


def _gelu(x):
    return 0.5 * x * (1.0 + jnp.tanh(math.sqrt(2.0 / math.pi) * (x + 0.044715 * (x * x * x))))


def _sigmoid(x):
    return 1.0 / (1.0 + jnp.exp(-x))


def _layer_norm_rows(x):
    mu = jnp.mean(x, axis=-1, keepdims=True)
    xc = x - mu
    var = jnp.mean(xc * xc, axis=-1, keepdims=True)
    return xc * lax.rsqrt(var + LN_EPS)


def _mod_kernel(c_ref, w_ref, b_ref, o_ref):
    c = c_ref[...]
    a = c * _sigmoid(c)
    o_ref[...] = jnp.dot(a, w_ref[...], preferred_element_type=F32, precision=HIGHEST) + b_ref[...]


def _mod_vectors(c_rows, w_mod, b_mod):
    d = c_rows.shape[1]
    n = w_mod.shape[1]
    tn = 1536
    return pl.pallas_call(
        _mod_kernel,
        out_shape=jax.ShapeDtypeStruct((SUBLANES, n), F32),
        grid=(n // tn,),
        in_specs=[pl.BlockSpec((SUBLANES, d), lambda j: (0, 0)),
                  pl.BlockSpec((d, tn), lambda j: (0, j)),
                  pl.BlockSpec((1, tn), lambda j: (0, j))],
        out_specs=pl.BlockSpec((SUBLANES, tn), lambda j: (0, j)),
        compiler_params=_cp(("arbitrary",)),
        name="mod_vectors",
    )(c_rows, w_mod, b_mod.reshape(1, n))


def _ln_mod_kernel(x_ref, mod_ref, o_ref, *, shift_idx, scale_idx):
    y = _layer_norm_rows(x_ref[0])
    m = mod_ref[0, 0]
    o = y * (1.0 + m[scale_idx:scale_idx + 1, :]) + m[shift_idx:shift_idx + 1, :]
    o_ref[0] = o.astype(o_ref.dtype)


def _ln_mod(xs, mod_sel, shift_idx, scale_idx, ctx_len):
    b, s, d = xs.shape
    tm = ctx_len
    return pl.pallas_call(
        functools.partial(_ln_mod_kernel, shift_idx=shift_idx, scale_idx=scale_idx),
        out_shape=jax.ShapeDtypeStruct((b, s, d), BF16),
        grid=(b, s // tm),
        in_specs=[pl.BlockSpec((1, tm, d), lambda i, j: (i, j, 0)),
                  pl.BlockSpec((1, 1, 6, d), lambda i, j: (i, jnp.minimum(j, 1), 0, 0))],
        out_specs=pl.BlockSpec((1, tm, d), lambda i, j: (i, j, 0)),
        compiler_params=_cp(("parallel", "parallel")),
        name="ln_mod",
    )(xs, mod_sel)


def _mm_kernel(a_ref, w_ref, o_ref):
    o_ref[...] = jnp.dot(a_ref[...], w_ref[...], preferred_element_type=F32).astype(o_ref.dtype)


def _pick_tile(n, prefs):
    for t in prefs:
        if n % t == 0:
            return t
    return n


def _matmul(a, w, out_dtype, tm=None, tn=None, name="matmul"):
    m, k = a.shape
    n = w.shape[1]
    tm = tm or _pick_tile(m, (1024, 512, 256, 128))
    tn = tn or _pick_tile(n, (768, 512, 384, 256, 128))
    return pl.pallas_call(
        _mm_kernel,
        out_shape=jax.ShapeDtypeStruct((m, n), out_dtype),
        grid=(m // tm, n // tn),
        in_specs=[pl.BlockSpec((tm, k), lambda i, j: (i, 0)),
                  pl.BlockSpec((k, tn), lambda i, j: (0, j))],
        out_specs=pl.BlockSpec((tm, tn), lambda i, j: (i, j)),
        compiler_params=_cp(("parallel", "parallel")),
        name=name,
    )(a, w)


def _mm_nt_kernel(w_ref, u_ref, o_ref):
    o_ref[0] = lax.dot_general(w_ref[...], u_ref[0], (((1,), (1,)), ((), ())),
                               preferred_element_type=F32).astype(o_ref.dtype)


def _matmul_nt(w_t, u, out_dtype, tn=256):
    c, k = w_t.shape
    b, s, _ = u.shape
    return pl.pallas_call(
        _mm_nt_kernel,
        out_shape=jax.ShapeDtypeStruct((b, c, s), out_dtype),
        grid=(b, s // tn),
        in_specs=[pl.BlockSpec((c, k), lambda i, j: (0, 0)),
                  pl.BlockSpec((1, tn, k), lambda i, j: (i, j, 0))],
        out_specs=pl.BlockSpec((1, c, tn), lambda i, j: (i, 0, j)),
        compiler_params=_cp(("parallel", "parallel")),
        name="matmul_nt",
    )(w_t, u)


LRU_CHUNK = 128


def _tile_scan(a, b, row, reverse):
    for s in (1, 2, 4):
        if reverse:
            keep = row < SUBLANES - s
            shift = SUBLANES - s
        else:
            keep = row >= s
            shift = s
        a_sh = pltpu.roll(a, shift, 0)
        b_sh = pltpu.roll(b, shift, 0)
        b = jnp.where(keep, a * b_sh, 0.0) + b
        a = jnp.where(keep, a * a_sh, a)
    return a, b


def _lru_kernel(pg_ref, px_ref, cw_ref, cb_ref, wg_ref, bg_ref, lam_ref, y_ref,
                xpad, a_f, b_f, a_b, b_b, *, s_len, ctx_len):
    ch = LRU_CHUNK
    n_chunks = s_len // ch
    zeros8 = jnp.zeros((SUBLANES, LANES), F32)
    xpad[0:SUBLANES, :] = zeros8
    xpad[s_len + SUBLANES:s_len + 2 * SUBLANES, :] = zeros8

    def copy_body(i, carry):
        r = pl.multiple_of(i * ch, ch)
        xpad[pl.ds(r + SUBLANES, ch), :] = px_ref[0, pl.ds(r, ch), :]
        return carry

    lax.fori_loop(0, n_chunks, copy_body, 0)

    lam = lam_ref[...]
    sp = jnp.log(1.0 + jnp.exp(-lam))

    def gates_body(i, carry):
        r = pl.multiple_of(i * ch, ch)
        win = xpad[pl.ds(r, ch + 2 * SUBLANES), :]
        rows = r + lax.broadcasted_iota(jnp.int32, (ch, LANES), 0)
        is_lat = rows >= ctx_len
        xc = jnp.zeros((ch, LANES), F32) + cb_ref[...]
        for k in range(LRU_CONV):
            off = k - LRU_CONV // 2
            sh = win[SUBLANES + off:SUBLANES + off + ch, :]
            if off != 0:
                sh = jnp.where(((rows + off) >= ctx_len) == is_lat, sh, 0.0)
            xc = xc + sh * cw_ref[k:k + 1, :]
        gl = jnp.dot(xc.astype(BF16), wg_ref[0], preferred_element_type=F32) + bg_ref[0]
        for d, (a_s, b_s) in enumerate(((a_f, b_f), (a_b, b_b))):
            g_r = _sigmoid(gl[:, d * 2 * LANES:d * 2 * LANES + LANES])
            g_i = _sigmoid(gl[:, d * 2 * LANES + LANES:(d + 1) * 2 * LANES])
            log_a = (-LRU_C) * g_r * sp[d:d + 1, :]
            a = jnp.exp(log_a)
            bb = jnp.sqrt(1.0 - jnp.exp(2.0 * log_a)) * g_i * xc
            a_s[pl.ds(r, ch), :] = a
            b_s[pl.ds(r, ch), :] = bb
        return carry

    lax.fori_loop(0, n_chunks, gates_body, 0)

    row = lax.broadcasted_iota(jnp.int32, (SUBLANES, LANES), 0)

    def fwd_body(t, h):
        r = pl.multiple_of(t * SUBLANES, SUBLANES)
        a, b = _tile_scan(a_f[pl.ds(r, SUBLANES), :], b_f[pl.ds(r, SUBLANES), :], row, False)
        hh = b + a * h
        b_f[pl.ds(r, SUBLANES), :] = hh
        return jnp.broadcast_to(hh[SUBLANES - 1:SUBLANES, :], (SUBLANES, LANES))

    lax.fori_loop(0, s_len // SUBLANES, fwd_body, zeros8)

    def make_bwd(last_tile):
        def bwd_body(i, h):
            t = last_tile - i
            r = pl.multiple_of(t * SUBLANES, SUBLANES)
            a, b = _tile_scan(a_b[pl.ds(r, SUBLANES), :], b_b[pl.ds(r, SUBLANES), :], row, True)
            hh = b + a * h
            b_b[pl.ds(r, SUBLANES), :] = hh
            return jnp.broadcast_to(hh[0:1, :], (SUBLANES, LANES))
        return bwd_body

    n_ctx_tiles = ctx_len // SUBLANES
    n_tiles = s_len // SUBLANES
    h_ctx = lax.fori_loop(0, n_ctx_tiles, make_bwd(n_ctx_tiles - 1), zeros8)
    lax.fori_loop(0, n_tiles - n_ctx_tiles, make_bwd(n_tiles - 1), h_ctx)

    def out_body(i, carry):
        r = pl.multiple_of(i * ch, ch)
        g = _gelu(pg_ref[0, pl.ds(r, ch), :])
        y = g * (b_f[pl.ds(r, ch), :] + b_b[pl.ds(r, ch), :])
        y_ref[0, pl.ds(r, ch), :] = y.astype(y_ref.dtype)
        return carry

    lax.fori_loop(0, n_chunks, out_body, 0)


def _lru_gate_weights(w_r, w_i, b_r, b_i):
    n_grp = LRU_WIDTH // LANES
    per = LANES // LRU_BLOCK

    def bd(w):
        w = w.reshape(n_grp, per, LRU_BLOCK, LRU_BLOCK)
        z = jnp.zeros((n_grp, LRU_BLOCK, LRU_BLOCK), w.dtype)
        top = jnp.concatenate([w[:, 0], z], axis=2)
        bot = jnp.concatenate([z, w[:, 1]], axis=2)
        return jnp.concatenate([top, bot], axis=1)

    wg = jnp.concatenate([bd(w_r[0]), bd(w_i[0]), bd(w_r[1]), bd(w_i[1])], axis=2).astype(BF16)
    bg = jnp.stack([b_r[0], b_i[0], b_r[1], b_i[1]], axis=0).reshape(4, n_grp, LANES)
    bg = jnp.transpose(bg, (1, 0, 2)).reshape(n_grp, 1, 4 * LANES)
    return wg, bg


def _lru_mixer(p_a, lp, ctx_len):
    b, s, _ = p_a.shape
    n_grp = LRU_WIDTH // LANES
    wg, bg = _lru_gate_weights(lp['lru_w_r'], lp['lru_w_i'], lp['lru_b_r'], lp['lru_b_i'])
    scr = pltpu.VMEM((s, LANES), F32)
    return pl.pallas_call(
        functools.partial(_lru_kernel, s_len=s, ctx_len=ctx_len),
        out_shape=jax.ShapeDtypeStruct((b, s, LRU_WIDTH), BF16),
        grid=(b, n_grp),
        in_specs=[pl.BlockSpec((1, s, LANES), lambda i, g: (i, 0, g)),
                  pl.BlockSpec((1, s, LANES), lambda i, g: (i, 0, n_grp + g)),
                  pl.BlockSpec((LRU_CONV, LANES), lambda i, g: (0, g)),
                  pl.BlockSpec((1, LANES), lambda i, g: (0, g)),
                  pl.BlockSpec((1, LANES, 4 * LANES), lambda i, g: (g, 0, 0)),
                  pl.BlockSpec((1, 1, 4 * LANES), lambda i, g: (g, 0, 0)),
                  pl.BlockSpec((2, LANES), lambda i, g: (0, g))],
        out_specs=pl.BlockSpec((1, s, LANES), lambda i, g: (i, 0, g)),
        scratch_shapes=[pltpu.VMEM((s + 2 * SUBLANES, LANES), F32), scr, scr, scr, scr],
        compiler_params=_cp(("parallel", "parallel")),
        name="rglru",
    )(p_a, p_a, lp['lru_conv_w'], lp['lru_conv_b'].reshape(1, LRU_WIDTH), wg, bg, lp['lru_lambda'])


S5_CHUNK = 256
S5_NSTATE = S5_GROUPS * S5_STATE


def _s5_kernel(u_ref, wb_ref, wc_ref, ap_ref, y_ref, st_ref, carry_ref, *, reverse):
    tc = S5_CHUNK
    n = S5_NSTATE
    j = pl.program_id(1)

    @pl.when(j == 0)
    def _():
        carry_ref[...] = jnp.zeros_like(carry_ref)

    st_ref[...] = jnp.dot(u_ref[0].astype(BF16), wb_ref[...], preferred_element_type=F32)
    row = lax.broadcasted_iota(jnp.int32, (SUBLANES, n), 0)
    n_tiles = tc // SUBLANES

    def body(i, carry):
        hr, hi = carry
        t = (n_tiles - 1 - i) if reverse else i
        r = pl.multiple_of(t * SUBLANES, SUBLANES)
        br = st_ref[pl.ds(r, SUBLANES), 0:n]
        bi = st_ref[pl.ds(r, SUBLANES), n:2 * n]
        for k, s in enumerate((1, 2, 4)):
            ar = ap_ref[SUBLANES + k:SUBLANES + k + 1, 0:n]
            ai = ap_ref[SUBLANES + k:SUBLANES + k + 1, n:2 * n]
            if reverse:
                keep = row < SUBLANES - s
                shift = SUBLANES - s
            else:
                keep = row >= s
                shift = s
            brs = pltpu.roll(br, shift, 0)
            bis = pltpu.roll(bi, shift, 0)
            nr = ar * brs - ai * bis
            ni = ar * bis + ai * brs
            br = br + jnp.where(keep, nr, 0.0)
            bi = bi + jnp.where(keep, ni, 0.0)
        cr = ap_ref[0:SUBLANES, 0:n]
        ci = ap_ref[0:SUBLANES, n:2 * n]
        out_r = br + (cr * hr - ci * hi)
        out_i = bi + (cr * hi + ci * hr)
        st_ref[pl.ds(r, SUBLANES), 0:n] = out_r
        st_ref[pl.ds(r, SUBLANES), n:2 * n] = out_i
        e = 0 if reverse else SUBLANES - 1
        return (jnp.broadcast_to(out_r[e:e + 1, :], (SUBLANES, n)),
                jnp.broadcast_to(out_i[e:e + 1, :], (SUBLANES, n)))

    hr, hi = lax.fori_loop(0, n_tiles, body, (carry_ref[:, 0:n], carry_ref[:, n:2 * n]))
    carry_ref[:, 0:n] = hr
    carry_ref[:, n:2 * n] = hi
    y_ref[0] = jnp.dot(st_ref[...].astype(BF16), wc_ref[...], preferred_element_type=F32)


def _s5_params(a_re, a_im, log_dt, b_re, b_im, c_re, c_im, reverse):
    dt = jnp.exp(log_dt)[:, None]
    den = a_re * a_re + a_im * a_im
    mag = jnp.exp(dt * a_re)
    ab_re = mag * jnp.cos(dt * a_im)
    ab_im = mag * jnp.sin(dt * a_im)
    f_re = ((ab_re - 1.0) * a_re + ab_im * a_im) / den
    f_im = (ab_im * a_re - (ab_re - 1.0) * a_im) / den
    bb_re = f_re[..., None] * b_re - f_im[..., None] * b_im
    bb_im = f_re[..., None] * b_im + f_im[..., None] * b_re
    eye = jnp.eye(S5_GROUPS, dtype=F32)
    wb_re = jnp.einsum('gpi,gh->gihp', bb_re, eye).reshape(S5_WIDTH, S5_NSTATE)
    wb_im = jnp.einsum('gpi,gh->gihp', bb_im, eye).reshape(S5_WIDTH, S5_NSTATE)
    wb = jnp.concatenate([wb_re, wb_im], axis=1).astype(BF16)
    wc_re = jnp.einsum('gop,gh->gpho', c_re, eye).reshape(S5_NSTATE, S5_WIDTH)
    wc_im = jnp.einsum('gop,gh->gpho', c_im, eye).reshape(S5_NSTATE, S5_WIDTH)
    wc = jnp.concatenate([wc_re, -wc_im], axis=0).astype(BF16)

    def apow(k):
        m = jnp.exp(k * dt[None] * a_re[None])
        return m * jnp.cos(k * dt[None] * a_im[None]), m * jnp.sin(k * dt[None] * a_im[None])

    i = jnp.arange(SUBLANES, dtype=F32)
    dist = (SUBLANES - i) if reverse else (i + 1.0)
    ks = jnp.concatenate([dist, jnp.array([1.0, 2.0, 4.0], F32), jnp.zeros((5,), F32)])
    pr, pi = apow(ks[:, None, None])
    ap = jnp.concatenate([pr.reshape(16, S5_NSTATE), pi.reshape(16, S5_NSTATE)], axis=1)
    return wb, wc, ap


def _s5_scan(p_d, lp, d, ctx_len):
    b, s, w = p_d.shape
    reverse = d == 1
    assert ctx_len == S5_CHUNK
    nc = s // S5_CHUNK
    wb, wc, ap = _s5_params(lp['s5_a_re'][d], lp['s5_a_im'][d], lp['s5_log_dt'][d], lp['s5_b_re'][d],
                            lp['s5_b_im'][d], lp['s5_c_re'][d], lp['s5_c_im'][d], reverse)
    if reverse:
        cidx = lambda i, j: (i, jnp.where(j == 0, 0, nc - j), 0)
    else:
        cidx = lambda i, j: (i, j, 0)
    return pl.pallas_call(
        functools.partial(_s5_kernel, reverse=reverse),
        out_shape=jax.ShapeDtypeStruct((b, s, w), F32),
        grid=(b, nc),
        in_specs=[pl.BlockSpec((1, S5_CHUNK, w), cidx),
                  pl.BlockSpec((w, 2 * S5_NSTATE), lambda i, j: (0, 0)),
                  pl.BlockSpec((2 * S5_NSTATE, w), lambda i, j: (0, 0)),
                  pl.BlockSpec((16, 2 * S5_NSTATE), lambda i, j: (0, 0))],
        out_specs=pl.BlockSpec((1, S5_CHUNK, w), cidx),
        scratch_shapes=[pltpu.VMEM((S5_CHUNK, 2 * S5_NSTATE), F32),
                        pltpu.VMEM((SUBLANES, 2 * S5_NSTATE), F32)],
        compiler_params=_cp(("parallel", "arbitrary")),
        name="s5_scan_bwd" if reverse else "s5_scan_fwd",
    )(p_d, wb, wc, ap)


def _s5_out_kernel(yf_ref, yb_ref, u_ref, d_ref, w_ref, b_ref, o_ref):
    y = yf_ref[...] + yb_ref[...] + d_ref[...] * u_ref[...]
    g = _gelu(y)
    z = jnp.dot(g.astype(BF16), w_ref[...], preferred_element_type=F32) + b_ref[...]
    o_ref[...] = (g * _sigmoid(z)).astype(o_ref.dtype)


def _s5_mixer(p_d, lp, ctx_len):
    b, s, w = p_d.shape
    yf = _s5_scan(p_d, lp, 0, ctx_len)
    yb = _s5_scan(p_d, lp, 1, ctx_len)
    m = b * s
    tm = _pick_tile(m, (1024, 512, 256))
    row = pl.BlockSpec((tm, w), lambda i: (i, 0))
    vec = pl.BlockSpec((1, w), lambda i: (0, 0))
    out = pl.pallas_call(
        _s5_out_kernel,
        out_shape=jax.ShapeDtypeStruct((m, w), BF16),
        grid=(m // tm,),
        in_specs=[row, row, row, vec, pl.BlockSpec((w, w), lambda i: (0, 0)), vec],
        out_specs=row,
        compiler_params=_cp(("parallel",)),
        name="s5_out",
    )(yf.reshape(m, w), yb.reshape(m, w), p_d.reshape(m, w), lp['s5_d'].reshape(1, w),
      lp['s5_w_glu'].astype(BF16), lp['s5_b_glu'].reshape(1, w))
    return out.reshape(b, s, w)


def _natten_bias(rpb, rows):
    kr = min(NA_WIN_R, rows)
    w = jnp.arange(GRID_W)
    col = jnp.arange(GRID_W)
    cs = jnp.clip(w - NA_WIN_C // 2, 0, GRID_W - NA_WIN_C)
    valid = (col[None, :] >= cs[:, None]) & (col[None, :] < cs[:, None] + NA_WIN_C)
    dc = jnp.clip(col[None, :] - w[:, None] + (NA_WIN_C - 1), 0, 2 * NA_WIN_C - 2)
    v = jnp.arange(NA_WIN_R)
    i = jnp.arange(kr)
    dr = jnp.clip(v[:, None] + i[None, :], 0, 2 * NA_WIN_R - 2)
    g = rpb[:, dr[:, :, None, None], dc[None, None, :, :]]
    g = jnp.where(valid[None, None, None], g, MASK_VALUE)
    g = jnp.transpose(g, (1, 0, 3, 2, 4))
    return g.reshape(NA_WIN_R, NA_HEADS, GRID_W, kr * GRID_W).astype(F32)


def _attend(q2, keys, vals, biases, lane):
    out = jnp.zeros(q2.shape, F32)
    for hh in range(2):
        sel = (lane >= hh * NA_HEAD_DIM) & (lane < (hh + 1) * NA_HEAD_DIM)
        qm = jnp.where(sel, q2, jnp.zeros_like(q2))
        ss = []
        for k_i, b_i in zip(keys, biases[hh]):
            s_i = lax.dot_general(qm, k_i, (((1,), (1,)), ((), ())), preferred_element_type=F32)
            if b_i is not None:
                s_i = s_i + b_i
            ss.append(s_i)
        m = ss[0].max(axis=-1, keepdims=True)
        for s_i in ss[1:]:
            m = jnp.maximum(m, s_i.max(axis=-1, keepdims=True))
        ps = [jnp.exp(s_i - m) for s_i in ss]
        den = ps[0].sum(axis=-1, keepdims=True)
        for p_i in ps[1:]:
            den = den + p_i.sum(axis=-1, keepdims=True)
        o = jnp.dot(ps[0].astype(BF16), vals[0], preferred_element_type=F32)
        for p_i, v_i in zip(ps[1:], vals[1:]):
            o = o + jnp.dot(p_i.astype(BF16), v_i, preferred_element_type=F32)
        o = o / den
        out = jnp.where(sel[:, :], o, out)
    return out


def _natten_kernel(q_ref, k_ref, v_ref, bias_ref, o_ref, *, rows, ctx_len):
    kr = min(NA_WIN_R, rows)
    r = pl.program_id(1)
    rs = jnp.clip(r - kr // 2, 0, rows - kr)
    base = pl.multiple_of(ctx_len + rs * GRID_W, GRID_W)
    lane = lax.broadcasted_iota(jnp.int32, (GRID_W, LANES), 1)
    scale = NA_HEAD_DIM ** -0.5
    for hp in range(NA_HEADS // 2):
        ls = slice(hp * LANES, (hp + 1) * LANES)
        q2 = (q_ref[0, :, ls] * scale).astype(BF16)
        kw = k_ref[0, pl.ds(base, kr * GRID_W), ls].astype(BF16)
        vw = v_ref[0, pl.ds(base, kr * GRID_W), ls].astype(BF16)
        kc = k_ref[0, 0:ctx_len, ls].astype(BF16)
        vc = v_ref[0, 0:ctx_len, ls].astype(BF16)
        biases = [[bias_ref[0, 2 * hp + hh], None] for hh in range(2)]
        out = _attend(q2, [kw, kc], [vw, vc], biases, lane)
        o_ref[0, :, ls] = out.astype(o_ref.dtype)


def _ctx_attn_kernel(q_ref, k_ref, v_ref, o_ref, *, ctx_len):
    lane = lax.broadcasted_iota(jnp.int32, (ctx_len, LANES), 1)
    scale = NA_HEAD_DIM ** -0.5
    for hp in range(NA_HEADS // 2):
        ls = slice(hp * LANES, (hp + 1) * LANES)
        q2 = (q_ref[0, :, ls] * scale).astype(BF16)
        kc = k_ref[0, :, ls].astype(BF16)
        vc = v_ref[0, :, ls].astype(BF16)
        out = _attend(q2, [kc], [vc], [[None], [None]], lane)
        o_ref[0, :, ls] = out.astype(o_ref.dtype)


def _natten_mixer(p_c, rpb, ctx_len, with_ctx):
    b, s, _ = p_c.shape
    l = s - ctx_len
    rows = l // GRID_W
    kr = min(NA_WIN_R, rows)
    bias = _natten_bias(rpb, rows)
    cb = ctx_len // GRID_W
    y_l = pl.pallas_call(
        functools.partial(_natten_kernel, rows=rows, ctx_len=ctx_len),
        out_shape=jax.ShapeDtypeStruct((b, s, NA_WIDTH), BF16),
        grid=(b, rows),
        in_specs=[pl.BlockSpec((1, GRID_W, NA_WIDTH), lambda i, r: (i, cb + r, 0)),
                  pl.BlockSpec((1, s, NA_WIDTH), lambda i, r: (i, 0, 1)),
                  pl.BlockSpec((1, s, NA_WIDTH), lambda i, r: (i, 0, 2)),
                  pl.BlockSpec((1, NA_HEADS, GRID_W, kr * GRID_W),
                               lambda i, r: ((NA_WIN_R - 1) + jnp.clip(r - kr // 2, 0, rows - kr) - r, 0, 0, 0))],
        out_specs=pl.BlockSpec((1, GRID_W, NA_WIDTH), lambda i, r: (i, cb + r, 0)),
        compiler_params=_cp(("parallel", "arbitrary")),
        name="natten",
    )(p_c, p_c, p_c, bias)
    if not with_ctx:
        return y_l, None
    y_c = pl.pallas_call(
        functools.partial(_ctx_attn_kernel, ctx_len=ctx_len),
        out_shape=jax.ShapeDtypeStruct((b, ctx_len, NA_WIDTH), BF16),
        grid=(b,),
        in_specs=[pl.BlockSpec((1, ctx_len, NA_WIDTH), lambda i: (i, 0, 0)),
                  pl.BlockSpec((1, ctx_len, NA_WIDTH), lambda i: (i, 0, 1)),
                  pl.BlockSpec((1, ctx_len, NA_WIDTH), lambda i: (i, 0, 2))],
        out_specs=pl.BlockSpec((1, ctx_len, NA_WIDTH), lambda i: (i, 0, 0)),
        compiler_params=_cp(("parallel",)),
        name="ctx_attn",
    )(p_c, p_c, p_c)
    return y_l, y_c


DFT_ROWS = 64


def _dft_gen_kernel(ca_ref, sa_ref, cb_ref, sb_ref, fwd_ref, inv_ref, *, l):
    i = pl.program_id(0)
    ca = ca_ref[0]
    sa = sa_ref[0]
    cb = cb_ref[...]
    sb = sb_ref[...]
    gc = ca * cb - sa * sb
    gs = sa * cb + ca * sb
    x = i * DFT_ROWS + lax.broadcasted_iota(jnp.int32, (DFT_ROWS, l), 0)
    y = lax.broadcasted_iota(jnp.int32, (DFT_ROWS, l), 1)
    n = 2.0 * l
    nyq_x = jnp.where((x & 1) == 0, 1.0, -1.0)
    fwd_ref[:, 0:l] = gc.astype(fwd_ref.dtype)
    fwd_ref[:, l:2 * l] = jnp.where(y == 0, nyq_x, -gs).astype(fwd_ref.dtype)
    scale = jnp.where(x == 0, 1.0 / n, 2.0 / n)
    nyq_y = jnp.where((y & 1) == 0, 1.0, -1.0)
    inv_ref[0] = (scale * gc).astype(inv_ref.dtype)
    inv_ref[1] = (scale * jnp.where(x == 0, nyq_y, -gs)).astype(inv_ref.dtype)


def _dft_matrices(l):
    n = 2 * l
    k1 = l // DFT_ROWS
    y = np.arange(l, dtype=np.int64)
    xa = (DFT_ROWS * np.arange(k1, dtype=np.int64))[:, None]
    xb = np.arange(DFT_ROWS, dtype=np.int64)[:, None]
    ang_a = jnp.asarray(((xa * y[None, :]) % n).astype(np.float32)) * F32(2.0 * math.pi / n)
    ang_b = jnp.asarray(((xb * y[None, :]) % n).astype(np.float32)) * F32(2.0 * math.pi / n)
    ca, sa = jnp.cos(ang_a).reshape(k1, 1, l), jnp.sin(ang_a).reshape(k1, 1, l)
    cb, sb = jnp.cos(ang_b), jnp.sin(ang_b)
    row = pl.BlockSpec((1, 1, l), lambda i: (i, 0, 0))
    tab = pl.BlockSpec((DFT_ROWS, l), lambda i: (0, 0))
    fwd, inv = pl.pallas_call(
        functools.partial(_dft_gen_kernel, l=l),
        out_shape=(jax.ShapeDtypeStruct((l, 2 * l), BF16), jax.ShapeDtypeStruct((2, l, l), BF16)),
        grid=(k1,),
        in_specs=[row, row, tab, tab],
        out_specs=(pl.BlockSpec((DFT_ROWS, 2 * l), lambda i: (i, 0)),
                   pl.BlockSpec((2, DFT_ROWS, l), lambda i: (0, i, 0))),
        compiler_params=_cp(("parallel",)),
        name="dft_gen",
    )(ca, sa, cb, sb)
    return fwd, inv.reshape(2 * l, l)


def _hy_filter_kernel(w1t_ref, w1c_ref, w1s_ref, b1_ref, w2_ref, b2_ref, w3_ref, b3_ref, fr_ref, dl_ref,
                      h_ref, asum_ref, *, l, tl):
    j = pl.program_id(0)
    t = (j * tl + lax.broadcasted_iota(jnp.int32, (1, tl), 1)).astype(F32)
    t_norm = t / l
    bands = (1 + lax.broadcasted_iota(jnp.int32, (HY_BANDS, 1), 0)).astype(F32)
    ang = (2.0 * math.pi / l) * t * bands
    fr = fr_ref[...]
    lin = (w1t_ref[...] * t_norm
           + jnp.dot(w1c_ref[...], jnp.cos(ang), preferred_element_type=F32, precision=HIGHEST)
           + jnp.dot(w1s_ref[...], jnp.sin(ang), preferred_element_type=F32, precision=HIGHEST))
    h = jnp.sin(fr * (lin + b1_ref[...]))
    h = jnp.sin(fr * (jnp.dot(w2_ref[...], h, preferred_element_type=F32, precision=HIGHEST) + b2_ref[...]))
    h = jnp.dot(w3_ref[...], h, preferred_element_type=F32, precision=HIGHEST) + b3_ref[...]
    window = jnp.exp(-t_norm * dl_ref[...])
    first = (j * tl + lax.broadcasted_iota(jnp.int32, (HY_WIDTH, tl), 1)) == 0

    @pl.when(j == 0)
    def _():
        asum_ref[...] = jnp.zeros_like(asum_ref)

    for blk in range(2 * HY_ORDER):
        rs = slice(blk * HY_WIDTH, (blk + 1) * HY_WIDTH)
        hb = h[rs, :] * window
        if blk >= HY_ORDER:
            hb = jnp.where(first, 0.0, hb)
        h_ref[rs, :] = hb
        asum_ref[rs, :] += jnp.sum(jnp.abs(hb), axis=1, keepdims=True)


def _hy_filters(lp, l):
    tl = min(l, 512)
    hid = HY_FILT_HID
    w1 = lp['hy_w1']
    col = lambda v: v.reshape(-1, 1)
    deltas = np.abs(np.linspace(HY_MIN_DECAY, HY_MAX_DECAY, HY_WIDTH, dtype=np.float32)).reshape(-1, 1)
    full = lambda shape: pl.BlockSpec(shape, lambda j: (0, 0))
    n_out = 2 * HY_ORDER * HY_WIDTH
    return pl.pallas_call(
        functools.partial(_hy_filter_kernel, l=l, tl=tl),
        out_shape=(jax.ShapeDtypeStruct((n_out, l), F32), jax.ShapeDtypeStruct((n_out, 1), F32)),
        grid=(l // tl,),
        in_specs=[full((hid, 1)), full((hid, HY_BANDS)), full((hid, HY_BANDS)), full((hid, 1)),
                  full((hid, hid)), full((hid, 1)), full((n_out, hid)), full((n_out, 1)),
                  full((hid, 1)), full((HY_WIDTH, 1))],
        out_specs=(pl.BlockSpec((n_out, tl), lambda j: (0, j)), pl.BlockSpec((n_out, 1), lambda j: (0, 0))),
        compiler_params=_cp(("arbitrary",)),
        name="hyena_filters",
    )(w1[0:1].T, w1[1:1 + HY_BANDS].T, w1[1 + HY_BANDS:].T, col(lp['hy_b1']), lp['hy_w2'].T, col(lp['hy_b2']),
      lp['hy_w3'].T, col(lp['hy_b3']), col(lp['hy_freq']), jnp.asarray(deltas))


def _hy_short_conv_kernel(x_ref, w_ref, b_ref, z_ref, zbf_ref, *, l):
    x = x_ref[0]
    t = lax.broadcasted_iota(jnp.int32, x.shape, 1)
    left = HY_SHORT // 2
    z = jnp.zeros(x.shape, F32) + b_ref[...]
    for k in range(HY_SHORT):
        off = k - left
        if off == 0:
            sh = x
        else:
            sh = pltpu.roll(x, (-off) % l, 1)
            sh = jnp.where((t + off >= 0) & (t + off < l), sh, 0.0)
        z = z + sh * w_ref[:, k:k + 1]
    z_ref[0] = z
    zbf_ref[0] = z.astype(zbf_ref.dtype)


def _hy_conv_kernel(y_ref, fc_ref, fs_ref, ic_ref, is_ref, kfc_ref, kfs_ref, kbc_ref, kbs_ref, n_ref,
                    o_ref, *, nb):
    j = pl.program_id(1)

    @pl.when(j == 0)
    def _():
        o_ref[...] = jnp.zeros_like(o_ref)

    y = y_ref[...]
    zr = jnp.dot(y, fc_ref[...], preferred_element_type=F32)
    zi = jnp.dot(y, fs_ref[...], preferred_element_type=F32)
    inv_n = 1.0 / (n_ref[...] + 1e-6)
    kr = (kfc_ref[...] + kbc_ref[...]) * inv_n
    ki = (kfs_ref[...] - kbs_ref[...]) * inv_n
    tn = kr.shape[1]
    f0 = (j * tn + lax.broadcasted_iota(jnp.int32, kr.shape, 1)) == 0
    ki = jnp.where(f0, (kfs_ref[...] + kbs_ref[...]) * inv_n, ki)
    prs, pis = [], []
    for bb in range(nb):
        rs = slice(bb * HY_WIDTH, (bb + 1) * HY_WIDTH)
        a, b = zr[rs], zi[rs]
        prs.append(a * kr - jnp.where(f0, 0.0, b * ki))
        pis.append(jnp.where(f0, b * ki, a * ki + b * kr))
    pr = jnp.concatenate(prs, axis=0).astype(BF16)
    pi = jnp.concatenate(pis, axis=0).astype(BF16)
    o_ref[...] += (jnp.dot(pr, ic_ref[...], preferred_element_type=F32)
                   + jnp.dot(pi, is_ref[...], preferred_element_type=F32))


def _hy_long_conv(ybf, n_tiles, nb, row_stride, fwd, inv, kf, asum, order, l):
    tmh = nb * HY_WIDTH
    m = n_tiles * tmh
    tn = min(l, 256)
    jn = l // tn
    o_f = order
    o_b = HY_ORDER + order
    return pl.pallas_call(
        functools.partial(_hy_conv_kernel, nb=nb),
        out_shape=jax.ShapeDtypeStruct((m, l), F32),
        grid=(n_tiles, jn),
        in_specs=[pl.BlockSpec((tmh, l), lambda i, j: (i * row_stride, 0)),
                  pl.BlockSpec((l, tn), lambda i, j: (0, j)),
                  pl.BlockSpec((l, tn), lambda i, j: (0, jn + j)),
                  pl.BlockSpec((tn, l), lambda i, j: (j, 0)),
                  pl.BlockSpec((tn, l), lambda i, j: (jn + j, 0)),
                  pl.BlockSpec((HY_WIDTH, tn), lambda i, j: (o_f, j)),
                  pl.BlockSpec((HY_WIDTH, tn), lambda i, j: (o_f, jn + j)),
                  pl.BlockSpec((HY_WIDTH, tn), lambda i, j: (o_b, j)),
                  pl.BlockSpec((HY_WIDTH, tn), lambda i, j: (o_b, jn + j)),
                  pl.BlockSpec((HY_WIDTH, 1), lambda i, j: (order, 0))],
        out_specs=pl.BlockSpec((tmh, l), lambda i, j: (i, 0)),
        compiler_params=_cp(("parallel", "arbitrary")),
        name="hyena_long_conv",
    )(ybf, fwd, fwd, inv, inv, kf, kf, kf, kf, asum)


def _hy_gate_kernel(g_ref, c_ref, y_ref, bias_ref, o_ref, obf_ref):
    o = g_ref[0] * (c_ref[0] + y_ref[0] * bias_ref[...])
    o_ref[0] = o
    obf_ref[0] = o.astype(obf_ref.dtype)


def _hy_gate(z, conv, y, bias_col, gate_blk, y_blk):
    b, _, l = z.shape
    tl = min(l, 1024)
    spec = lambda blk: pl.BlockSpec((1, HY_WIDTH, tl), lambda i, j: (i, blk, j))
    return pl.pallas_call(
        _hy_gate_kernel,
        out_shape=(jax.ShapeDtypeStruct((b, HY_WIDTH, l), F32), jax.ShapeDtypeStruct((b, HY_WIDTH, l), BF16)),
        grid=(b, l // tl),
        in_specs=[spec(gate_blk), spec(0), spec(y_blk), pl.BlockSpec((HY_WIDTH, 1), lambda i, j: (0, 0))],
        out_specs=(spec(0), spec(0)),
        compiler_params=_cp(("parallel", "parallel")),
        name="hyena_gate",
    )(z, conv, y, bias_col)


def _hyena_sequence(p_bt, lp, dft):
    b, c3, l = p_bt.shape
    fwd, inv = dft
    h, asum = _hy_filters(lp, l)
    kf = _matmul(h.astype(BF16), fwd, F32, name="hyena_filter_dft")
    asum2 = asum.reshape(2, HY_ORDER * HY_WIDTH).sum(axis=0).reshape(HY_ORDER * HY_WIDTH, 1)
    blk = pl.BlockSpec((1, LANES, l), lambda i, g: (i, g, 0))
    z, z_bf = pl.pallas_call(
        functools.partial(_hy_short_conv_kernel, l=l),
        out_shape=(jax.ShapeDtypeStruct((b, c3, l), F32), jax.ShapeDtypeStruct((b, c3, l), BF16)),
        grid=(b, c3 // LANES),
        in_specs=[blk,
                  pl.BlockSpec((LANES, HY_SHORT), lambda i, g: (g, 0)),
                  pl.BlockSpec((LANES, 1), lambda i, g: (g, 0))],
        out_specs=(blk, blk),
        compiler_params=_cp(("parallel", "parallel")),
        name="hyena_short_conv",
    )(p_bt, lp['hy_conv_w'].T, lp['hy_conv_b'].reshape(c3, 1))
    n_blk = c3 // HY_WIDTH
    nb2 = 2 if b % 2 == 0 else 1
    conv1 = _hy_long_conv(z_bf.reshape(b * c3, l), b, 1, n_blk, fwd, inv, kf, asum2, 0, l).reshape(b, HY_WIDTH, l)
    y1, y1_bf = _hy_gate(z, conv1, z, lp['hy_bias'][0].reshape(HY_WIDTH, 1), 1, 0)
    conv2 = _hy_long_conv(y1_bf.reshape(b * HY_WIDTH, l), b // nb2, nb2, 1, fwd, inv, kf, asum2, 1, l)
    _, y2_bf = _hy_gate(z, conv2.reshape(b, HY_WIDTH, l), y1, lp['hy_bias'][1].reshape(HY_WIDTH, 1), 2, 0)
    return y2_bf


def _merge_kernel(ya_ref, ybl_ref, ybc_ref, yc_ref, yd_ref, pg_ref, x_ref, mod_ref, wa_ref, wb_ref, wc_ref,
                  wd_ref, wo_ref, lng_ref, lnb_ref, x_out_ref, u_out_ref, *, alpha, first_tile):
    j = pl.program_id(1) + first_tile
    d = x_ref.shape[-1]
    yb_t = jnp.where(j == 0, ybc_ref[0], ybl_ref[0])
    yb = yb_t.astype(F32).T.astype(BF16)
    projs = [jnp.dot(ya_ref[0], wa_ref[...], preferred_element_type=F32),
             jnp.dot(yb, wb_ref[...], preferred_element_type=F32),
             jnp.dot(yc_ref[0], wc_ref[...], preferred_element_type=F32),
             jnp.dot(yd_ref[0], wd_ref[...], preferred_element_type=F32)]
    merged = None
    for i, pr in enumerate(projs):
        term = _sigmoid(pg_ref[0, :, i * d:(i + 1) * d]) * pr
        merged = term if merged is None else merged + term
    m = jnp.dot(merged.astype(BF16), wo_ref[...], preferred_element_type=F32)
    mod = mod_ref[0, 0]
    xn = _layer_norm_rows(alpha * x_ref[0] + mod[2:3, :] * m) * lng_ref[...] + lnb_ref[...]
    x_out_ref[0] = xn
    u = _layer_norm_rows(xn) * (1.0 + mod[4:5, :]) + mod[3:4, :]
    u_out_ref[0] = u.astype(u_out_ref.dtype)


def _merge(ya, yb_lat, yb_ctx, yc, yd, pg, xs, mod_sel, lp, ln_g, ln_b, alpha, ctx_len, with_ctx):
    b, s, d = xs.shape
    tm = ctx_len
    first = 0 if with_ctx else 1
    nt = s // tm - first
    tok = lambda w: pl.BlockSpec((1, tm, w), lambda i, j: (i, j + first, 0))
    full = lambda shape: pl.BlockSpec(shape, lambda i, j: (0,) * len(shape))
    x_new, u2 = pl.pallas_call(
        functools.partial(_merge_kernel, alpha=alpha, first_tile=first),
        out_shape=(jax.ShapeDtypeStruct((b, s, d), F32), jax.ShapeDtypeStruct((b, s, d), BF16)),
        grid=(b, nt),
        in_specs=[tok(LRU_WIDTH),
                  pl.BlockSpec((1, HY_WIDTH, tm), lambda i, j: (i, 0, jnp.maximum(j + first - 1, 0))),
                  pl.BlockSpec((1, HY_WIDTH, tm), lambda i, j: (i, 0, 0)),
                  tok(NA_WIDTH), tok(S5_WIDTH), tok(N_BRANCH * d), tok(d),
                  pl.BlockSpec((1, 1, 6, d), lambda i, j: (i, jnp.minimum(j + first, 1), 0, 0)),
                  full((LRU_WIDTH, d)), full((HY_WIDTH, d)), full((NA_WIDTH, d)), full((S5_WIDTH, d)),
                  full((d, d)), full((1, d)), full((1, d))],
        out_specs=(tok(d), tok(d)),
        compiler_params=_cp(("parallel", "parallel")),
        name="merge",
    )(ya, yb_lat, yb_ctx, yc, yd, pg, xs, mod_sel,
      lp['w_br_a'].astype(BF16), lp['w_br_b'].astype(BF16), lp['w_br_c'].astype(BF16),
      lp['w_br_d'].astype(BF16), lp['w_out'].astype(BF16), ln_g.reshape(1, d), ln_b.reshape(1, d))
    return x_new, u2


def _ffn_kernel(te_ref, nu_ref, x_ref, wg_ref, wu_ref, wd_ref, o_ref, acc_ref):
    i = pl.program_id(0)
    j = pl.program_id(1)

    @pl.when(j == 0)
    def _():
        acc_ref[...] = jnp.zeros_like(acc_ref)

    @pl.when(i < nu_ref[0])
    def _():
        x = x_ref[...]
        g = jnp.dot(x, wg_ref[0], preferred_element_type=F32)
        u = jnp.dot(x, wu_ref[0], preferred_element_type=F32)
        h = (g * _sigmoid(g)) * u
        acc_ref[...] += jnp.dot(h.astype(BF16), wd_ref[0], preferred_element_type=F32)

    @pl.when(j == pl.num_programs(1) - 1)
    def _():
        o_ref[...] = acc_ref[...].astype(o_ref.dtype)


def _grouped_ffn(x_rows, tile_expert, n_used, w_gate, w_up, w_down, tm, tf, out_dtype):
    n, d = x_rows.shape
    ff = w_gate.shape[2]
    nt = n // tm
    grid_spec = pltpu.PrefetchScalarGridSpec(
        num_scalar_prefetch=2,
        grid=(nt, ff // tf),
        in_specs=[pl.BlockSpec((tm, d), lambda i, j, te, nu: (i, 0)),
                  pl.BlockSpec((1, d, tf), lambda i, j, te, nu: (te[i], 0, j)),
                  pl.BlockSpec((1, d, tf), lambda i, j, te, nu: (te[i], 0, j)),
                  pl.BlockSpec((1, tf, d), lambda i, j, te, nu: (te[i], j, 0))],
        out_specs=pl.BlockSpec((tm, d), lambda i, j, te, nu: (i, 0)),
        scratch_shapes=[pltpu.VMEM((tm, d), F32)],
    )
    return pl.pallas_call(
        _ffn_kernel,
        out_shape=jax.ShapeDtypeStruct((n, d), out_dtype),
        grid_spec=grid_spec,
        compiler_params=_cp(("arbitrary", "arbitrary")),
        name="grouped_swiglu",
    )(tile_expert, n_used, x_rows, w_gate, w_up, w_down)


def _res_ln_kernel(x_ref, f_ref, mod_ref, lng_ref, lnb_ref, o_ref, *, alpha):
    mod = mod_ref[0, 0]
    o_ref[0] = _layer_norm_rows(alpha * x_ref[0] + mod[5:6, :] * f_ref[0]) * lng_ref[...] + lnb_ref[...]


def _res_ln(xs, f, mod_sel, ln_g, ln_b, alpha, ctx_len):
    b, s, d = xs.shape
    tm = ctx_len
    tok = pl.BlockSpec((1, tm, d), lambda i, j: (i, j, 0))
    vec = pl.BlockSpec((1, d), lambda i, j: (0, 0))
    return pl.pallas_call(
        functools.partial(_res_ln_kernel, alpha=alpha),
        out_shape=jax.ShapeDtypeStruct((b, s, d), F32),
        grid=(b, s // tm),
        in_specs=[tok, tok, pl.BlockSpec((1, 1, 6, d), lambda i, j: (i, jnp.minimum(j, 1), 0, 0)), vec, vec],
        out_specs=tok,
        compiler_params=_cp(("parallel", "parallel")),
        name="residual_ln",
    )(xs, f, mod_sel, ln_g.reshape(1, d), ln_b.reshape(1, d))


def _router_kernel(x_ref, mod_ref, w_ref, o_ref):
    mod = mod_ref[0, 0]
    u = _layer_norm_rows(x_ref[0]) * (1.0 + mod[4:5, :]) + mod[3:4, :]
    logits = jnp.dot(u, w_ref[...], preferred_element_type=F32, precision=HIGHEST)
    lane = lax.broadcasted_iota(jnp.int32, logits.shape, 1)
    lg = jnp.where(lane < N_EXPERTS, logits, -jnp.inf)
    v1 = lg.max(axis=-1, keepdims=True)
    i1 = jnp.min(jnp.where(lg == v1, lane, LANES), axis=-1, keepdims=True)
    lg2 = jnp.where(lane == i1, -jnp.inf, lg)
    v2 = lg2.max(axis=-1, keepdims=True)
    i2 = jnp.min(jnp.where(lg2 == v2, lane, LANES), axis=-1, keepdims=True)
    e2 = jnp.exp(v2 - v1)
    w1 = 1.0 / (1.0 + e2)
    w2 = e2 / (1.0 + e2)
    out = jnp.where(lane == 0, i1.astype(F32), 0.0)
    out = jnp.where(lane == 1, i2.astype(F32), out)
    out = jnp.where(lane == 2, w1, out)
    out = jnp.where(lane == 3, w2, out)
    o_ref[0] = out


def _router(x_lat, mod_lat, w_router):
    b, l, d = x_lat.shape
    tm = 256
    w_pad = jnp.zeros((d, LANES), F32).at[:, :N_EXPERTS].set(w_router)
    return pl.pallas_call(
        _router_kernel,
        out_shape=jax.ShapeDtypeStruct((b, l, LANES), F32),
        grid=(b, l // tm),
        in_specs=[pl.BlockSpec((1, tm, d), lambda i, j: (i, j, 0)),
                  pl.BlockSpec((1, 1, 6, d), lambda i, j: (i, 0, 0, 0)),
                  pl.BlockSpec((d, LANES), lambda i, j: (0, 0))],
        out_specs=pl.BlockSpec((1, tm, LANES), lambda i, j: (i, j, 0)),
        compiler_params=_cp(("parallel", "parallel")),
        name="router_top2",
    )(x_lat, mod_lat, w_pad)


def _moe_combine_kernel(x_ref, y1_ref, y2_ref, w_ref, mod_ref, lng_ref, lnb_ref, o_ref, *, alpha):
    mod = mod_ref[0, 0]
    w = w_ref[0]
    f = w[:, 2:3] * y1_ref[0].astype(F32) + w[:, 3:4] * y2_ref[0].astype(F32)
    o_ref[0] = _layer_norm_rows(alpha * x_ref[0] + mod[5:6, :] * f) * lng_ref[...] + lnb_ref[...]


def _moe_ffn(x_lat, u_lat, mod_lat, lp_moe, ln_g, ln_b, alpha):
    b, l, d = x_lat.shape
    t = b * l
    tm = 512
    route = _router(x_lat, mod_lat, lp_moe['router'])
    ids = route[..., 0:TOP_K].astype(jnp.int32).reshape(t * TOP_K)
    onehot = (ids[:, None] == jnp.arange(N_EXPERTS)[None, :]).astype(jnp.int32)
    csum = jnp.cumsum(onehot, axis=0)
    rank = jnp.take_along_axis(csum, ids[:, None], axis=1)[:, 0] - 1
    counts = csum[-1]
    padded = ((counts + tm - 1) // tm) * tm
    ends = jnp.cumsum(padded)
    starts = ends - padded
    slot = starts[ids] + rank
    n_slots = t * TOP_K + N_EXPERTS * tm
    nt = n_slots // tm
    src = jnp.zeros((n_slots,), jnp.int32).at[slot].set(jnp.arange(t * TOP_K, dtype=jnp.int32) // TOP_K)
    tile_start = jnp.arange(nt, dtype=jnp.int32) * tm
    tile_expert = jnp.minimum(jnp.sum(tile_start[:, None] >= ends[None, :], axis=1), N_EXPERTS - 1).astype(jnp.int32)
    n_used = (ends[-1] // tm).astype(jnp.int32).reshape(1)
    last_e = tile_expert[jnp.maximum(n_used[0] - 1, 0)]
    tile_expert = jnp.where(jnp.arange(nt) < n_used[0], tile_expert, last_e)
    x_rows = jnp.take(u_lat.reshape(t, d), src, axis=0)
    y_rows = _grouped_ffn(x_rows, tile_expert, n_used, lp_moe['w_gate'], lp_moe['w_up'], lp_moe['w_down'],
                          tm, 512, BF16)
    y_tok = jnp.take(y_rows, slot, axis=0).reshape(b, l, TOP_K, d)
    tmc = 256
    tok = pl.BlockSpec((1, tmc, d), lambda i, j: (i, j, 0))
    vec = pl.BlockSpec((1, d), lambda i, j: (0, 0))
    return pl.pallas_call(
        functools.partial(_moe_combine_kernel, alpha=alpha),
        out_shape=jax.ShapeDtypeStruct((b, l, d), F32),
        grid=(b, l // tmc),
        in_specs=[tok, tok, tok, pl.BlockSpec((1, tmc, LANES), lambda i, j: (i, j, 0)),
                  pl.BlockSpec((1, 1, 6, d), lambda i, j: (i, 0, 0, 0)), vec, vec],
        out_specs=tok,
        compiler_params=_cp(("parallel", "parallel")),
        name="moe_combine_ln",
    )(x_lat, y_tok[:, :, 0], y_tok[:, :, 1], route, mod_lat, ln_g.reshape(1, d), ln_b.reshape(1, d))


def kernel(x, c, ctx, c_ctx, w_mod, b_mod, w_in, lru_conv_w, lru_conv_b, lru_w_r, lru_b_r, lru_w_i, lru_b_i, lru_lambda, hy_conv_w, hy_conv_b, hy_w1, hy_b1, hy_w2, hy_b2, hy_w3, hy_b3, hy_freq, hy_bias, na_rpb, s5_a_re, s5_a_im, s5_log_dt, s5_b_re, s5_b_im, s5_c_re, s5_c_im, s5_d, s5_w_glu, s5_b_glu, w_br_a, w_br_b, w_br_c, w_br_d, w_out, ln1_g, ln1_b, ln2_g, ln2_b, ff_w_gate, ff_w_up, ff_w_down, moe_router, moe_w_gate, moe_w_up, moe_w_down):
    bsz, l, d = x.shape
    ctx_len = ctx.shape[1]
    depth = w_in.shape[0]
    s = ctx_len + l
    alpha = (2.0 * depth) ** 0.25
    xs = jnp.concatenate([ctx, x], axis=1)
    c_rows = jnp.zeros((SUBLANES, d), F32).at[0:bsz].set(c).at[bsz].set(c_ctx)
    dft_lat = _dft_matrices(l)
    dft_ctx = _dft_matrices(ctx_len) if depth > 1 else None

    for li in range(depth):
        with_ctx = li < depth - 1
        lp = {
            'lru_conv_w': lru_conv_w[li], 'lru_conv_b': lru_conv_b[li], 'lru_w_r': lru_w_r[li],
            'lru_b_r': lru_b_r[li], 'lru_w_i': lru_w_i[li], 'lru_b_i': lru_b_i[li], 'lru_lambda': lru_lambda[li],
            'hy_conv_w': hy_conv_w[li], 'hy_conv_b': hy_conv_b[li], 'hy_w1': hy_w1[li], 'hy_b1': hy_b1[li],
            'hy_w2': hy_w2[li], 'hy_b2': hy_b2[li], 'hy_w3': hy_w3[li], 'hy_b3': hy_b3[li],
            'hy_freq': hy_freq[li], 'hy_bias': hy_bias[li],
            's5_a_re': s5_a_re[li], 's5_a_im': s5_a_im[li], 's5_log_dt': s5_log_dt[li], 's5_b_re': s5_b_re[li],
            's5_b_im': s5_b_im[li], 's5_c_re': s5_c_re[li], 's5_c_im': s5_c_im[li], 's5_d': s5_d[li],
            's5_w_glu': s5_w_glu[li], 's5_b_glu': s5_b_glu[li], 'w_br_a': w_br_a[li], 'w_br_b': w_br_b[li],
            'w_br_c': w_br_c[li], 'w_br_d': w_br_d[li], 'w_out': w_out[li],
        }
        mod = _mod_vectors(c_rows, w_mod[li], b_mod[li]).reshape(SUBLANES, 6, d)
        mod_sel = jnp.stack([jnp.broadcast_to(mod[bsz], (bsz, 6, d)), mod[0:bsz]], axis=1)

        u1 = _ln_mod(xs, mod_sel, 0, 1, ctx_len)
        u1f = u1.reshape(bsz * s, d)
        wi = w_in[li].astype(BF16)
        p_a = _matmul(u1f, wi[:, OFF_A:OFF_B], F32, name="proj_lru").reshape(bsz, s, OFF_B - OFF_A)
        p_c = _matmul(u1f, wi[:, OFF_C:OFF_D], F32, name="proj_natten").reshape(bsz, s, OFF_D - OFF_C)
        p_d = _matmul(u1f, wi[:, OFF_D:OFF_G], F32, name="proj_s5").reshape(bsz, s, OFF_G - OFF_D)
        p_g = _matmul(u1f, wi[:, OFF_G:], F32, name="proj_gates").reshape(bsz, s, N_BRANCH * d)
        p_bt = _matmul_nt(wi[:, OFF_B:OFF_C].T, u1, F32)

        ya = _lru_mixer(p_a, lp, ctx_len)
        yb_lat = _hyena_sequence(p_bt[:, :, ctx_len:], lp, dft_lat)
        if with_ctx:
            yb_ctx = _hyena_sequence(p_bt[:, :, :ctx_len], lp, dft_ctx)
        else:
            yb_ctx = jnp.zeros((bsz, HY_WIDTH, ctx_len), BF16)
        yc_l, yc_c = _natten_mixer(p_c, na_rpb[li], ctx_len, with_ctx)
        yc = yc_l if yc_c is None else lax.dynamic_update_slice(yc_l, yc_c, (0, 0, 0))
        yd = _s5_mixer(p_d, lp, ctx_len)
        xs, u2 = _merge(ya, yb_lat, yb_ctx, yc, yd, p_g, xs, mod_sel, lp, ln1_g[li], ln1_b[li], alpha,
                        ctx_len, with_ctx)

        if li % 2 == 0:
            e = li // 2
            n_rows = bsz * s
            tm = _pick_tile(n_rows, (1024, 512, 256))
            nt = n_rows // tm
            f = _grouped_ffn(u2.reshape(n_rows, d), jnp.zeros((nt,), jnp.int32), jnp.full((1,), nt, jnp.int32),
                             ff_w_gate[e:e + 1].astype(BF16), ff_w_up[e:e + 1].astype(BF16),
                             ff_w_down[e:e + 1].astype(BF16), tm, 256, F32)
            xs = _res_ln(xs, f.reshape(bsz, s, d), mod_sel, ln2_g[li], ln2_b[li], alpha, ctx_len)
        else:
            e = li // 2
            lp_moe = {'router': moe_router[e], 'w_gate': moe_w_gate[e].astype(BF16),
                      'w_up': moe_w_up[e].astype(BF16), 'w_down': moe_w_down[e].astype(BF16)}
            x_lat = _moe_ffn(xs[:, ctx_len:], u2[:, ctx_len:], mod_sel[:, 1:2], lp_moe, ln2_g[li], ln2_b[li], alpha)
            xs = jnp.concatenate([xs[:, :ctx_len], x_lat], axis=1)
    return xs[:, ctx_len:]
```

```python
import functools
import math

import numpy as np
import jax
import jax.numpy as jnp
from jax import lax
from jax.experimental import pallas as pl
from jax.experimental.pallas import tpu as pltpu

F32 = jnp.float32
BF16 = jnp.bfloat16
HIGHEST = lax.Precision.HIGHEST

LRU_WIDTH = 384
LRU_BLOCK = 64
LRU_CONV = 4
LRU_C = 8.0
HY_WIDTH = 256
HY_ORDER = 2
HY_SHORT = 3
HY_BANDS = 16
HY_FILT_HID = 64
HY_MAX_DECAY = math.log(1e-2) / 0.3
HY_MIN_DECAY = math.log(1e-2) / 1.5
NA_HEADS = 6
NA_HEAD_DIM = 64
NA_WIDTH = NA_HEADS * NA_HEAD_DIM
NA_WIN_R = 8
NA_WIN_C = 16
GRID_W = 64
S5_WIDTH = 256
S5_GROUP = 16
S5_GROUPS = 16
S5_STATE = 64
N_BRANCH = 4
OFF_A = 0
OFF_B = OFF_A + 2 * LRU_WIDTH
OFF_C = OFF_B + 3 * HY_WIDTH
OFF_D = OFF_C + 3 * NA_WIDTH
OFF_G = OFF_D + S5_WIDTH
N_EXPERTS = 8
TOP_K = 2
LN_EPS = 1e-5
MASK_VALUE = -1e30

LANES = 128
SUBLANES = 8
VMEM_LIMIT = 56 * 1024 * 1024


def _cp(sem, vmem=VMEM_LIMIT):
    return pltpu.CompilerParams(dimension_semantics=sem, vmem_limit_bytes=vmem)


def _gelu(x):
    return 0.5 * x * (1.0 + jnp.tanh(math.sqrt(2.0 / math.pi) * (x + 0.044715 * (x * x * x))))


def _sigmoid(x):
    return 1.0 / (1.0 + jnp.exp(-x))


def _layer_norm_rows(x):
    mu = jnp.mean(x, axis=-1, keepdims=True)
    xc = x - mu
    var = jnp.mean(xc * xc, axis=-1, keepdims=True)
    return xc * lax.rsqrt(var + LN_EPS)


def _mod_kernel(c_ref, w_ref, b_ref, o_ref):
    c = c_ref[...]
    a = c * _sigmoid(c)
    o_ref[...] = jnp.dot(a, w_ref[...], preferred_element_type=F32, precision=HIGHEST) + b_ref[...]


def _mod_vectors(c_rows, w_mod, b_mod):
    d = c_rows.shape[1]
    n = w_mod.shape[1]
    tn = 1536
    return pl.pallas_call(
        _mod_kernel,
        out_shape=jax.ShapeDtypeStruct((SUBLANES, n), F32),
        grid=(n // tn,),
        in_specs=[pl.BlockSpec((SUBLANES, d), lambda j: (0, 0)),
                  pl.BlockSpec((d, tn), lambda j: (0, j)),
                  pl.BlockSpec((1, tn), lambda j: (0, j))],
        out_specs=pl.BlockSpec((SUBLANES, tn), lambda j: (0, j)),
        compiler_params=_cp(("arbitrary",)),
        name="mod_vectors",
    )(c_rows, w_mod, b_mod.reshape(1, n))


def _ln_mod_kernel(x_ref, mod_ref, o_ref, *, shift_idx, scale_idx):
    y = _layer_norm_rows(x_ref[0])
    m = mod_ref[0, 0]
    o = y * (1.0 + m[scale_idx:scale_idx + 1, :]) + m[shift_idx:shift_idx + 1, :]
    o_ref[0] = o.astype(o_ref.dtype)


def _ln_mod(xs, mod_sel, shift_idx, scale_idx, ctx_len):
    b, s, d = xs.shape
    tm = ctx_len
    return pl.pallas_call(
        functools.partial(_ln_mod_kernel, shift_idx=shift_idx, scale_idx=scale_idx),
        out_shape=jax.ShapeDtypeStruct((b, s, d), BF16),
        grid=(b, s // tm),
        in_specs=[pl.BlockSpec((1, tm, d), lambda i, j: (i, j, 0)),
                  pl.BlockSpec((1, 1, 6, d), lambda i, j: (i, jnp.minimum(j, 1), 0, 0))],
        out_specs=pl.BlockSpec((1, tm, d), lambda i, j: (i, j, 0)),
        compiler_params=_cp(("parallel", "parallel")),
        name="ln_mod",
    )(xs, mod_sel)


def _mm_kernel(a_ref, w_ref, o_ref):
    o_ref[...] = jnp.dot(a_ref[...], w_ref[...], preferred_element_type=F32).astype(o_ref.dtype)


def _pick_tile(n, prefs):
    for t in prefs:
        if n % t == 0:
            return t
    return n


def _matmul(a, w, out_dtype, tm=None, tn=None, name="matmul"):
    m, k = a.shape
    n = w.shape[1]
    tm = tm or _pick_tile(m, (1024, 512, 256, 128))
    tn = tn or _pick_tile(n, (768, 512, 384, 256, 128))
    return pl.pallas_call(
        _mm_kernel,
        out_shape=jax.ShapeDtypeStruct((m, n), out_dtype),
        grid=(m // tm, n // tn),
        in_specs=[pl.BlockSpec((tm, k), lambda i, j: (i, 0)),
                  pl.BlockSpec((k, tn), lambda i, j: (0, j))],
        out_specs=pl.BlockSpec((tm, tn), lambda i, j: (i, j)),
        compiler_params=_cp(("parallel", "parallel")),
        name=name,
    )(a, w)


def _mm_nt_kernel(w_ref, u_ref, o_ref):
    o_ref[0] = lax.dot_general(w_ref[...], u_ref[0], (((1,), (1,)), ((), ())),
                               preferred_element_type=F32).astype(o_ref.dtype)


def _matmul_nt(w_t, u, out_dtype, tok0, ntok, tn=256):
    c, k = w_t.shape
    b = u.shape[0]
    j0 = tok0 // tn
    return pl.pallas_call(
        _mm_nt_kernel,
        out_shape=jax.ShapeDtypeStruct((b, c, ntok), out_dtype),
        grid=(b, ntok // tn),
        in_specs=[pl.BlockSpec((c, k), lambda i, j: (0, 0)),
                  pl.BlockSpec((1, tn, k), lambda i, j: (i, j + j0, 0))],
        out_specs=pl.BlockSpec((1, c, tn), lambda i, j: (i, 0, j)),
        compiler_params=_cp(("parallel", "parallel")),
        name="matmul_nt",
    )(w_t, u)


LRU_CHUNK = 128


def _tile_scan(a, b, row, reverse):
    for s in (1, 2, 4):
        if reverse:
            keep = row < SUBLANES - s
            shift = SUBLANES - s
        else:
            keep = row >= s
            shift = s
        a_sh = pltpu.roll(a, shift, 0)
        b_sh = pltpu.roll(b, shift, 0)
        b = jnp.where(keep, a * b_sh, 0.0) + b
        a = jnp.where(keep, a * a_sh, a)
    return a, b


def _lru_kernel(pg_ref, px_ref, cw_ref, cb_ref, wg_ref, bg_ref, lam_ref, y_ref,
                xpad, a_f, b_f, a_b, b_b, *, s_len, ctx_len):
    ch = LRU_CHUNK
    n_chunks = s_len // ch
    zeros8 = jnp.zeros((SUBLANES, LANES), F32)
    xpad[0:SUBLANES, :] = zeros8
    xpad[s_len + SUBLANES:s_len + 2 * SUBLANES, :] = zeros8

    def copy_body(i, carry):
        r = pl.multiple_of(i * ch, ch)
        xpad[pl.ds(r + SUBLANES, ch), :] = px_ref[0, pl.ds(r, ch), :]
        return carry

    lax.fori_loop(0, n_chunks, copy_body, 0)

    lam = lam_ref[...]
    sp = jnp.log(1.0 + jnp.exp(-lam))

    def gates_body(i, carry):
        r = pl.multiple_of(i * ch, ch)
        win = xpad[pl.ds(r, ch + 2 * SUBLANES), :]
        rows = r + lax.broadcasted_iota(jnp.int32, (ch, LANES), 0)
        is_lat = rows >= ctx_len
        xc = jnp.zeros((ch, LANES), F32) + cb_ref[...]
        for k in range(LRU_CONV):
            off = k - LRU_CONV // 2
            sh = win[SUBLANES + off:SUBLANES + off + ch, :]
            if off != 0:
                sh = jnp.where(((rows + off) >= ctx_len) == is_lat, sh, 0.0)
            xc = xc + sh * cw_ref[k:k + 1, :]
        gl = jnp.dot(xc.astype(BF16), wg_ref[0], preferred_element_type=F32) + bg_ref[0]
        for d, (a_s, b_s) in enumerate(((a_f, b_f), (a_b, b_b))):
            g_r = _sigmoid(gl[:, d * 2 * LANES:d * 2 * LANES + LANES])
            g_i = _sigmoid(gl[:, d * 2 * LANES + LANES:(d + 1) * 2 * LANES])
            log_a = (-LRU_C) * g_r * sp[d:d + 1, :]
            a = jnp.exp(log_a)
            bb = jnp.sqrt(1.0 - jnp.exp(2.0 * log_a)) * g_i * xc
            a_s[pl.ds(r, ch), :] = a
            b_s[pl.ds(r, ch), :] = bb
        return carry

    lax.fori_loop(0, n_chunks, gates_body, 0)

    row = lax.broadcasted_iota(jnp.int32, (SUBLANES, LANES), 0)

    def fwd_body(t, h):
        r = pl.multiple_of(t * SUBLANES, SUBLANES)
        a, b = _tile_scan(a_f[pl.ds(r, SUBLANES), :], b_f[pl.ds(r, SUBLANES), :], row, False)
        hh = b + a * h
        b_f[pl.ds(r, SUBLANES), :] = hh
        return jnp.broadcast_to(hh[SUBLANES - 1:SUBLANES, :], (SUBLANES, LANES))

    lax.fori_loop(0, s_len // SUBLANES, fwd_body, zeros8)

    def make_bwd(last_tile):
        def bwd_body(i, h):
            t = last_tile - i
            r = pl.multiple_of(t * SUBLANES, SUBLANES)
            a, b = _tile_scan(a_b[pl.ds(r, SUBLANES), :], b_b[pl.ds(r, SUBLANES), :], row, True)
            hh = b + a * h
            b_b[pl.ds(r, SUBLANES), :] = hh
            return jnp.broadcast_to(hh[0:1, :], (SUBLANES, LANES))
        return bwd_body

    n_ctx_tiles = ctx_len // SUBLANES
    n_tiles = s_len // SUBLANES
    h_ctx = lax.fori_loop(0, n_ctx_tiles, make_bwd(n_ctx_tiles - 1), zeros8)
    lax.fori_loop(0, n_tiles - n_ctx_tiles, make_bwd(n_tiles - 1), h_ctx)

    def out_body(i, carry):
        r = pl.multiple_of(i * ch, ch)
        g = _gelu(pg_ref[0, pl.ds(r, ch), :])
        y = g * (b_f[pl.ds(r, ch), :] + b_b[pl.ds(r, ch), :])
        y_ref[0, pl.ds(r, ch), :] = y.astype(y_ref.dtype)
        return carry

    lax.fori_loop(0, n_chunks, out_body, 0)


def _lru_gate_weights(w_r, w_i, b_r, b_i):
    n_grp = LRU_WIDTH // LANES
    per = LANES // LRU_BLOCK

    def bd(w):
        w = w.reshape(n_grp, per, LRU_BLOCK, LRU_BLOCK)
        z = jnp.zeros((n_grp, LRU_BLOCK, LRU_BLOCK), w.dtype)
        top = jnp.concatenate([w[:, 0], z], axis=2)
        bot = jnp.concatenate([z, w[:, 1]], axis=2)
        return jnp.concatenate([top, bot], axis=1)

    wg = jnp.concatenate([bd(w_r[0]), bd(w_i[0]), bd(w_r[1]), bd(w_i[1])], axis=2).astype(BF16)
    bg = jnp.stack([b_r[0], b_i[0], b_r[1], b_i[1]], axis=0).reshape(4, n_grp, LANES)
    bg = jnp.transpose(bg, (1, 0, 2)).reshape(n_grp, 1, 4 * LANES)
    return wg, bg


def _lru_mixer(p_a, lp, ctx_len):
    b, s, _ = p_a.shape
    n_grp = LRU_WIDTH // LANES
    wg, bg = _lru_gate_weights(lp['lru_w_r'], lp['lru_w_i'], lp['lru_b_r'], lp['lru_b_i'])
    scr = pltpu.VMEM((s, LANES), F32)
    return pl.pallas_call(
        functools.partial(_lru_kernel, s_len=s, ctx_len=ctx_len),
        out_shape=jax.ShapeDtypeStruct((b, s, LRU_WIDTH), BF16),
        grid=(b, n_grp),
        in_specs=[pl.BlockSpec((1, s, LANES), lambda i, g: (i, 0, g)),
                  pl.BlockSpec((1, s, LANES), lambda i, g: (i, 0, n_grp + g)),
                  pl.BlockSpec((LRU_CONV, LANES), lambda i, g: (0, g)),
                  pl.BlockSpec((1, LANES), lambda i, g: (0, g)),
                  pl.BlockSpec((1, LANES, 4 * LANES), lambda i, g: (g, 0, 0)),
                  pl.BlockSpec((1, 1, 4 * LANES), lambda i, g: (g, 0, 0)),
                  pl.BlockSpec((2, LANES), lambda i, g: (0, g))],
        out_specs=pl.BlockSpec((1, s, LANES), lambda i, g: (i, 0, g)),
        scratch_shapes=[pltpu.VMEM((s + 2 * SUBLANES, LANES), F32), scr, scr, scr, scr],
        compiler_params=_cp(("parallel", "parallel")),
        name="rglru",
    )(p_a, p_a, lp['lru_conv_w'], lp['lru_conv_b'].reshape(1, LRU_WIDTH), wg, bg, lp['lru_lambda'])


S5_R = 4
S5_NSTATE = S5_GROUPS * S5_STATE


def _s5_kernel(x_ref, winj_ref, wloc_ref, wro_ref, ap_ref, y_ref, g_ref, *, reverse, n_ctx_tiles):
    n = S5_NSTATE
    x = x_ref[0]
    g_ref[...] = jnp.dot(x, winj_ref[...], preferred_element_type=F32)
    n_tiles = g_ref.shape[0] // SUBLANES
    row = lax.broadcasted_iota(jnp.int32, (SUBLANES, n), 0)
    zeros = jnp.zeros((SUBLANES, n), F32)
    if reverse:
        shift1, e_in, e_out = SUBLANES - 1, SUBLANES - 1, 0
    else:
        shift1, e_in, e_out = 1, 0, SUBLANES - 1

    def make_body(first_tile):
        def body(i, carry):
            hr, hi = carry
            t = (first_tile - i) if reverse else (first_tile + i)
            r = pl.multiple_of(t * SUBLANES, SUBLANES)
            br = g_ref[pl.ds(r, SUBLANES), 0:n]
            bi = g_ref[pl.ds(r, SUBLANES), n:2 * n]
            for k, s in enumerate((1, 2, 4)):
                ar = ap_ref[SUBLANES + k:SUBLANES + k + 1, 0:n]
                ai = ap_ref[SUBLANES + k:SUBLANES + k + 1, n:2 * n]
                if reverse:
                    keep = row < SUBLANES - s
                    shift = SUBLANES - s
                else:
                    keep = row >= s
                    shift = s
                brs = pltpu.roll(br, shift, 0)
                bis = pltpu.roll(bi, shift, 0)
                nr = ar * brs - ai * bis
                ni = ar * bis + ai * brs
                br = br + jnp.where(keep, nr, 0.0)
                bi = bi + jnp.where(keep, ni, 0.0)
            cr = ap_ref[0:SUBLANES, 0:n]
            ci = ap_ref[0:SUBLANES, n:2 * n]
            out_r = br + (cr * hr - ci * hi)
            out_i = bi + (cr * hi + ci * hr)
            g_ref[pl.ds(r, SUBLANES), 0:n] = jnp.where(row == e_in, hr, pltpu.roll(out_r, shift1, 0))
            g_ref[pl.ds(r, SUBLANES), n:2 * n] = jnp.where(row == e_in, hi, pltpu.roll(out_i, shift1, 0))
            return (jnp.broadcast_to(out_r[e_out:e_out + 1, :], (SUBLANES, n)),
                    jnp.broadcast_to(out_i[e_out:e_out + 1, :], (SUBLANES, n)))
        return body

    if reverse:
        carry = lax.fori_loop(0, n_ctx_tiles, make_body(n_ctx_tiles - 1), (zeros, zeros))
        lax.fori_loop(0, n_tiles - n_ctx_tiles, make_body(n_tiles - 1), carry)
    else:
        lax.fori_loop(0, n_tiles, make_body(0), (zeros, zeros))
    y_ref[0] = (jnp.dot(x, wloc_ref[...], preferred_element_type=F32)
                + jnp.dot(g_ref[...].astype(BF16), wro_ref[...], preferred_element_type=F32))


def _s5_params(a_re, a_im, log_dt, b_re, b_im, c_re, c_im, reverse):
    rr = S5_R
    dt = jnp.exp(log_dt)[:, None]
    den = a_re * a_re + a_im * a_im
    mag = jnp.exp(dt * a_re)
    ab_re = mag * jnp.cos(dt * a_im)
    ab_im = mag * jnp.sin(dt * a_im)
    f_re = ((ab_re - 1.0) * a_re + ab_im * a_im) / den
    f_im = (ab_im * a_re - (ab_re - 1.0) * a_im) / den
    bb_re = f_re[..., None] * b_re - f_im[..., None] * b_im
    bb_im = f_re[..., None] * b_im + f_im[..., None] * b_re
    eye = jnp.eye(S5_GROUPS, dtype=F32)

    def apow(k):
        k = k.astype(F32)[:, None, None]
        m = jnp.exp(k * dt[None] * a_re[None])
        return m * jnp.cos(k * dt[None] * a_im[None]), m * jnp.sin(k * dt[None] * a_im[None])

    steps = jnp.arange(rr)
    rows = rr * S5_WIDTH
    er, ei = apow(steps if reverse else (rr - 1 - steps))
    inj_re = er[..., None] * bb_re[None] - ei[..., None] * bb_im[None]
    inj_im = er[..., None] * bb_im[None] + ei[..., None] * bb_re[None]
    winj = jnp.concatenate([jnp.einsum('igpc,gh->igchp', inj_re, eye).reshape(rows, S5_NSTATE),
                            jnp.einsum('igpc,gh->igchp', inj_im, eye).reshape(rows, S5_NSTATE)], axis=1)
    fr, fi = apow((rr - steps) if reverse else (steps + 1))
    ro_re = c_re[None] * fr[:, :, None, :] - c_im[None] * fi[:, :, None, :]
    ro_im = c_re[None] * fi[:, :, None, :] + c_im[None] * fr[:, :, None, :]
    wro = jnp.concatenate([jnp.einsum('igop,gh->gpiho', ro_re, eye).reshape(S5_NSTATE, rows),
                           -jnp.einsum('igop,gh->gpiho', ro_im, eye).reshape(S5_NSTATE, rows)], axis=0)
    kr, ki = apow(steps)
    ab_r = kr[..., None] * bb_re[None] - ki[..., None] * bb_im[None]
    ab_i = kr[..., None] * bb_im[None] + ki[..., None] * bb_re[None]
    kk = jnp.einsum('gop,kgpc->kgoc', c_re, ab_r) - jnp.einsum('gop,kgpc->kgoc', c_im, ab_i)
    src = jnp.arange(rr)[:, None]
    tgt = jnp.arange(rr)[None, :]
    lag = (src - tgt) if reverse else (tgt - src)
    kmat = jnp.where((lag >= 0)[:, :, None, None, None], kk[jnp.clip(lag, 0, rr - 1)], 0.0)
    wloc = jnp.einsum('sigoc,gh->sgciho', kmat, eye).reshape(rows, rows)
    i8 = jnp.arange(SUBLANES)
    dist = (SUBLANES - i8) if reverse else (i8 + 1)
    ks = jnp.concatenate([dist, jnp.array([1, 2, 4]), jnp.zeros((5,), dist.dtype)]) * rr
    pr, pi = apow(ks)
    ap = jnp.concatenate([pr.reshape(16, S5_NSTATE), pi.reshape(16, S5_NSTATE)], axis=1)
    return winj.astype(BF16), wloc.astype(BF16), wro.astype(BF16), ap


def _s5_scan(x_ss, lp, d, ctx_len):
    b, nr, w = x_ss.shape
    reverse = d == 1
    n_ctx_tiles = ctx_len // (S5_R * SUBLANES)
    winj, wloc, wro, ap = _s5_params(lp['s5_a_re'][d], lp['s5_a_im'][d], lp['s5_log_dt'][d], lp['s5_b_re'][d],
                                     lp['s5_b_im'][d], lp['s5_c_re'][d], lp['s5_c_im'][d], reverse)
    full = lambda shape: pl.BlockSpec(shape, lambda i: (0, 0))
    return pl.pallas_call(
        functools.partial(_s5_kernel, reverse=reverse, n_ctx_tiles=n_ctx_tiles),
        out_shape=jax.ShapeDtypeStruct((b, nr, w), F32),
        grid=(b,),
        in_specs=[pl.BlockSpec((1, nr, w), lambda i: (i, 0, 0)),
                  full((w, 2 * S5_NSTATE)), full((w, w)), full((2 * S5_NSTATE, w)), full((16, 2 * S5_NSTATE))],
        out_specs=pl.BlockSpec((1, nr, w), lambda i: (i, 0, 0)),
        scratch_shapes=[pltpu.VMEM((nr, 2 * S5_NSTATE), F32)],
        compiler_params=_cp(("parallel",)),
        name="s5_scan_bwd" if reverse else "s5_scan_fwd",
    )(x_ss, winj, wloc, wro, ap)


def _s5_out_kernel(yf_ref, yb_ref, u_ref, d_ref, w_ref, b_ref, o_ref):
    y = yf_ref[...] + yb_ref[...] + d_ref[...] * u_ref[...]
    g = _gelu(y)
    z = jnp.dot(g.astype(BF16), w_ref[...], preferred_element_type=F32) + b_ref[...]
    o_ref[...] = (g * _sigmoid(z)).astype(o_ref.dtype)


def _s5_mixer(p_d, lp, ctx_len):
    b, s, w = p_d.shape
    x_ss = p_d.astype(BF16).reshape(b, s // S5_R, S5_R * w)
    yf = _s5_scan(x_ss, lp, 0, ctx_len)
    yb = _s5_scan(x_ss, lp, 1, ctx_len)
    m = b * s
    tm = _pick_tile(m, (1024, 512, 256))
    row = pl.BlockSpec((tm, w), lambda i: (i, 0))
    vec = pl.BlockSpec((1, w), lambda i: (0, 0))
    out = pl.pallas_call(
        _s5_out_kernel,
        out_shape=jax.ShapeDtypeStruct((m, w), BF16),
        grid=(m // tm,),
        in_specs=[row, row, row, vec, pl.BlockSpec((w, w), lambda i: (0, 0)), vec],
        out_specs=row,
        compiler_params=_cp(("parallel",)),
        name="s5_out",
    )(yf.reshape(m, w), yb.reshape(m, w), p_d.reshape(m, w), lp['s5_d'].reshape(1, w),
      lp['s5_w_glu'].astype(BF16), lp['s5_b_glu'].reshape(1, w))
    return out.reshape(b, s, w)


NA_QROWS = 4


def _natten_plan(rows):
    kr = min(NA_WIN_R, rows)
    span = kr + NA_QROWS - 1
    variants, index, blk_var = [], {}, []
    for blk in range(rows // NA_QROWS):
        r0 = blk * NA_QROWS
        ws = int(np.clip(r0 - kr // 2, 0, rows - span))
        dr = np.zeros((NA_QROWS, span), np.int32)
        ok = np.zeros((NA_QROWS, span), bool)
        for q in range(NA_QROWS):
            r = r0 + q
            rs = int(np.clip(r - kr // 2, 0, rows - kr))
            for i in range(span):
                ok[q, i] = rs <= ws + i < rs + kr
                dr[q, i] = (ws + i - r + (NA_WIN_R - 1)) if ok[q, i] else 0
        key = dr.tobytes() + ok.tobytes()
        if key not in index:
            index[key] = len(variants)
            variants.append((dr, ok))
        blk_var.append(index[key])
    return (np.stack([v[0] for v in variants]), np.stack([v[1] for v in variants]),
            np.asarray(blk_var, np.int32))


def _natten_bias(rpb, dr, ok):
    w = np.arange(GRID_W)
    cs = np.clip(w - NA_WIN_C // 2, 0, GRID_W - NA_WIN_C)
    ok_col = (w[None, :] >= cs[:, None]) & (w[None, :] < cs[:, None] + NA_WIN_C)
    dc = np.clip(w[None, :] - w[:, None] + (NA_WIN_C - 1), 0, 2 * NA_WIN_C - 2)
    g = rpb[:, dr[:, :, None, :, None], dc[None, None, :, None, :]]
    ok_all = ok[:, :, None, :, None] & ok_col[None, None, :, None, :]
    g = jnp.where(jnp.asarray(ok_all)[None], g, MASK_VALUE)
    g = jnp.transpose(g, (1, 0, 2, 3, 4, 5))
    nv, nq, span = dr.shape
    return g.reshape(nv, NA_HEADS, nq * GRID_W, span * GRID_W).astype(F32)


def _attend(q2, keys, vals, biases, lane):
    out = jnp.zeros(q2.shape, F32)
    for hh in range(2):
        sel = (lane >= hh * NA_HEAD_DIM) & (lane < (hh + 1) * NA_HEAD_DIM)
        qm = jnp.where(sel, q2, jnp.zeros_like(q2))
        ss = []
        for k_i, b_i in zip(keys, biases[hh]):
            s_i = lax.dot_general(qm, k_i, (((1,), (1,)), ((), ())), preferred_element_type=F32)
            if b_i is not None:
                s_i = s_i + b_i
            ss.append(s_i)
        m = ss[0].max(axis=-1, keepdims=True)
        for s_i in ss[1:]:
            m = jnp.maximum(m, s_i.max(axis=-1, keepdims=True))
        ps = [jnp.exp(s_i - m) for s_i in ss]
        den = ps[0].sum(axis=-1, keepdims=True)
        for p_i in ps[1:]:
            den = den + p_i.sum(axis=-1, keepdims=True)
        o = jnp.dot(ps[0].astype(BF16), vals[0], preferred_element_type=F32)
        for p_i, v_i in zip(ps[1:], vals[1:]):
            o = o + jnp.dot(p_i.astype(BF16), v_i, preferred_element_type=F32)
        o = o / den
        out = jnp.where(sel[:, :], o, out)
    return out


def _natten_kernel(var_ref, q_ref, k_ref, v_ref, bias_ref, o_ref, *, rows, ctx_len):
    kr = min(NA_WIN_R, rows)
    span = kr + NA_QROWS - 1
    r0 = pl.program_id(1) * NA_QROWS
    ws = jnp.clip(r0 - kr // 2, 0, rows - span)
    base = pl.multiple_of(ctx_len + ws * GRID_W, GRID_W)
    lane = lax.broadcasted_iota(jnp.int32, (NA_QROWS * GRID_W, LANES), 1)
    scale = NA_HEAD_DIM ** -0.5
    for hp in range(NA_HEADS // 2):
        ls = slice(hp * LANES, (hp + 1) * LANES)
        q2 = q_ref[0, :, ls] * scale
        kw = k_ref[0, pl.ds(base, span * GRID_W), ls]
        vw = v_ref[0, pl.ds(base, span * GRID_W), ls]
        kc = k_ref[0, 0:ctx_len, ls]
        vc = v_ref[0, 0:ctx_len, ls]
        biases = [[bias_ref[0, 2 * hp + hh], None] for hh in range(2)]
        out = _attend(q2, [kw, kc], [vw, vc], biases, lane)
        o_ref[0, :, ls] = out.astype(o_ref.dtype)


def _ctx_attn_kernel(q_ref, k_ref, v_ref, o_ref, *, ctx_len):
    lane = lax.broadcasted_iota(jnp.int32, (ctx_len, LANES), 1)
    scale = NA_HEAD_DIM ** -0.5
    for hp in range(NA_HEADS // 2):
        ls = slice(hp * LANES, (hp + 1) * LANES)
        out = _attend(q_ref[0, :, ls] * scale, [k_ref[0, :, ls]], [v_ref[0, :, ls]], [[None], [None]], lane)
        o_ref[0, :, ls] = out.astype(o_ref.dtype)


def _natten_mixer(p_c, rpb, ctx_len, with_ctx):
    b, s, _ = p_c.shape
    l = s - ctx_len
    rows = l // GRID_W
    kr = min(NA_WIN_R, rows)
    span = kr + NA_QROWS - 1
    nq = NA_QROWS * GRID_W
    dr, ok, blk_var = _natten_plan(rows)
    bias = _natten_bias(rpb, dr, ok)
    cb = ctx_len // nq
    grid_spec = pltpu.PrefetchScalarGridSpec(
        num_scalar_prefetch=1,
        grid=(b, rows // NA_QROWS),
        in_specs=[pl.BlockSpec((1, nq, NA_WIDTH), lambda i, r, var: (i, cb + r, 0)),
                  pl.BlockSpec((1, s, NA_WIDTH), lambda i, r, var: (i, 0, 1)),
                  pl.BlockSpec((1, s, NA_WIDTH), lambda i, r, var: (i, 0, 2)),
                  pl.BlockSpec((1, NA_HEADS, nq, span * GRID_W), lambda i, r, var: (var[r], 0, 0, 0))],
        out_specs=pl.BlockSpec((1, nq, NA_WIDTH), lambda i, r, var: (i, r, 0)),
    )
    y_l = pl.pallas_call(
        functools.partial(_natten_kernel, rows=rows, ctx_len=ctx_len),
        out_shape=jax.ShapeDtypeStruct((b, l, NA_WIDTH), BF16),
        grid_spec=grid_spec,
        compiler_params=_cp(("parallel", "arbitrary")),
        name="natten",
    )(jnp.asarray(blk_var), p_c, p_c, p_c, bias)
    if not with_ctx:
        return y_l, None
    y_c = pl.pallas_call(
        functools.partial(_ctx_attn_kernel, ctx_len=ctx_len),
        out_shape=jax.ShapeDtypeStruct((b, ctx_len, NA_WIDTH), BF16),
        grid=(b,),
        in_specs=[pl.BlockSpec((1, ctx_len, NA_WIDTH), lambda i: (i, 0, 0)),
                  pl.BlockSpec((1, ctx_len, NA_WIDTH), lambda i: (i, 0, 1)),
                  pl.BlockSpec((1, ctx_len, NA_WIDTH), lambda i: (i, 0, 2))],
        out_specs=pl.BlockSpec((1, ctx_len, NA_WIDTH), lambda i: (i, 0, 0)),
        compiler_params=_cp(("parallel",)),
        name="ctx_attn",
    )(p_c, p_c, p_c)
    return y_l, y_c


DFT_ROWS = 64


def _dft_gen_kernel(ca_ref, sa_ref, cb_ref, sb_ref, fwd_ref, inv_ref, *, l):
    i = pl.program_id(0)
    ca = ca_ref[0]
    sa = sa_ref[0]
    cb = cb_ref[...]
    sb = sb_ref[...]
    gc = ca * cb - sa * sb
    gs = sa * cb + ca * sb
    x = i * DFT_ROWS + lax.broadcasted_iota(jnp.int32, (DFT_ROWS, l), 0)
    y = lax.broadcasted_iota(jnp.int32, (DFT_ROWS, l), 1)
    n = 2.0 * l
    nyq_x = jnp.where((x & 1) == 0, 1.0, -1.0)
    fwd_ref[:, 0:l] = gc.astype(fwd_ref.dtype)
    fwd_ref[:, l:2 * l] = jnp.where(y == 0, nyq_x, -gs).astype(fwd_ref.dtype)
    scale = jnp.where(x == 0, 1.0 / n, 2.0 / n)
    nyq_y = jnp.where((y & 1) == 0, 1.0, -1.0)
    inv_ref[0] = (scale * gc).astype(inv_ref.dtype)
    inv_ref[1] = (scale * jnp.where(x == 0, nyq_y, -gs)).astype(inv_ref.dtype)


def _dft_matrices(l):
    n = 2 * l
    k1 = l // DFT_ROWS
    y = np.arange(l, dtype=np.int64)
    xa = (DFT_ROWS * np.arange(k1, dtype=np.int64))[:, None]
    xb = np.arange(DFT_ROWS, dtype=np.int64)[:, None]
    ang_a = jnp.asarray(((xa * y[None, :]) % n).astype(np.float32)) * F32(2.0 * math.pi / n)
    ang_b = jnp.asarray(((xb * y[None, :]) % n).astype(np.float32)) * F32(2.0 * math.pi / n)
    ca, sa = jnp.cos(ang_a).reshape(k1, 1, l), jnp.sin(ang_a).reshape(k1, 1, l)
    cb, sb = jnp.cos(ang_b), jnp.sin(ang_b)
    row = pl.BlockSpec((1, 1, l), lambda i: (i, 0, 0))
    tab = pl.BlockSpec((DFT_ROWS, l), lambda i: (0, 0))
    fwd, inv = pl.pallas_call(
        functools.partial(_dft_gen_kernel, l=l),
        out_shape=(jax.ShapeDtypeStruct((l, 2 * l), BF16), jax.ShapeDtypeStruct((2, l, l), BF16)),
        grid=(k1,),
        in_specs=[row, row, tab, tab],
        out_specs=(pl.BlockSpec((DFT_ROWS, 2 * l), lambda i: (i, 0)),
                   pl.BlockSpec((2, DFT_ROWS, l), lambda i: (0, i, 0))),
        compiler_params=_cp(("parallel",)),
        name="dft_gen",
    )(ca, sa, cb, sb)
    return fwd, inv.reshape(2 * l, l)


def _hy_filter_kernel(w1t_ref, w1c_ref, w1s_ref, b1_ref, w2_ref, b2_ref, w3_ref, b3_ref, fr_ref, dl_ref,
                      h_ref, asum_ref, *, l, tl):
    j = pl.program_id(0)
    t = (j * tl + lax.broadcasted_iota(jnp.int32, (1, tl), 1)).astype(F32)
    t_norm = t / l
    bands = (1 + lax.broadcasted_iota(jnp.int32, (HY_BANDS, 1), 0)).astype(F32)
    ang = (2.0 * math.pi / l) * t * bands
    fr = fr_ref[...]
    lin = (w1t_ref[...] * t_norm
           + jnp.dot(w1c_ref[...], jnp.cos(ang), preferred_element_type=F32, precision=HIGHEST)
           + jnp.dot(w1s_ref[...], jnp.sin(ang), preferred_element_type=F32, precision=HIGHEST))
    h = jnp.sin(fr * (lin + b1_ref[...]))
    h = jnp.sin(fr * (jnp.dot(w2_ref[...], h, preferred_element_type=F32, precision=HIGHEST) + b2_ref[...]))
    h = jnp.dot(w3_ref[...], h, preferred_element_type=F32, precision=HIGHEST) + b3_ref[...]
    window = jnp.exp(-t_norm * dl_ref[...])
    first = (j * tl + lax.broadcasted_iota(jnp.int32, (HY_WIDTH, tl), 1)) == 0

    @pl.when(j == 0)
    def _():
        asum_ref[...] = jnp.zeros_like(asum_ref)

    for blk in range(2 * HY_ORDER):
        rs = slice(blk * HY_WIDTH, (blk + 1) * HY_WIDTH)
        hb = h[rs, :] * window
        if blk >= HY_ORDER:
            hb = jnp.where(first, 0.0, hb)
        h_ref[rs, :] = hb
        asum_ref[rs, :] += jnp.sum(jnp.abs(hb), axis=1, keepdims=True)


def _hy_filters(lp, l):
    tl = min(l, 512)
    hid = HY_FILT_HID
    w1 = lp['hy_w1']
    col = lambda v: v.reshape(-1, 1)
    deltas = np.abs(np.linspace(HY_MIN_DECAY, HY_MAX_DECAY, HY_WIDTH, dtype=np.float32)).reshape(-1, 1)
    full = lambda shape: pl.BlockSpec(shape, lambda j: (0, 0))
    n_out = 2 * HY_ORDER * HY_WIDTH
    return pl.pallas_call(
        functools.partial(_hy_filter_kernel, l=l, tl=tl),
        out_shape=(jax.ShapeDtypeStruct((n_out, l), F32), jax.ShapeDtypeStruct((n_out, 1), F32)),
        grid=(l // tl,),
        in_specs=[full((hid, 1)), full((hid, HY_BANDS)), full((hid, HY_BANDS)), full((hid, 1)),
                  full((hid, hid)), full((hid, 1)), full((n_out, hid)), full((n_out, 1)),
                  full((hid, 1)), full((HY_WIDTH, 1))],
        out_specs=(pl.BlockSpec((n_out, tl), lambda j: (0, j)), pl.BlockSpec((n_out, 1), lambda j: (0, 0))),
        compiler_params=_cp(("arbitrary",)),
        name="hyena_filters",
    )(w1[0:1].T, w1[1:1 + HY_BANDS].T, w1[1 + HY_BANDS:].T, col(lp['hy_b1']), lp['hy_w2'].T, col(lp['hy_b2']),
      lp['hy_w3'].T, col(lp['hy_b3']), col(lp['hy_freq']), jnp.asarray(deltas))


def _hy_short_conv_kernel(x_ref, w_ref, b_ref, z_ref, zbf_ref, *, l):
    x = x_ref[0]
    t = lax.broadcasted_iota(jnp.int32, x.shape, 1)
    left = HY_SHORT // 2
    z = jnp.zeros(x.shape, F32) + b_ref[...]
    for k in range(HY_SHORT):
        off = k - left
        if off == 0:
            sh = x
        else:
            sh = pltpu.roll(x, (-off) % l, 1)
            sh = jnp.where((t + off >= 0) & (t + off < l), sh, 0.0)
        z = z + sh * w_ref[:, k:k + 1]
    z_ref[0] = z
    zbf_ref[0] = z.astype(zbf_ref.dtype)


def _hy_conv_kernel(y_ref, fc_ref, fs_ref, ic_ref, is_ref, kfc_ref, kfs_ref, kbc_ref, kbs_ref, n_ref,
                    o_ref, *, nb):
    j = pl.program_id(1)

    @pl.when(j == 0)
    def _():
        o_ref[...] = jnp.zeros_like(o_ref)

    y = y_ref[...]
    zr = jnp.dot(y, fc_ref[...], preferred_element_type=F32)
    zi = jnp.dot(y, fs_ref[...], preferred_element_type=F32)
    inv_n = 1.0 / (n_ref[...] + 1e-6)
    kr = (kfc_ref[...] + kbc_ref[...]) * inv_n
    ki = (kfs_ref[...] - kbs_ref[...]) * inv_n
    tn = kr.shape[1]
    f0 = (j * tn + lax.broadcasted_iota(jnp.int32, kr.shape, 1)) == 0
    ki = jnp.where(f0, (kfs_ref[...] + kbs_ref[...]) * inv_n, ki)
    prs, pis = [], []
    for bb in range(nb):
        rs = slice(bb * HY_WIDTH, (bb + 1) * HY_WIDTH)
        a, b = zr[rs], zi[rs]
        prs.append(a * kr - jnp.where(f0, 0.0, b * ki))
        pis.append(jnp.where(f0, b * ki, a * ki + b * kr))
    pr = jnp.concatenate(prs, axis=0).astype(BF16)
    pi = jnp.concatenate(pis, axis=0).astype(BF16)
    o_ref[...] += (jnp.dot(pr, ic_ref[...], preferred_element_type=F32)
                   + jnp.dot(pi, is_ref[...], preferred_element_type=F32))


def _hy_long_conv(ybf, n_tiles, nb, row_stride, fwd, inv, kf, asum, order, l):
    tmh = nb * HY_WIDTH
    m = n_tiles * tmh
    tn = min(l, 256)
    jn = l // tn
    o_f = order
    o_b = HY_ORDER + order
    return pl.pallas_call(
        functools.partial(_hy_conv_kernel, nb=nb),
        out_shape=jax.ShapeDtypeStruct((m, l), F32),
        grid=(n_tiles, jn),
        in_specs=[pl.BlockSpec((tmh, l), lambda i, j: (i * row_stride, 0)),
                  pl.BlockSpec((l, tn), lambda i, j: (0, j)),
                  pl.BlockSpec((l, tn), lambda i, j: (0, jn + j)),
                  pl.BlockSpec((tn, l), lambda i, j: (j, 0)),
                  pl.BlockSpec((tn, l), lambda i, j: (jn + j, 0)),
                  pl.BlockSpec((HY_WIDTH, tn), lambda i, j: (o_f, j)),
                  pl.BlockSpec((HY_WIDTH, tn), lambda i, j: (o_f, jn + j)),
                  pl.BlockSpec((HY_WIDTH, tn), lambda i, j: (o_b, j)),
                  pl.BlockSpec((HY_WIDTH, tn), lambda i, j: (o_b, jn + j)),
                  pl.BlockSpec((HY_WIDTH, 1), lambda i, j: (order, 0))],
        out_specs=pl.BlockSpec((tmh, l), lambda i, j: (i, 0)),
        compiler_params=_cp(("parallel", "arbitrary")),
        name="hyena_long_conv",
    )(ybf, fwd, fwd, inv, inv, kf, kf, kf, kf, asum)


def _hy_gate_kernel(g_ref, c_ref, y_ref, bias_ref, o_ref, obf_ref):
    o = g_ref[0] * (c_ref[0] + y_ref[0] * bias_ref[...])
    o_ref[0] = o
    obf_ref[0] = o.astype(obf_ref.dtype)


def _hy_gate(z, conv, y, bias_col, gate_blk, y_blk):
    b, _, l = z.shape
    tl = min(l, 1024)
    spec = lambda blk: pl.BlockSpec((1, HY_WIDTH, tl), lambda i, j: (i, blk, j))
    return pl.pallas_call(
        _hy_gate_kernel,
        out_shape=(jax.ShapeDtypeStruct((b, HY_WIDTH, l), F32), jax.ShapeDtypeStruct((b, HY_WIDTH, l), BF16)),
        grid=(b, l // tl),
        in_specs=[spec(gate_blk), spec(0), spec(y_blk), pl.BlockSpec((HY_WIDTH, 1), lambda i, j: (0, 0))],
        out_specs=(spec(0), spec(0)),
        compiler_params=_cp(("parallel", "parallel")),
        name="hyena_gate",
    )(z, conv, y, bias_col)


def _hyena_sequence(p_bt, lp, dft):
    b, c3, l = p_bt.shape
    fwd, inv = dft
    h, asum = _hy_filters(lp, l)
    kf = _matmul(h.astype(BF16), fwd, F32, name="hyena_filter_dft")
    asum2 = asum.reshape(2, HY_ORDER * HY_WIDTH).sum(axis=0).reshape(HY_ORDER * HY_WIDTH, 1)
    blk = pl.BlockSpec((1, LANES, l), lambda i, g: (i, g, 0))
    z, z_bf = pl.pallas_call(
        functools.partial(_hy_short_conv_kernel, l=l),
        out_shape=(jax.ShapeDtypeStruct((b, c3, l), F32), jax.ShapeDtypeStruct((b, c3, l), BF16)),
        grid=(b, c3 // LANES),
        in_specs=[blk,
                  pl.BlockSpec((LANES, HY_SHORT), lambda i, g: (g, 0)),
                  pl.BlockSpec((LANES, 1), lambda i, g: (g, 0))],
        out_specs=(blk, blk),
        compiler_params=_cp(("parallel", "parallel")),
        name="hyena_short_conv",
    )(p_bt, lp['hy_conv_w'].T, lp['hy_conv_b'].reshape(c3, 1))
    n_blk = c3 // HY_WIDTH
    nb2 = 2 if b % 2 == 0 else 1
    conv1 = _hy_long_conv(z_bf.reshape(b * c3, l), b, 1, n_blk, fwd, inv, kf, asum2, 0, l).reshape(b, HY_WIDTH, l)
    y1, y1_bf = _hy_gate(z, conv1, z, lp['hy_bias'][0].reshape(HY_WIDTH, 1), 1, 0)
    conv2 = _hy_long_conv(y1_bf.reshape(b * HY_WIDTH, l), b // nb2, nb2, 1, fwd, inv, kf, asum2, 1, l)
    _, y2_bf = _hy_gate(z, conv2.reshape(b, HY_WIDTH, l), y1, lp['hy_bias'][1].reshape(HY_WIDTH, 1), 2, 0)
    return y2_bf


def _merge_kernel(ya_ref, ybl_ref, ybc_ref, ycl_ref, ycc_ref, yd_ref, pg_ref, x_ref, mod_ref, wa_ref, wb_ref,
                  wc_ref, wd_ref, wo_ref, lng_ref, lnb_ref, x_out_ref, u_out_ref, *, alpha, first_tile):
    j = pl.program_id(1) + first_tile
    d = x_ref.shape[-1]
    yb_t = jnp.where(j == 0, ybc_ref[0], ybl_ref[0])
    yb = yb_t.astype(F32).T.astype(BF16)
    yc = jnp.where(j == 0, ycc_ref[0], ycl_ref[0])
    projs = [jnp.dot(ya_ref[0], wa_ref[...], preferred_element_type=F32),
             jnp.dot(yb, wb_ref[...], preferred_element_type=F32),
             jnp.dot(yc, wc_ref[...], preferred_element_type=F32),
             jnp.dot(yd_ref[0], wd_ref[...], preferred_element_type=F32)]
    merged = None
    for i, pr in enumerate(projs):
        term = _sigmoid(pg_ref[0, :, i * d:(i + 1) * d].astype(F32)) * pr
        merged = term if merged is None else merged + term
    m = jnp.dot(merged.astype(BF16), wo_ref[...], preferred_element_type=F32)
    mod = mod_ref[0, 0]
    xn = _layer_norm_rows(alpha * x_ref[0] + mod[2:3, :] * m) * lng_ref[...] + lnb_ref[...]
    x_out_ref[0] = xn
    u = _layer_norm_rows(xn) * (1.0 + mod[4:5, :]) + mod[3:4, :]
    u_out_ref[0] = u.astype(u_out_ref.dtype)


def _merge(ya, yb_lat, yb_ctx, yc_lat, yc_ctx, yd, pg, xs, mod_sel, lp, ln_g, ln_b, alpha, ctx_len, with_ctx,
           u_dtype):
    b, s, d = xs.shape
    tm = ctx_len
    first = 0 if with_ctx else 1
    nt = s // tm - first
    tok = lambda w: pl.BlockSpec((1, tm, w), lambda i, j: (i, j + first, 0))
    lat = lambda j: jnp.maximum(j + first - 1, 0)
    out = pl.BlockSpec((1, tm, d), lambda i, j: (i, j, 0))
    full = lambda shape: pl.BlockSpec(shape, lambda i, j: (0,) * len(shape))
    x_new, u2 = pl.pallas_call(
        functools.partial(_merge_kernel, alpha=alpha, first_tile=first),
        out_shape=(jax.ShapeDtypeStruct((b, nt * tm, d), F32), jax.ShapeDtypeStruct((b, nt * tm, d), u_dtype)),
        grid=(b, nt),
        in_specs=[tok(LRU_WIDTH),
                  pl.BlockSpec((1, HY_WIDTH, tm), lambda i, j: (i, 0, lat(j))),
                  pl.BlockSpec((1, HY_WIDTH, tm), lambda i, j: (i, 0, 0)),
                  pl.BlockSpec((1, tm, NA_WIDTH), lambda i, j: (i, lat(j), 0)),
                  pl.BlockSpec((1, tm, NA_WIDTH), lambda i, j: (i, 0, 0)),
                  tok(S5_WIDTH), tok(N_BRANCH * d), tok(d),
                  pl.BlockSpec((1, 1, 6, d), lambda i, j: (i, jnp.minimum(j + first, 1), 0, 0)),
                  full((LRU_WIDTH, d)), full((HY_WIDTH, d)), full((NA_WIDTH, d)), full((S5_WIDTH, d)),
                  full((d, d)), full((1, d)), full((1, d))],
        out_specs=(out, out),
        compiler_params=_cp(("parallel", "parallel")),
        name="merge",
    )(ya, yb_lat, yb_ctx, yc_lat, yc_ctx, yd, pg, xs, mod_sel,
      lp['w_br_a'].astype(BF16), lp['w_br_b'].astype(BF16), lp['w_br_c'].astype(BF16),
      lp['w_br_d'].astype(BF16), lp['w_out'].astype(BF16), ln_g.reshape(1, d), ln_b.reshape(1, d))
    return x_new, u2


def _ffn_kernel(te_ref, nu_ref, x_ref, wg_ref, wu_ref, wd_ref, o_ref, acc_ref):
    i = pl.program_id(0)
    j = pl.program_id(1)

    @pl.when(j == 0)
    def _():
        acc_ref[...] = jnp.zeros_like(acc_ref)

    @pl.when(i < nu_ref[0])
    def _():
        x = x_ref[...]
        g = jnp.dot(x, wg_ref[0], preferred_element_type=F32)
        u = jnp.dot(x, wu_ref[0], preferred_element_type=F32)
        h = (g * _sigmoid(g)) * u
        acc_ref[...] += jnp.dot(h.astype(BF16), wd_ref[0], preferred_element_type=F32)

    @pl.when(j == pl.num_programs(1) - 1)
    def _():
        o_ref[...] = acc_ref[...].astype(o_ref.dtype)


def _grouped_ffn(x_rows, tile_expert, n_used, w_gate, w_up, w_down, tm, tf, out_dtype):
    n, d = x_rows.shape
    ff = w_gate.shape[2]
    nt = n // tm
    grid_spec = pltpu.PrefetchScalarGridSpec(
        num_scalar_prefetch=2,
        grid=(nt, ff // tf),
        in_specs=[pl.BlockSpec((tm, d), lambda i, j, te, nu: (i, 0)),
                  pl.BlockSpec((1, d, tf), lambda i, j, te, nu: (te[i], 0, j)),
                  pl.BlockSpec((1, d, tf), lambda i, j, te, nu: (te[i], 0, j)),
                  pl.BlockSpec((1, tf, d), lambda i, j, te, nu: (te[i], j, 0))],
        out_specs=pl.BlockSpec((tm, d), lambda i, j, te, nu: (i, 0)),
        scratch_shapes=[pltpu.VMEM((tm, d), F32)],
    )
    return pl.pallas_call(
        _ffn_kernel,
        out_shape=jax.ShapeDtypeStruct((n, d), out_dtype),
        grid_spec=grid_spec,
        compiler_params=_cp(("arbitrary", "arbitrary")),
        name="grouped_swiglu",
    )(tile_expert, n_used, x_rows, w_gate, w_up, w_down)


def _res_ln_kernel(x_ref, f_ref, mod_ref, lng_ref, lnb_ref, o_ref, *, alpha):
    mod = mod_ref[0, 0]
    o_ref[0] = _layer_norm_rows(alpha * x_ref[0] + mod[5:6, :] * f_ref[0]) * lng_ref[...] + lnb_ref[...]


def _res_ln(xs, f, mod_sel, ln_g, ln_b, alpha, ctx_len):
    b, s, d = xs.shape
    tm = ctx_len
    tok = pl.BlockSpec((1, tm, d), lambda i, j: (i, j, 0))
    vec = pl.BlockSpec((1, d), lambda i, j: (0, 0))
    return pl.pallas_call(
        functools.partial(_res_ln_kernel, alpha=alpha),
        out_shape=jax.ShapeDtypeStruct((b, s, d), F32),
        grid=(b, s // tm),
        in_specs=[tok, tok, pl.BlockSpec((1, 1, 6, d), lambda i, j: (i, jnp.minimum(j, 1), 0, 0)), vec, vec],
        out_specs=tok,
        compiler_params=_cp(("parallel", "parallel")),
        name="residual_ln",
    )(xs, f, mod_sel, ln_g.reshape(1, d), ln_b.reshape(1, d))


def _router_kernel(x_ref, mod_ref, w_ref, o_ref):
    mod = mod_ref[0, 0]
    u = _layer_norm_rows(x_ref[0]) * (1.0 + mod[4:5, :]) + mod[3:4, :]
    logits = jnp.dot(u, w_ref[...], preferred_element_type=F32, precision=HIGHEST)
    lane = lax.broadcasted_iota(jnp.int32, logits.shape, 1)
    lg = jnp.where(lane < N_EXPERTS, logits, -jnp.inf)
    v1 = lg.max(axis=-1, keepdims=True)
    i1 = jnp.min(jnp.where(lg == v1, lane, LANES), axis=-1, keepdims=True)
    lg2 = jnp.where(lane == i1, -jnp.inf, lg)
    v2 = lg2.max(axis=-1, keepdims=True)
    i2 = jnp.min(jnp.where(lg2 == v2, lane, LANES), axis=-1, keepdims=True)
    e2 = jnp.exp(v2 - v1)
    w1 = 1.0 / (1.0 + e2)
    w2 = e2 / (1.0 + e2)
    out = jnp.where(lane == 0, i1.astype(F32), 0.0)
    out = jnp.where(lane == 1, i2.astype(F32), out)
    out = jnp.where(lane == 2, w1, out)
    out = jnp.where(lane == 3, w2, out)
    o_ref[0] = out


def _router(x_lat, mod_sel, w_router):
    b, l, d = x_lat.shape
    tm = 256
    w_pad = jnp.zeros((d, LANES), F32).at[:, :N_EXPERTS].set(w_router)
    return pl.pallas_call(
        _router_kernel,
        out_shape=jax.ShapeDtypeStruct((b, l, LANES), F32),
        grid=(b, l // tm),
        in_specs=[pl.BlockSpec((1, tm, d), lambda i, j: (i, j, 0)),
                  pl.BlockSpec((1, 1, 6, d), lambda i, j: (i, 1, 0, 0)),
                  pl.BlockSpec((d, LANES), lambda i, j: (0, 0))],
        out_specs=pl.BlockSpec((1, tm, LANES), lambda i, j: (i, j, 0)),
        compiler_params=_cp(("parallel", "parallel")),
        name="router_top2",
    )(x_lat, mod_sel, w_pad)


MOE_TM = 512
MOE_TF = 512


def _moe_kernel(te_ref, nu_ref, dst_ref, u_hbm, wg_ref, wu_ref, wd_ref, out_hbm,
                xbuf, xbf, acc, gsem, ssem, *, n_rows):
    tm = MOE_TM
    i = pl.program_id(0)
    j = pl.program_id(1)
    nt = pl.num_programs(0)
    nj = pl.num_programs(1)
    n_used = nu_ref[0]
    slot = i % 2

    def gather_start(tile, sl):
        base = tile * tm

        def body(r, carry):
            p = dst_ref[base + r]
            row = jnp.maximum(p, 0) >> 1
            pltpu.make_async_copy(u_hbm.at[pl.ds(row, 1)], xbuf.at[sl, pl.ds(r, 1)], gsem.at[sl]).start()
            return carry

        lax.fori_loop(0, tm, body, 0, unroll=8)

    def gather_wait(sl):
        pltpu.make_async_copy(u_hbm.at[pl.ds(0, tm)], xbuf.at[sl], gsem.at[sl]).wait()

    def scatter_wait():
        pltpu.make_async_copy(acc.at[0], out_hbm.at[pl.ds(0, tm)], ssem.at[0]).wait()

    used = i < n_used

    @pl.when(used & (j == 0))
    def _():
        @pl.when(i == 0)
        def _():
            gather_start(0, 0)
            acc[1] = jnp.zeros((tm, acc.shape[2]), F32)
            dump = pltpu.make_async_copy(acc.at[1], out_hbm.at[pl.ds(n_rows, tm)], ssem.at[0])
            dump.start()
            dump.wait()

        gather_wait(slot)

        @pl.when(i + 1 < n_used)
        def _():
            gather_start(i + 1, 1 - slot)

        xbf[...] = xbuf[slot].astype(BF16)
        acc[slot] = jnp.zeros((tm, acc.shape[2]), F32)

    @pl.when(used)
    def _():
        x = xbf[...]
        g = jnp.dot(x, wg_ref[0], preferred_element_type=F32)
        u = jnp.dot(x, wu_ref[0], preferred_element_type=F32)
        h = (g * _sigmoid(g)) * u
        acc[slot] += jnp.dot(h.astype(BF16), wd_ref[0], preferred_element_type=F32)

    @pl.when(used & (j == nj - 1))
    def _():
        @pl.when(i > 0)
        def _():
            scatter_wait()

        base = i * tm

        def body(r, carry):
            p = dst_ref[base + r]
            row = jnp.where(p >= 0, p, n_rows + r)
            pltpu.make_async_copy(acc.at[slot, pl.ds(r, 1)], out_hbm.at[pl.ds(row, 1)], ssem.at[0]).start()
            return carry

        lax.fori_loop(0, tm, body, 0, unroll=8)

    @pl.when((i == nt - 1) & (j == nj - 1))
    def _():
        scatter_wait()


def _moe_experts(u_rows, dst, tile_expert, n_used, w_gate, w_up, w_down):
    r, d = u_rows.shape
    ff = w_gate.shape[2]
    tm, tf = MOE_TM, MOE_TF
    nt = dst.shape[0] // tm
    nj = ff // tf
    n_rows = TOP_K * r

    def jmap(i, j, nu):
        return jnp.where(i < nu[0], j, nj - 1)

    grid_spec = pltpu.PrefetchScalarGridSpec(
        num_scalar_prefetch=3,
        grid=(nt, nj),
        in_specs=[pl.BlockSpec(memory_space=pl.ANY),
                  pl.BlockSpec((1, d, tf), lambda i, j, te, nu, ds: (te[i], 0, jmap(i, j, nu))),
                  pl.BlockSpec((1, d, tf), lambda i, j, te, nu, ds: (te[i], 0, jmap(i, j, nu))),
                  pl.BlockSpec((1, tf, d), lambda i, j, te, nu, ds: (te[i], jmap(i, j, nu), 0))],
        out_specs=pl.BlockSpec(memory_space=pl.ANY),
        scratch_shapes=[pltpu.VMEM((2, tm, d), F32), pltpu.VMEM((tm, d), BF16), pltpu.VMEM((2, tm, d), F32),
                        pltpu.SemaphoreType.DMA((2,)), pltpu.SemaphoreType.DMA((1,))],
    )
    return pl.pallas_call(
        functools.partial(_moe_kernel, n_rows=n_rows),
        out_shape=jax.ShapeDtypeStruct((n_rows + tm, d), F32),
        grid_spec=grid_spec,
        compiler_params=_cp(("arbitrary", "arbitrary")),
        name="moe_experts",
    )(tile_expert, n_used, dst, u_rows, w_gate, w_up, w_down)


def _moe_combine_kernel(x_ref, y_ref, w_ref, mod_ref, lng_ref, lnb_ref, o_ref, *, alpha):
    mod = mod_ref[0, 0]
    w = w_ref[0]
    d = x_ref.shape[-1]
    f = w[:, 2:3] * y_ref[:, 0:d] + w[:, 3:4] * y_ref[:, d:2 * d]
    o_ref[0] = _layer_norm_rows(alpha * x_ref[0] + mod[5:6, :] * f) * lng_ref[...] + lnb_ref[...]


def _moe_ffn(x_lat, u_lat, mod_sel, lp_moe, ln_g, ln_b, alpha):
    b, l, d = x_lat.shape
    t = b * l
    tm = MOE_TM
    route = _router(x_lat, mod_sel, lp_moe['router'])
    ids = route[..., 0:TOP_K].astype(jnp.int32).reshape(t * TOP_K)
    onehot = (ids[:, None] == jnp.arange(N_EXPERTS)[None, :]).astype(jnp.int32)
    csum = jnp.cumsum(onehot, axis=0)
    rank = jnp.take_along_axis(csum, ids[:, None], axis=1)[:, 0] - 1
    counts = csum[-1]
    padded = ((counts + tm - 1) // tm) * tm
    ends = jnp.cumsum(padded)
    starts = ends - padded
    slot = starts[ids] + rank
    n_slots = t * TOP_K + N_EXPERTS * tm
    nt = n_slots // tm
    dst = jnp.full((n_slots,), -1, jnp.int32).at[slot].set(jnp.arange(t * TOP_K, dtype=jnp.int32))
    tile_start = jnp.arange(nt, dtype=jnp.int32) * tm
    tile_expert = jnp.minimum(jnp.sum(tile_start[:, None] >= ends[None, :], axis=1), N_EXPERTS - 1).astype(jnp.int32)
    n_used = (ends[-1] // tm).astype(jnp.int32).reshape(1)
    last_e = tile_expert[jnp.maximum(n_used[0] - 1, 0)]
    tile_expert = jnp.where(jnp.arange(nt) < n_used[0], tile_expert, last_e)
    y = _moe_experts(u_lat.reshape(t, d), dst, tile_expert, n_used, lp_moe['w_gate'], lp_moe['w_up'],
                     lp_moe['w_down'])
    y = y.reshape(y.shape[0] // TOP_K, TOP_K * d)
    tmc = 256
    nl = l // tmc
    vec = pl.BlockSpec((1, d), lambda i, j: (0, 0))
    return pl.pallas_call(
        functools.partial(_moe_combine_kernel, alpha=alpha),
        out_shape=jax.ShapeDtypeStruct((b, l, d), F32),
        grid=(b, nl),
        in_specs=[pl.BlockSpec((1, tmc, d), lambda i, j: (i, j, 0)),
                  pl.BlockSpec((tmc, TOP_K * d), lambda i, j: (i * nl + j, 0)),
                  pl.BlockSpec((1, tmc, LANES), lambda i, j: (i, j, 0)),
                  pl.BlockSpec((1, 1, 6, d), lambda i, j: (i, 1, 0, 0)), vec, vec],
        out_specs=pl.BlockSpec((1, tmc, d), lambda i, j: (i, j, 0)),
        compiler_params=_cp(("parallel", "parallel")),
        name="moe_combine_ln",
    )(x_lat, y, route, mod_sel, ln_g.reshape(1, d), ln_b.reshape(1, d))


def kernel(x, c, ctx, c_ctx, w_mod, b_mod, w_in, lru_conv_w, lru_conv_b, lru_w_r, lru_b_r, lru_w_i, lru_b_i, lru_lambda, hy_conv_w, hy_conv_b, hy_w1, hy_b1, hy_w2, hy_b2, hy_w3, hy_b3, hy_freq, hy_bias, na_rpb, s5_a_re, s5_a_im, s5_log_dt, s5_b_re, s5_b_im, s5_c_re, s5_c_im, s5_d, s5_w_glu, s5_b_glu, w_br_a, w_br_b, w_br_c, w_br_d, w_out, ln1_g, ln1_b, ln2_g, ln2_b, ff_w_gate, ff_w_up, ff_w_down, moe_router, moe_w_gate, moe_w_up, moe_w_down):
    bsz, l, d = x.shape
    ctx_len = ctx.shape[1]
    depth = w_in.shape[0]
    s = ctx_len + l
    alpha = (2.0 * depth) ** 0.25
    xs = jnp.concatenate([ctx, x], axis=1)
    c_rows = jnp.zeros((SUBLANES, d), F32).at[0:bsz].set(c).at[bsz].set(c_ctx)
    dft_lat = _dft_matrices(l)
    dft_ctx = _dft_matrices(ctx_len) if depth > 1 else None

    for li in range(depth):
        with_ctx = li < depth - 1
        lp = {
            'lru_conv_w': lru_conv_w[li], 'lru_conv_b': lru_conv_b[li], 'lru_w_r': lru_w_r[li],
            'lru_b_r': lru_b_r[li], 'lru_w_i': lru_w_i[li], 'lru_b_i': lru_b_i[li], 'lru_lambda': lru_lambda[li],
            'hy_conv_w': hy_conv_w[li], 'hy_conv_b': hy_conv_b[li], 'hy_w1': hy_w1[li], 'hy_b1': hy_b1[li],
            'hy_w2': hy_w2[li], 'hy_b2': hy_b2[li], 'hy_w3': hy_w3[li], 'hy_b3': hy_b3[li],
            'hy_freq': hy_freq[li], 'hy_bias': hy_bias[li],
            's5_a_re': s5_a_re[li], 's5_a_im': s5_a_im[li], 's5_log_dt': s5_log_dt[li], 's5_b_re': s5_b_re[li],
            's5_b_im': s5_b_im[li], 's5_c_re': s5_c_re[li], 's5_c_im': s5_c_im[li], 's5_d': s5_d[li],
            's5_w_glu': s5_w_glu[li], 's5_b_glu': s5_b_glu[li], 'w_br_a': w_br_a[li], 'w_br_b': w_br_b[li],
            'w_br_c': w_br_c[li], 'w_br_d': w_br_d[li], 'w_out': w_out[li],
        }
        mod = _mod_vectors(c_rows, w_mod[li], b_mod[li]).reshape(SUBLANES, 6, d)
        mod_sel = jnp.stack([jnp.broadcast_to(mod[bsz], (bsz, 6, d)), mod[0:bsz]], axis=1)

        u1 = _ln_mod(xs, mod_sel, 0, 1, ctx_len)
        u1f = u1.reshape(bsz * s, d)
        wi = w_in[li].astype(BF16)
        p_a = _matmul(u1f, wi[:, OFF_A:OFF_B], F32, name="proj_lru").reshape(bsz, s, OFF_B - OFF_A)
        p_c = _matmul(u1f, wi[:, OFF_C:OFF_D], BF16, name="proj_natten").reshape(bsz, s, OFF_D - OFF_C)
        p_d = _matmul(u1f, wi[:, OFF_D:OFF_G], F32, name="proj_s5").reshape(bsz, s, OFF_G - OFF_D)
        p_g = _matmul(u1f, wi[:, OFF_G:], BF16, name="proj_gates").reshape(bsz, s, N_BRANCH * d)
        wi_bt = wi[:, OFF_B:OFF_C].T

        ya = _lru_mixer(p_a, lp, ctx_len)
        yb_lat = _hyena_sequence(_matmul_nt(wi_bt, u1, F32, ctx_len, l), lp, dft_lat)
        if with_ctx:
            yb_ctx = _hyena_sequence(_matmul_nt(wi_bt, u1, F32, 0, ctx_len), lp, dft_ctx)
        else:
            yb_ctx = jnp.zeros((bsz, HY_WIDTH, ctx_len), BF16)
        yc_l, yc_c = _natten_mixer(p_c, na_rpb[li], ctx_len, with_ctx)
        if yc_c is None:
            yc_c = jnp.zeros((bsz, ctx_len, NA_WIDTH), BF16)
        yd = _s5_mixer(p_d, lp, ctx_len)
        x1, u2 = _merge(ya, yb_lat, yb_ctx, yc_l, yc_c, yd, p_g, xs, mod_sel, lp, ln1_g[li], ln1_b[li], alpha,
                        ctx_len, with_ctx, BF16 if li % 2 == 0 else F32)

        e = li // 2
        if li % 2 == 0:
            assert with_ctx
            n_rows = bsz * s
            tm = _pick_tile(n_rows, (1024, 512, 256))
            nt = n_rows // tm
            f = _grouped_ffn(u2.reshape(n_rows, d), jnp.zeros((nt,), jnp.int32), jnp.full((1,), nt, jnp.int32),
                             ff_w_gate[e:e + 1].astype(BF16), ff_w_up[e:e + 1].astype(BF16),
                             ff_w_down[e:e + 1].astype(BF16), tm, 256, F32)
            xs = _res_ln(x1, f.reshape(bsz, s, d), mod_sel, ln2_g[li], ln2_b[li], alpha, ctx_len)
        else:
            assert not with_ctx
            lp_moe = {'router': moe_router[e], 'w_gate': moe_w_gate[e].astype(BF16),
                      'w_up': moe_w_up[e].astype(BF16), 'w_down': moe_w_down[e].astype(BF16)}
            return _moe_ffn(x1, u2, mod_sel, lp_moe, ln2_g[li], ln2_b[li], alpha)
```

```python
import functools
import math

import numpy as np
import jax
import jax.numpy as jnp
from jax import lax
from jax.experimental import pallas as pl
from jax.experimental.pallas import tpu as pltpu

F32 = jnp.float32
BF16 = jnp.bfloat16
HIGHEST = lax.Precision.HIGHEST

LRU_WIDTH = 384
LRU_BLOCK = 64
LRU_CONV = 4
LRU_C = 8.0
HY_WIDTH = 256
HY_ORDER = 2
HY_SHORT = 3
HY_BANDS = 16
HY_FILT_HID = 64
HY_MAX_DECAY = math.log(1e-2) / 0.3
HY_MIN_DECAY = math.log(1e-2) / 1.5
NA_HEADS = 6
NA_HEAD_DIM = 64
NA_WIDTH = NA_HEADS * NA_HEAD_DIM
NA_WIN_R = 8
NA_WIN_C = 16
GRID_W = 64
S5_WIDTH = 256
S5_GROUP = 16
S5_GROUPS = 16
S5_STATE = 64
N_BRANCH = 4
OFF_A = 0
OFF_B = OFF_A + 2 * LRU_WIDTH
OFF_C = OFF_B + 3 * HY_WIDTH
OFF_D = OFF_C + 3 * NA_WIDTH
OFF_G = OFF_D + S5_WIDTH
N_EXPERTS = 8
TOP_K = 2
LN_EPS = 1e-5
MASK_VALUE = -1e30

LANES = 128
SUBLANES = 8
VMEM_LIMIT = 56 * 1024 * 1024


def _cp(sem, vmem=VMEM_LIMIT):
    return pltpu.CompilerParams(dimension_semantics=sem, vmem_limit_bytes=vmem)


def _gelu(x):
    return 0.5 * x * (1.0 + jnp.tanh(math.sqrt(2.0 / math.pi) * (x + 0.044715 * (x * x * x))))


def _sigmoid(x):
    return 0.5 + 0.5 * jnp.tanh(0.5 * x)


def _layer_norm_rows(x):
    mu = jnp.mean(x, axis=-1, keepdims=True)
    xc = x - mu
    var = jnp.mean(xc * xc, axis=-1, keepdims=True)
    return xc * lax.rsqrt(var + LN_EPS)


def _mod_kernel(c_ref, w_ref, b_ref, o_ref):
    c = c_ref[...]
    a = c * _sigmoid(c)
    o_ref[...] = jnp.dot(a, w_ref[...], preferred_element_type=F32, precision=HIGHEST) + b_ref[...]


def _mod_vectors(c_rows, w_mod, b_mod):
    d = c_rows.shape[1]
    n = w_mod.shape[1]
    tn = 1536
    return pl.pallas_call(
        _mod_kernel,
        out_shape=jax.ShapeDtypeStruct((SUBLANES, n), F32),
        grid=(n // tn,),
        in_specs=[pl.BlockSpec((SUBLANES, d), lambda j: (0, 0)),
                  pl.BlockSpec((d, tn), lambda j: (0, j)),
                  pl.BlockSpec((1, tn), lambda j: (0, j))],
        out_specs=pl.BlockSpec((SUBLANES, tn), lambda j: (0, j)),
        compiler_params=_cp(("arbitrary",)),
        name="mod_vectors",
    )(c_rows, w_mod, b_mod.reshape(1, n))


def _ln_mod_kernel(x_ref, mod_ref, o_ref, *, shift_idx, scale_idx):
    y = _layer_norm_rows(x_ref[0])
    m = mod_ref[0, 0]
    o = y * (1.0 + m[scale_idx:scale_idx + 1, :]) + m[shift_idx:shift_idx + 1, :]
    o_ref[0] = o.astype(o_ref.dtype)


def _ln_mod(xs, mod_sel, shift_idx, scale_idx, ctx_len):
    b, s, d = xs.shape
    tm = ctx_len
    return pl.pallas_call(
        functools.partial(_ln_mod_kernel, shift_idx=shift_idx, scale_idx=scale_idx),
        out_shape=jax.ShapeDtypeStruct((b, s, d), BF16),
        grid=(b, s // tm),
        in_specs=[pl.BlockSpec((1, tm, d), lambda i, j: (i, j, 0)),
                  pl.BlockSpec((1, 1, 6, d), lambda i, j: (i, jnp.minimum(j, 1), 0, 0))],
        out_specs=pl.BlockSpec((1, tm, d), lambda i, j: (i, j, 0)),
        compiler_params=_cp(("parallel", "parallel")),
        name="ln_mod",
    )(xs, mod_sel)


def _mm_kernel(a_ref, w_ref, o_ref):
    o_ref[...] = jnp.dot(a_ref[...], w_ref[...], preferred_element_type=F32).astype(o_ref.dtype)


def _pick_tile(n, prefs):
    for t in prefs:
        if n % t == 0:
            return t
    return n


def _matmul(a, w, out_dtype, tm=None, tn=None, name="matmul"):
    m, k = a.shape
    n = w.shape[1]
    tm = tm or _pick_tile(m, (1024, 512, 256, 128))
    tn = tn or _pick_tile(n, (768, 512, 384, 256, 128))
    return pl.pallas_call(
        _mm_kernel,
        out_shape=jax.ShapeDtypeStruct((m, n), out_dtype),
        grid=(m // tm, n // tn),
        in_specs=[pl.BlockSpec((tm, k), lambda i, j: (i, 0)),
                  pl.BlockSpec((k, tn), lambda i, j: (0, j))],
        out_specs=pl.BlockSpec((tm, tn), lambda i, j: (i, j)),
        compiler_params=_cp(("parallel", "parallel")),
        name=name,
    )(a, w)


def _mm_nt_kernel(w_ref, u_ref, o_ref):
    o_ref[0] = lax.dot_general(w_ref[...], u_ref[0], (((1,), (1,)), ((), ())),
                               preferred_element_type=F32).astype(o_ref.dtype)


def _matmul_nt(w_t, u, out_dtype, tok0, ntok, tn=256):
    c, k = w_t.shape
    b = u.shape[0]
    j0 = tok0 // tn
    return pl.pallas_call(
        _mm_nt_kernel,
        out_shape=jax.ShapeDtypeStruct((b, c, ntok), out_dtype),
        grid=(b, ntok // tn),
        in_specs=[pl.BlockSpec((c, k), lambda i, j: (0, 0)),
                  pl.BlockSpec((1, tn, k), lambda i, j: (i, j + j0, 0))],
        out_specs=pl.BlockSpec((1, c, tn), lambda i, j: (i, 0, j)),
        compiler_params=_cp(("parallel", "parallel")),
        name="matmul_nt",
    )(w_t, u)


LRU_CHUNK = 128


def _tile_scan(a, b, row, reverse):
    for s in (1, 2, 4):
        if reverse:
            keep = row < SUBLANES - s
            shift = SUBLANES - s
        else:
            keep = row >= s
            shift = s
        a_sh = pltpu.roll(a, shift, 0)
        b_sh = pltpu.roll(b, shift, 0)
        b = jnp.where(keep, a * b_sh, 0.0) + b
        a = jnp.where(keep, a * a_sh, a)
    return a, b


def _lru_kernel(pg_ref, px_ref, cw_ref, cb_ref, wg_ref, bg_ref, lam_ref, y_ref,
                xpad, a_f, b_f, a_b, b_b, *, s_len, ctx_len):
    ch = LRU_CHUNK
    n_chunks = s_len // ch
    zeros8 = jnp.zeros((SUBLANES, LANES), F32)
    xpad[0:SUBLANES, :] = zeros8
    xpad[ctx_len + SUBLANES:ctx_len + 2 * SUBLANES, :] = zeros8
    xpad[s_len + 2 * SUBLANES:s_len + 3 * SUBLANES, :] = zeros8

    def pad_row(r):
        return pl.multiple_of(r + jnp.where(r >= ctx_len, 2 * SUBLANES, SUBLANES), SUBLANES)

    def copy_body(i, carry):
        r = pl.multiple_of(i * ch, ch)
        xpad[pl.ds(pad_row(r), ch), :] = px_ref[0, pl.ds(r, ch), :]
        return carry

    lax.fori_loop(0, n_chunks, copy_body, 0)

    lam = lam_ref[...]
    sp = jnp.log(1.0 + jnp.exp(-lam))

    def gates_body(i, carry):
        r = pl.multiple_of(i * ch, ch)
        win = xpad[pl.ds(pad_row(r) - SUBLANES, ch + 2 * SUBLANES), :]
        xc = jnp.zeros((ch, LANES), F32) + cb_ref[...]
        for k in range(LRU_CONV):
            off = k - LRU_CONV // 2
            xc = xc + win[SUBLANES + off:SUBLANES + off + ch, :] * cw_ref[k:k + 1, :]
        gl = jnp.dot(xc.astype(BF16), wg_ref[0], preferred_element_type=F32) + bg_ref[0]
        for d, (a_s, b_s) in enumerate(((a_f, b_f), (a_b, b_b))):
            g_r = _sigmoid(gl[:, d * 2 * LANES:d * 2 * LANES + LANES])
            g_i = _sigmoid(gl[:, d * 2 * LANES + LANES:(d + 1) * 2 * LANES])
            log_a = (-LRU_C) * g_r * sp[d:d + 1, :]
            a = jnp.exp(log_a)
            bb = jnp.sqrt(1.0 - a * a) * g_i * xc
            a_s[pl.ds(r, ch), :] = a
            b_s[pl.ds(r, ch), :] = bb
        return carry

    lax.fori_loop(0, n_chunks, gates_body, 0)

    row = lax.broadcasted_iota(jnp.int32, (SUBLANES, LANES), 0)

    n_ctx_tiles = ctx_len // SUBLANES
    n_tiles = s_len // SUBLANES

    def scan_body(i, carry):
        h_f, h_b = carry
        r = pl.multiple_of(i * SUBLANES, SUBLANES)
        a, b = _tile_scan(a_f[pl.ds(r, SUBLANES), :], b_f[pl.ds(r, SUBLANES), :], row, False)
        hf = b + a * h_f
        b_f[pl.ds(r, SUBLANES), :] = hf
        t = jnp.where(i < n_ctx_tiles, n_ctx_tiles - 1 - i, n_tiles + n_ctx_tiles - 1 - i)
        rb = pl.multiple_of(t * SUBLANES, SUBLANES)
        a, b = _tile_scan(a_b[pl.ds(rb, SUBLANES), :], b_b[pl.ds(rb, SUBLANES), :], row, True)
        hb = b + a * h_b
        b_b[pl.ds(rb, SUBLANES), :] = hb
        return (jnp.broadcast_to(hf[SUBLANES - 1:SUBLANES, :], (SUBLANES, LANES)),
                jnp.broadcast_to(hb[0:1, :], (SUBLANES, LANES)))

    lax.fori_loop(0, n_tiles, scan_body, (zeros8, zeros8), unroll=2)

    def out_body(i, carry):
        r = pl.multiple_of(i * ch, ch)
        g = _gelu(pg_ref[0, pl.ds(r, ch), :])
        y = g * (b_f[pl.ds(r, ch), :] + b_b[pl.ds(r, ch), :])
        y_ref[0, pl.ds(r, ch), :] = y.astype(y_ref.dtype)
        return carry

    lax.fori_loop(0, n_chunks, out_body, 0)


def _lru_gate_weights(w_r, w_i, b_r, b_i):
    n_grp = LRU_WIDTH // LANES
    per = LANES // LRU_BLOCK

    def bd(w):
        w = w.reshape(n_grp, per, LRU_BLOCK, LRU_BLOCK)
        z = jnp.zeros((n_grp, LRU_BLOCK, LRU_BLOCK), w.dtype)
        top = jnp.concatenate([w[:, 0], z], axis=2)
        bot = jnp.concatenate([z, w[:, 1]], axis=2)
        return jnp.concatenate([top, bot], axis=1)

    wg = jnp.concatenate([bd(w_r[0]), bd(w_i[0]), bd(w_r[1]), bd(w_i[1])], axis=2).astype(BF16)
    bg = jnp.stack([b_r[0], b_i[0], b_r[1], b_i[1]], axis=0).reshape(4, n_grp, LANES)
    bg = jnp.transpose(bg, (1, 0, 2)).reshape(n_grp, 1, 4 * LANES)
    return wg, bg


def _lru_mixer(p_a, lp, ctx_len):
    b, s, _ = p_a.shape
    n_grp = LRU_WIDTH // LANES
    wg, bg = _lru_gate_weights(lp['lru_w_r'], lp['lru_w_i'], lp['lru_b_r'], lp['lru_b_i'])
    scr = pltpu.VMEM((s, LANES), F32)
    return pl.pallas_call(
        functools.partial(_lru_kernel, s_len=s, ctx_len=ctx_len),
        out_shape=jax.ShapeDtypeStruct((b, s, LRU_WIDTH), BF16),
        grid=(b, n_grp),
        in_specs=[pl.BlockSpec((1, s, LANES), lambda i, g: (i, 0, g)),
                  pl.BlockSpec((1, s, LANES), lambda i, g: (i, 0, n_grp + g)),
                  pl.BlockSpec((LRU_CONV, LANES), lambda i, g: (0, g)),
                  pl.BlockSpec((1, LANES), lambda i, g: (0, g)),
                  pl.BlockSpec((1, LANES, 4 * LANES), lambda i, g: (g, 0, 0)),
                  pl.BlockSpec((1, 1, 4 * LANES), lambda i, g: (g, 0, 0)),
                  pl.BlockSpec((2, LANES), lambda i, g: (0, g))],
        out_specs=pl.BlockSpec((1, s, LANES), lambda i, g: (i, 0, g)),
        scratch_shapes=[pltpu.VMEM((s + 3 * SUBLANES, LANES), F32), scr, scr, scr, scr],
        compiler_params=_cp(("parallel", "parallel")),
        name="rglru",
    )(p_a, p_a, lp['lru_conv_w'], lp['lru_conv_b'].reshape(1, LRU_WIDTH), wg, bg, lp['lru_lambda'])


S5_R = 4
S5_NSTATE = S5_GROUPS * S5_STATE


def _s5_kernel(x_ref, winj_ref, wloc_ref, wro_ref, ap_ref, y_ref, g_ref, *, reverse, n_ctx_tiles):
    n = S5_NSTATE
    x = x_ref[0]
    g_ref[...] = jnp.dot(x, winj_ref[...], preferred_element_type=F32)
    n_tiles = g_ref.shape[0] // SUBLANES
    row = lax.broadcasted_iota(jnp.int32, (SUBLANES, n), 0)
    zeros = jnp.zeros((SUBLANES, n), F32)
    if reverse:
        shift1, e_in, e_out = SUBLANES - 1, SUBLANES - 1, 0
    else:
        shift1, e_in, e_out = 1, 0, SUBLANES - 1

    def make_body(first_tile):
        def body(i, carry):
            hr, hi = carry
            t = (first_tile - i) if reverse else (first_tile + i)
            r = pl.multiple_of(t * SUBLANES, SUBLANES)
            br = g_ref[pl.ds(r, SUBLANES), 0:n]
            bi = g_ref[pl.ds(r, SUBLANES), n:2 * n]
            for k, s in enumerate((1, 2, 4)):
                ar = ap_ref[SUBLANES + k:SUBLANES + k + 1, 0:n]
                ai = ap_ref[SUBLANES + k:SUBLANES + k + 1, n:2 * n]
                if reverse:
                    keep = row < SUBLANES - s
                    shift = SUBLANES - s
                else:
                    keep = row >= s
                    shift = s
                brs = pltpu.roll(br, shift, 0)
                bis = pltpu.roll(bi, shift, 0)
                nr = ar * brs - ai * bis
                ni = ar * bis + ai * brs
                br = br + jnp.where(keep, nr, 0.0)
                bi = bi + jnp.where(keep, ni, 0.0)
            cr = ap_ref[0:SUBLANES, 0:n]
            ci = ap_ref[0:SUBLANES, n:2 * n]
            out_r = br + (cr * hr - ci * hi)
            out_i = bi + (cr * hi + ci * hr)
            g_ref[pl.ds(r, SUBLANES), 0:n] = jnp.where(row == e_in, hr, pltpu.roll(out_r, shift1, 0))
            g_ref[pl.ds(r, SUBLANES), n:2 * n] = jnp.where(row == e_in, hi, pltpu.roll(out_i, shift1, 0))
            return (jnp.broadcast_to(out_r[e_out:e_out + 1, :], (SUBLANES, n)),
                    jnp.broadcast_to(out_i[e_out:e_out + 1, :], (SUBLANES, n)))
        return body

    if reverse:
        carry = lax.fori_loop(0, n_ctx_tiles, make_body(n_ctx_tiles - 1), (zeros, zeros))
        lax.fori_loop(0, n_tiles - n_ctx_tiles, make_body(n_tiles - 1), carry)
    else:
        lax.fori_loop(0, n_tiles, make_body(0), (zeros, zeros))
    y_ref[0] = (jnp.dot(x, wloc_ref[...], preferred_element_type=F32)
                + jnp.dot(g_ref[...].astype(BF16), wro_ref[...], preferred_element_type=F32))


def _s5_params(a_re, a_im, log_dt, b_re, b_im, c_re, c_im, reverse):
    rr = S5_R
    dt = jnp.exp(log_dt)[:, None]
    den = a_re * a_re + a_im * a_im
    mag = jnp.exp(dt * a_re)
    ab_re = mag * jnp.cos(dt * a_im)
    ab_im = mag * jnp.sin(dt * a_im)
    f_re = ((ab_re - 1.0) * a_re + ab_im * a_im) / den
    f_im = (ab_im * a_re - (ab_re - 1.0) * a_im) / den
    bb_re = f_re[..., None] * b_re - f_im[..., None] * b_im
    bb_im = f_re[..., None] * b_im + f_im[..., None] * b_re
    eye = jnp.eye(S5_GROUPS, dtype=F32)

    def apow(k):
        k = k.astype(F32)[:, None, None]
        m = jnp.exp(k * dt[None] * a_re[None])
        return m * jnp.cos(k * dt[None] * a_im[None]), m * jnp.sin(k * dt[None] * a_im[None])

    steps = jnp.arange(rr)
    rows = rr * S5_WIDTH
    er, ei = apow(steps if reverse else (rr - 1 - steps))
    inj_re = er[..., None] * bb_re[None] - ei[..., None] * bb_im[None]
    inj_im = er[..., None] * bb_im[None] + ei[..., None] * bb_re[None]
    winj = jnp.concatenate([jnp.einsum('igpc,gh->igchp', inj_re, eye).reshape(rows, S5_NSTATE),
                            jnp.einsum('igpc,gh->igchp', inj_im, eye).reshape(rows, S5_NSTATE)], axis=1)
    fr, fi = apow((rr - steps) if reverse else (steps + 1))
    ro_re = c_re[None] * fr[:, :, None, :] - c_im[None] * fi[:, :, None, :]
    ro_im = c_re[None] * fi[:, :, None, :] + c_im[None] * fr[:, :, None, :]
    wro = jnp.concatenate([jnp.einsum('igop,gh->gpiho', ro_re, eye).reshape(S5_NSTATE, rows),
                           -jnp.einsum('igop,gh->gpiho', ro_im, eye).reshape(S5_NSTATE, rows)], axis=0)
    kr, ki = apow(steps)
    ab_r = kr[..., None] * bb_re[None] - ki[..., None] * bb_im[None]
    ab_i = kr[..., None] * bb_im[None] + ki[..., None] * bb_re[None]
    kk = jnp.einsum('gop,kgpc->kgoc', c_re, ab_r) - jnp.einsum('gop,kgpc->kgoc', c_im, ab_i)
    src = jnp.arange(rr)[:, None]
    tgt = jnp.arange(rr)[None, :]
    lag = (src - tgt) if reverse else (tgt - src)
    kmat = jnp.where((lag >= 0)[:, :, None, None, None], kk[jnp.clip(lag, 0, rr - 1)], 0.0)
    wloc = jnp.einsum('sigoc,gh->sgciho', kmat, eye).reshape(rows, rows)
    i8 = jnp.arange(SUBLANES)
    dist = (SUBLANES - i8) if reverse else (i8 + 1)
    ks = jnp.concatenate([dist, jnp.array([1, 2, 4]), jnp.zeros((5,), dist.dtype)]) * rr
    pr, pi = apow(ks)
    ap = jnp.concatenate([pr.reshape(16, S5_NSTATE), pi.reshape(16, S5_NSTATE)], axis=1)
    return winj.astype(BF16), wloc.astype(BF16), wro.astype(BF16), ap


def _s5_scan(x_ss, lp, d, ctx_len):
    b, nr, w = x_ss.shape
    reverse = d == 1
    n_ctx_tiles = ctx_len // (S5_R * SUBLANES)
    winj, wloc, wro, ap = _s5_params(lp['s5_a_re'][d], lp['s5_a_im'][d], lp['s5_log_dt'][d], lp['s5_b_re'][d],
                                     lp['s5_b_im'][d], lp['s5_c_re'][d], lp['s5_c_im'][d], reverse)
    full = lambda shape: pl.BlockSpec(shape, lambda i: (0, 0))
    return pl.pallas_call(
        functools.partial(_s5_kernel, reverse=reverse, n_ctx_tiles=n_ctx_tiles),
        out_shape=jax.ShapeDtypeStruct((b, nr, w), F32),
        grid=(b,),
        in_specs=[pl.BlockSpec((1, nr, w), lambda i: (i, 0, 0)),
                  full((w, 2 * S5_NSTATE)), full((w, w)), full((2 * S5_NSTATE, w)), full((16, 2 * S5_NSTATE))],
        out_specs=pl.BlockSpec((1, nr, w), lambda i: (i, 0, 0)),
        scratch_shapes=[pltpu.VMEM((nr, 2 * S5_NSTATE), F32)],
        compiler_params=_cp(("parallel",)),
        name="s5_scan_bwd" if reverse else "s5_scan_fwd",
    )(x_ss, winj, wloc, wro, ap)


def _s5_out_kernel(yf_ref, yb_ref, u_ref, d_ref, w_ref, b_ref, o_ref):
    y = yf_ref[...] + yb_ref[...] + d_ref[...] * u_ref[...]
    g = _gelu(y)
    z = jnp.dot(g.astype(BF16), w_ref[...], preferred_element_type=F32) + b_ref[...]
    o_ref[...] = (g * _sigmoid(z)).astype(o_ref.dtype)


def _s5_mixer(p_d, lp, ctx_len):
    b, s, w = p_d.shape
    x_ss = p_d.astype(BF16).reshape(b, s // S5_R, S5_R * w)
    yf = _s5_scan(x_ss, lp, 0, ctx_len)
    yb = _s5_scan(x_ss, lp, 1, ctx_len)
    m = b * s
    tm = _pick_tile(m, (1024, 512, 256))
    row = pl.BlockSpec((tm, w), lambda i: (i, 0))
    vec = pl.BlockSpec((1, w), lambda i: (0, 0))
    out = pl.pallas_call(
        _s5_out_kernel,
        out_shape=jax.ShapeDtypeStruct((m, w), BF16),
        grid=(m // tm,),
        in_specs=[row, row, row, vec, pl.BlockSpec((w, w), lambda i: (0, 0)), vec],
        out_specs=row,
        compiler_params=_cp(("parallel",)),
        name="s5_out",
    )(yf.reshape(m, w), yb.reshape(m, w), p_d.reshape(m, w), lp['s5_d'].reshape(1, w),
      lp['s5_w_glu'].astype(BF16), lp['s5_b_glu'].reshape(1, w))
    return out.reshape(b, s, w)


NA_QROWS = 4


def _natten_plan(rows):
    kr = min(NA_WIN_R, rows)
    span = kr + NA_QROWS - 1
    variants, index, blk_var = [], {}, []
    for blk in range(rows // NA_QROWS):
        r0 = blk * NA_QROWS
        ws = int(np.clip(r0 - kr // 2, 0, rows - span))
        dr = np.zeros((NA_QROWS, span), np.int32)
        ok = np.zeros((NA_QROWS, span), bool)
        for q in range(NA_QROWS):
            r = r0 + q
            rs = int(np.clip(r - kr // 2, 0, rows - kr))
            for i in range(span):
                ok[q, i] = rs <= ws + i < rs + kr
                dr[q, i] = (ws + i - r + (NA_WIN_R - 1)) if ok[q, i] else 0
        key = dr.tobytes() + ok.tobytes()
        if key not in index:
            index[key] = len(variants)
            variants.append((dr, ok))
        blk_var.append(index[key])
    return (np.stack([v[0] for v in variants]), np.stack([v[1] for v in variants]),
            np.asarray(blk_var, np.int32))


def _natten_bias(rpb, dr, ok):
    nv, nq, span = dr.shape
    n_dr, n_dc = 2 * NA_WIN_R - 1, 2 * NA_WIN_C - 1
    w = np.arange(GRID_W)
    cs = np.clip(w - NA_WIN_C // 2, 0, GRID_W - NA_WIN_C)
    ok_col = (w[None, :] >= cs[:, None]) & (w[None, :] < cs[:, None] + NA_WIN_C)
    dc = w[None, :] - w[:, None] + (NA_WIN_C - 1)
    e_dc = (dc[None] == np.arange(n_dc)[:, None, None]).astype(np.float32)
    e_dr = ((dr[..., None] == np.arange(n_dr)) & ok[..., None]).astype(np.float32)
    g = jnp.einsum('vqir,hrk,kwc->vhqwic', e_dr, rpb, e_dc, precision=HIGHEST)
    ok_all = ok[:, None, :, None, :, None] & ok_col[None, None, None, :, None, :]
    g = jnp.where(jnp.asarray(ok_all), g, MASK_VALUE)
    return g.reshape(nv, NA_HEADS, nq * GRID_W, span * GRID_W).astype(F32)


def _attend(q2, keys, vals, biases, lane):
    out = jnp.zeros(q2.shape, F32)
    for hh in range(2):
        sel = (lane >= hh * NA_HEAD_DIM) & (lane < (hh + 1) * NA_HEAD_DIM)
        qm = jnp.where(sel, q2, jnp.zeros_like(q2))
        ss = []
        for k_i, b_i in zip(keys, biases[hh]):
            s_i = lax.dot_general(qm, k_i, (((1,), (1,)), ((), ())), preferred_element_type=F32)
            if b_i is not None:
                s_i = s_i + b_i
            ss.append(s_i)
        m = ss[0].max(axis=-1, keepdims=True)
        for s_i in ss[1:]:
            m = jnp.maximum(m, s_i.max(axis=-1, keepdims=True))
        ps = [jnp.exp(s_i - m) for s_i in ss]
        den = ps[0].sum(axis=-1, keepdims=True)
        for p_i in ps[1:]:
            den = den + p_i.sum(axis=-1, keepdims=True)
        o = jnp.dot(ps[0].astype(BF16), vals[0], preferred_element_type=F32)
        for p_i, v_i in zip(ps[1:], vals[1:]):
            o = o + jnp.dot(p_i.astype(BF16), v_i, preferred_element_type=F32)
        o = o / den
        out = jnp.where(sel[:, :], o, out)
    return out


def _natten_kernel(var_ref, q_ref, k_ref, v_ref, bias_ref, o_ref, *, rows, ctx_len):
    kr = min(NA_WIN_R, rows)
    span = kr + NA_QROWS - 1
    r0 = pl.program_id(1) * NA_QROWS
    ws = jnp.clip(r0 - kr // 2, 0, rows - span)
    base = pl.multiple_of(ctx_len + ws * GRID_W, GRID_W)
    lane = lax.broadcasted_iota(jnp.int32, (NA_QROWS * GRID_W, LANES), 1)
    scale = NA_HEAD_DIM ** -0.5
    for hp in range(NA_HEADS // 2):
        ls = slice(hp * LANES, (hp + 1) * LANES)
        q2 = q_ref[0, :, ls] * scale
        kw = k_ref[0, pl.ds(base, span * GRID_W), ls]
        vw = v_ref[0, pl.ds(base, span * GRID_W), ls]
        kc = k_ref[0, 0:ctx_len, ls]
        vc = v_ref[0, 0:ctx_len, ls]
        biases = [[bias_ref[0, 2 * hp + hh], None] for hh in range(2)]
        out = _attend(q2, [kw, kc], [vw, vc], biases, lane)
        o_ref[0, :, ls] = out.astype(o_ref.dtype)


def _ctx_attn_kernel(q_ref, k_ref, v_ref, o_ref, *, ctx_len):
    lane = lax.broadcasted_iota(jnp.int32, (ctx_len, LANES), 1)
    scale = NA_HEAD_DIM ** -0.5
    for hp in range(NA_HEADS // 2):
        ls = slice(hp * LANES, (hp + 1) * LANES)
        out = _attend(q_ref[0, :, ls] * scale, [k_ref[0, :, ls]], [v_ref[0, :, ls]], [[None], [None]], lane)
        o_ref[0, :, ls] = out.astype(o_ref.dtype)


def _natten_mixer(p_c, rpb, ctx_len, with_ctx):
    b, s, _ = p_c.shape
    l = s - ctx_len
    rows = l // GRID_W
    kr = min(NA_WIN_R, rows)
    span = kr + NA_QROWS - 1
    nq = NA_QROWS * GRID_W
    dr, ok, blk_var = _natten_plan(rows)
    bias = _natten_bias(rpb, dr, ok)
    cb = ctx_len // nq
    grid_spec = pltpu.PrefetchScalarGridSpec(
        num_scalar_prefetch=1,
        grid=(b, rows // NA_QROWS),
        in_specs=[pl.BlockSpec((1, nq, NA_WIDTH), lambda i, r, var: (i, cb + r, 0)),
                  pl.BlockSpec((1, s, NA_WIDTH), lambda i, r, var: (i, 0, 1)),
                  pl.BlockSpec((1, s, NA_WIDTH), lambda i, r, var: (i, 0, 2)),
                  pl.BlockSpec((1, NA_HEADS, nq, span * GRID_W), lambda i, r, var: (var[r], 0, 0, 0))],
        out_specs=pl.BlockSpec((1, nq, NA_WIDTH), lambda i, r, var: (i, r, 0)),
    )
    y_l = pl.pallas_call(
        functools.partial(_natten_kernel, rows=rows, ctx_len=ctx_len),
        out_shape=jax.ShapeDtypeStruct((b, l, NA_WIDTH), BF16),
        grid_spec=grid_spec,
        compiler_params=_cp(("parallel", "arbitrary")),
        name="natten",
    )(jnp.asarray(blk_var), p_c, p_c, p_c, bias)
    if not with_ctx:
        return y_l, None
    y_c = pl.pallas_call(
        functools.partial(_ctx_attn_kernel, ctx_len=ctx_len),
        out_shape=jax.ShapeDtypeStruct((b, ctx_len, NA_WIDTH), BF16),
        grid=(b,),
        in_specs=[pl.BlockSpec((1, ctx_len, NA_WIDTH), lambda i: (i, 0, 0)),
                  pl.BlockSpec((1, ctx_len, NA_WIDTH), lambda i: (i, 0, 1)),
                  pl.BlockSpec((1, ctx_len, NA_WIDTH), lambda i: (i, 0, 2))],
        out_specs=pl.BlockSpec((1, ctx_len, NA_WIDTH), lambda i: (i, 0, 0)),
        compiler_params=_cp(("parallel",)),
        name="ctx_attn",
    )(p_c, p_c, p_c)
    return y_l, y_c


DFT_ROWS = 64


def _dft_gen_kernel(ca_ref, sa_ref, cb_ref, sb_ref, fwd_ref, inv_ref, *, l):
    i = pl.program_id(0)
    ca = ca_ref[0]
    sa = sa_ref[0]
    cb = cb_ref[...]
    sb = sb_ref[...]
    gc = ca * cb - sa * sb
    gs = sa * cb + ca * sb
    x = i * DFT_ROWS + lax.broadcasted_iota(jnp.int32, (DFT_ROWS, l), 0)
    y = lax.broadcasted_iota(jnp.int32, (DFT_ROWS, l), 1)
    n = 2.0 * l
    nyq_x = jnp.where((x & 1) == 0, 1.0, -1.0)
    fwd_ref[:, 0:l] = gc.astype(fwd_ref.dtype)
    fwd_ref[:, l:2 * l] = jnp.where(y == 0, nyq_x, -gs).astype(fwd_ref.dtype)
    scale = jnp.where(x == 0, 1.0 / n, 2.0 / n)
    nyq_y = jnp.where((y & 1) == 0, 1.0, -1.0)
    inv_ref[0] = (scale * gc).astype(inv_ref.dtype)
    inv_ref[1] = (scale * jnp.where(x == 0, nyq_y, -gs)).astype(inv_ref.dtype)


def _dft_matrices(l):
    n = 2 * l
    k1 = l // DFT_ROWS
    y = np.arange(l, dtype=np.int64)
    xa = (DFT_ROWS * np.arange(k1, dtype=np.int64))[:, None]
    xb = np.arange(DFT_ROWS, dtype=np.int64)[:, None]
    ang_a = jnp.asarray(((xa * y[None, :]) % n).astype(np.float32)) * F32(2.0 * math.pi / n)
    ang_b = jnp.asarray(((xb * y[None, :]) % n).astype(np.float32)) * F32(2.0 * math.pi / n)
    ca, sa = jnp.cos(ang_a).reshape(k1, 1, l), jnp.sin(ang_a).reshape(k1, 1, l)
    cb, sb = jnp.cos(ang_b), jnp.sin(ang_b)
    row = pl.BlockSpec((1, 1, l), lambda i: (i, 0, 0))
    tab = pl.BlockSpec((DFT_ROWS, l), lambda i: (0, 0))
    fwd, inv = pl.pallas_call(
        functools.partial(_dft_gen_kernel, l=l),
        out_shape=(jax.ShapeDtypeStruct((l, 2 * l), BF16), jax.ShapeDtypeStruct((2, l, l), BF16)),
        grid=(k1,),
        in_specs=[row, row, tab, tab],
        out_specs=(pl.BlockSpec((DFT_ROWS, 2 * l), lambda i: (i, 0)),
                   pl.BlockSpec((2, DFT_ROWS, l), lambda i: (0, i, 0))),
        compiler_params=_cp(("parallel",)),
        name="dft_gen",
    )(ca, sa, cb, sb)
    return fwd, inv.reshape(2 * l, l)


def _hy_filter_kernel(w1t_ref, w1c_ref, w1s_ref, b1_ref, w2_ref, b2_ref, w3_ref, b3_ref, fr_ref, dl_ref,
                      h_ref, asum_ref, *, l, tl):
    j = pl.program_id(0)
    t = (j * tl + lax.broadcasted_iota(jnp.int32, (1, tl), 1)).astype(F32)
    t_norm = t / l
    bands = (1 + lax.broadcasted_iota(jnp.int32, (HY_BANDS, 1), 0)).astype(F32)
    ang = (2.0 * math.pi / l) * t * bands
    fr = fr_ref[...]
    lin = (w1t_ref[...] * t_norm
           + jnp.dot(w1c_ref[...], jnp.cos(ang), preferred_element_type=F32, precision=HIGHEST)
           + jnp.dot(w1s_ref[...], jnp.sin(ang), preferred_element_type=F32, precision=HIGHEST))
    h = jnp.sin(fr * (lin + b1_ref[...]))
    h = jnp.sin(fr * (jnp.dot(w2_ref[...], h, preferred_element_type=F32, precision=HIGHEST) + b2_ref[...]))
    h = jnp.dot(w3_ref[...], h, preferred_element_type=F32, precision=HIGHEST) + b3_ref[...]
    window = jnp.exp(-t_norm * dl_ref[...])
    first = (j * tl + lax.broadcasted_iota(jnp.int32, (HY_WIDTH, tl), 1)) == 0

    @pl.when(j == 0)
    def _():
        asum_ref[...] = jnp.zeros_like(asum_ref)

    for blk in range(2 * HY_ORDER):
        rs = slice(blk * HY_WIDTH, (blk + 1) * HY_WIDTH)
        hb = h[rs, :] * window
        if blk >= HY_ORDER:
            hb = jnp.where(first, 0.0, hb)
        h_ref[rs, :] = hb
        asum_ref[rs, :] += jnp.sum(jnp.abs(hb), axis=1, keepdims=True)


def _hy_filters(lp, l):
    tl = min(l, 512)
    hid = HY_FILT_HID
    w1 = lp['hy_w1']
    col = lambda v: v.reshape(-1, 1)
    deltas = np.abs(np.linspace(HY_MIN_DECAY, HY_MAX_DECAY, HY_WIDTH, dtype=np.float32)).reshape(-1, 1)
    full = lambda shape: pl.BlockSpec(shape, lambda j: (0, 0))
    n_out = 2 * HY_ORDER * HY_WIDTH
    return pl.pallas_call(
        functools.partial(_hy_filter_kernel, l=l, tl=tl),
        out_shape=(jax.ShapeDtypeStruct((n_out, l), F32), jax.ShapeDtypeStruct((n_out, 1), F32)),
        grid=(l // tl,),
        in_specs=[full((hid, 1)), full((hid, HY_BANDS)), full((hid, HY_BANDS)), full((hid, 1)),
                  full((hid, hid)), full((hid, 1)), full((n_out, hid)), full((n_out, 1)),
                  full((hid, 1)), full((HY_WIDTH, 1))],
        out_specs=(pl.BlockSpec((n_out, tl), lambda j: (0, j)), pl.BlockSpec((n_out, 1), lambda j: (0, 0))),
        compiler_params=_cp(("arbitrary",)),
        name="hyena_filters",
    )(w1[0:1].T, w1[1:1 + HY_BANDS].T, w1[1 + HY_BANDS:].T, col(lp['hy_b1']), lp['hy_w2'].T, col(lp['hy_b2']),
      lp['hy_w3'].T, col(lp['hy_b3']), col(lp['hy_freq']), jnp.asarray(deltas))


def _hy_short_conv_kernel(x_ref, w_ref, b_ref, z_ref, zbf_ref, *, l):
    x = x_ref[0]
    t = lax.broadcasted_iota(jnp.int32, x.shape, 1)
    left = HY_SHORT // 2
    z = jnp.zeros(x.shape, F32) + b_ref[...]
    for k in range(HY_SHORT):
        off = k - left
        if off == 0:
            sh = x
        else:
            sh = pltpu.roll(x, (-off) % l, 1)
            sh = jnp.where((t + off >= 0) & (t + off < l), sh, 0.0)
        z = z + sh * w_ref[:, k:k + 1]
    z_ref[0] = z
    zbf_ref[0] = z.astype(zbf_ref.dtype)


def _hy_conv_kernel(y_ref, fc_ref, fs_ref, ic_ref, is_ref, kfc_ref, kfs_ref, kbc_ref, kbs_ref, n_ref,
                    o_ref, *, nb):
    j = pl.program_id(1)

    @pl.when(j == 0)
    def _():
        o_ref[...] = jnp.zeros_like(o_ref)

    y = y_ref[...]
    zr = jnp.dot(y, fc_ref[...], preferred_element_type=F32)
    zi = jnp.dot(y, fs_ref[...], preferred_element_type=F32)
    inv_n = 1.0 / (n_ref[...] + 1e-6)
    kr = (kfc_ref[...] + kbc_ref[...]) * inv_n
    ki = (kfs_ref[...] - kbs_ref[...]) * inv_n
    tn = kr.shape[1]
    f0 = (j * tn + lax.broadcasted_iota(jnp.int32, kr.shape, 1)) == 0
    ki = jnp.where(f0, (kfs_ref[...] + kbs_ref[...]) * inv_n, ki)
    prs, pis = [], []
    for bb in range(nb):
        rs = slice(bb * HY_WIDTH, (bb + 1) * HY_WIDTH)
        a, b = zr[rs], zi[rs]
        prs.append(a * kr - jnp.where(f0, 0.0, b * ki))
        pis.append(jnp.where(f0, b * ki, a * ki + b * kr))
    pr = jnp.concatenate(prs, axis=0).astype(BF16)
    pi = jnp.concatenate(pis, axis=0).astype(BF16)
    o_ref[...] += (jnp.dot(pr, ic_ref[...], preferred_element_type=F32)
                   + jnp.dot(pi, is_ref[...], preferred_element_type=F32))


def _hy_long_conv(ybf, n_tiles, nb, row_stride, fwd, inv, kf, asum, order, l):
    tmh = nb * HY_WIDTH
    m = n_tiles * tmh
    tn = min(l, 256)
    jn = l // tn
    o_f = order
    o_b = HY_ORDER + order
    return pl.pallas_call(
        functools.partial(_hy_conv_kernel, nb=nb),
        out_shape=jax.ShapeDtypeStruct((m, l), F32),
        grid=(n_tiles, jn),
        in_specs=[pl.BlockSpec((tmh, l), lambda i, j: (i * row_stride, 0)),
                  pl.BlockSpec((l, tn), lambda i, j: (0, j)),
                  pl.BlockSpec((l, tn), lambda i, j: (0, jn + j)),
                  pl.BlockSpec((tn, l), lambda i, j: (j, 0)),
                  pl.BlockSpec((tn, l), lambda i, j: (jn + j, 0)),
                  pl.BlockSpec((HY_WIDTH, tn), lambda i, j: (o_f, j)),
                  pl.BlockSpec((HY_WIDTH, tn), lambda i, j: (o_f, jn + j)),
                  pl.BlockSpec((HY_WIDTH, tn), lambda i, j: (o_b, j)),
                  pl.BlockSpec((HY_WIDTH, tn), lambda i, j: (o_b, jn + j)),
                  pl.BlockSpec((HY_WIDTH, 1), lambda i, j: (order, 0))],
        out_specs=pl.BlockSpec((tmh, l), lambda i, j: (i, 0)),
        compiler_params=_cp(("parallel", "arbitrary")),
        name="hyena_long_conv",
    )(ybf, fwd, fwd, inv, inv, kf, kf, kf, kf, asum)


def _hy_gate_kernel(g_ref, c_ref, y_ref, bias_ref, o_ref, obf_ref):
    o = g_ref[0] * (c_ref[0] + y_ref[0] * bias_ref[...])
    o_ref[0] = o
    obf_ref[0] = o.astype(obf_ref.dtype)


def _hy_gate(z, conv, y, bias_col, gate_blk, y_blk):
    b, _, l = z.shape
    tl = min(l, 1024)
    spec = lambda blk: pl.BlockSpec((1, HY_WIDTH, tl), lambda i, j: (i, blk, j))
    return pl.pallas_call(
        _hy_gate_kernel,
        out_shape=(jax.ShapeDtypeStruct((b, HY_WIDTH, l), F32), jax.ShapeDtypeStruct((b, HY_WIDTH, l), BF16)),
        grid=(b, l // tl),
        in_specs=[spec(gate_blk), spec(0), spec(y_blk), pl.BlockSpec((HY_WIDTH, 1), lambda i, j: (0, 0))],
        out_specs=(spec(0), spec(0)),
        compiler_params=_cp(("parallel", "parallel")),
        name="hyena_gate",
    )(z, conv, y, bias_col)


def _hyena_sequence(p_bt, lp, dft):
    b, c3, l = p_bt.shape
    fwd, inv = dft
    h, asum = _hy_filters(lp, l)
    kf = _matmul(h.astype(BF16), fwd, F32, name="hyena_filter_dft")
    asum2 = asum.reshape(2, HY_ORDER * HY_WIDTH).sum(axis=0).reshape(HY_ORDER * HY_WIDTH, 1)
    blk = pl.BlockSpec((1, LANES, l), lambda i, g: (i, g, 0))
    z, z_bf = pl.pallas_call(
        functools.partial(_hy_short_conv_kernel, l=l),
        out_shape=(jax.ShapeDtypeStruct((b, c3, l), F32), jax.ShapeDtypeStruct((b, c3, l), BF16)),
        grid=(b, c3 // LANES),
        in_specs=[blk,
                  pl.BlockSpec((LANES, HY_SHORT), lambda i, g: (g, 0)),
                  pl.BlockSpec((LANES, 1), lambda i, g: (g, 0))],
        out_specs=(blk, blk),
        compiler_params=_cp(("parallel", "parallel")),
        name="hyena_short_conv",
    )(p_bt, lp['hy_conv_w'].T, lp['hy_conv_b'].reshape(c3, 1))
    n_blk = c3 // HY_WIDTH
    nb2 = 2 if b % 2 == 0 else 1
    conv1 = _hy_long_conv(z_bf.reshape(b * c3, l), b, 1, n_blk, fwd, inv, kf, asum2, 0, l).reshape(b, HY_WIDTH, l)
    y1, y1_bf = _hy_gate(z, conv1, z, lp['hy_bias'][0].reshape(HY_WIDTH, 1), 1, 0)
    conv2 = _hy_long_conv(y1_bf.reshape(b * HY_WIDTH, l), b // nb2, nb2, 1, fwd, inv, kf, asum2, 1, l)
    _, y2_bf = _hy_gate(z, conv2.reshape(b, HY_WIDTH, l), y1, lp['hy_bias'][1].reshape(HY_WIDTH, 1), 2, 0)
    return y2_bf


def _merge_kernel(ya_ref, ybl_ref, ybc_ref, ycl_ref, ycc_ref, yd_ref, pg_ref, x_ref, mod_ref, wa_ref, wb_ref,
                  wc_ref, wd_ref, wo_ref, lng_ref, lnb_ref, x_out_ref, u_out_ref, *, alpha, first_tile):
    j = pl.program_id(1) + first_tile
    d = x_ref.shape[-1]
    yb_t = jnp.where(j == 0, ybc_ref[0], ybl_ref[0])
    yb = yb_t.astype(F32).T.astype(BF16)
    yc = jnp.where(j == 0, ycc_ref[0], ycl_ref[0])
    projs = [jnp.dot(ya_ref[0], wa_ref[...], preferred_element_type=F32),
             jnp.dot(yb, wb_ref[...], preferred_element_type=F32),
             jnp.dot(yc, wc_ref[...], preferred_element_type=F32),
             jnp.dot(yd_ref[0], wd_ref[...], preferred_element_type=F32)]
    merged = None
    for i, pr in enumerate(projs):
        term = _sigmoid(pg_ref[0, :, i * d:(i + 1) * d].astype(F32)) * pr
        merged = term if merged is None else merged + term
    m = jnp.dot(merged.astype(BF16), wo_ref[...], preferred_element_type=F32)
    mod = mod_ref[0, 0]
    xn = _layer_norm_rows(alpha * x_ref[0] + mod[2:3, :] * m) * lng_ref[...] + lnb_ref[...]
    x_out_ref[0] = xn
    u = _layer_norm_rows(xn) * (1.0 + mod[4:5, :]) + mod[3:4, :]
    u_out_ref[0] = u.astype(u_out_ref.dtype)


def _merge(ya, yb_lat, yb_ctx, yc_lat, yc_ctx, yd, pg, xs, mod_sel, lp, ln_g, ln_b, alpha, ctx_len, with_ctx,
           u_dtype):
    b, s, d = xs.shape
    tm = ctx_len
    first = 0 if with_ctx else 1
    nt = s // tm - first
    tok = lambda w: pl.BlockSpec((1, tm, w), lambda i, j: (i, j + first, 0))
    lat = lambda j: jnp.maximum(j + first - 1, 0)
    out = pl.BlockSpec((1, tm, d), lambda i, j: (i, j, 0))
    full = lambda shape: pl.BlockSpec(shape, lambda i, j: (0,) * len(shape))
    x_new, u2 = pl.pallas_call(
        functools.partial(_merge_kernel, alpha=alpha, first_tile=first),
        out_shape=(jax.ShapeDtypeStruct((b, nt * tm, d), F32), jax.ShapeDtypeStruct((b, nt * tm, d), u_dtype)),
        grid=(b, nt),
        in_specs=[tok(LRU_WIDTH),
                  pl.BlockSpec((1, HY_WIDTH, tm), lambda i, j: (i, 0, lat(j))),
                  pl.BlockSpec((1, HY_WIDTH, tm), lambda i, j: (i, 0, 0)),
                  pl.BlockSpec((1, tm, NA_WIDTH), lambda i, j: (i, lat(j), 0)),
                  pl.BlockSpec((1, tm, NA_WIDTH), lambda i, j: (i, 0, 0)),
                  tok(S5_WIDTH), tok(N_BRANCH * d), tok(d),
                  pl.BlockSpec((1, 1, 6, d), lambda i, j: (i, jnp.minimum(j + first, 1), 0, 0)),
                  full((LRU_WIDTH, d)), full((HY_WIDTH, d)), full((NA_WIDTH, d)), full((S5_WIDTH, d)),
                  full((d, d)), full((1, d)), full((1, d))],
        out_specs=(out, out),
        compiler_params=_cp(("parallel", "parallel")),
        name="merge",
    )(ya, yb_lat, yb_ctx, yc_lat, yc_ctx, yd, pg, xs, mod_sel,
      lp['w_br_a'].astype(BF16), lp['w_br_b'].astype(BF16), lp['w_br_c'].astype(BF16),
      lp['w_br_d'].astype(BF16), lp['w_out'].astype(BF16), ln_g.reshape(1, d), ln_b.reshape(1, d))
    return x_new, u2


def _ffn_kernel(te_ref, nu_ref, x_ref, wg_ref, wu_ref, wd_ref, o_ref, acc_ref):
    i = pl.program_id(0)
    j = pl.program_id(1)

    @pl.when(j == 0)
    def _():
        acc_ref[...] = jnp.zeros_like(acc_ref)

    @pl.when(i < nu_ref[0])
    def _():
        x = x_ref[...]
        g = jnp.dot(x, wg_ref[0], preferred_element_type=F32)
        u = jnp.dot(x, wu_ref[0], preferred_element_type=F32)
        h = (g * _sigmoid(g)) * u
        acc_ref[...] += jnp.dot(h.astype(BF16), wd_ref[0], preferred_element_type=F32)

    @pl.when(j == pl.num_programs(1) - 1)
    def _():
        o_ref[...] = acc_ref[...].astype(o_ref.dtype)


def _grouped_ffn(x_rows, tile_expert, n_used, w_gate, w_up, w_down, tm, tf, out_dtype):
    n, d = x_rows.shape
    ff = w_gate.shape[2]
    nt = n // tm
    grid_spec = pltpu.PrefetchScalarGridSpec(
        num_scalar_prefetch=2,
        grid=(nt, ff // tf),
        in_specs=[pl.BlockSpec((tm, d), lambda i, j, te, nu: (i, 0)),
                  pl.BlockSpec((1, d, tf), lambda i, j, te, nu: (te[i], 0, j)),
                  pl.BlockSpec((1, d, tf), lambda i, j, te, nu: (te[i], 0, j)),
                  pl.BlockSpec((1, tf, d), lambda i, j, te, nu: (te[i], j, 0))],
        out_specs=pl.BlockSpec((tm, d), lambda i, j, te, nu: (i, 0)),
        scratch_shapes=[pltpu.VMEM((tm, d), F32)],
    )
    return pl.pallas_call(
        _ffn_kernel,
        out_shape=jax.ShapeDtypeStruct((n, d), out_dtype),
        grid_spec=grid_spec,
        compiler_params=_cp(("arbitrary", "arbitrary")),
        name="grouped_swiglu",
    )(tile_expert, n_used, x_rows, w_gate, w_up, w_down)


def _res_ln_kernel(x_ref, f_ref, mod_ref, lng_ref, lnb_ref, o_ref, *, alpha):
    mod = mod_ref[0, 0]
    o_ref[0] = _layer_norm_rows(alpha * x_ref[0] + mod[5:6, :] * f_ref[0]) * lng_ref[...] + lnb_ref[...]


def _res_ln(xs, f, mod_sel, ln_g, ln_b, alpha, ctx_len):
    b, s, d = xs.shape
    tm = ctx_len
    tok = pl.BlockSpec((1, tm, d), lambda i, j: (i, j, 0))
    vec = pl.BlockSpec((1, d), lambda i, j: (0, 0))
    return pl.pallas_call(
        functools.partial(_res_ln_kernel, alpha=alpha),
        out_shape=jax.ShapeDtypeStruct((b, s, d), F32),
        grid=(b, s // tm),
        in_specs=[tok, tok, pl.BlockSpec((1, 1, 6, d), lambda i, j: (i, jnp.minimum(j, 1), 0, 0)), vec, vec],
        out_specs=tok,
        compiler_params=_cp(("parallel", "parallel")),
        name="residual_ln",
    )(xs, f, mod_sel, ln_g.reshape(1, d), ln_b.reshape(1, d))


def _router_kernel(x_ref, mod_ref, w_ref, o_ref):
    mod = mod_ref[0, 0]
    u = _layer_norm_rows(x_ref[0]) * (1.0 + mod[4:5, :]) + mod[3:4, :]
    logits = jnp.dot(u, w_ref[...], preferred_element_type=F32, precision=HIGHEST)
    lane = lax.broadcasted_iota(jnp.int32, logits.shape, 1)
    lg = jnp.where(lane < N_EXPERTS, logits, -jnp.inf)
    v1 = lg.max(axis=-1, keepdims=True)
    i1 = jnp.min(jnp.where(lg == v1, lane, LANES), axis=-1, keepdims=True)
    lg2 = jnp.where(lane == i1, -jnp.inf, lg)
    v2 = lg2.max(axis=-1, keepdims=True)
    i2 = jnp.min(jnp.where(lg2 == v2, lane, LANES), axis=-1, keepdims=True)
    e2 = jnp.exp(v2 - v1)
    w1 = 1.0 / (1.0 + e2)
    w2 = e2 / (1.0 + e2)
    out = jnp.where(lane == 0, i1.astype(F32), 0.0)
    out = jnp.where(lane == 1, i2.astype(F32), out)
    out = jnp.where(lane == 2, w1, out)
    out = jnp.where(lane == 3, w2, out)
    o_ref[0] = out


def _router(x_lat, mod_sel, w_router):
    b, l, d = x_lat.shape
    tm = 256
    w_pad = jnp.zeros((d, LANES), F32).at[:, :N_EXPERTS].set(w_router)
    return pl.pallas_call(
        _router_kernel,
        out_shape=jax.ShapeDtypeStruct((b, l, LANES), F32),
        grid=(b, l // tm),
        in_specs=[pl.BlockSpec((1, tm, d), lambda i, j: (i, j, 0)),
                  pl.BlockSpec((1, 1, 6, d), lambda i, j: (i, 1, 0, 0)),
                  pl.BlockSpec((d, LANES), lambda i, j: (0, 0))],
        out_specs=pl.BlockSpec((1, tm, LANES), lambda i, j: (i, j, 0)),
        compiler_params=_cp(("parallel", "parallel")),
        name="router_top2",
    )(x_lat, mod_sel, w_pad)


MOE_TM = 512
MOE_TF = 512


def _moe_kernel(te_ref, nu_ref, dst_ref, u_hbm, wg_ref, wu_ref, wd_ref, out_hbm,
                xbuf, xbf, acc, gsem, ssem, *, n_rows):
    tm = MOE_TM
    n_tok = n_rows // TOP_K
    i = pl.program_id(0)
    j = pl.program_id(1)
    nt = pl.num_programs(0)
    nj = pl.num_programs(1)
    n_used = nu_ref[0]
    slot = i % 2

    def gather_start(tile, sl):
        base = tile * tm

        def body(r, carry):
            p = dst_ref[base + r]
            row = jnp.where(p >= n_tok, p - n_tok, jnp.maximum(p, 0))
            pltpu.make_async_copy(u_hbm.at[pl.ds(row, 1)], xbuf.at[sl, pl.ds(r, 1)], gsem.at[sl]).start()
            return carry

        lax.fori_loop(0, tm, body, 0, unroll=8)

    def gather_wait(sl):
        pltpu.make_async_copy(u_hbm.at[pl.ds(0, tm)], xbuf.at[sl], gsem.at[sl]).wait()

    def scatter_wait():
        pltpu.make_async_copy(acc.at[0], out_hbm.at[pl.ds(0, tm)], ssem.at[0]).wait()

    used = i < n_used

    @pl.when(used & (j == 0))
    def _():
        @pl.when(i == 0)
        def _():
            gather_start(0, 0)
            acc[1] = jnp.zeros((tm, acc.shape[2]), F32)
            dump = pltpu.make_async_copy(acc.at[1], out_hbm.at[pl.ds(n_rows, tm)], ssem.at[0])
            dump.start()
            dump.wait()

        gather_wait(slot)

        @pl.when(i + 1 < n_used)
        def _():
            gather_start(i + 1, 1 - slot)

        xbf[...] = xbuf[slot].astype(BF16)
        acc[slot] = jnp.zeros((tm, acc.shape[2]), F32)

    @pl.when(used)
    def _():
        x = xbf[...]
        g = jnp.dot(x, wg_ref[0], preferred_element_type=F32)
        u = jnp.dot(x, wu_ref[0], preferred_element_type=F32)
        h = (g * _sigmoid(g)) * u
        acc[slot] += jnp.dot(h.astype(BF16), wd_ref[0], preferred_element_type=F32)

    @pl.when(used & (j == nj - 1))
    def _():
        @pl.when(i > 0)
        def _():
            scatter_wait()

        base = i * tm

        def body(r, carry):
            p = dst_ref[base + r]
            row = jnp.where(p >= 0, p, n_rows + r)
            pltpu.make_async_copy(acc.at[slot, pl.ds(r, 1)], out_hbm.at[pl.ds(row, 1)], ssem.at[0]).start()
            return carry

        lax.fori_loop(0, tm, body, 0, unroll=8)

    @pl.when((i == nt - 1) & (j == nj - 1))
    def _():
        scatter_wait()


def _moe_experts(u_rows, dst, tile_expert, n_used, w_gate, w_up, w_down):
    r, d = u_rows.shape
    ff = w_gate.shape[2]
    tm, tf = MOE_TM, MOE_TF
    nt = dst.shape[0] // tm
    nj = ff // tf
    n_rows = TOP_K * r

    def jmap(i, j, nu):
        return jnp.where(i < nu[0], j, nj - 1)

    grid_spec = pltpu.PrefetchScalarGridSpec(
        num_scalar_prefetch=3,
        grid=(nt, nj),
        in_specs=[pl.BlockSpec(memory_space=pl.ANY),
                  pl.BlockSpec((1, d, tf), lambda i, j, te, nu, ds: (te[i], 0, jmap(i, j, nu))),
                  pl.BlockSpec((1, d, tf), lambda i, j, te, nu, ds: (te[i], 0, jmap(i, j, nu))),
                  pl.BlockSpec((1, tf, d), lambda i, j, te, nu, ds: (te[i], jmap(i, j, nu), 0))],
        out_specs=pl.BlockSpec(memory_space=pl.ANY),
        scratch_shapes=[pltpu.VMEM((2, tm, d), F32), pltpu.VMEM((tm, d), BF16), pltpu.VMEM((2, tm, d), F32),
                        pltpu.SemaphoreType.DMA((2,)), pltpu.SemaphoreType.DMA((1,))],
    )
    return pl.pallas_call(
        functools.partial(_moe_kernel, n_rows=n_rows),
        out_shape=jax.ShapeDtypeStruct((n_rows + tm, d), F32),
        grid_spec=grid_spec,
        compiler_params=_cp(("arbitrary", "arbitrary")),
        name="moe_experts",
    )(tile_expert, n_used, dst, u_rows, w_gate, w_up, w_down)


def _moe_combine_kernel(x_ref, y1_ref, y2_ref, w_ref, mod_ref, lng_ref, lnb_ref, o_ref, *, alpha):
    mod = mod_ref[0, 0]
    w = w_ref[0]
    f = w[:, 2:3] * y1_ref[...] + w[:, 3:4] * y2_ref[...]
    o_ref[0] = _layer_norm_rows(alpha * x_ref[0] + mod[5:6, :] * f) * lng_ref[...] + lnb_ref[...]


def _moe_ffn(x_lat, u_lat, mod_sel, lp_moe, ln_g, ln_b, alpha):
    b, l, d = x_lat.shape
    t = b * l
    tm = MOE_TM
    route = _router(x_lat, mod_sel, lp_moe['router'])
    ids = route[..., 0:TOP_K].astype(jnp.int32).reshape(t * TOP_K)
    onehot = (ids[:, None] == jnp.arange(N_EXPERTS)[None, :]).astype(jnp.int32)
    csum = jnp.cumsum(onehot, axis=0)
    rank = jnp.take_along_axis(csum, ids[:, None], axis=1)[:, 0] - 1
    counts = csum[-1]
    padded = ((counts + tm - 1) // tm) * tm
    ends = jnp.cumsum(padded)
    starts = ends - padded
    slot = starts[ids] + rank
    n_slots = t * TOP_K + N_EXPERTS * tm
    nt = n_slots // tm
    pair = jnp.arange(t * TOP_K, dtype=jnp.int32)
    dst = jnp.full((n_slots,), -1, jnp.int32).at[slot].set((pair % TOP_K) * t + pair // TOP_K)
    tile_start = jnp.arange(nt, dtype=jnp.int32) * tm
    tile_expert = jnp.minimum(jnp.sum(tile_start[:, None] >= ends[None, :], axis=1), N_EXPERTS - 1).astype(jnp.int32)
    n_used = (ends[-1] // tm).astype(jnp.int32).reshape(1)
    last_e = tile_expert[jnp.maximum(n_used[0] - 1, 0)]
    tile_expert = jnp.where(jnp.arange(nt) < n_used[0], tile_expert, last_e)
    y = _moe_experts(u_lat.reshape(t, d), dst, tile_expert, n_used, lp_moe['w_gate'], lp_moe['w_up'],
                     lp_moe['w_down'])
    tmc = 256
    nl = l // tmc
    vec = pl.BlockSpec((1, d), lambda i, j: (0, 0))
    return pl.pallas_call(
        functools.partial(_moe_combine_kernel, alpha=alpha),
        out_shape=jax.ShapeDtypeStruct((b, l, d), F32),
        grid=(b, nl),
        in_specs=[pl.BlockSpec((1, tmc, d), lambda i, j: (i, j, 0)),
                  pl.BlockSpec((tmc, d), lambda i, j: (i * nl + j, 0)),
                  pl.BlockSpec((tmc, d), lambda i, j: (b * nl + i * nl + j, 0)),
                  pl.BlockSpec((1, tmc, LANES), lambda i, j: (i, j, 0)),
                  pl.BlockSpec((1, 1, 6, d), lambda i, j: (i, 1, 0, 0)), vec, vec],
        out_specs=pl.BlockSpec((1, tmc, d), lambda i, j: (i, j, 0)),
        compiler_params=_cp(("parallel", "parallel")),
        name="moe_combine_ln",
    )(x_lat, y, y, route, mod_sel, ln_g.reshape(1, d), ln_b.reshape(1, d))


def kernel(x, c, ctx, c_ctx, w_mod, b_mod, w_in, lru_conv_w, lru_conv_b, lru_w_r, lru_b_r, lru_w_i, lru_b_i, lru_lambda, hy_conv_w, hy_conv_b, hy_w1, hy_b1, hy_w2, hy_b2, hy_w3, hy_b3, hy_freq, hy_bias, na_rpb, s5_a_re, s5_a_im, s5_log_dt, s5_b_re, s5_b_im, s5_c_re, s5_c_im, s5_d, s5_w_glu, s5_b_glu, w_br_a, w_br_b, w_br_c, w_br_d, w_out, ln1_g, ln1_b, ln2_g, ln2_b, ff_w_gate, ff_w_up, ff_w_down, moe_router, moe_w_gate, moe_w_up, moe_w_down):
    bsz, l, d = x.shape
    ctx_len = ctx.shape[1]
    depth = w_in.shape[0]
    s = ctx_len + l
    alpha = (2.0 * depth) ** 0.25
    xs = jnp.concatenate([ctx, x], axis=1)
    c_rows = jnp.zeros((SUBLANES, d), F32).at[0:bsz].set(c).at[bsz].set(c_ctx)
    dft_lat = _dft_matrices(l)
    dft_ctx = _dft_matrices(ctx_len) if depth > 1 else None

    for li in range(depth):
        with_ctx = li < depth - 1
        lp = {
            'lru_conv_w': lru_conv_w[li], 'lru_conv_b': lru_conv_b[li], 'lru_w_r': lru_w_r[li],
            'lru_b_r': lru_b_r[li], 'lru_w_i': lru_w_i[li], 'lru_b_i': lru_b_i[li], 'lru_lambda': lru_lambda[li],
            'hy_conv_w': hy_conv_w[li], 'hy_conv_b': hy_conv_b[li], 'hy_w1': hy_w1[li], 'hy_b1': hy_b1[li],
            'hy_w2': hy_w2[li], 'hy_b2': hy_b2[li], 'hy_w3': hy_w3[li], 'hy_b3': hy_b3[li],
            'hy_freq': hy_freq[li], 'hy_bias': hy_bias[li],
            's5_a_re': s5_a_re[li], 's5_a_im': s5_a_im[li], 's5_log_dt': s5_log_dt[li], 's5_b_re': s5_b_re[li],
            's5_b_im': s5_b_im[li], 's5_c_re': s5_c_re[li], 's5_c_im': s5_c_im[li], 's5_d': s5_d[li],
            's5_w_glu': s5_w_glu[li], 's5_b_glu': s5_b_glu[li], 'w_br_a': w_br_a[li], 'w_br_b': w_br_b[li],
            'w_br_c': w_br_c[li], 'w_br_d': w_br_d[li], 'w_out': w_out[li],
        }
        mod = _mod_vectors(c_rows, w_mod[li], b_mod[li]).reshape(SUBLANES, 6, d)
        mod_sel = jnp.stack([jnp.broadcast_to(mod[bsz], (bsz, 6, d)), mod[0:bsz]], axis=1)

        u1 = _ln_mod(xs, mod_sel, 0, 1, ctx_len)
        u1f = u1.reshape(bsz * s, d)
        wi = w_in[li].astype(BF16)
        p_a = _matmul(u1f, wi[:, OFF_A:OFF_B], F32, name="proj_lru").reshape(bsz, s, OFF_B - OFF_A)
        p_c = _matmul(u1f, wi[:, OFF_C:OFF_D], BF16, name="proj_natten").reshape(bsz, s, OFF_D - OFF_C)
        p_d = _matmul(u1f, wi[:, OFF_D:OFF_G], F32, name="proj_s5").reshape(bsz, s, OFF_G - OFF_D)
        p_g = _matmul(u1f, wi[:, OFF_G:], BF16, name="proj_gates").reshape(bsz, s, N_BRANCH * d)
        wi_bt = wi[:, OFF_B:OFF_C].T

        ya = _lru_mixer(p_a, lp, ctx_len)
        yb_lat = _hyena_sequence(_matmul_nt(wi_bt, u1, F32, ctx_len, l), lp, dft_lat)
        if with_ctx:
            yb_ctx = _hyena_sequence(_matmul_nt(wi_bt, u1, F32, 0, ctx_len), lp, dft_ctx)
        else:
            yb_ctx = jnp.zeros((bsz, HY_WIDTH, ctx_len), BF16)
        yc_l, yc_c = _natten_mixer(p_c, na_rpb[li], ctx_len, with_ctx)
        if yc_c is None:
            yc_c = jnp.zeros((bsz, ctx_len, NA_WIDTH), BF16)
        yd = _s5_mixer(p_d, lp, ctx_len)
        x1, u2 = _merge(ya, yb_lat, yb_ctx, yc_l, yc_c, yd, p_g, xs, mod_sel, lp, ln1_g[li], ln1_b[li], alpha,
                        ctx_len, with_ctx, BF16 if li % 2 == 0 else F32)

        e = li // 2
        if li % 2 == 0:
            assert with_ctx
            n_rows = bsz * s
            tm = _pick_tile(n_rows, (1024, 512, 256))
            nt = n_rows // tm
            f = _grouped_ffn(u2.reshape(n_rows, d), jnp.zeros((nt,), jnp.int32), jnp.full((1,), nt, jnp.int32),
                             ff_w_gate[e:e + 1].astype(BF16), ff_w_up[e:e + 1].astype(BF16),
                             ff_w_down[e:e + 1].astype(BF16), tm, 256, F32)
            xs = _res_ln(x1, f.reshape(bsz, s, d), mod_sel, ln2_g[li], ln2_b[li], alpha, ctx_len)
        else:
            assert not with_ctx
            lp_moe = {'router': moe_router[e], 'w_gate': moe_w_gate[e].astype(BF16),
                      'w_up': moe_w_up[e].astype(BF16), 'w_down': moe_w_down[e].astype(BF16)}
            return _moe_ffn(x1, u2, mod_sel, lp_moe, ln2_g[li], ln2_b[li], alpha)
```

```python
import functools
import math

import numpy as np
import jax
import jax.numpy as jnp
from jax import lax
from jax.experimental import pallas as pl
from jax.experimental.pallas import tpu as pltpu

F32 = jnp.float32
BF16 = jnp.bfloat16
HIGHEST = lax.Precision.HIGHEST

LRU_WIDTH = 384
LRU_BLOCK = 64
LRU_CONV = 4
LRU_C = 8.0
HY_WIDTH = 256
HY_ORDER = 2
HY_SHORT = 3
HY_BANDS = 16
HY_FILT_HID = 64
HY_MAX_DECAY = math.log(1e-2) / 0.3
HY_MIN_DECAY = math.log(1e-2) / 1.5
NA_HEADS = 6
NA_HEAD_DIM = 64
NA_WIDTH = NA_HEADS * NA_HEAD_DIM
NA_WIN_R = 8
NA_WIN_C = 16
GRID_W = 64
S5_WIDTH = 256
S5_GROUP = 16
S5_GROUPS = 16
S5_STATE = 64
N_BRANCH = 4
OFF_A = 0
OFF_B = OFF_A + 2 * LRU_WIDTH
OFF_C = OFF_B + 3 * HY_WIDTH
OFF_D = OFF_C + 3 * NA_WIDTH
OFF_G = OFF_D + S5_WIDTH
N_EXPERTS = 8
TOP_K = 2
LN_EPS = 1e-5
MASK_VALUE = -1e30

LANES = 128
SUBLANES = 8
VMEM_LIMIT = 56 * 1024 * 1024


def _cp(sem, vmem=VMEM_LIMIT):
    return pltpu.CompilerParams(dimension_semantics=sem, vmem_limit_bytes=vmem)


def _gelu(x):
    return 0.5 * x * (1.0 + jnp.tanh(math.sqrt(2.0 / math.pi) * (x + 0.044715 * (x * x * x))))


def _sigmoid(x):
    return 0.5 + 0.5 * jnp.tanh(0.5 * x)


def _layer_norm_rows(x):
    mu = jnp.mean(x, axis=-1, keepdims=True)
    xc = x - mu
    var = jnp.mean(xc * xc, axis=-1, keepdims=True)
    return xc * lax.rsqrt(var + LN_EPS)


def _mod_kernel(c_ref, w_ref, b_ref, o_ref):
    c = c_ref[...]
    a = c * _sigmoid(c)
    o_ref[0] = jnp.dot(a, w_ref[0], preferred_element_type=F32, precision=HIGHEST) + b_ref[0]


def _mod_vectors(c_rows, w_mod, b_mod):
    d = c_rows.shape[1]
    depth, _, n = w_mod.shape
    tn = 1536
    return pl.pallas_call(
        _mod_kernel,
        out_shape=jax.ShapeDtypeStruct((depth, SUBLANES, n), F32),
        grid=(depth, n // tn),
        in_specs=[pl.BlockSpec((SUBLANES, d), lambda l, j: (0, 0)),
                  pl.BlockSpec((1, d, tn), lambda l, j: (l, 0, j)),
                  pl.BlockSpec((1, 1, tn), lambda l, j: (l, 0, j))],
        out_specs=pl.BlockSpec((1, SUBLANES, tn), lambda l, j: (l, 0, j)),
        compiler_params=_cp(("arbitrary", "arbitrary")),
        name="mod_vectors",
    )(c_rows, w_mod, b_mod.reshape(depth, 1, n))


def _ln_mod_kernel(x_ref, mod_ref, o_ref, *, shift_idx, scale_idx):
    y = _layer_norm_rows(x_ref[0])
    m = mod_ref[0, 0]
    o = y * (1.0 + m[scale_idx:scale_idx + 1, :]) + m[shift_idx:shift_idx + 1, :]
    o_ref[0] = o.astype(o_ref.dtype)


def _ln_mod(xs, mod_sel, shift_idx, scale_idx, ctx_len):
    b, s, d = xs.shape
    tm = ctx_len
    return pl.pallas_call(
        functools.partial(_ln_mod_kernel, shift_idx=shift_idx, scale_idx=scale_idx),
        out_shape=jax.ShapeDtypeStruct((b, s, d), BF16),
        grid=(b, s // tm),
        in_specs=[pl.BlockSpec((1, tm, d), lambda i, j: (i, j, 0)),
                  pl.BlockSpec((1, 1, 6, d), lambda i, j: (i, jnp.minimum(j, 1), 0, 0))],
        out_specs=pl.BlockSpec((1, tm, d), lambda i, j: (i, j, 0)),
        compiler_params=_cp(("parallel", "parallel")),
        name="ln_mod",
    )(xs, mod_sel)


def _mm_kernel(a_ref, w_ref, o_ref):
    o_ref[...] = jnp.dot(a_ref[...], w_ref[...], preferred_element_type=F32).astype(o_ref.dtype)


def _pick_tile(n, prefs):
    for t in prefs:
        if n % t == 0:
            return t
    return n


def _matmul(a, w, out_dtype, tm=None, tn=None, name="matmul"):
    m, k = a.shape
    n = w.shape[1]
    tm = tm or _pick_tile(m, (1024, 512, 256, 128))
    tn = tn or (n if n <= 1536 else _pick_tile(n, (1024, 768, 512, 384, 256, 128)))
    return pl.pallas_call(
        _mm_kernel,
        out_shape=jax.ShapeDtypeStruct((m, n), out_dtype),
        grid=(m // tm, n // tn),
        in_specs=[pl.BlockSpec((tm, k), lambda i, j: (i, 0)),
                  pl.BlockSpec((k, tn), lambda i, j: (0, j))],
        out_specs=pl.BlockSpec((tm, tn), lambda i, j: (i, j)),
        compiler_params=_cp(("parallel", "parallel")),
        name=name,
    )(a, w)


def _mm_nt_kernel(w_ref, u_ref, o_ref):
    o_ref[0] = lax.dot_general(w_ref[...], u_ref[0], (((1,), (1,)), ((), ())),
                               preferred_element_type=F32).astype(o_ref.dtype)


def _matmul_nt(w_t, u, out_dtype, tok0, ntok, tn=256):
    c, k = w_t.shape
    b = u.shape[0]
    j0 = tok0 // tn
    return pl.pallas_call(
        _mm_nt_kernel,
        out_shape=jax.ShapeDtypeStruct((b, c, ntok), out_dtype),
        grid=(b, ntok // tn),
        in_specs=[pl.BlockSpec((c, k), lambda i, j: (0, 0)),
                  pl.BlockSpec((1, tn, k), lambda i, j: (i, j + j0, 0))],
        out_specs=pl.BlockSpec((1, c, tn), lambda i, j: (i, 0, j)),
        compiler_params=_cp(("parallel", "parallel")),
        name="matmul_nt",
    )(w_t, u)


LRU_CHUNK = 128


def _tile_scan(a, b, row, reverse):
    for s in (1, 2, 4):
        if reverse:
            keep = row < SUBLANES - s
            shift = SUBLANES - s
        else:
            keep = row >= s
            shift = s
        a_sh = pltpu.roll(a, shift, 0)
        b_sh = pltpu.roll(b, shift, 0)
        b = jnp.where(keep, a * b_sh, 0.0) + b
        a = jnp.where(keep, a * a_sh, a)
    return a, b


def _lru_kernel(pg_ref, px_ref, cw_ref, cb_ref, wg_ref, bg_ref, lam_ref, y_ref,
                xpad, a_f, b_f, a_b, b_b, *, s_len, ctx_len):
    ch = LRU_CHUNK
    n_chunks = s_len // ch
    zeros8 = jnp.zeros((SUBLANES, LANES), F32)
    xpad[0:SUBLANES, :] = zeros8
    xpad[ctx_len + SUBLANES:ctx_len + 2 * SUBLANES, :] = zeros8
    xpad[s_len + 2 * SUBLANES:s_len + 3 * SUBLANES, :] = zeros8

    def pad_row(r):
        return pl.multiple_of(r + jnp.where(r >= ctx_len, 2 * SUBLANES, SUBLANES), SUBLANES)

    def copy_body(i, carry):
        r = pl.multiple_of(i * ch, ch)
        xpad[pl.ds(pad_row(r), ch), :] = px_ref[0, pl.ds(r, ch), :]
        return carry

    lax.fori_loop(0, n_chunks, copy_body, 0)

    lam = lam_ref[...]
    sp = jnp.log(1.0 + jnp.exp(-lam))

    def gates_body(i, carry):
        r = pl.multiple_of(i * ch, ch)
        win = xpad[pl.ds(pad_row(r) - SUBLANES, ch + 2 * SUBLANES), :]
        xc = jnp.zeros((ch, LANES), F32) + cb_ref[...]
        for k in range(LRU_CONV):
            off = k - LRU_CONV // 2
            xc = xc + win[SUBLANES + off:SUBLANES + off + ch, :] * cw_ref[k:k + 1, :]
        gl = jnp.dot(xc.astype(BF16), wg_ref[0], preferred_element_type=F32) + bg_ref[0]
        for d, (a_s, b_s) in enumerate(((a_f, b_f), (a_b, b_b))):
            g_r = _sigmoid(gl[:, d * 2 * LANES:d * 2 * LANES + LANES])
            g_i = _sigmoid(gl[:, d * 2 * LANES + LANES:(d + 1) * 2 * LANES])
            log_a = (-LRU_C) * g_r * sp[d:d + 1, :]
            a = jnp.exp(log_a)
            bb = jnp.sqrt(1.0 - a * a) * g_i * xc
            a_s[pl.ds(r, ch), :] = a
            b_s[pl.ds(r, ch), :] = bb
        return carry

    lax.fori_loop(0, n_chunks, gates_body, 0)

    row = lax.broadcasted_iota(jnp.int32, (SUBLANES, LANES), 0)

    n_ctx_tiles = ctx_len // SUBLANES
    n_tiles = s_len // SUBLANES

    def scan_body(i, carry):
        h_f, h_b = carry
        r = pl.multiple_of(i * SUBLANES, SUBLANES)
        a, b = _tile_scan(a_f[pl.ds(r, SUBLANES), :], b_f[pl.ds(r, SUBLANES), :], row, False)
        hf = b + a * h_f
        b_f[pl.ds(r, SUBLANES), :] = hf
        t = jnp.where(i < n_ctx_tiles, n_ctx_tiles - 1 - i, n_tiles + n_ctx_tiles - 1 - i)
        rb = pl.multiple_of(t * SUBLANES, SUBLANES)
        a, b = _tile_scan(a_b[pl.ds(rb, SUBLANES), :], b_b[pl.ds(rb, SUBLANES), :], row, True)
        hb = b + a * h_b
        b_b[pl.ds(rb, SUBLANES), :] = hb
        return (jnp.broadcast_to(hf[SUBLANES - 1:SUBLANES, :], (SUBLANES, LANES)),
                jnp.broadcast_to(hb[0:1, :], (SUBLANES, LANES)))

    lax.fori_loop(0, n_tiles, scan_body, (zeros8, zeros8), unroll=2)

    def out_body(i, carry):
        r = pl.multiple_of(i * ch, ch)
        g = _gelu(pg_ref[0, pl.ds(r, ch), :])
        y = g * (b_f[pl.ds(r, ch), :] + b_b[pl.ds(r, ch), :])
        y_ref[0, pl.ds(r, ch), :] = y.astype(y_ref.dtype)
        return carry

    lax.fori_loop(0, n_chunks, out_body, 0)


def _lru_gate_weights(w_r, w_i, b_r, b_i):
    n_grp = LRU_WIDTH // LANES
    per = LANES // LRU_BLOCK

    def bd(w):
        w = w.reshape(n_grp, per, LRU_BLOCK, LRU_BLOCK)
        z = jnp.zeros((n_grp, LRU_BLOCK, LRU_BLOCK), w.dtype)
        top = jnp.concatenate([w[:, 0], z], axis=2)
        bot = jnp.concatenate([z, w[:, 1]], axis=2)
        return jnp.concatenate([top, bot], axis=1)

    wg = jnp.concatenate([bd(w_r[0]), bd(w_i[0]), bd(w_r[1]), bd(w_i[1])], axis=2).astype(BF16)
    bg = jnp.stack([b_r[0], b_i[0], b_r[1], b_i[1]], axis=0).reshape(4, n_grp, LANES)
    bg = jnp.transpose(bg, (1, 0, 2)).reshape(n_grp, 1, 4 * LANES)
    return wg, bg


def _lru_mixer(p_a, lp, ctx_len):
    b, s, _ = p_a.shape
    n_grp = LRU_WIDTH // LANES
    wg, bg = _lru_gate_weights(lp['lru_w_r'], lp['lru_w_i'], lp['lru_b_r'], lp['lru_b_i'])
    scr = pltpu.VMEM((s, LANES), F32)
    return pl.pallas_call(
        functools.partial(_lru_kernel, s_len=s, ctx_len=ctx_len),
        out_shape=jax.ShapeDtypeStruct((b, s, LRU_WIDTH), BF16),
        grid=(b, n_grp),
        in_specs=[pl.BlockSpec((1, s, LANES), lambda i, g: (i, 0, g)),
                  pl.BlockSpec((1, s, LANES), lambda i, g: (i, 0, n_grp + g)),
                  pl.BlockSpec((LRU_CONV, LANES), lambda i, g: (0, g)),
                  pl.BlockSpec((1, LANES), lambda i, g: (0, g)),
                  pl.BlockSpec((1, LANES, 4 * LANES), lambda i, g: (g, 0, 0)),
                  pl.BlockSpec((1, 1, 4 * LANES), lambda i, g: (g, 0, 0)),
                  pl.BlockSpec((2, LANES), lambda i, g: (0, g))],
        out_specs=pl.BlockSpec((1, s, LANES), lambda i, g: (i, 0, g)),
        scratch_shapes=[pltpu.VMEM((s + 3 * SUBLANES, LANES), F32), scr, scr, scr, scr],
        compiler_params=_cp(("parallel", "parallel")),
        name="rglru",
    )(p_a, p_a, lp['lru_conv_w'], lp['lru_conv_b'].reshape(1, LRU_WIDTH), wg, bg, lp['lru_lambda'])


S5_R = 4
S5_NSTATE = S5_GROUPS * S5_STATE


def _s5_kernel(xa_ref, xb_ref, winj_ref, wloc_ref, wro_ref, ap_ref, ya_ref, yb_ref, g_ref, *, reverse, n_ctx_tiles):
    n = S5_NSTATE
    nr = g_ref.shape[0]
    x = jnp.concatenate([h[0, pl.ds(i, nr, stride=S5_R), :] for i in range(S5_R) for h in (xa_ref, xb_ref)],
                        axis=1).astype(BF16)
    g_ref[...] = jnp.dot(x, winj_ref[...], preferred_element_type=F32)
    n_tiles = g_ref.shape[0] // SUBLANES
    row = lax.broadcasted_iota(jnp.int32, (SUBLANES, n), 0)
    zeros = jnp.zeros((SUBLANES, n), F32)
    if reverse:
        shift1, e_in, e_out = SUBLANES - 1, SUBLANES - 1, 0
    else:
        shift1, e_in, e_out = 1, 0, SUBLANES - 1

    def make_body(first_tile):
        def body(i, carry):
            hr, hi = carry
            t = (first_tile - i) if reverse else (first_tile + i)
            r = pl.multiple_of(t * SUBLANES, SUBLANES)
            br = g_ref[pl.ds(r, SUBLANES), 0:n]
            bi = g_ref[pl.ds(r, SUBLANES), n:2 * n]
            for k, s in enumerate((1, 2, 4)):
                ar = ap_ref[SUBLANES + k:SUBLANES + k + 1, 0:n]
                ai = ap_ref[SUBLANES + k:SUBLANES + k + 1, n:2 * n]
                if reverse:
                    keep = row < SUBLANES - s
                    shift = SUBLANES - s
                else:
                    keep = row >= s
                    shift = s
                brs = pltpu.roll(br, shift, 0)
                bis = pltpu.roll(bi, shift, 0)
                nr = ar * brs - ai * bis
                ni = ar * bis + ai * brs
                br = br + jnp.where(keep, nr, 0.0)
                bi = bi + jnp.where(keep, ni, 0.0)
            cr = ap_ref[0:SUBLANES, 0:n]
            ci = ap_ref[0:SUBLANES, n:2 * n]
            out_r = br + (cr * hr - ci * hi)
            out_i = bi + (cr * hi + ci * hr)
            g_ref[pl.ds(r, SUBLANES), 0:n] = jnp.where(row == e_in, hr, pltpu.roll(out_r, shift1, 0))
            g_ref[pl.ds(r, SUBLANES), n:2 * n] = jnp.where(row == e_in, hi, pltpu.roll(out_i, shift1, 0))
            return (jnp.broadcast_to(out_r[e_out:e_out + 1, :], (SUBLANES, n)),
                    jnp.broadcast_to(out_i[e_out:e_out + 1, :], (SUBLANES, n)))
        return body

    if reverse:
        carry = lax.fori_loop(0, n_ctx_tiles, make_body(n_ctx_tiles - 1), (zeros, zeros))
        lax.fori_loop(0, n_tiles - n_ctx_tiles, make_body(n_tiles - 1), carry)
    else:
        lax.fori_loop(0, n_tiles, make_body(0), (zeros, zeros))
    y = (jnp.dot(x, wloc_ref[...], preferred_element_type=F32)
         + jnp.dot(g_ref[...].astype(BF16), wro_ref[...], preferred_element_type=F32))
    for i in range(S5_R):
        for k, h in enumerate((ya_ref, yb_ref)):
            h[0, pl.ds(i, nr, stride=S5_R), :] = y[:, (2 * i + k) * LANES:(2 * i + k + 1) * LANES]


def _s5_params(a_re, a_im, log_dt, b_re, b_im, c_re, c_im, reverse):
    rr = S5_R
    dt = jnp.exp(log_dt)[:, None]
    den = a_re * a_re + a_im * a_im
    mag = jnp.exp(dt * a_re)
    ab_re = mag * jnp.cos(dt * a_im)
    ab_im = mag * jnp.sin(dt * a_im)
    f_re = ((ab_re - 1.0) * a_re + ab_im * a_im) / den
    f_im = (ab_im * a_re - (ab_re - 1.0) * a_im) / den
    bb_re = f_re[..., None] * b_re - f_im[..., None] * b_im
    bb_im = f_re[..., None] * b_im + f_im[..., None] * b_re
    grp_tok = (np.arange(rr * S5_WIDTH) // S5_GROUP) % S5_GROUPS
    grp_state = np.arange(S5_NSTATE) // S5_STATE

    def block_diag(t, lead, grp_rows, grp_cols):
        n_lead, minor, ncols = t.shape
        full = jnp.broadcast_to(t[:, None], (n_lead, S5_GROUPS, minor, ncols)).reshape(-1, ncols)
        return jnp.where(jnp.asarray(grp_rows[:, None] == grp_cols[None, :]), full, 0.0)

    def apow(k):
        k = k.astype(F32)[:, None, None]
        m = jnp.exp(k * dt[None] * a_re[None])
        return m * jnp.cos(k * dt[None] * a_im[None]), m * jnp.sin(k * dt[None] * a_im[None])

    steps = jnp.arange(rr)
    rows = rr * S5_WIDTH
    er, ei = apow(steps if reverse else (rr - 1 - steps))
    inj_re = er[..., None] * bb_re[None] - ei[..., None] * bb_im[None]
    inj_im = er[..., None] * bb_im[None] + ei[..., None] * bb_re[None]
    def inj_map(t):
        t = jnp.transpose(t, (0, 3, 1, 2)).reshape(rr, S5_GROUP, S5_NSTATE)
        return block_diag(t, rr, grp_tok, grp_state)

    winj = jnp.concatenate([inj_map(inj_re), inj_map(inj_im)], axis=1)
    fr, fi = apow((rr - steps) if reverse else (steps + 1))
    ro_re = c_re[None] * fr[:, :, None, :] - c_im[None] * fi[:, :, None, :]
    ro_im = c_re[None] * fi[:, :, None, :] + c_im[None] * fr[:, :, None, :]
    def ro_map(t):
        t = jnp.transpose(t, (3, 0, 1, 2)).reshape(1, S5_STATE, rows)
        return block_diag(t, 1, grp_state, grp_tok)

    wro = jnp.concatenate([ro_map(ro_re), -ro_map(ro_im)], axis=0)
    kr, ki = apow(steps)
    ab_r = kr[..., None] * bb_re[None] - ki[..., None] * bb_im[None]
    ab_i = kr[..., None] * bb_im[None] + ki[..., None] * bb_re[None]
    kk = jnp.einsum('gop,kgpc->kgoc', c_re, ab_r) - jnp.einsum('gop,kgpc->kgoc', c_im, ab_i)
    src = jnp.arange(rr)[:, None]
    tgt = jnp.arange(rr)[None, :]
    lag = (src - tgt) if reverse else (tgt - src)
    kmat = jnp.where((lag >= 0)[:, :, None, None, None], kk[jnp.clip(lag, 0, rr - 1)], 0.0)
    wloc = block_diag(jnp.transpose(kmat, (0, 4, 1, 2, 3)).reshape(rr, S5_GROUP, rows), rr, grp_tok, grp_tok)
    i8 = jnp.arange(SUBLANES)
    dist = (SUBLANES - i8) if reverse else (i8 + 1)
    ks = jnp.concatenate([dist, jnp.array([1, 2, 4]), jnp.zeros((5,), dist.dtype)]) * rr
    pr, pi = apow(ks)
    ap = jnp.concatenate([pr.reshape(16, S5_NSTATE), pi.reshape(16, S5_NSTATE)], axis=1)
    return winj.astype(BF16), wloc.astype(BF16), wro.astype(BF16), ap


def _s5_scan(p_d, lp, d, ctx_len):
    b, s, w = p_d.shape
    nr = s // S5_R
    wr = S5_R * w
    reverse = d == 1
    n_ctx_tiles = ctx_len // (S5_R * SUBLANES)
    winj, wloc, wro, ap = _s5_params(lp['s5_a_re'][d], lp['s5_a_im'][d], lp['s5_log_dt'][d], lp['s5_b_re'][d],
                                     lp['s5_b_im'][d], lp['s5_c_re'][d], lp['s5_c_im'][d], reverse)
    assert w == 2 * LANES
    full = lambda shape: pl.BlockSpec(shape, lambda i: (0, 0))
    half = jax.ShapeDtypeStruct((b, s, LANES), F32)
    return pl.pallas_call(
        functools.partial(_s5_kernel, reverse=reverse, n_ctx_tiles=n_ctx_tiles),
        out_shape=(half, half),
        grid=(b,),
        in_specs=[pl.BlockSpec((1, s, LANES), lambda i: (i, 0, 0)), pl.BlockSpec((1, s, LANES), lambda i: (i, 0, 1)),
                  full((wr, 2 * S5_NSTATE)), full((wr, wr)), full((2 * S5_NSTATE, wr)), full((16, 2 * S5_NSTATE))],
        out_specs=(pl.BlockSpec((1, s, LANES), lambda i: (i, 0, 0)), pl.BlockSpec((1, s, LANES), lambda i: (i, 0, 0))),
        scratch_shapes=[pltpu.VMEM((nr, 2 * S5_NSTATE), F32)],
        compiler_params=_cp(("parallel",)),
        name="s5_scan_bwd" if reverse else "s5_scan_fwd",
    )(p_d, p_d, winj, wloc, wro, ap)


def _s5_out_kernel(yfa_ref, yfb_ref, yba_ref, ybb_ref, u_ref, d_ref, w_ref, b_ref, o_ref):
    y = (jnp.concatenate([yfa_ref[...] + yba_ref[...], yfb_ref[...] + ybb_ref[...]], axis=1)
         + d_ref[...] * u_ref[...])
    g = _gelu(y)
    z = jnp.dot(g.astype(BF16), w_ref[...], preferred_element_type=F32) + b_ref[...]
    o_ref[...] = (g * _sigmoid(z)).astype(o_ref.dtype)


def _s5_mixer(p_d, lp, ctx_len):
    b, s, w = p_d.shape
    yfa, yfb = _s5_scan(p_d, lp, 0, ctx_len)
    yba, ybb = _s5_scan(p_d, lp, 1, ctx_len)
    m = b * s
    tm = _pick_tile(m, (1024, 512, 256))
    row = pl.BlockSpec((tm, w), lambda i: (i, 0))
    hrow = pl.BlockSpec((tm, LANES), lambda i: (i, 0))
    vec = pl.BlockSpec((1, w), lambda i: (0, 0))
    flat = lambda a: a.reshape(m, LANES)
    out = pl.pallas_call(
        _s5_out_kernel,
        out_shape=jax.ShapeDtypeStruct((m, w), BF16),
        grid=(m // tm,),
        in_specs=[hrow, hrow, hrow, hrow, row, vec, pl.BlockSpec((w, w), lambda i: (0, 0)), vec],
        out_specs=row,
        compiler_params=_cp(("parallel",)),
        name="s5_out",
    )(flat(yfa), flat(yfb), flat(yba), flat(ybb), p_d.reshape(m, w), lp['s5_d'].reshape(1, w),
      lp['s5_w_glu'].astype(BF16), lp['s5_b_glu'].reshape(1, w))
    return out.reshape(b, s, w)


NA_QROWS = 4


def _natten_plan(rows):
    kr = min(NA_WIN_R, rows)
    span = kr + NA_QROWS - 1
    variants, index, blk_var = [], {}, []
    for blk in range(rows // NA_QROWS):
        r0 = blk * NA_QROWS
        ws = int(np.clip(r0 - kr // 2, 0, rows - span))
        dr = np.zeros((NA_QROWS, span), np.int32)
        ok = np.zeros((NA_QROWS, span), bool)
        for q in range(NA_QROWS):
            r = r0 + q
            rs = int(np.clip(r - kr // 2, 0, rows - kr))
            for i in range(span):
                ok[q, i] = rs <= ws + i < rs + kr
                dr[q, i] = (ws + i - r + (NA_WIN_R - 1)) if ok[q, i] else 0
        key = dr.tobytes() + ok.tobytes()
        if key not in index:
            index[key] = len(variants)
            variants.append((dr, ok))
        blk_var.append(index[key])
    return (np.stack([v[0] for v in variants]), np.stack([v[1] for v in variants]),
            np.asarray(blk_var, np.int32))


def _natten_bias(rpb, dr, ok):
    nv, nq, span = dr.shape
    n_dr, n_dc = 2 * NA_WIN_R - 1, 2 * NA_WIN_C - 1
    w = np.arange(GRID_W)
    cs = np.clip(w - NA_WIN_C // 2, 0, GRID_W - NA_WIN_C)
    ok_col = (w[None, :] >= cs[:, None]) & (w[None, :] < cs[:, None] + NA_WIN_C)
    dc = w[None, :] - w[:, None] + (NA_WIN_C - 1)
    e_dc = (dc[None] == np.arange(n_dc)[:, None, None]).astype(np.float32)
    e_dr = ((dr[..., None] == np.arange(n_dr)) & ok[..., None]).astype(np.float32)
    g = jnp.einsum('vqir,hrk,kwc->vhqwic', e_dr, rpb, e_dc, precision=HIGHEST)
    ok_all = ok[:, None, :, None, :, None] & ok_col[None, None, None, :, None, :]
    g = jnp.where(jnp.asarray(ok_all), g, MASK_VALUE)
    return g.reshape(nv, NA_HEADS, nq * GRID_W, span * GRID_W).astype(F32)


def _attend(q2, keys, vals, biases, lane):
    out = jnp.zeros(q2.shape, F32)
    for hh in range(2):
        sel = (lane >= hh * NA_HEAD_DIM) & (lane < (hh + 1) * NA_HEAD_DIM)
        qm = jnp.where(sel, q2, jnp.zeros_like(q2))
        ss = []
        for k_i, b_i in zip(keys, biases[hh]):
            s_i = lax.dot_general(qm, k_i, (((1,), (1,)), ((), ())), preferred_element_type=F32)
            if b_i is not None:
                s_i = s_i + b_i
            ss.append(s_i)
        m = ss[0].max(axis=-1, keepdims=True)
        for s_i in ss[1:]:
            m = jnp.maximum(m, s_i.max(axis=-1, keepdims=True))
        ps = [jnp.exp(s_i - m) for s_i in ss]
        den = ps[0].sum(axis=-1, keepdims=True)
        for p_i in ps[1:]:
            den = den + p_i.sum(axis=-1, keepdims=True)
        o = jnp.dot(ps[0].astype(BF16), vals[0], preferred_element_type=F32)
        for p_i, v_i in zip(ps[1:], vals[1:]):
            o = o + jnp.dot(p_i.astype(BF16), v_i, preferred_element_type=F32)
        o = o / den
        out = jnp.where(sel[:, :], o, out)
    return out


def _natten_kernel(var_ref, q_ref, k_ref, v_ref, bias_ref, o_ref, *, rows, ctx_len):
    kr = min(NA_WIN_R, rows)
    span = kr + NA_QROWS - 1
    r0 = pl.program_id(1) * NA_QROWS
    ws = jnp.clip(r0 - kr // 2, 0, rows - span)
    base = pl.multiple_of(ctx_len + ws * GRID_W, GRID_W)
    lane = lax.broadcasted_iota(jnp.int32, (NA_QROWS * GRID_W, LANES), 1)
    scale = NA_HEAD_DIM ** -0.5
    for hp in range(NA_HEADS // 2):
        ls = slice(hp * LANES, (hp + 1) * LANES)
        q2 = q_ref[0, :, ls] * scale
        kw = k_ref[0, pl.ds(base, span * GRID_W), ls]
        vw = v_ref[0, pl.ds(base, span * GRID_W), ls]
        kc = k_ref[0, 0:ctx_len, ls]
        vc = v_ref[0, 0:ctx_len, ls]
        biases = [[bias_ref[0, 2 * hp + hh], None] for hh in range(2)]
        out = _attend(q2, [kw, kc], [vw, vc], biases, lane)
        o_ref[0, :, ls] = out.astype(o_ref.dtype)


def _ctx_attn_kernel(q_ref, k_ref, v_ref, o_ref, *, ctx_len):
    lane = lax.broadcasted_iota(jnp.int32, (ctx_len, LANES), 1)
    scale = NA_HEAD_DIM ** -0.5
    for hp in range(NA_HEADS // 2):
        ls = slice(hp * LANES, (hp + 1) * LANES)
        out = _attend(q_ref[0, :, ls] * scale, [k_ref[0, :, ls]], [v_ref[0, :, ls]], [[None], [None]], lane)
        o_ref[0, :, ls] = out.astype(o_ref.dtype)


def _natten_mixer(p_c, rpb, ctx_len, with_ctx):
    b, s, _ = p_c.shape
    l = s - ctx_len
    rows = l // GRID_W
    kr = min(NA_WIN_R, rows)
    span = kr + NA_QROWS - 1
    nq = NA_QROWS * GRID_W
    dr, ok, blk_var = _natten_plan(rows)
    bias = _natten_bias(rpb, dr, ok)
    cb = ctx_len // nq
    grid_spec = pltpu.PrefetchScalarGridSpec(
        num_scalar_prefetch=1,
        grid=(b, rows // NA_QROWS),
        in_specs=[pl.BlockSpec((1, nq, NA_WIDTH), lambda i, r, var: (i, cb + r, 0)),
                  pl.BlockSpec((1, s, NA_WIDTH), lambda i, r, var: (i, 0, 1)),
                  pl.BlockSpec((1, s, NA_WIDTH), lambda i, r, var: (i, 0, 2)),
                  pl.BlockSpec((1, NA_HEADS, nq, span * GRID_W), lambda i, r, var: (var[r], 0, 0, 0))],
        out_specs=pl.BlockSpec((1, nq, NA_WIDTH), lambda i, r, var: (i, r, 0)),
    )
    y_l = pl.pallas_call(
        functools.partial(_natten_kernel, rows=rows, ctx_len=ctx_len),
        out_shape=jax.ShapeDtypeStruct((b, l, NA_WIDTH), BF16),
        grid_spec=grid_spec,
        compiler_params=_cp(("parallel", "arbitrary")),
        name="natten",
    )(jnp.asarray(blk_var), p_c, p_c, p_c, bias)
    if not with_ctx:
        return y_l, None
    y_c = pl.pallas_call(
        functools.partial(_ctx_attn_kernel, ctx_len=ctx_len),
        out_shape=jax.ShapeDtypeStruct((b, ctx_len, NA_WIDTH), BF16),
        grid=(b,),
        in_specs=[pl.BlockSpec((1, ctx_len, NA_WIDTH), lambda i: (i, 0, 0)),
                  pl.BlockSpec((1, ctx_len, NA_WIDTH), lambda i: (i, 0, 1)),
                  pl.BlockSpec((1, ctx_len, NA_WIDTH), lambda i: (i, 0, 2))],
        out_specs=pl.BlockSpec((1, ctx_len, NA_WIDTH), lambda i: (i, 0, 0)),
        compiler_params=_cp(("parallel",)),
        name="ctx_attn",
    )(p_c, p_c, p_c)
    return y_l, y_c


DFT_ROWS = 64


def _dft_gen_kernel(ca_ref, sa_ref, cb_ref, sb_ref, fwd_ref, inv_ref, *, l):
    i = pl.program_id(0)
    ca = ca_ref[0]
    sa = sa_ref[0]
    cb = cb_ref[...]
    sb = sb_ref[...]
    gc = ca * cb - sa * sb
    gs = sa * cb + ca * sb
    x = i * DFT_ROWS + lax.broadcasted_iota(jnp.int32, (DFT_ROWS, l), 0)
    y = lax.broadcasted_iota(jnp.int32, (DFT_ROWS, l), 1)
    n = 2.0 * l
    nyq_x = jnp.where((x & 1) == 0, 1.0, -1.0)
    fwd_ref[:, 0:l] = gc.astype(fwd_ref.dtype)
    fwd_ref[:, l:2 * l] = jnp.where(y == 0, nyq_x, -gs).astype(fwd_ref.dtype)
    scale = jnp.where(x == 0, 1.0 / n, 2.0 / n)
    nyq_y = jnp.where((y & 1) == 0, 1.0, -1.0)
    inv_ref[0] = (scale * gc).astype(inv_ref.dtype)
    inv_ref[1] = (scale * jnp.where(x == 0, nyq_y, -gs)).astype(inv_ref.dtype)


def _dft_matrices(l):
    n = 2 * l
    k1 = l // DFT_ROWS
    y = np.arange(l, dtype=np.int64)
    xa = (DFT_ROWS * np.arange(k1, dtype=np.int64))[:, None]
    xb = np.arange(DFT_ROWS, dtype=np.int64)[:, None]
    ang_a = jnp.asarray(((xa * y[None, :]) % n).astype(np.float32)) * F32(2.0 * math.pi / n)
    ang_b = jnp.asarray(((xb * y[None, :]) % n).astype(np.float32)) * F32(2.0 * math.pi / n)
    ca, sa = jnp.cos(ang_a).reshape(k1, 1, l), jnp.sin(ang_a).reshape(k1, 1, l)
    cb, sb = jnp.cos(ang_b), jnp.sin(ang_b)
    row = pl.BlockSpec((1, 1, l), lambda i: (i, 0, 0))
    tab = pl.BlockSpec((DFT_ROWS, l), lambda i: (0, 0))
    fwd, inv = pl.pallas_call(
        functools.partial(_dft_gen_kernel, l=l),
        out_shape=(jax.ShapeDtypeStruct((l, 2 * l), BF16), jax.ShapeDtypeStruct((2, l, l), BF16)),
        grid=(k1,),
        in_specs=[row, row, tab, tab],
        out_specs=(pl.BlockSpec((DFT_ROWS, 2 * l), lambda i: (i, 0)),
                   pl.BlockSpec((2, DFT_ROWS, l), lambda i: (0, i, 0))),
        compiler_params=_cp(("parallel",)),
        name="dft_gen",
    )(ca, sa, cb, sb)
    return fwd, inv.reshape(2 * l, l)


def _hy_filter_kernel(w1t_ref, w1c_ref, w1s_ref, b1_ref, w2_ref, b2_ref, w3_ref, b3_ref, fr_ref, dl_ref,
                      h_ref, asum_ref, *, l, tl):
    j = pl.program_id(0)
    t = (j * tl + lax.broadcasted_iota(jnp.int32, (1, tl), 1)).astype(F32)
    t_norm = t / l
    bands = (1 + lax.broadcasted_iota(jnp.int32, (HY_BANDS, 1), 0)).astype(F32)
    ang = (2.0 * math.pi / l) * t * bands
    fr = fr_ref[...]
    lin = (w1t_ref[...] * t_norm
           + jnp.dot(w1c_ref[...], jnp.cos(ang), preferred_element_type=F32, precision=HIGHEST)
           + jnp.dot(w1s_ref[...], jnp.sin(ang), preferred_element_type=F32, precision=HIGHEST))
    h = jnp.sin(fr * (lin + b1_ref[...]))
    h = jnp.sin(fr * (jnp.dot(w2_ref[...], h, preferred_element_type=F32, precision=HIGHEST) + b2_ref[...]))
    h = jnp.dot(w3_ref[...], h, preferred_element_type=F32, precision=HIGHEST) + b3_ref[...]
    window = jnp.exp(-t_norm * dl_ref[...])
    first = (j * tl + lax.broadcasted_iota(jnp.int32, (HY_WIDTH, tl), 1)) == 0

    @pl.when(j == 0)
    def _():
        asum_ref[...] = jnp.zeros_like(asum_ref)

    for blk in range(2 * HY_ORDER):
        rs = slice(blk * HY_WIDTH, (blk + 1) * HY_WIDTH)
        hb = h[rs, :] * window
        if blk >= HY_ORDER:
            hb = jnp.where(first, 0.0, hb)
        h_ref[rs, :] = hb
        asum_ref[rs, :] += jnp.sum(jnp.abs(hb), axis=1, keepdims=True)


def _hy_filters(lp, l):
    tl = min(l, 512)
    hid = HY_FILT_HID
    w1 = lp['hy_w1']
    col = lambda v: v.reshape(-1, 1)
    deltas = np.abs(np.linspace(HY_MIN_DECAY, HY_MAX_DECAY, HY_WIDTH, dtype=np.float32)).reshape(-1, 1)
    full = lambda shape: pl.BlockSpec(shape, lambda j: (0, 0))
    n_out = 2 * HY_ORDER * HY_WIDTH
    return pl.pallas_call(
        functools.partial(_hy_filter_kernel, l=l, tl=tl),
        out_shape=(jax.ShapeDtypeStruct((n_out, l), F32), jax.ShapeDtypeStruct((n_out, 1), F32)),
        grid=(l // tl,),
        in_specs=[full((hid, 1)), full((hid, HY_BANDS)), full((hid, HY_BANDS)), full((hid, 1)),
                  full((hid, hid)), full((hid, 1)), full((n_out, hid)), full((n_out, 1)),
                  full((hid, 1)), full((HY_WIDTH, 1))],
        out_specs=(pl.BlockSpec((n_out, tl), lambda j: (0, j)), pl.BlockSpec((n_out, 1), lambda j: (0, 0))),
        compiler_params=_cp(("arbitrary",)),
        name="hyena_filters",
    )(w1[0:1].T, w1[1:1 + HY_BANDS].T, w1[1 + HY_BANDS:].T, col(lp['hy_b1']), lp['hy_w2'].T, col(lp['hy_b2']),
      lp['hy_w3'].T, col(lp['hy_b3']), col(lp['hy_freq']), jnp.asarray(deltas))


def _hy_short_conv_kernel(x_ref, w_ref, b_ref, z_ref, zbf_ref, *, l):
    x = x_ref[0]
    t = lax.broadcasted_iota(jnp.int32, x.shape, 1)
    left = HY_SHORT // 2
    z = jnp.zeros(x.shape, F32) + b_ref[...]
    for k in range(HY_SHORT):
        off = k - left
        if off == 0:
            sh = x
        else:
            sh = pltpu.roll(x, (-off) % l, 1)
            sh = jnp.where((t + off >= 0) & (t + off < l), sh, 0.0)
        z = z + sh * w_ref[:, k:k + 1]
    z_ref[0] = z
    zbf_ref[0] = z.astype(zbf_ref.dtype)


def _hy_conv_kernel(y_ref, fc_ref, fs_ref, ic_ref, is_ref, kfc_ref, kfs_ref, kbc_ref, kbs_ref, n_ref,
                    o_ref, *, nb):
    j = pl.program_id(1)

    @pl.when(j == 0)
    def _():
        o_ref[...] = jnp.zeros_like(o_ref)

    y = y_ref[...]
    zr = jnp.dot(y, fc_ref[...], preferred_element_type=F32)
    zi = jnp.dot(y, fs_ref[...], preferred_element_type=F32)
    inv_n = 1.0 / (n_ref[...] + 1e-6)
    kr = (kfc_ref[...] + kbc_ref[...]) * inv_n
    ki = (kfs_ref[...] - kbs_ref[...]) * inv_n
    tn = kr.shape[1]
    f0 = (j * tn + lax.broadcasted_iota(jnp.int32, kr.shape, 1)) == 0
    ki = jnp.where(f0, (kfs_ref[...] + kbs_ref[...]) * inv_n, ki)
    prs, pis = [], []
    for bb in range(nb):
        rs = slice(bb * HY_WIDTH, (bb + 1) * HY_WIDTH)
        a, b = zr[rs], zi[rs]
        prs.append(a * kr - jnp.where(f0, 0.0, b * ki))
        pis.append(jnp.where(f0, b * ki, a * ki + b * kr))
    pr = jnp.concatenate(prs, axis=0).astype(BF16)
    pi = jnp.concatenate(pis, axis=0).astype(BF16)
    o_ref[...] += (jnp.dot(pr, ic_ref[...], preferred_element_type=F32)
                   + jnp.dot(pi, is_ref[...], preferred_element_type=F32))


def _hy_long_conv(ybf, n_tiles, nb, row_stride, fwd, inv, kf, asum, order, l):
    tmh = nb * HY_WIDTH
    m = n_tiles * tmh
    tn = min(l, 256)
    jn = l // tn
    o_f = order
    o_b = HY_ORDER + order
    return pl.pallas_call(
        functools.partial(_hy_conv_kernel, nb=nb),
        out_shape=jax.ShapeDtypeStruct((m, l), F32),
        grid=(n_tiles, jn),
        in_specs=[pl.BlockSpec((tmh, l), lambda i, j: (i * row_stride, 0)),
                  pl.BlockSpec((l, tn), lambda i, j: (0, j)),
                  pl.BlockSpec((l, tn), lambda i, j: (0, jn + j)),
                  pl.BlockSpec((tn, l), lambda i, j: (j, 0)),
                  pl.BlockSpec((tn, l), lambda i, j: (jn + j, 0)),
                  pl.BlockSpec((HY_WIDTH, tn), lambda i, j: (o_f, j)),
                  pl.BlockSpec((HY_WIDTH, tn), lambda i, j: (o_f, jn + j)),
                  pl.BlockSpec((HY_WIDTH, tn), lambda i, j: (o_b, j)),
                  pl.BlockSpec((HY_WIDTH, tn), lambda i, j: (o_b, jn + j)),
                  pl.BlockSpec((HY_WIDTH, 1), lambda i, j: (order, 0))],
        out_specs=pl.BlockSpec((tmh, l), lambda i, j: (i, 0)),
        compiler_params=_cp(("parallel", "arbitrary")),
        name="hyena_long_conv",
    )(ybf, fwd, fwd, inv, inv, kf, kf, kf, kf, asum)


def _hy_gate_kernel(g_ref, c_ref, y_ref, bias_ref, o_ref, obf_ref):
    o = g_ref[0] * (c_ref[0] + y_ref[0] * bias_ref[...])
    o_ref[0] = o
    obf_ref[0] = o.astype(obf_ref.dtype)


def _hy_gate(z, conv, y, bias_col, gate_blk, y_blk):
    b, _, l = z.shape
    tl = min(l, 1024)
    spec = lambda blk: pl.BlockSpec((1, HY_WIDTH, tl), lambda i, j: (i, blk, j))
    return pl.pallas_call(
        _hy_gate_kernel,
        out_shape=(jax.ShapeDtypeStruct((b, HY_WIDTH, l), F32), jax.ShapeDtypeStruct((b, HY_WIDTH, l), BF16)),
        grid=(b, l // tl),
        in_specs=[spec(gate_blk), spec(0), spec(y_blk), pl.BlockSpec((HY_WIDTH, 1), lambda i, j: (0, 0))],
        out_specs=(spec(0), spec(0)),
        compiler_params=_cp(("parallel", "parallel")),
        name="hyena_gate",
    )(z, conv, y, bias_col)


def _hyena_sequence(p_bt, lp, dft):
    b, c3, l = p_bt.shape
    fwd, inv = dft
    h, asum = _hy_filters(lp, l)
    kf = _matmul(h.astype(BF16), fwd, F32, name="hyena_filter_dft")
    asum2 = asum.reshape(2, HY_ORDER * HY_WIDTH).sum(axis=0).reshape(HY_ORDER * HY_WIDTH, 1)
    blk = pl.BlockSpec((1, LANES, l), lambda i, g: (i, g, 0))
    z, z_bf = pl.pallas_call(
        functools.partial(_hy_short_conv_kernel, l=l),
        out_shape=(jax.ShapeDtypeStruct((b, c3, l), F32), jax.ShapeDtypeStruct((b, c3, l), BF16)),
        grid=(b, c3 // LANES),
        in_specs=[blk,
                  pl.BlockSpec((LANES, HY_SHORT), lambda i, g: (g, 0)),
                  pl.BlockSpec((LANES, 1), lambda i, g: (g, 0))],
        out_specs=(blk, blk),
        compiler_params=_cp(("parallel", "parallel")),
        name="hyena_short_conv",
    )(p_bt, lp['hy_conv_w'].T, lp['hy_conv_b'].reshape(c3, 1))
    n_blk = c3 // HY_WIDTH
    nb2 = 2 if b % 2 == 0 else 1
    conv1 = _hy_long_conv(z_bf.reshape(b * c3, l), b, 1, n_blk, fwd, inv, kf, asum2, 0, l).reshape(b, HY_WIDTH, l)
    y1, y1_bf = _hy_gate(z, conv1, z, lp['hy_bias'][0].reshape(HY_WIDTH, 1), 1, 0)
    conv2 = _hy_long_conv(y1_bf.reshape(b * HY_WIDTH, l), b // nb2, nb2, 1, fwd, inv, kf, asum2, 1, l)
    _, y2_bf = _hy_gate(z, conv2.reshape(b, HY_WIDTH, l), y1, lp['hy_bias'][1].reshape(HY_WIDTH, 1), 2, 0)
    return y2_bf


def _top2_route(logits):
    lane = lax.broadcasted_iota(jnp.int32, logits.shape, 1)
    lg = jnp.where(lane < N_EXPERTS, logits, -jnp.inf)
    v1 = lg.max(axis=-1, keepdims=True)
    i1 = jnp.min(jnp.where(lg == v1, lane, LANES), axis=-1, keepdims=True)
    lg2 = jnp.where(lane == i1, -jnp.inf, lg)
    v2 = lg2.max(axis=-1, keepdims=True)
    i2 = jnp.min(jnp.where(lg2 == v2, lane, LANES), axis=-1, keepdims=True)
    e2 = jnp.exp(v2 - v1)
    w1 = 1.0 / (1.0 + e2)
    w2 = e2 / (1.0 + e2)
    out = jnp.where(lane == 0, i1.astype(F32), 0.0)
    out = jnp.where(lane == 1, i2.astype(F32), out)
    out = jnp.where(lane == 2, w1, out)
    return jnp.where(lane == 3, w2, out)


def _merge_kernel(ya_ref, ybl_ref, ybc_ref, ycl_ref, ycc_ref, yd_ref, pg_ref, x_ref, mod_ref, wa_ref, wb_ref,
                  wc_ref, wd_ref, wo_ref, lng_ref, lnb_ref, *rest, alpha, first_tile, with_router):
    if with_router:
        wr_ref, x_out_ref, u_out_ref, route_ref = rest
    else:
        x_out_ref, u_out_ref = rest
    j = pl.program_id(1) + first_tile
    d = x_ref.shape[-1]
    yb_t = jnp.where(j == 0, ybc_ref[0], ybl_ref[0])
    yb = yb_t.astype(F32).T.astype(BF16)
    yc = jnp.where(j == 0, ycc_ref[0], ycl_ref[0])
    projs = [jnp.dot(ya_ref[0], wa_ref[...], preferred_element_type=F32),
             jnp.dot(yb, wb_ref[...], preferred_element_type=F32),
             jnp.dot(yc, wc_ref[...], preferred_element_type=F32),
             jnp.dot(yd_ref[0], wd_ref[...], preferred_element_type=F32)]
    merged = None
    for i, pr in enumerate(projs):
        term = _sigmoid(pg_ref[0, :, i * d:(i + 1) * d].astype(F32)) * pr
        merged = term if merged is None else merged + term
    m = jnp.dot(merged.astype(BF16), wo_ref[...], preferred_element_type=F32)
    mod = mod_ref[0, 0]
    xn = _layer_norm_rows(alpha * x_ref[0] + mod[2:3, :] * m) * lng_ref[...] + lnb_ref[...]
    x_out_ref[0] = xn
    u = _layer_norm_rows(xn) * (1.0 + mod[4:5, :]) + mod[3:4, :]
    u_out_ref[0] = u.astype(u_out_ref.dtype)
    if with_router:
        logits = jnp.dot(u, wr_ref[...], preferred_element_type=F32, precision=HIGHEST)
        route_ref[0] = _top2_route(logits)


def _merge(ya, yb_lat, yb_ctx, yc_lat, yc_ctx, yd, pg, xs, mod_sel, lp, ln_g, ln_b, alpha, ctx_len, with_ctx,
           u_dtype, w_router=None):
    b, s, d = xs.shape
    tm = ctx_len
    first = 0 if with_ctx else 1
    nt = s // tm - first
    tok = lambda w: pl.BlockSpec((1, tm, w), lambda i, j: (i, j + first, 0))
    lat = lambda j: jnp.maximum(j + first - 1, 0)
    out = lambda w: pl.BlockSpec((1, tm, w), lambda i, j: (i, j, 0))
    full = lambda shape: pl.BlockSpec(shape, lambda i, j: (0,) * len(shape))
    in_specs = [tok(LRU_WIDTH),
                pl.BlockSpec((1, HY_WIDTH, tm), lambda i, j: (i, 0, lat(j))),
                pl.BlockSpec((1, HY_WIDTH, tm), lambda i, j: (i, 0, 0)),
                pl.BlockSpec((1, tm, NA_WIDTH), lambda i, j: (i, lat(j), 0)),
                pl.BlockSpec((1, tm, NA_WIDTH), lambda i, j: (i, 0, 0)),
                tok(S5_WIDTH), tok(N_BRANCH * d), tok(d),
                pl.BlockSpec((1, 1, 6, d), lambda i, j: (i, jnp.minimum(j + first, 1), 0, 0)),
                full((LRU_WIDTH, d)), full((HY_WIDTH, d)), full((NA_WIDTH, d)), full((S5_WIDTH, d)),
                full((d, d)), full((1, d)), full((1, d))]
    args = [ya, yb_lat, yb_ctx, yc_lat, yc_ctx, yd, pg, xs, mod_sel,
            lp['w_br_a'].astype(BF16), lp['w_br_b'].astype(BF16), lp['w_br_c'].astype(BF16),
            lp['w_br_d'].astype(BF16), lp['w_out'].astype(BF16), ln_g.reshape(1, d), ln_b.reshape(1, d)]
    out_shape = [jax.ShapeDtypeStruct((b, nt * tm, d), F32), jax.ShapeDtypeStruct((b, nt * tm, d), u_dtype)]
    out_specs = [out(d), out(d)]
    if w_router is not None:
        in_specs.append(full((d, LANES)))
        args.append(jnp.zeros((d, LANES), F32).at[:, :N_EXPERTS].set(w_router))
        out_shape.append(jax.ShapeDtypeStruct((b, nt * tm, LANES), F32))
        out_specs.append(out(LANES))
    return pl.pallas_call(
        functools.partial(_merge_kernel, alpha=alpha, first_tile=first, with_router=w_router is not None),
        out_shape=tuple(out_shape),
        grid=(b, nt),
        in_specs=in_specs,
        out_specs=tuple(out_specs),
        compiler_params=_cp(("parallel", "parallel")),
        name="merge",
    )(*args)


FFN_TF = 256


def _dense_ffn_kernel(u_ref, x_ref, mod_ref, modn_ref, wg_ref, wu_ref, wd_ref, lng_ref, lnb_ref,
                      xs_ref, un_ref, acc_ref, *, alpha, ctx_len, tiles_per_seq):
    i = pl.program_id(0)
    j = pl.program_id(1)

    @pl.when(j == 0)
    def _():
        acc_ref[...] = jnp.zeros_like(acc_ref)

    u = u_ref[...]
    g = jnp.dot(u, wg_ref[...], preferred_element_type=F32)
    v = jnp.dot(u, wu_ref[...], preferred_element_type=F32)
    h = (g * _sigmoid(g)) * v
    acc_ref[...] += jnp.dot(h.astype(BF16), wd_ref[...], preferred_element_type=F32)

    @pl.when(j == pl.num_programs(1) - 1)
    def _():
        row = lax.broadcasted_iota(jnp.int32, (acc_ref.shape[0], 1), 0)
        is_ctx = (row < ctx_len) & ((i % tiles_per_seq) == 0)
        pick = lambda m, k: jnp.where(is_ctx, m[0, 0, k:k + 1, :], m[0, 1, k:k + 1, :])
        x2 = (_layer_norm_rows(alpha * x_ref[...] + pick(mod_ref, 5) * acc_ref[...]) * lng_ref[...]
              + lnb_ref[...])
        xs_ref[...] = x2
        un_ref[...] = (_layer_norm_rows(x2) * (1.0 + pick(modn_ref, 1)) + pick(modn_ref, 0)).astype(un_ref.dtype)


def _dense_ffn(u2, x1, mod_sel, mod_next, w_gate, w_up, w_down, ln_g, ln_b, alpha, ctx_len):
    b, s, d = x1.shape
    ff = w_gate.shape[1]
    tiles_per_seq = 4
    tm = s // tiles_per_seq
    assert tm % 16 == 0 and tm >= ctx_len
    tf = FFN_TF
    rows = pl.BlockSpec((tm, d), lambda i, j: (i, 0))
    modspec = pl.BlockSpec((1, 2, 6, d), lambda i, j: (i // tiles_per_seq, 0, 0, 0))
    vec = pl.BlockSpec((1, d), lambda i, j: (0, 0))
    xs, un = pl.pallas_call(
        functools.partial(_dense_ffn_kernel, alpha=alpha, ctx_len=ctx_len, tiles_per_seq=tiles_per_seq),
        out_shape=(jax.ShapeDtypeStruct((b * s, d), F32), jax.ShapeDtypeStruct((b * s, d), BF16)),
        grid=(b * tiles_per_seq, ff // tf),
        in_specs=[rows, rows, modspec, modspec,
                  pl.BlockSpec((d, tf), lambda i, j: (0, j)), pl.BlockSpec((d, tf), lambda i, j: (0, j)),
                  pl.BlockSpec((tf, d), lambda i, j: (j, 0)), vec, vec],
        out_specs=(rows, rows),
        scratch_shapes=[pltpu.VMEM((tm, d), F32)],
        compiler_params=_cp(("parallel", "arbitrary")),
        name="dense_swiglu_ln",
    )(u2.reshape(b * s, d), x1.reshape(b * s, d), mod_sel, mod_next, w_gate, w_up, w_down,
      ln_g.reshape(1, d), ln_b.reshape(1, d))
    return xs.reshape(b, s, d), un.reshape(b, s, d)


MOE_TM = 512
MOE_TF = 512


def _moe_kernel(te_ref, nu_ref, dst_ref, u_hbm, wg_ref, wu_ref, wd_ref, out_hbm,
                xbuf, xbf, acc, gsem, ssem, *, n_rows):
    tm = MOE_TM
    n_tok = n_rows // TOP_K
    i = pl.program_id(0)
    j = pl.program_id(1)
    nt = pl.num_programs(0)
    nj = pl.num_programs(1)
    n_used = nu_ref[0]
    slot = i % 2

    def gather_start(tile, sl):
        base = tile * tm

        def body(r, carry):
            p = dst_ref[base + r]
            row = jnp.where(p >= n_tok, p - n_tok, jnp.maximum(p, 0))
            pltpu.make_async_copy(u_hbm.at[pl.ds(row, 1)], xbuf.at[sl, pl.ds(r, 1)], gsem.at[sl]).start()
            return carry

        lax.fori_loop(0, tm, body, 0, unroll=8)

    def gather_wait(sl):
        pltpu.make_async_copy(u_hbm.at[pl.ds(0, tm)], xbuf.at[sl], gsem.at[sl]).wait()

    def scatter_wait():
        pltpu.make_async_copy(acc.at[0], out_hbm.at[pl.ds(0, tm)], ssem.at[0]).wait()

    used = i < n_used

    @pl.when(used & (j == 0))
    def _():
        @pl.when(i == 0)
        def _():
            gather_start(0, 0)
            acc[1] = jnp.zeros((tm, acc.shape[2]), F32)
            dump = pltpu.make_async_copy(acc.at[1], out_hbm.at[pl.ds(n_rows, tm)], ssem.at[0])
            dump.start()
            dump.wait()

        gather_wait(slot)

        @pl.when(i + 1 < n_used)
        def _():
            gather_start(i + 1, 1 - slot)

        xbf[...] = xbuf[slot].astype(BF16)
        acc[slot] = jnp.zeros((tm, acc.shape[2]), F32)

    @pl.when(used)
    def _():
        x = xbf[...]
        g = jnp.dot(x, wg_ref[0], preferred_element_type=F32)
        u = jnp.dot(x, wu_ref[0], preferred_element_type=F32)
        h = (g * _sigmoid(g)) * u
        acc[slot] += jnp.dot(h.astype(BF16), wd_ref[0], preferred_element_type=F32)

    @pl.when(used & (j == nj - 1))
    def _():
        @pl.when(i > 0)
        def _():
            scatter_wait()

        base = i * tm

        def body(r, carry):
            p = dst_ref[base + r]
            row = jnp.where(p >= 0, p, n_rows + r)
            pltpu.make_async_copy(acc.at[slot, pl.ds(r, 1)], out_hbm.at[pl.ds(row, 1)], ssem.at[0]).start()
            return carry

        lax.fori_loop(0, tm, body, 0, unroll=8)

    @pl.when((i == nt - 1) & (j == nj - 1))
    def _():
        scatter_wait()


def _moe_experts(u_rows, dst, tile_expert, n_used, w_gate, w_up, w_down):
    r, d = u_rows.shape
    ff = w_gate.shape[2]
    tm, tf = MOE_TM, MOE_TF
    nt = dst.shape[0] // tm
    nj = ff // tf
    n_rows = TOP_K * r

    def jmap(i, j, nu):
        return jnp.where(i < nu[0], j, nj - 1)

    grid_spec = pltpu.PrefetchScalarGridSpec(
        num_scalar_prefetch=3,
        grid=(nt, nj),
        in_specs=[pl.BlockSpec(memory_space=pl.ANY),
                  pl.BlockSpec((1, d, tf), lambda i, j, te, nu, ds: (te[i], 0, jmap(i, j, nu))),
                  pl.BlockSpec((1, d, tf), lambda i, j, te, nu, ds: (te[i], 0, jmap(i, j, nu))),
                  pl.BlockSpec((1, tf, d), lambda i, j, te, nu, ds: (te[i], jmap(i, j, nu), 0))],
        out_specs=pl.BlockSpec(memory_space=pl.ANY),
        scratch_shapes=[pltpu.VMEM((2, tm, d), F32), pltpu.VMEM((tm, d), BF16), pltpu.VMEM((2, tm, d), F32),
                        pltpu.SemaphoreType.DMA((2,)), pltpu.SemaphoreType.DMA((1,))],
    )
    return pl.pallas_call(
        functools.partial(_moe_kernel, n_rows=n_rows),
        out_shape=jax.ShapeDtypeStruct((n_rows + tm, d), F32),
        grid_spec=grid_spec,
        compiler_params=_cp(("arbitrary", "arbitrary")),
        name="moe_experts",
    )(tile_expert, n_used, dst, u_rows, w_gate, w_up, w_down)


def _moe_combine_kernel(x_ref, y1_ref, y2_ref, w_ref, mod_ref, lng_ref, lnb_ref, o_ref, *, alpha):
    mod = mod_ref[0, 0]
    w = w_ref[0]
    f = w[:, 2:3] * y1_ref[...] + w[:, 3:4] * y2_ref[...]
    o_ref[0] = _layer_norm_rows(alpha * x_ref[0] + mod[5:6, :] * f) * lng_ref[...] + lnb_ref[...]


def _moe_ffn(x_lat, u_lat, route, mod_sel, lp_moe, ln_g, ln_b, alpha):
    b, l, d = x_lat.shape
    t = b * l
    tm = MOE_TM
    ids = route[..., 0:TOP_K].astype(jnp.int32).reshape(t * TOP_K)
    onehot = (ids[:, None] == jnp.arange(N_EXPERTS)[None, :]).astype(jnp.int32)
    csum = jnp.cumsum(onehot, axis=0)
    rank = jnp.take_along_axis(csum, ids[:, None], axis=1)[:, 0] - 1
    counts = csum[-1]
    padded = ((counts + tm - 1) // tm) * tm
    ends = jnp.cumsum(padded)
    starts = ends - padded
    slot = starts[ids] + rank
    n_slots = t * TOP_K + N_EXPERTS * tm
    nt = n_slots // tm
    pair = jnp.arange(t * TOP_K, dtype=jnp.int32)
    dst = jnp.full((n_slots,), -1, jnp.int32).at[slot].set((pair % TOP_K) * t + pair // TOP_K)
    tile_start = jnp.arange(nt, dtype=jnp.int32) * tm
    tile_expert = jnp.minimum(jnp.sum(tile_start[:, None] >= ends[None, :], axis=1), N_EXPERTS - 1).astype(jnp.int32)
    n_used = (ends[-1] // tm).astype(jnp.int32).reshape(1)
    last_e = tile_expert[jnp.maximum(n_used[0] - 1, 0)]
    tile_expert = jnp.where(jnp.arange(nt) < n_used[0], tile_expert, last_e)
    y = _moe_experts(u_lat.reshape(t, d), dst, tile_expert, n_used, lp_moe['w_gate'], lp_moe['w_up'],
                     lp_moe['w_down'])
    tmc = 256
    nl = l // tmc
    vec = pl.BlockSpec((1, d), lambda i, j: (0, 0))
    return pl.pallas_call(
        functools.partial(_moe_combine_kernel, alpha=alpha),
        out_shape=jax.ShapeDtypeStruct((b, l, d), F32),
        grid=(b, nl),
        in_specs=[pl.BlockSpec((1, tmc, d), lambda i, j: (i, j, 0)),
                  pl.BlockSpec((tmc, d), lambda i, j: (i * nl + j, 0)),
                  pl.BlockSpec((tmc, d), lambda i, j: (b * nl + i * nl + j, 0)),
                  pl.BlockSpec((1, tmc, LANES), lambda i, j: (i, j, 0)),
                  pl.BlockSpec((1, 1, 6, d), lambda i, j: (i, 1, 0, 0)), vec, vec],
        out_specs=pl.BlockSpec((1, tmc, d), lambda i, j: (i, j, 0)),
        compiler_params=_cp(("parallel", "parallel")),
        name="moe_combine_ln",
    )(x_lat, y, y, route, mod_sel, ln_g.reshape(1, d), ln_b.reshape(1, d))


def kernel(x, c, ctx, c_ctx, w_mod, b_mod, w_in, lru_conv_w, lru_conv_b, lru_w_r, lru_b_r, lru_w_i, lru_b_i, lru_lambda, hy_conv_w, hy_conv_b, hy_w1, hy_b1, hy_w2, hy_b2, hy_w3, hy_b3, hy_freq, hy_bias, na_rpb, s5_a_re, s5_a_im, s5_log_dt, s5_b_re, s5_b_im, s5_c_re, s5_c_im, s5_d, s5_w_glu, s5_b_glu, w_br_a, w_br_b, w_br_c, w_br_d, w_out, ln1_g, ln1_b, ln2_g, ln2_b, ff_w_gate, ff_w_up, ff_w_down, moe_router, moe_w_gate, moe_w_up, moe_w_down):
    bsz, l, d = x.shape
    ctx_len = ctx.shape[1]
    depth = w_in.shape[0]
    s = ctx_len + l
    alpha = (2.0 * depth) ** 0.25
    xs = jnp.concatenate([ctx, x], axis=1)
    c_rows = jnp.zeros((SUBLANES, d), F32).at[0:bsz].set(c).at[bsz].set(c_ctx)
    dft_lat = _dft_matrices(l)
    dft_ctx = _dft_matrices(ctx_len) if depth > 1 else None
    assert depth == 2
    mod_all = _mod_vectors(c_rows, w_mod, b_mod).reshape(depth, SUBLANES, 6, d)
    mods = [jnp.stack([jnp.broadcast_to(mod_all[li, bsz], (bsz, 6, d)), mod_all[li, 0:bsz]], axis=1)
            for li in range(depth)]
    mods.append(mods[-1])
    u1 = _ln_mod(xs, mods[0], 0, 1, ctx_len)

    for li in range(depth):
        with_ctx = li < depth - 1
        lp = {
            'lru_conv_w': lru_conv_w[li], 'lru_conv_b': lru_conv_b[li], 'lru_w_r': lru_w_r[li],
            'lru_b_r': lru_b_r[li], 'lru_w_i': lru_w_i[li], 'lru_b_i': lru_b_i[li], 'lru_lambda': lru_lambda[li],
            'hy_conv_w': hy_conv_w[li], 'hy_conv_b': hy_conv_b[li], 'hy_w1': hy_w1[li], 'hy_b1': hy_b1[li],
            'hy_w2': hy_w2[li], 'hy_b2': hy_b2[li], 'hy_w3': hy_w3[li], 'hy_b3': hy_b3[li],
            'hy_freq': hy_freq[li], 'hy_bias': hy_bias[li],
            's5_a_re': s5_a_re[li], 's5_a_im': s5_a_im[li], 's5_log_dt': s5_log_dt[li], 's5_b_re': s5_b_re[li],
            's5_b_im': s5_b_im[li], 's5_c_re': s5_c_re[li], 's5_c_im': s5_c_im[li], 's5_d': s5_d[li],
            's5_w_glu': s5_w_glu[li], 's5_b_glu': s5_b_glu[li], 'w_br_a': w_br_a[li], 'w_br_b': w_br_b[li],
            'w_br_c': w_br_c[li], 'w_br_d': w_br_d[li], 'w_out': w_out[li],
        }
        mod_sel = mods[li]

        u1f = u1.reshape(bsz * s, d)
        wi = w_in[li].astype(BF16)
        p_a = _matmul(u1f, wi[:, OFF_A:OFF_B], F32, name="proj_lru").reshape(bsz, s, OFF_B - OFF_A)
        p_c = _matmul(u1f, wi[:, OFF_C:OFF_D], BF16, name="proj_natten").reshape(bsz, s, OFF_D - OFF_C)
        p_d = _matmul(u1f, wi[:, OFF_D:OFF_G], F32, name="proj_s5").reshape(bsz, s, OFF_G - OFF_D)
        p_g = _matmul(u1f, wi[:, OFF_G:], BF16, name="proj_gates").reshape(bsz, s, N_BRANCH * d)
        wi_bt = wi[:, OFF_B:OFF_C].T

        ya = _lru_mixer(p_a, lp, ctx_len)
        yb_lat = _hyena_sequence(_matmul_nt(wi_bt, u1, F32, ctx_len, l), lp, dft_lat)
        if with_ctx:
            yb_ctx = _hyena_sequence(_matmul_nt(wi_bt, u1, F32, 0, ctx_len), lp, dft_ctx)
        else:
            yb_ctx = jnp.zeros((bsz, HY_WIDTH, ctx_len), BF16)
        yc_l, yc_c = _natten_mixer(p_c, na_rpb[li], ctx_len, with_ctx)
        if yc_c is None:
            yc_c = jnp.zeros((bsz, ctx_len, NA_WIDTH), BF16)
        yd = _s5_mixer(p_d, lp, ctx_len)
        e = li // 2
        if li % 2 == 0:
            assert with_ctx
            x1, u2 = _merge(ya, yb_lat, yb_ctx, yc_l, yc_c, yd, p_g, xs, mod_sel, lp, ln1_g[li], ln1_b[li], alpha,
                            ctx_len, True, BF16)
            xs, u1 = _dense_ffn(u2, x1, mod_sel, mods[li + 1], ff_w_gate[e].astype(BF16), ff_w_up[e].astype(BF16),
                                ff_w_down[e].astype(BF16), ln2_g[li], ln2_b[li], alpha, ctx_len)
        else:
            assert not with_ctx
            x1, u2, route = _merge(ya, yb_lat, yb_ctx, yc_l, yc_c, yd, p_g, xs, mod_sel, lp, ln1_g[li], ln1_b[li],
                                   alpha, ctx_len, False, F32, moe_router[e])
            lp_moe = {'w_gate': moe_w_gate[e].astype(BF16), 'w_up': moe_w_up[e].astype(BF16),
                      'w_down': moe_w_down[e].astype(BF16)}
            return _moe_ffn(x1, u2, route, mod_sel, lp_moe, ln2_g[li], ln2_b[li], alpha)
```

```python
import functools
import math

import numpy as np
import jax
import jax.numpy as jnp
from jax import lax
from jax.experimental import pallas as pl
from jax.experimental.pallas import tpu as pltpu

F32 = jnp.float32
BF16 = jnp.bfloat16
HIGHEST = lax.Precision.HIGHEST

LRU_WIDTH = 384
LRU_BLOCK = 64
LRU_CONV = 4
LRU_C = 8.0
HY_WIDTH = 256
HY_ORDER = 2
HY_SHORT = 3
HY_BANDS = 16
HY_FILT_HID = 64
HY_MAX_DECAY = math.log(1e-2) / 0.3
HY_MIN_DECAY = math.log(1e-2) / 1.5
NA_HEADS = 6
NA_HEAD_DIM = 64
NA_WIDTH = NA_HEADS * NA_HEAD_DIM
NA_WIN_R = 8
NA_WIN_C = 16
GRID_W = 64
S5_WIDTH = 256
S5_GROUP = 16
S5_GROUPS = 16
S5_STATE = 64
N_BRANCH = 4
OFF_A = 0
OFF_B = OFF_A + 2 * LRU_WIDTH
OFF_C = OFF_B + 3 * HY_WIDTH
OFF_D = OFF_C + 3 * NA_WIDTH
OFF_G = OFF_D + S5_WIDTH
N_EXPERTS = 8
TOP_K = 2
LN_EPS = 1e-5
MASK_VALUE = -1e30

LANES = 128
SUBLANES = 8
VMEM_LIMIT = 56 * 1024 * 1024


def _cp(sem, vmem=VMEM_LIMIT):
    return pltpu.CompilerParams(dimension_semantics=sem, vmem_limit_bytes=vmem)


def _gelu(x):
    return 0.5 * x * (1.0 + jnp.tanh(math.sqrt(2.0 / math.pi) * (x + 0.044715 * (x * x * x))))


def _sigmoid(x):
    return 0.5 + 0.5 * jnp.tanh(0.5 * x)


def _layer_norm_rows(x):
    mu = jnp.mean(x, axis=-1, keepdims=True)
    xc = x - mu
    var = jnp.mean(xc * xc, axis=-1, keepdims=True)
    return xc * lax.rsqrt(var + LN_EPS)


def _mod_kernel(c_ref, w_ref, b_ref, o_ref):
    c = c_ref[...]
    a = c * _sigmoid(c)
    o_ref[0] = jnp.dot(a, w_ref[0], preferred_element_type=F32, precision=HIGHEST) + b_ref[0]


def _mod_vectors(c_rows, w_mod, b_mod):
    d = c_rows.shape[1]
    depth, _, n = w_mod.shape
    tn = 1536
    return pl.pallas_call(
        _mod_kernel,
        out_shape=jax.ShapeDtypeStruct((depth, SUBLANES, n), F32),
        grid=(depth, n // tn),
        in_specs=[pl.BlockSpec((SUBLANES, d), lambda l, j: (0, 0)),
                  pl.BlockSpec((1, d, tn), lambda l, j: (l, 0, j)),
                  pl.BlockSpec((1, 1, tn), lambda l, j: (l, 0, j))],
        out_specs=pl.BlockSpec((1, SUBLANES, tn), lambda l, j: (l, 0, j)),
        compiler_params=_cp(("arbitrary", "arbitrary")),
        name="mod_vectors",
    )(c_rows, w_mod, b_mod.reshape(depth, 1, n))


def _ln_mod_kernel(x_ref, mod_ref, o_ref, *, shift_idx, scale_idx):
    y = _layer_norm_rows(x_ref[0])
    m = mod_ref[0, 0]
    o = y * (1.0 + m[scale_idx:scale_idx + 1, :]) + m[shift_idx:shift_idx + 1, :]
    o_ref[0] = o.astype(o_ref.dtype)


def _ln_mod(xs, mod_sel, shift_idx, scale_idx, ctx_len):
    b, s, d = xs.shape
    tm = ctx_len
    return pl.pallas_call(
        functools.partial(_ln_mod_kernel, shift_idx=shift_idx, scale_idx=scale_idx),
        out_shape=jax.ShapeDtypeStruct((b, s, d), BF16),
        grid=(b, s // tm),
        in_specs=[pl.BlockSpec((1, tm, d), lambda i, j: (i, j, 0)),
                  pl.BlockSpec((1, 1, 6, d), lambda i, j: (i, jnp.minimum(j, 1), 0, 0))],
        out_specs=pl.BlockSpec((1, tm, d), lambda i, j: (i, j, 0)),
        compiler_params=_cp(("parallel", "parallel")),
        name="ln_mod",
    )(xs, mod_sel)


def _mm_kernel(a_ref, w_ref, o_ref):
    o_ref[...] = jnp.dot(a_ref[...], w_ref[...], preferred_element_type=F32).astype(o_ref.dtype)


def _pick_tile(n, prefs):
    for t in prefs:
        if n % t == 0:
            return t
    return n


def _matmul(a, w, out_dtype, tm=None, tn=None, name="matmul"):
    m, k = a.shape
    n = w.shape[1]
    tm = tm or _pick_tile(m, (1024, 512, 256, 128))
    tn = tn or (n if n <= 1536 else _pick_tile(n, (1024, 768, 512, 384, 256, 128)))
    return pl.pallas_call(
        _mm_kernel,
        out_shape=jax.ShapeDtypeStruct((m, n), out_dtype),
        grid=(m // tm, n // tn),
        in_specs=[pl.BlockSpec((tm, k), lambda i, j: (i, 0)),
                  pl.BlockSpec((k, tn), lambda i, j: (0, j))],
        out_specs=pl.BlockSpec((tm, tn), lambda i, j: (i, j)),
        compiler_params=_cp(("parallel", "parallel")),
        name=name,
    )(a, w)


def _mm_nt_kernel(w_ref, u_ref, o_ref):
    o_ref[0] = lax.dot_general(w_ref[...], u_ref[0], (((1,), (1,)), ((), ())),
                               preferred_element_type=F32).astype(o_ref.dtype)


def _matmul_nt(w_t, u, out_dtype, tok0, ntok, tn=256):
    c, k = w_t.shape
    b = u.shape[0]
    j0 = tok0 // tn
    return pl.pallas_call(
        _mm_nt_kernel,
        out_shape=jax.ShapeDtypeStruct((b, c, ntok), out_dtype),
        grid=(b, ntok // tn),
        in_specs=[pl.BlockSpec((c, k), lambda i, j: (0, 0)),
                  pl.BlockSpec((1, tn, k), lambda i, j: (i, j + j0, 0))],
        out_specs=pl.BlockSpec((1, c, tn), lambda i, j: (i, 0, j)),
        compiler_params=_cp(("parallel", "parallel")),
        name="matmul_nt",
    )(w_t, u)


LRU_CHUNK = 128


def _tile_scan(a, b, row, reverse):
    for s in (1, 2, 4):
        if reverse:
            keep = row < SUBLANES - s
            shift = SUBLANES - s
        else:
            keep = row >= s
            shift = s
        a_sh = pltpu.roll(a, shift, 0)
        b_sh = pltpu.roll(b, shift, 0)
        b = jnp.where(keep, a * b_sh, 0.0) + b
        a = jnp.where(keep, a * a_sh, a)
    return a, b


def _lru_kernel(pg_ref, px_ref, cw_ref, cb_ref, wg_ref, bg_ref, lam_ref, y_ref,
                xpad, a_f, b_f, a_b, b_b, *, s_len, ctx_len):
    ch = LRU_CHUNK
    n_chunks = s_len // ch
    zeros8 = jnp.zeros((SUBLANES, LANES), F32)
    xpad[0:SUBLANES, :] = zeros8
    xpad[ctx_len + SUBLANES:ctx_len + 2 * SUBLANES, :] = zeros8
    xpad[s_len + 2 * SUBLANES:s_len + 3 * SUBLANES, :] = zeros8

    def pad_row(r):
        return pl.multiple_of(r + jnp.where(r >= ctx_len, 2 * SUBLANES, SUBLANES), SUBLANES)

    def copy_body(i, carry):
        r = pl.multiple_of(i * ch, ch)
        xpad[pl.ds(pad_row(r), ch), :] = px_ref[0, pl.ds(r, ch), :]
        return carry

    lax.fori_loop(0, n_chunks, copy_body, 0)

    lam = lam_ref[...]
    sp = jnp.log(1.0 + jnp.exp(-lam))

    def gates_body(i, carry):
        r = pl.multiple_of(i * ch, ch)
        win = xpad[pl.ds(pad_row(r) - SUBLANES, ch + 2 * SUBLANES), :]
        xc = jnp.zeros((ch, LANES), F32) + cb_ref[...]
        for k in range(LRU_CONV):
            off = k - LRU_CONV // 2
            xc = xc + win[SUBLANES + off:SUBLANES + off + ch, :] * cw_ref[k:k + 1, :]
        gl = jnp.dot(xc.astype(BF16), wg_ref[0], preferred_element_type=F32) + bg_ref[0]
        for d, (a_s, b_s) in enumerate(((a_f, b_f), (a_b, b_b))):
            g_r = _sigmoid(gl[:, d * 2 * LANES:d * 2 * LANES + LANES])
            g_i = _sigmoid(gl[:, d * 2 * LANES + LANES:(d + 1) * 2 * LANES])
            log_a = (-LRU_C) * g_r * sp[d:d + 1, :]
            a = jnp.exp(log_a)
            bb = jnp.sqrt(1.0 - a * a) * g_i * xc
            a_s[pl.ds(r, ch), :] = a
            b_s[pl.ds(r, ch), :] = bb
        return carry

    lax.fori_loop(0, n_chunks, gates_body, 0)

    row = lax.broadcasted_iota(jnp.int32, (SUBLANES, LANES), 0)

    n_ctx_tiles = ctx_len // SUBLANES
    n_tiles = s_len // SUBLANES

    def scan_body(i, carry):
        h_f, h_b = carry
        r = pl.multiple_of(i * SUBLANES, SUBLANES)
        a, b = _tile_scan(a_f[pl.ds(r, SUBLANES), :], b_f[pl.ds(r, SUBLANES), :], row, False)
        hf = b + a * h_f
        b_f[pl.ds(r, SUBLANES), :] = hf
        t = jnp.where(i < n_ctx_tiles, n_ctx_tiles - 1 - i, n_tiles + n_ctx_tiles - 1 - i)
        rb = pl.multiple_of(t * SUBLANES, SUBLANES)
        a, b = _tile_scan(a_b[pl.ds(rb, SUBLANES), :], b_b[pl.ds(rb, SUBLANES), :], row, True)
        hb = b + a * h_b
        b_b[pl.ds(rb, SUBLANES), :] = hb
        return (jnp.broadcast_to(hf[SUBLANES - 1:SUBLANES, :], (SUBLANES, LANES)),
                jnp.broadcast_to(hb[0:1, :], (SUBLANES, LANES)))

    lax.fori_loop(0, n_tiles, scan_body, (zeros8, zeros8), unroll=2)

    def out_body(i, carry):
        r = pl.multiple_of(i * ch, ch)
        g = _gelu(pg_ref[0, pl.ds(r, ch), :])
        y = g * (b_f[pl.ds(r, ch), :] + b_b[pl.ds(r, ch), :])
        y_ref[0, pl.ds(r, ch), :] = y.astype(y_ref.dtype)
        return carry

    lax.fori_loop(0, n_chunks, out_body, 0)


def _lru_gate_weights(w_r, w_i, b_r, b_i):
    n_grp = LRU_WIDTH // LANES
    per = LANES // LRU_BLOCK

    def bd(w):
        w = w.reshape(n_grp, per, LRU_BLOCK, LRU_BLOCK)
        z = jnp.zeros((n_grp, LRU_BLOCK, LRU_BLOCK), w.dtype)
        top = jnp.concatenate([w[:, 0], z], axis=2)
        bot = jnp.concatenate([z, w[:, 1]], axis=2)
        return jnp.concatenate([top, bot], axis=1)

    wg = jnp.concatenate([bd(w_r[0]), bd(w_i[0]), bd(w_r[1]), bd(w_i[1])], axis=2).astype(BF16)
    bg = jnp.stack([b_r[0], b_i[0], b_r[1], b_i[1]], axis=0).reshape(4, n_grp, LANES)
    bg = jnp.transpose(bg, (1, 0, 2)).reshape(n_grp, 1, 4 * LANES)
    return wg, bg


def _lru_mixer(p_a, lp, ctx_len):
    b, s, _ = p_a.shape
    n_grp = LRU_WIDTH // LANES
    wg, bg = _lru_gate_weights(lp['lru_w_r'], lp['lru_w_i'], lp['lru_b_r'], lp['lru_b_i'])
    scr = pltpu.VMEM((s, LANES), F32)
    return pl.pallas_call(
        functools.partial(_lru_kernel, s_len=s, ctx_len=ctx_len),
        out_shape=jax.ShapeDtypeStruct((b, s, LRU_WIDTH), BF16),
        grid=(b, n_grp),
        in_specs=[pl.BlockSpec((1, s, LANES), lambda i, g: (i, 0, g)),
                  pl.BlockSpec((1, s, LANES), lambda i, g: (i, 0, n_grp + g)),
                  pl.BlockSpec((LRU_CONV, LANES), lambda i, g: (0, g)),
                  pl.BlockSpec((1, LANES), lambda i, g: (0, g)),
                  pl.BlockSpec((1, LANES, 4 * LANES), lambda i, g: (g, 0, 0)),
                  pl.BlockSpec((1, 1, 4 * LANES), lambda i, g: (g, 0, 0)),
                  pl.BlockSpec((2, LANES), lambda i, g: (0, g))],
        out_specs=pl.BlockSpec((1, s, LANES), lambda i, g: (i, 0, g)),
        scratch_shapes=[pltpu.VMEM((s + 3 * SUBLANES, LANES), F32), scr, scr, scr, scr],
        compiler_params=_cp(("parallel", "parallel")),
        name="rglru",
    )(p_a, p_a, lp['lru_conv_w'], lp['lru_conv_b'].reshape(1, LRU_WIDTH), wg, bg, lp['lru_lambda'])


S5_R = 4
S5_NSTATE = S5_GROUPS * S5_STATE


def _s5_kernel(xa_ref, xb_ref, winj_ref, wloc_ref, wro_ref, ap_ref, ya_ref, yb_ref, g_ref, *, reverse, n_ctx_tiles):
    n = S5_NSTATE
    nr = g_ref.shape[0]
    x = jnp.concatenate([h[0, pl.ds(i, nr, stride=S5_R), :] for i in range(S5_R) for h in (xa_ref, xb_ref)],
                        axis=1).astype(BF16)
    g_ref[...] = jnp.dot(x, winj_ref[...], preferred_element_type=F32)
    n_tiles = g_ref.shape[0] // SUBLANES
    row = lax.broadcasted_iota(jnp.int32, (SUBLANES, n), 0)
    zeros = jnp.zeros((SUBLANES, n), F32)
    if reverse:
        shift1, e_in, e_out = SUBLANES - 1, SUBLANES - 1, 0
    else:
        shift1, e_in, e_out = 1, 0, SUBLANES - 1

    def make_body(first_tile):
        def body(i, carry):
            hr, hi = carry
            t = (first_tile - i) if reverse else (first_tile + i)
            r = pl.multiple_of(t * SUBLANES, SUBLANES)
            br = g_ref[pl.ds(r, SUBLANES), 0:n]
            bi = g_ref[pl.ds(r, SUBLANES), n:2 * n]
            for k, s in enumerate((1, 2, 4)):
                ar = ap_ref[SUBLANES + k:SUBLANES + k + 1, 0:n]
                ai = ap_ref[SUBLANES + k:SUBLANES + k + 1, n:2 * n]
                if reverse:
                    keep = row < SUBLANES - s
                    shift = SUBLANES - s
                else:
                    keep = row >= s
                    shift = s
                brs = pltpu.roll(br, shift, 0)
                bis = pltpu.roll(bi, shift, 0)
                nr = ar * brs - ai * bis
                ni = ar * bis + ai * brs
                br = br + jnp.where(keep, nr, 0.0)
                bi = bi + jnp.where(keep, ni, 0.0)
            cr = ap_ref[0:SUBLANES, 0:n]
            ci = ap_ref[0:SUBLANES, n:2 * n]
            out_r = br + (cr * hr - ci * hi)
            out_i = bi + (cr * hi + ci * hr)
            g_ref[pl.ds(r, SUBLANES), 0:n] = jnp.where(row == e_in, hr, pltpu.roll(out_r, shift1, 0))
            g_ref[pl.ds(r, SUBLANES), n:2 * n] = jnp.where(row == e_in, hi, pltpu.roll(out_i, shift1, 0))
            return (jnp.broadcast_to(out_r[e_out:e_out + 1, :], (SUBLANES, n)),
                    jnp.broadcast_to(out_i[e_out:e_out + 1, :], (SUBLANES, n)))
        return body

    if reverse:
        carry = lax.fori_loop(0, n_ctx_tiles, make_body(n_ctx_tiles - 1), (zeros, zeros))
        lax.fori_loop(0, n_tiles - n_ctx_tiles, make_body(n_tiles - 1), carry)
    else:
        lax.fori_loop(0, n_tiles, make_body(0), (zeros, zeros))
    y = (jnp.dot(x, wloc_ref[...], preferred_element_type=F32)
         + jnp.dot(g_ref[...].astype(BF16), wro_ref[...], preferred_element_type=F32))
    for i in range(S5_R):
        for k, h in enumerate((ya_ref, yb_ref)):
            h[0, pl.ds(i, nr, stride=S5_R), :] = y[:, (2 * i + k) * LANES:(2 * i + k + 1) * LANES]


def _s5_params(a_re, a_im, log_dt, b_re, b_im, c_re, c_im, reverse):
    rr = S5_R
    dt = jnp.exp(log_dt)[:, None]
    den = a_re * a_re + a_im * a_im
    mag = jnp.exp(dt * a_re)
    ab_re = mag * jnp.cos(dt * a_im)
    ab_im = mag * jnp.sin(dt * a_im)
    f_re = ((ab_re - 1.0) * a_re + ab_im * a_im) / den
    f_im = (ab_im * a_re - (ab_re - 1.0) * a_im) / den
    bb_re = f_re[..., None] * b_re - f_im[..., None] * b_im
    bb_im = f_re[..., None] * b_im + f_im[..., None] * b_re
    grp_tok = (np.arange(rr * S5_WIDTH) // S5_GROUP) % S5_GROUPS
    grp_state = np.arange(S5_NSTATE) // S5_STATE

    def block_diag(t, lead, grp_rows, grp_cols):
        n_lead, minor, ncols = t.shape
        full = jnp.broadcast_to(t[:, None], (n_lead, S5_GROUPS, minor, ncols)).reshape(-1, ncols)
        return jnp.where(jnp.asarray(grp_rows[:, None] == grp_cols[None, :]), full, 0.0)

    def apow(k):
        k = k.astype(F32)[:, None, None]
        m = jnp.exp(k * dt[None] * a_re[None])
        return m * jnp.cos(k * dt[None] * a_im[None]), m * jnp.sin(k * dt[None] * a_im[None])

    steps = jnp.arange(rr)
    rows = rr * S5_WIDTH
    er, ei = apow(steps if reverse else (rr - 1 - steps))
    inj_re = er[..., None] * bb_re[None] - ei[..., None] * bb_im[None]
    inj_im = er[..., None] * bb_im[None] + ei[..., None] * bb_re[None]
    def inj_map(t):
        t = jnp.transpose(t, (0, 3, 1, 2)).reshape(rr, S5_GROUP, S5_NSTATE)
        return block_diag(t, rr, grp_tok, grp_state)

    winj = jnp.concatenate([inj_map(inj_re), inj_map(inj_im)], axis=1)
    fr, fi = apow((rr - steps) if reverse else (steps + 1))
    ro_re = c_re[None] * fr[:, :, None, :] - c_im[None] * fi[:, :, None, :]
    ro_im = c_re[None] * fi[:, :, None, :] + c_im[None] * fr[:, :, None, :]
    def ro_map(t):
        t = jnp.transpose(t, (3, 0, 1, 2)).reshape(1, S5_STATE, rows)
        return block_diag(t, 1, grp_state, grp_tok)

    wro = jnp.concatenate([ro_map(ro_re), -ro_map(ro_im)], axis=0)
    kr, ki = apow(steps)
    ab_r = kr[..., None] * bb_re[None] - ki[..., None] * bb_im[None]
    ab_i = kr[..., None] * bb_im[None] + ki[..., None] * bb_re[None]
    kk = jnp.einsum('gop,kgpc->kgoc', c_re, ab_r) - jnp.einsum('gop,kgpc->kgoc', c_im, ab_i)
    src = jnp.arange(rr)[:, None]
    tgt = jnp.arange(rr)[None, :]
    lag = (src - tgt) if reverse else (tgt - src)
    kmat = jnp.where((lag >= 0)[:, :, None, None, None], kk[jnp.clip(lag, 0, rr - 1)], 0.0)
    wloc = block_diag(jnp.transpose(kmat, (0, 4, 1, 2, 3)).reshape(rr, S5_GROUP, rows), rr, grp_tok, grp_tok)
    i8 = jnp.arange(SUBLANES)
    dist = (SUBLANES - i8) if reverse else (i8 + 1)
    ks = jnp.concatenate([dist, jnp.array([1, 2, 4]), jnp.zeros((5,), dist.dtype)]) * rr
    pr, pi = apow(ks)
    ap = jnp.concatenate([pr.reshape(16, S5_NSTATE), pi.reshape(16, S5_NSTATE)], axis=1)
    return winj.astype(BF16), wloc.astype(BF16), wro.astype(BF16), ap


def _s5_scan(p_d, lp, d, ctx_len):
    b, s, w = p_d.shape
    nr = s // S5_R
    wr = S5_R * w
    reverse = d == 1
    n_ctx_tiles = ctx_len // (S5_R * SUBLANES)
    winj, wloc, wro, ap = _s5_params(lp['s5_a_re'][d], lp['s5_a_im'][d], lp['s5_log_dt'][d], lp['s5_b_re'][d],
                                     lp['s5_b_im'][d], lp['s5_c_re'][d], lp['s5_c_im'][d], reverse)
    assert w == 2 * LANES
    full = lambda shape: pl.BlockSpec(shape, lambda i: (0, 0))
    half = jax.ShapeDtypeStruct((b, s, LANES), F32)
    return pl.pallas_call(
        functools.partial(_s5_kernel, reverse=reverse, n_ctx_tiles=n_ctx_tiles),
        out_shape=(half, half),
        grid=(b,),
        in_specs=[pl.BlockSpec((1, s, LANES), lambda i: (i, 0, 0)), pl.BlockSpec((1, s, LANES), lambda i: (i, 0, 1)),
                  full((wr, 2 * S5_NSTATE)), full((wr, wr)), full((2 * S5_NSTATE, wr)), full((16, 2 * S5_NSTATE))],
        out_specs=(pl.BlockSpec((1, s, LANES), lambda i: (i, 0, 0)), pl.BlockSpec((1, s, LANES), lambda i: (i, 0, 0))),
        scratch_shapes=[pltpu.VMEM((nr, 2 * S5_NSTATE), F32)],
        compiler_params=_cp(("parallel",)),
        name="s5_scan_bwd" if reverse else "s5_scan_fwd",
    )(p_d, p_d, winj, wloc, wro, ap)


def _s5_out_kernel(yfa_ref, yfb_ref, yba_ref, ybb_ref, u_ref, d_ref, w_ref, b_ref, o_ref):
    y = (jnp.concatenate([yfa_ref[...] + yba_ref[...], yfb_ref[...] + ybb_ref[...]], axis=1)
         + d_ref[...] * u_ref[...])
    g = _gelu(y)
    z = jnp.dot(g.astype(BF16), w_ref[...], preferred_element_type=F32) + b_ref[...]
    o_ref[...] = (g * _sigmoid(z)).astype(o_ref.dtype)


def _s5_mixer(p_d, lp, ctx_len):
    b, s, w = p_d.shape
    yfa, yfb = _s5_scan(p_d, lp, 0, ctx_len)
    yba, ybb = _s5_scan(p_d, lp, 1, ctx_len)
    m = b * s
    tm = _pick_tile(m, (1024, 512, 256))
    row = pl.BlockSpec((tm, w), lambda i: (i, 0))
    hrow = pl.BlockSpec((tm, LANES), lambda i: (i, 0))
    vec = pl.BlockSpec((1, w), lambda i: (0, 0))
    flat = lambda a: a.reshape(m, LANES)
    out = pl.pallas_call(
        _s5_out_kernel,
        out_shape=jax.ShapeDtypeStruct((m, w), BF16),
        grid=(m // tm,),
        in_specs=[hrow, hrow, hrow, hrow, row, vec, pl.BlockSpec((w, w), lambda i: (0, 0)), vec],
        out_specs=row,
        compiler_params=_cp(("parallel",)),
        name="s5_out",
    )(flat(yfa), flat(yfb), flat(yba), flat(ybb), p_d.reshape(m, w), lp['s5_d'].reshape(1, w),
      lp['s5_w_glu'].astype(BF16), lp['s5_b_glu'].reshape(1, w))
    return out.reshape(b, s, w)


NA_QROWS = 4


def _natten_plan(rows):
    kr = min(NA_WIN_R, rows)
    span = kr + NA_QROWS - 1
    variants, index, blk_var = [], {}, []
    for blk in range(rows // NA_QROWS):
        r0 = blk * NA_QROWS
        ws = int(np.clip(r0 - kr // 2, 0, rows - span))
        dr = np.zeros((NA_QROWS, span), np.int32)
        ok = np.zeros((NA_QROWS, span), bool)
        for q in range(NA_QROWS):
            r = r0 + q
            rs = int(np.clip(r - kr // 2, 0, rows - kr))
            for i in range(span):
                ok[q, i] = rs <= ws + i < rs + kr
                dr[q, i] = (ws + i - r + (NA_WIN_R - 1)) if ok[q, i] else 0
        key = dr.tobytes() + ok.tobytes()
        if key not in index:
            index[key] = len(variants)
            variants.append((dr, ok))
        blk_var.append(index[key])
    return (np.stack([v[0] for v in variants]), np.stack([v[1] for v in variants]),
            np.asarray(blk_var, np.int32))


def _natten_bias(rpb, dr, ok):
    nv, nq, span = dr.shape
    n_dr, n_dc = 2 * NA_WIN_R - 1, 2 * NA_WIN_C - 1
    w = np.arange(GRID_W)
    cs = np.clip(w - NA_WIN_C // 2, 0, GRID_W - NA_WIN_C)
    ok_col = (w[None, :] >= cs[:, None]) & (w[None, :] < cs[:, None] + NA_WIN_C)
    dc = w[None, :] - w[:, None] + (NA_WIN_C - 1)
    e_dc = (dc[None] == np.arange(n_dc)[:, None, None]).astype(np.float32)
    e_dr = ((dr[..., None] == np.arange(n_dr)) & ok[..., None]).astype(np.float32)
    g = jnp.einsum('vqir,hrk,kwc->vhqwic', e_dr, rpb, e_dc, precision=HIGHEST)
    ok_all = ok[:, None, :, None, :, None] & ok_col[None, None, None, :, None, :]
    g = jnp.where(jnp.asarray(ok_all), g, MASK_VALUE)
    return g.reshape(nv, NA_HEADS, nq * GRID_W, span * GRID_W).astype(F32)


def _attend(q2, keys, vals, biases, lane):
    out = jnp.zeros(q2.shape, F32)
    for hh in range(2):
        sel = (lane >= hh * NA_HEAD_DIM) & (lane < (hh + 1) * NA_HEAD_DIM)
        qm = jnp.where(sel, q2, jnp.zeros_like(q2))
        ss = []
        for k_i, b_i in zip(keys, biases[hh]):
            s_i = lax.dot_general(qm, k_i, (((1,), (1,)), ((), ())), preferred_element_type=F32)
            if b_i is not None:
                s_i = s_i + b_i
            ss.append(s_i)
        m = ss[0].max(axis=-1, keepdims=True)
        for s_i in ss[1:]:
            m = jnp.maximum(m, s_i.max(axis=-1, keepdims=True))
        ps = [jnp.exp(s_i - m) for s_i in ss]
        den = ps[0].sum(axis=-1, keepdims=True)
        for p_i in ps[1:]:
            den = den + p_i.sum(axis=-1, keepdims=True)
        o = jnp.dot(ps[0].astype(BF16), vals[0], preferred_element_type=F32)
        for p_i, v_i in zip(ps[1:], vals[1:]):
            o = o + jnp.dot(p_i.astype(BF16), v_i, preferred_element_type=F32)
        o = o / den
        out = jnp.where(sel[:, :], o, out)
    return out


def _natten_kernel(var_ref, q_ref, k_ref, v_ref, bias_ref, o_ref, *, rows, ctx_len):
    kr = min(NA_WIN_R, rows)
    span = kr + NA_QROWS - 1
    r0 = pl.program_id(1) * NA_QROWS
    ws = jnp.clip(r0 - kr // 2, 0, rows - span)
    base = pl.multiple_of(ctx_len + ws * GRID_W, GRID_W)
    lane = lax.broadcasted_iota(jnp.int32, (NA_QROWS * GRID_W, LANES), 1)
    scale = NA_HEAD_DIM ** -0.5
    for hp in range(NA_HEADS // 2):
        ls = slice(hp * LANES, (hp + 1) * LANES)
        q2 = q_ref[0, :, ls] * scale
        kw = k_ref[0, pl.ds(base, span * GRID_W), ls]
        vw = v_ref[0, pl.ds(base, span * GRID_W), ls]
        kc = k_ref[0, 0:ctx_len, ls]
        vc = v_ref[0, 0:ctx_len, ls]
        biases = [[bias_ref[0, 2 * hp + hh], None] for hh in range(2)]
        out = _attend(q2, [kw, kc], [vw, vc], biases, lane)
        o_ref[0, :, ls] = out.astype(o_ref.dtype)


def _ctx_attn_kernel(q_ref, k_ref, v_ref, o_ref, *, ctx_len):
    lane = lax.broadcasted_iota(jnp.int32, (ctx_len, LANES), 1)
    scale = NA_HEAD_DIM ** -0.5
    for hp in range(NA_HEADS // 2):
        ls = slice(hp * LANES, (hp + 1) * LANES)
        out = _attend(q_ref[0, :, ls] * scale, [k_ref[0, :, ls]], [v_ref[0, :, ls]], [[None], [None]], lane)
        o_ref[0, :, ls] = out.astype(o_ref.dtype)


def _natten_mixer(p_c, rpb, ctx_len, with_ctx):
    b, s, _ = p_c.shape
    l = s - ctx_len
    rows = l // GRID_W
    kr = min(NA_WIN_R, rows)
    span = kr + NA_QROWS - 1
    nq = NA_QROWS * GRID_W
    dr, ok, blk_var = _natten_plan(rows)
    bias = _natten_bias(rpb, dr, ok)
    cb = ctx_len // nq
    grid_spec = pltpu.PrefetchScalarGridSpec(
        num_scalar_prefetch=1,
        grid=(b, rows // NA_QROWS),
        in_specs=[pl.BlockSpec((1, nq, NA_WIDTH), lambda i, r, var: (i, cb + r, 0)),
                  pl.BlockSpec((1, s, NA_WIDTH), lambda i, r, var: (i, 0, 1)),
                  pl.BlockSpec((1, s, NA_WIDTH), lambda i, r, var: (i, 0, 2)),
                  pl.BlockSpec((1, NA_HEADS, nq, span * GRID_W), lambda i, r, var: (var[r], 0, 0, 0))],
        out_specs=pl.BlockSpec((1, nq, NA_WIDTH), lambda i, r, var: (i, r, 0)),
    )
    y_l = pl.pallas_call(
        functools.partial(_natten_kernel, rows=rows, ctx_len=ctx_len),
        out_shape=jax.ShapeDtypeStruct((b, l, NA_WIDTH), BF16),
        grid_spec=grid_spec,
        compiler_params=_cp(("parallel", "arbitrary")),
        name="natten",
    )(jnp.asarray(blk_var), p_c, p_c, p_c, bias)
    if not with_ctx:
        return y_l, None
    y_c = pl.pallas_call(
        functools.partial(_ctx_attn_kernel, ctx_len=ctx_len),
        out_shape=jax.ShapeDtypeStruct((b, ctx_len, NA_WIDTH), BF16),
        grid=(b,),
        in_specs=[pl.BlockSpec((1, ctx_len, NA_WIDTH), lambda i: (i, 0, 0)),
                  pl.BlockSpec((1, ctx_len, NA_WIDTH), lambda i: (i, 0, 1)),
                  pl.BlockSpec((1, ctx_len, NA_WIDTH), lambda i: (i, 0, 2))],
        out_specs=pl.BlockSpec((1, ctx_len, NA_WIDTH), lambda i: (i, 0, 0)),
        compiler_params=_cp(("parallel",)),
        name="ctx_attn",
    )(p_c, p_c, p_c)
    return y_l, y_c


DFT_ROWS = 64


def _dft_gen_kernel(ca_ref, sa_ref, cb_ref, sb_ref, fwd_ref, inv_ref, *, l):
    i = pl.program_id(0)
    ca = ca_ref[0]
    sa = sa_ref[0]
    cb = cb_ref[...]
    sb = sb_ref[...]
    gc = ca * cb - sa * sb
    gs = sa * cb + ca * sb
    x = i * DFT_ROWS + lax.broadcasted_iota(jnp.int32, (DFT_ROWS, l), 0)
    y = lax.broadcasted_iota(jnp.int32, (DFT_ROWS, l), 1)
    n = 2.0 * l
    nyq_x = jnp.where((x & 1) == 0, 1.0, -1.0)
    fwd_ref[:, 0:l] = gc.astype(fwd_ref.dtype)
    fwd_ref[:, l:2 * l] = jnp.where(y == 0, nyq_x, -gs).astype(fwd_ref.dtype)
    scale = jnp.where(x == 0, 1.0 / n, 2.0 / n)
    nyq_y = jnp.where((y & 1) == 0, 1.0, -1.0)
    inv_ref[0] = (scale * gc).astype(inv_ref.dtype)
    inv_ref[1] = (scale * jnp.where(x == 0, nyq_y, -gs)).astype(inv_ref.dtype)


def _dft_matrices(l):
    n = 2 * l
    k1 = l // DFT_ROWS
    y = np.arange(l, dtype=np.int64)
    xa = (DFT_ROWS * np.arange(k1, dtype=np.int64))[:, None]
    xb = np.arange(DFT_ROWS, dtype=np.int64)[:, None]
    ang_a = jnp.asarray(((xa * y[None, :]) % n).astype(np.float32)) * F32(2.0 * math.pi / n)
    ang_b = jnp.asarray(((xb * y[None, :]) % n).astype(np.float32)) * F32(2.0 * math.pi / n)
    ca, sa = jnp.cos(ang_a).reshape(k1, 1, l), jnp.sin(ang_a).reshape(k1, 1, l)
    cb, sb = jnp.cos(ang_b), jnp.sin(ang_b)
    row = pl.BlockSpec((1, 1, l), lambda i: (i, 0, 0))
    tab = pl.BlockSpec((DFT_ROWS, l), lambda i: (0, 0))
    fwd, inv = pl.pallas_call(
        functools.partial(_dft_gen_kernel, l=l),
        out_shape=(jax.ShapeDtypeStruct((l, 2 * l), BF16), jax.ShapeDtypeStruct((2, l, l), BF16)),
        grid=(k1,),
        in_specs=[row, row, tab, tab],
        out_specs=(pl.BlockSpec((DFT_ROWS, 2 * l), lambda i: (i, 0)),
                   pl.BlockSpec((2, DFT_ROWS, l), lambda i: (0, i, 0))),
        compiler_params=_cp(("parallel",)),
        name="dft_gen",
    )(ca, sa, cb, sb)
    return fwd, inv.reshape(2 * l, l)


def _hy_filter_kernel(w1t_ref, w1c_ref, w1s_ref, b1_ref, w2_ref, b2_ref, w3_ref, b3_ref, fr_ref, dl_ref,
                      h_ref, asum_ref, *, l, tl):
    j = pl.program_id(0)
    t = (j * tl + lax.broadcasted_iota(jnp.int32, (1, tl), 1)).astype(F32)
    t_norm = t / l
    bands = (1 + lax.broadcasted_iota(jnp.int32, (HY_BANDS, 1), 0)).astype(F32)
    ang = (2.0 * math.pi / l) * t * bands
    fr = fr_ref[...]
    lin = (w1t_ref[...] * t_norm
           + jnp.dot(w1c_ref[...], jnp.cos(ang), preferred_element_type=F32, precision=HIGHEST)
           + jnp.dot(w1s_ref[...], jnp.sin(ang), preferred_element_type=F32, precision=HIGHEST))
    h = jnp.sin(fr * (lin + b1_ref[...]))
    h = jnp.sin(fr * (jnp.dot(w2_ref[...], h, preferred_element_type=F32, precision=HIGHEST) + b2_ref[...]))
    h = jnp.dot(w3_ref[...], h, preferred_element_type=F32, precision=HIGHEST) + b3_ref[...]
    window = jnp.exp(-t_norm * dl_ref[...])
    first = (j * tl + lax.broadcasted_iota(jnp.int32, (HY_WIDTH, tl), 1)) == 0

    @pl.when(j == 0)
    def _():
        asum_ref[...] = jnp.zeros_like(asum_ref)

    for blk in range(2 * HY_ORDER):
        rs = slice(blk * HY_WIDTH, (blk + 1) * HY_WIDTH)
        hb = h[rs, :] * window
        if blk >= HY_ORDER:
            hb = jnp.where(first, 0.0, hb)
        h_ref[rs, :] = hb
        asum_ref[rs, :] += jnp.sum(jnp.abs(hb), axis=1, keepdims=True)


def _hy_filters(lp, l):
    tl = min(l, 512)
    hid = HY_FILT_HID
    w1 = lp['hy_w1']
    col = lambda v: v.reshape(-1, 1)
    deltas = np.abs(np.linspace(HY_MIN_DECAY, HY_MAX_DECAY, HY_WIDTH, dtype=np.float32)).reshape(-1, 1)
    full = lambda shape: pl.BlockSpec(shape, lambda j: (0, 0))
    n_out = 2 * HY_ORDER * HY_WIDTH
    return pl.pallas_call(
        functools.partial(_hy_filter_kernel, l=l, tl=tl),
        out_shape=(jax.ShapeDtypeStruct((n_out, l), F32), jax.ShapeDtypeStruct((n_out, 1), F32)),
        grid=(l // tl,),
        in_specs=[full((hid, 1)), full((hid, HY_BANDS)), full((hid, HY_BANDS)), full((hid, 1)),
                  full((hid, hid)), full((hid, 1)), full((n_out, hid)), full((n_out, 1)),
                  full((hid, 1)), full((HY_WIDTH, 1))],
        out_specs=(pl.BlockSpec((n_out, tl), lambda j: (0, j)), pl.BlockSpec((n_out, 1), lambda j: (0, 0))),
        compiler_params=_cp(("arbitrary",)),
        name="hyena_filters",
    )(w1[0:1].T, w1[1:1 + HY_BANDS].T, w1[1 + HY_BANDS:].T, col(lp['hy_b1']), lp['hy_w2'].T, col(lp['hy_b2']),
      lp['hy_w3'].T, col(lp['hy_b3']), col(lp['hy_freq']), jnp.asarray(deltas))


def _hy_short_conv_kernel(x_ref, w_ref, b_ref, z_ref, zbf_ref, *, l):
    x = x_ref[0]
    t = lax.broadcasted_iota(jnp.int32, x.shape, 1)
    left = HY_SHORT // 2
    z = jnp.zeros(x.shape, F32) + b_ref[...]
    for k in range(HY_SHORT):
        off = k - left
        if off == 0:
            sh = x
        else:
            sh = pltpu.roll(x, (-off) % l, 1)
            sh = jnp.where((t + off >= 0) & (t + off < l), sh, 0.0)
        z = z + sh * w_ref[:, k:k + 1]
    z_ref[0] = z
    zbf_ref[0] = z.astype(zbf_ref.dtype)


def _hy_conv_kernel(y_ref, fc_ref, fs_ref, ic_ref, is_ref, kfc_ref, kfs_ref, kbc_ref, kbs_ref, n_ref,
                    o_ref, *, nb):
    j = pl.program_id(1)

    @pl.when(j == 0)
    def _():
        o_ref[...] = jnp.zeros_like(o_ref)

    y = y_ref[...]
    zr = jnp.dot(y, fc_ref[...], preferred_element_type=F32)
    zi = jnp.dot(y, fs_ref[...], preferred_element_type=F32)
    inv_n = 1.0 / (n_ref[...] + 1e-6)
    kr = (kfc_ref[...] + kbc_ref[...]) * inv_n
    ki = (kfs_ref[...] - kbs_ref[...]) * inv_n
    tn = kr.shape[1]
    f0 = (j * tn + lax.broadcasted_iota(jnp.int32, kr.shape, 1)) == 0
    ki = jnp.where(f0, (kfs_ref[...] + kbs_ref[...]) * inv_n, ki)
    prs, pis = [], []
    for bb in range(nb):
        rs = slice(bb * HY_WIDTH, (bb + 1) * HY_WIDTH)
        a, b = zr[rs], zi[rs]
        prs.append(a * kr - jnp.where(f0, 0.0, b * ki))
        pis.append(jnp.where(f0, b * ki, a * ki + b * kr))
    pr = jnp.concatenate(prs, axis=0).astype(BF16)
    pi = jnp.concatenate(pis, axis=0).astype(BF16)
    o_ref[...] += (jnp.dot(pr, ic_ref[...], preferred_element_type=F32)
                   + jnp.dot(pi, is_ref[...], preferred_element_type=F32))


def _hy_long_conv(ybf, n_tiles, nb, row_stride, fwd, inv, kf, asum, order, l):
    tmh = nb * HY_WIDTH
    m = n_tiles * tmh
    tn = min(l, 256)
    jn = l // tn
    o_f = order
    o_b = HY_ORDER + order
    return pl.pallas_call(
        functools.partial(_hy_conv_kernel, nb=nb),
        out_shape=jax.ShapeDtypeStruct((m, l), F32),
        grid=(n_tiles, jn),
        in_specs=[pl.BlockSpec((tmh, l), lambda i, j: (i * row_stride, 0)),
                  pl.BlockSpec((l, tn), lambda i, j: (0, j)),
                  pl.BlockSpec((l, tn), lambda i, j: (0, jn + j)),
                  pl.BlockSpec((tn, l), lambda i, j: (j, 0)),
                  pl.BlockSpec((tn, l), lambda i, j: (jn + j, 0)),
                  pl.BlockSpec((HY_WIDTH, tn), lambda i, j: (o_f, j)),
                  pl.BlockSpec((HY_WIDTH, tn), lambda i, j: (o_f, jn + j)),
                  pl.BlockSpec((HY_WIDTH, tn), lambda i, j: (o_b, j)),
                  pl.BlockSpec((HY_WIDTH, tn), lambda i, j: (o_b, jn + j)),
                  pl.BlockSpec((HY_WIDTH, 1), lambda i, j: (order, 0))],
        out_specs=pl.BlockSpec((tmh, l), lambda i, j: (i, 0)),
        compiler_params=_cp(("parallel", "arbitrary")),
        name="hyena_long_conv",
    )(ybf, fwd, fwd, inv, inv, kf, kf, kf, kf, asum)


def _hy_gate_kernel(g_ref, c_ref, y_ref, bias_ref, o_ref, obf_ref):
    o = g_ref[0] * (c_ref[0] + y_ref[0] * bias_ref[...])
    o_ref[0] = o
    obf_ref[0] = o.astype(obf_ref.dtype)


def _hy_gate(z, conv, y, bias_col, gate_blk, y_blk):
    b, _, l = z.shape
    tl = min(l, 1024)
    spec = lambda blk: pl.BlockSpec((1, HY_WIDTH, tl), lambda i, j: (i, blk, j))
    return pl.pallas_call(
        _hy_gate_kernel,
        out_shape=(jax.ShapeDtypeStruct((b, HY_WIDTH, l), F32), jax.ShapeDtypeStruct((b, HY_WIDTH, l), BF16)),
        grid=(b, l // tl),
        in_specs=[spec(gate_blk), spec(0), spec(y_blk), pl.BlockSpec((HY_WIDTH, 1), lambda i, j: (0, 0))],
        out_specs=(spec(0), spec(0)),
        compiler_params=_cp(("parallel", "parallel")),
        name="hyena_gate",
    )(z, conv, y, bias_col)


def _hyena_sequence(p_bt, lp, dft):
    b, c3, l = p_bt.shape
    fwd, inv = dft
    h, asum = _hy_filters(lp, l)
    kf = _matmul(h.astype(BF16), fwd, F32, name="hyena_filter_dft")
    asum2 = asum.reshape(2, HY_ORDER * HY_WIDTH).sum(axis=0).reshape(HY_ORDER * HY_WIDTH, 1)
    blk = pl.BlockSpec((1, LANES, l), lambda i, g: (i, g, 0))
    z, z_bf = pl.pallas_call(
        functools.partial(_hy_short_conv_kernel, l=l),
        out_shape=(jax.ShapeDtypeStruct((b, c3, l), F32), jax.ShapeDtypeStruct((b, c3, l), BF16)),
        grid=(b, c3 // LANES),
        in_specs=[blk,
                  pl.BlockSpec((LANES, HY_SHORT), lambda i, g: (g, 0)),
                  pl.BlockSpec((LANES, 1), lambda i, g: (g, 0))],
        out_specs=(blk, blk),
        compiler_params=_cp(("parallel", "parallel")),
        name="hyena_short_conv",
    )(p_bt, lp['hy_conv_w'].T, lp['hy_conv_b'].reshape(c3, 1))
    n_blk = c3 // HY_WIDTH
    nb2 = 2 if b % 2 == 0 else 1
    conv1 = _hy_long_conv(z_bf.reshape(b * c3, l), b, 1, n_blk, fwd, inv, kf, asum2, 0, l).reshape(b, HY_WIDTH, l)
    y1, y1_bf = _hy_gate(z, conv1, z, lp['hy_bias'][0].reshape(HY_WIDTH, 1), 1, 0)
    conv2 = _hy_long_conv(y1_bf.reshape(b * HY_WIDTH, l), b // nb2, nb2, 1, fwd, inv, kf, asum2, 1, l)
    _, y2_bf = _hy_gate(z, conv2.reshape(b, HY_WIDTH, l), y1, lp['hy_bias'][1].reshape(HY_WIDTH, 1), 2, 0)
    return y2_bf


FFT_N2 = 128
FFT_KB = 8
FFT_NB = 8


def _fft_stage1(xa_ref, xb_ref, f1_ref, za, zb, n_in, n1):
    f1 = f1_ref[...]
    for c0 in range(0, FFT_N2, FFT_NB):
        cols = []
        for q in range(FFT_NB):
            cols.append(xa_ref[0, pl.ds(c0 + q, n_in, stride=FFT_N2), :])
            cols.append(xb_ref[0, pl.ds(c0 + q, n_in, stride=FFT_N2), :])
        z = jnp.dot(f1, jnp.concatenate(cols, axis=1).astype(BF16), preferred_element_type=F32)
        for q in range(FFT_NB):
            za[pl.ds(c0 + q, 2 * n1, stride=FFT_N2), :] = z[:, (2 * q) * LANES:(2 * q + 1) * LANES]
            zb[pl.ds(c0 + q, 2 * n1, stride=FFT_N2), :] = z[:, (2 * q + 1) * LANES:(2 * q + 2) * LANES]


def _fft_load_z(za, zb, k1, n1):
    r_re = pl.multiple_of(k1 * FFT_N2, FFT_N2)
    r_im = pl.multiple_of((n1 + k1) * FFT_N2, FFT_N2)
    z = jnp.concatenate(
        [jnp.concatenate([za[pl.ds(r_re, FFT_N2), :], zb[pl.ds(r_re, FFT_N2), :]], axis=1),
         jnp.concatenate([za[pl.ds(r_im, FFT_N2), :], zb[pl.ds(r_im, FFT_N2), :]], axis=1)], axis=0)
    return z.astype(BF16), r_re, r_im


def _hy_fft_filter_kernel(xa_ref, xb_ref, f1_ref, ef_ref, y_ref, za, zb, *, n1):
    j = pl.program_id(1)

    @pl.when(j == 0)
    def _():
        _fft_stage1(xa_ref, xb_ref, f1_ref, za, zb, n1, n1)

    for q in range(FFT_KB):
        z, _, _ = _fft_load_z(za, zb, j * FFT_KB + q, n1)
        y = jnp.dot(ef_ref[q], z, preferred_element_type=F32)
        y_ref[0, 0, q] = y[:FFT_N2]
        y_ref[0, 1, q] = y[FFT_N2:]


def _hy_fft_conv_kernel(xa_ref, xb_ref, ga_ref, gb_ref, bias_ref, f1_ref, f1i_ref, ef_ref, ei_ref, kf_ref, n_ref,
                        oa_ref, ob_ref, za, zb, *, n1):
    n_in = n1 // 2
    j = pl.program_id(1)

    @pl.when(j == 0)
    def _():
        _fft_stage1(xa_ref, xb_ref, f1_ref, za, zb, n_in, n1)

    inv_n = 1.0 / (n_ref[0] + 1e-6)
    for q in range(FFT_KB):
        z, r_re, r_im = _fft_load_z(za, zb, j * FFT_KB + q, n1)
        y = jnp.dot(ef_ref[q], z, preferred_element_type=F32)
        kr = kf_ref[0, 0, q] * inv_n
        ki = kf_ref[0, 1, q] * inv_n
        yr, yi = y[:FFT_N2], y[FFT_N2:]
        p = jnp.concatenate([yr * kr - yi * ki, yr * ki + yi * kr], axis=0).astype(BF16)
        u = jnp.dot(ei_ref[q], p, preferred_element_type=F32)
        za[pl.ds(r_re, FFT_N2), :] = u[:FFT_N2, :LANES]
        zb[pl.ds(r_re, FFT_N2), :] = u[:FFT_N2, LANES:]
        za[pl.ds(r_im, FFT_N2), :] = u[FFT_N2:, :LANES]
        zb[pl.ds(r_im, FFT_N2), :] = u[FFT_N2:, LANES:]

    @pl.when(j == pl.num_programs(1) - 1)
    def _():
        f1i = f1i_ref[...]
        bias = bias_ref[0]
        for c0 in range(0, FFT_N2, FFT_NB):
            cols = []
            for q in range(FFT_NB):
                cols.append(za[pl.ds(c0 + q, 2 * n1, stride=FFT_N2), :])
                cols.append(zb[pl.ds(c0 + q, 2 * n1, stride=FFT_N2), :])
            yv = jnp.dot(f1i, jnp.concatenate(cols, axis=1).astype(BF16), preferred_element_type=F32)
            for q in range(FFT_NB):
                rows = pl.ds(c0 + q, n_in, stride=FFT_N2)
                for h, (x_r, g_r, o_r) in enumerate(((xa_ref, ga_ref, oa_ref), (xb_ref, gb_ref, ob_ref))):
                    conv = yv[:, (2 * q + h) * LANES:(2 * q + h + 1) * LANES]
                    o_r[0, rows, :] = g_r[0, rows, :] * (conv + x_r[0, rows, :] * bias[:, h * LANES:(h + 1) * LANES])


def _fft_tables(l):
    n = 2 * l
    n1 = n // FFT_N2
    n_in = n1 // 2
    k1 = np.arange(n1)
    phi = 2.0 * np.pi * np.outer(k1, np.arange(n1)) / n1
    f1_full = np.concatenate([np.cos(phi), -np.sin(phi)], axis=0).astype(np.float32)
    f1 = f1_full[:, :n_in]
    f1i = (np.concatenate([np.cos(phi[:, :n_in]), -np.sin(phi[:, :n_in])], axis=0).T / n).astype(np.float32)
    n2 = np.arange(FFT_N2)
    alpha = 2.0 * np.pi * np.outer(k1, n2) / n
    beta = 2.0 * np.pi * np.outer(np.arange(FFT_N2), n2) / FFT_N2
    ca, sa = jnp.asarray(np.cos(alpha), F32)[:, None, :], jnp.asarray(np.sin(alpha), F32)[:, None, :]
    cb, sb = jnp.asarray(np.cos(beta), F32)[None], jnp.asarray(np.sin(beta), F32)[None]
    er = ca * cb - sa * sb
    ei = -(sa * cb + ca * sb)
    ef = jnp.concatenate([jnp.concatenate([er, -ei], axis=2), jnp.concatenate([ei, er], axis=2)], axis=1)
    ert, eit = jnp.swapaxes(er, 1, 2), jnp.swapaxes(ei, 1, 2)
    einv = jnp.concatenate([jnp.concatenate([ert, eit], axis=2), jnp.concatenate([-eit, ert], axis=2)], axis=1)
    return {'n1': n1, 'f1': jnp.asarray(f1, BF16), 'f1_full': jnp.asarray(f1_full, BF16), 'f1i': jnp.asarray(f1i, BF16),
            'ef': ef.astype(BF16), 'ei': einv.astype(BF16)}


def _hy_filter_tm_kernel(w1t_ref, w1c_ref, w1s_ref, b1_ref, w2_ref, b2_ref, w3_ref, b3_ref, fr_ref, dl_ref,
                         k_ref, asum_ref, *, l, tl):
    j = pl.program_id(0)
    n = j * tl + lax.broadcasted_iota(jnp.int32, (tl, 1), 0)
    t = jnp.where(n < l, n, 2 * l - n).astype(F32)
    t_norm = t / l
    bands = (1 + lax.broadcasted_iota(jnp.int32, (1, HY_BANDS), 1)).astype(F32)
    ang = (2.0 * math.pi / l) * t * bands
    fr = fr_ref[...]
    lin = (t_norm * w1t_ref[...]
           + jnp.dot(jnp.cos(ang), w1c_ref[...], preferred_element_type=F32, precision=HIGHEST)
           + jnp.dot(jnp.sin(ang), w1s_ref[...], preferred_element_type=F32, precision=HIGHEST))
    h = jnp.sin(fr * (lin + b1_ref[...]))
    h = jnp.sin(fr * (jnp.dot(h, w2_ref[...], preferred_element_type=F32, precision=HIGHEST) + b2_ref[...]))
    h = jnp.dot(h, w3_ref[...], preferred_element_type=F32, precision=HIGHEST) + b3_ref[...]
    window = jnp.exp(-t_norm * dl_ref[...])

    @pl.when(j == 0)
    def _():
        asum_ref[...] = jnp.zeros_like(asum_ref)

    for o in range(HY_ORDER):
        hf = h[:, o * HY_WIDTH:(o + 1) * HY_WIDTH]
        hb = h[:, (HY_ORDER + o) * HY_WIDTH:(HY_ORDER + o + 1) * HY_WIDTH]
        k = jnp.where(n > l, hb, hf) * window
        k = jnp.where(n == l, 0.0, k)
        k_ref[o] = k
        asum_ref[o] += jnp.sum(jnp.abs(k), axis=0, keepdims=True)


def _hyena_latent(p_b, lp, tabs, ctx_len):
    b, s, c3 = p_b.shape
    l = s - ctx_len
    n = 2 * l
    n1 = tabs['n1']
    n_in = n1 // 2
    hid = HY_FILT_HID
    w = HY_WIDTH
    full = lambda shape: pl.BlockSpec(shape, lambda *idx: (0,) * len(shape))
    tl = 512
    w1 = lp['hy_w1']
    rowv = lambda v: v.reshape(1, -1)
    deltas = np.abs(np.linspace(HY_MIN_DECAY, HY_MAX_DECAY, w, dtype=np.float32)).reshape(1, -1)
    n_out = 2 * HY_ORDER * w
    k_tm, asum = pl.pallas_call(
        functools.partial(_hy_filter_tm_kernel, l=l, tl=tl),
        out_shape=(jax.ShapeDtypeStruct((HY_ORDER, n, w), F32), jax.ShapeDtypeStruct((HY_ORDER, 1, w), F32)),
        grid=(n // tl,),
        in_specs=[full((1, hid)), full((HY_BANDS, hid)), full((HY_BANDS, hid)), full((1, hid)), full((hid, hid)),
                  full((1, hid)), full((hid, n_out)), full((1, n_out)), full((1, hid)), full((1, w))],
        out_specs=(pl.BlockSpec((HY_ORDER, tl, w), lambda j: (0, j, 0)),
                   pl.BlockSpec((HY_ORDER, 1, w), lambda j: (0, 0, 0))),
        compiler_params=_cp(("arbitrary",)),
        name="hyena_filters_tm",
    )(w1[0:1], w1[1:1 + HY_BANDS], w1[1 + HY_BANDS:], rowv(lp['hy_b1']), lp['hy_w2'], rowv(lp['hy_b2']),
      lp['hy_w3'], rowv(lp['hy_b3']), rowv(lp['hy_freq']), jnp.asarray(deltas))
    nj = n1 // FFT_KB
    scr = pltpu.VMEM((2 * n1 * FFT_N2, LANES), F32)
    half = lambda blk: pl.BlockSpec((1, n, LANES), lambda i, j: (i, 0, blk))
    kf = pl.pallas_call(
        functools.partial(_hy_fft_filter_kernel, n1=n1),
        out_shape=jax.ShapeDtypeStruct((HY_ORDER, 2, n1, FFT_N2, w), F32),
        grid=(HY_ORDER, nj),
        in_specs=[half(0), half(1), full((2 * n1, n1)),
                  pl.BlockSpec((FFT_KB, 2 * FFT_N2, 2 * FFT_N2), lambda i, j: (j, 0, 0))],
        out_specs=pl.BlockSpec((1, 2, FFT_KB, FFT_N2, w), lambda i, j: (i, 0, j, 0, 0)),
        scratch_shapes=[scr, scr],
        compiler_params=_cp(("parallel", "arbitrary")),
        name="hyena_filter_fft",
    )(k_tm, k_tm, tabs['f1_full'], tabs['ef'])
    z = pl.pallas_call(
        functools.partial(_hy_short_conv_tm_kernel, l=l, ctx_len=ctx_len),
        out_shape=jax.ShapeDtypeStruct((b, l, c3), F32),
        grid=(b, c3 // LANES),
        in_specs=[pl.BlockSpec((1, s, LANES), lambda i, g: (i, 0, g)),
                  pl.BlockSpec((HY_SHORT, LANES), lambda i, g: (0, g)),
                  pl.BlockSpec((1, LANES), lambda i, g: (0, g))],
        out_specs=pl.BlockSpec((1, l, LANES), lambda i, g: (i, 0, g)),
        scratch_shapes=[pltpu.VMEM((l + 2 * SUBLANES, LANES), F32)],
        compiler_params=_cp(("parallel", "parallel")),
        name="hyena_short_conv_tm",
    )(p_b, lp['hy_conv_w'], lp['hy_conv_b'].reshape(1, c3))

    def conv(xa, xb, xblk, gate_blk, order):
        lane = lambda arr, blk: pl.BlockSpec((1, l, LANES), lambda i, j: (i, 0, blk))
        half_out = jax.ShapeDtypeStruct((b, l, LANES), F32)
        xb_blk = xblk + 1 if xa is xb else xblk
        return pl.pallas_call(
            functools.partial(_hy_fft_conv_kernel, n1=n1),
            out_shape=(half_out, half_out),
            grid=(b, nj),
            in_specs=[lane(xa, xblk), lane(xb, xb_blk), lane(z, gate_blk), lane(z, gate_blk + 1),
                      pl.BlockSpec((1, 1, w), lambda i, j: (order, 0, 0)),
                      full((2 * n1, n_in)), full((n_in, 2 * n1)),
                      pl.BlockSpec((FFT_KB, 2 * FFT_N2, 2 * FFT_N2), lambda i, j: (j, 0, 0)),
                      pl.BlockSpec((FFT_KB, 2 * FFT_N2, 2 * FFT_N2), lambda i, j: (j, 0, 0)),
                      pl.BlockSpec((1, 2, FFT_KB, FFT_N2, w), lambda i, j: (order, 0, j, 0, 0)),
                      pl.BlockSpec((1, 1, w), lambda i, j: (order, 0, 0))],
            out_specs=(pl.BlockSpec((1, l, LANES), lambda i, j: (i, 0, 0)),
                       pl.BlockSpec((1, l, LANES), lambda i, j: (i, 0, 0))),
            scratch_shapes=[scr, scr],
            compiler_params=_cp(("parallel", "arbitrary")),
            name="hyena_fft_conv",
        )(xa, xb, z, z, lp['hy_bias'].reshape(HY_ORDER, 1, w), tabs['f1'], tabs['f1i'], tabs['ef'], tabs['ei'],
          kf, asum)

    y1a, y1b = conv(z, z, 0, 2, 0)
    return conv(y1a, y1b, 0, 4, 1)


def _hy_short_conv_tm_kernel(x_ref, w_ref, b_ref, z_ref, xpad, *, l, ctx_len):
    ch = LANES
    zeros8 = jnp.zeros((SUBLANES, LANES), F32)
    xpad[0:SUBLANES, :] = zeros8
    xpad[l + SUBLANES:l + 2 * SUBLANES, :] = zeros8

    def copy_body(i, carry):
        r = pl.multiple_of(i * ch, ch)
        xpad[pl.ds(r + SUBLANES, ch), :] = x_ref[0, pl.ds(r + ctx_len, ch), :]
        return carry

    lax.fori_loop(0, l // ch, copy_body, 0)

    def conv_body(i, carry):
        r = pl.multiple_of(i * ch, ch)
        win = xpad[pl.ds(r, ch + 2 * SUBLANES), :]
        acc = jnp.zeros((ch, LANES), F32) + b_ref[...]
        for k in range(HY_SHORT):
            off = k - HY_SHORT // 2
            acc = acc + win[SUBLANES + off:SUBLANES + off + ch, :] * w_ref[k:k + 1, :]
        z_ref[0, pl.ds(r, ch), :] = acc
        return carry

    lax.fori_loop(0, l // ch, conv_body, 0)


def _top2_route(logits):
    lane = lax.broadcasted_iota(jnp.int32, logits.shape, 1)
    lg = jnp.where(lane < N_EXPERTS, logits, -jnp.inf)
    v1 = lg.max(axis=-1, keepdims=True)
    i1 = jnp.min(jnp.where(lg == v1, lane, LANES), axis=-1, keepdims=True)
    lg2 = jnp.where(lane == i1, -jnp.inf, lg)
    v2 = lg2.max(axis=-1, keepdims=True)
    i2 = jnp.min(jnp.where(lg2 == v2, lane, LANES), axis=-1, keepdims=True)
    e2 = jnp.exp(v2 - v1)
    w1 = 1.0 / (1.0 + e2)
    w2 = e2 / (1.0 + e2)
    out = jnp.where(lane == 0, i1.astype(F32), 0.0)
    out = jnp.where(lane == 1, i2.astype(F32), out)
    out = jnp.where(lane == 2, w1, out)
    return jnp.where(lane == 3, w2, out)


def _merge_kernel(ya_ref, yba_ref, ybb_ref, ybc_ref, ycl_ref, ycc_ref, yd_ref, pg_ref, x_ref, mod_ref, wa_ref, wb_ref,
                  wc_ref, wd_ref, wo_ref, lng_ref, lnb_ref, *rest, alpha, first_tile, with_router):
    if with_router:
        wr_ref, x_out_ref, u_out_ref, route_ref = rest
    else:
        x_out_ref, u_out_ref = rest
    j = pl.program_id(1) + first_tile
    d = x_ref.shape[-1]
    yb = jnp.concatenate([yba_ref[0], ybb_ref[0]], axis=1)
    yc = ycl_ref[0]
    if first_tile == 0:
        yb = jnp.where(j == 0, ybc_ref[0].astype(F32).T, yb)
        yc = jnp.where(j == 0, ycc_ref[0], yc)
    yb = yb.astype(BF16)
    projs = [jnp.dot(ya_ref[0], wa_ref[...], preferred_element_type=F32),
             jnp.dot(yb, wb_ref[...], preferred_element_type=F32),
             jnp.dot(yc, wc_ref[...], preferred_element_type=F32),
             jnp.dot(yd_ref[0], wd_ref[...], preferred_element_type=F32)]
    merged = None
    for i, pr in enumerate(projs):
        term = _sigmoid(pg_ref[0, :, i * d:(i + 1) * d].astype(F32)) * pr
        merged = term if merged is None else merged + term
    m = jnp.dot(merged.astype(BF16), wo_ref[...], preferred_element_type=F32)
    mod = mod_ref[0, 0]
    xn = _layer_norm_rows(alpha * x_ref[0] + mod[2:3, :] * m) * lng_ref[...] + lnb_ref[...]
    x_out_ref[0] = xn
    u = _layer_norm_rows(xn) * (1.0 + mod[4:5, :]) + mod[3:4, :]
    u_out_ref[0] = u.astype(u_out_ref.dtype)
    if with_router:
        u_hi = u.astype(BF16)
        u_lo = (u - u_hi.astype(F32)).astype(BF16)
        wr = wr_ref[...]
        w_hi = wr.astype(BF16)
        w_lo = (wr - w_hi.astype(F32)).astype(BF16)
        logits = (jnp.dot(u_hi, w_hi, preferred_element_type=F32) + jnp.dot(u_lo, w_hi, preferred_element_type=F32)
                  + jnp.dot(u_hi, w_lo, preferred_element_type=F32))
        route_ref[0] = _top2_route(logits)


def _merge(ya, yb_lat_a, yb_lat_b, yb_ctx, yc_lat, yc_ctx, yd, pg, xs, mod_sel, lp, ln_g, ln_b, alpha, ctx_len,
           with_ctx, u_dtype, w_router=None):
    b, s, d = xs.shape
    tm = ctx_len
    first = 0 if with_ctx else 1
    nt = s // tm - first
    tok = lambda w: pl.BlockSpec((1, tm, w), lambda i, j: (i, j + first, 0))
    lat = lambda j: jnp.maximum(j + first - 1, 0)
    out = lambda w: pl.BlockSpec((1, tm, w), lambda i, j: (i, j, 0))
    full = lambda shape: pl.BlockSpec(shape, lambda i, j: (0,) * len(shape))
    in_specs = [tok(LRU_WIDTH),
                pl.BlockSpec((1, tm, LANES), lambda i, j: (i, lat(j), 0)),
                pl.BlockSpec((1, tm, LANES), lambda i, j: (i, lat(j), 0)),
                pl.BlockSpec((1, HY_WIDTH, tm), lambda i, j: (i, 0, 0)),
                pl.BlockSpec((1, tm, NA_WIDTH), lambda i, j: (i, lat(j), 0)),
                pl.BlockSpec((1, tm, NA_WIDTH), lambda i, j: (i, 0, 0)),
                tok(S5_WIDTH), tok(N_BRANCH * d), tok(d),
                pl.BlockSpec((1, 1, 6, d), lambda i, j: (i, jnp.minimum(j + first, 1), 0, 0)),
                full((LRU_WIDTH, d)), full((HY_WIDTH, d)), full((NA_WIDTH, d)), full((S5_WIDTH, d)),
                full((d, d)), full((1, d)), full((1, d))]
    args = [ya, yb_lat_a, yb_lat_b, yb_ctx, yc_lat, yc_ctx, yd, pg, xs, mod_sel,
            lp['w_br_a'].astype(BF16), lp['w_br_b'].astype(BF16), lp['w_br_c'].astype(BF16),
            lp['w_br_d'].astype(BF16), lp['w_out'].astype(BF16), ln_g.reshape(1, d), ln_b.reshape(1, d)]
    out_shape = [jax.ShapeDtypeStruct((b, nt * tm, d), F32), jax.ShapeDtypeStruct((b, nt * tm, d), u_dtype)]
    out_specs = [out(d), out(d)]
    if w_router is not None:
        in_specs.append(full((d, LANES)))
        args.append(jnp.zeros((d, LANES), F32).at[:, :N_EXPERTS].set(w_router))
        out_shape.append(jax.ShapeDtypeStruct((b, nt * tm, LANES), F32))
        out_specs.append(out(LANES))
    return pl.pallas_call(
        functools.partial(_merge_kernel, alpha=alpha, first_tile=first, with_router=w_router is not None),
        out_shape=tuple(out_shape),
        grid=(b, nt),
        in_specs=in_specs,
        out_specs=tuple(out_specs),
        compiler_params=_cp(("parallel", "parallel")),
        name="merge",
    )(*args)


FFN_TF = 256


def _dense_ffn_kernel(u_ref, x_ref, mod_ref, modn_ref, wg_ref, wu_ref, wd_ref, lng_ref, lnb_ref,
                      xs_ref, un_ref, acc_ref, *, alpha, ctx_len, tiles_per_seq):
    i = pl.program_id(0)
    j = pl.program_id(1)

    @pl.when(j == 0)
    def _():
        acc_ref[...] = jnp.zeros_like(acc_ref)

    u = u_ref[...]
    g = jnp.dot(u, wg_ref[...], preferred_element_type=F32)
    v = jnp.dot(u, wu_ref[...], preferred_element_type=F32)
    h = (g * _sigmoid(g)) * v
    acc_ref[...] += jnp.dot(h.astype(BF16), wd_ref[...], preferred_element_type=F32)

    @pl.when(j == pl.num_programs(1) - 1)
    def _():
        row = lax.broadcasted_iota(jnp.int32, (acc_ref.shape[0], 1), 0)
        is_ctx = (row < ctx_len) & ((i % tiles_per_seq) == 0)
        pick = lambda m, k: jnp.where(is_ctx, m[0, 0, k:k + 1, :], m[0, 1, k:k + 1, :])
        x2 = (_layer_norm_rows(alpha * x_ref[...] + pick(mod_ref, 5) * acc_ref[...]) * lng_ref[...]
              + lnb_ref[...])
        xs_ref[...] = x2
        un_ref[...] = (_layer_norm_rows(x2) * (1.0 + pick(modn_ref, 1)) + pick(modn_ref, 0)).astype(un_ref.dtype)


def _dense_ffn(u2, x1, mod_sel, mod_next, w_gate, w_up, w_down, ln_g, ln_b, alpha, ctx_len):
    b, s, d = x1.shape
    ff = w_gate.shape[1]
    tiles_per_seq = 4
    tm = s // tiles_per_seq
    assert tm % 16 == 0 and tm >= ctx_len
    tf = FFN_TF
    rows = pl.BlockSpec((tm, d), lambda i, j: (i, 0))
    modspec = pl.BlockSpec((1, 2, 6, d), lambda i, j: (i // tiles_per_seq, 0, 0, 0))
    vec = pl.BlockSpec((1, d), lambda i, j: (0, 0))
    xs, un = pl.pallas_call(
        functools.partial(_dense_ffn_kernel, alpha=alpha, ctx_len=ctx_len, tiles_per_seq=tiles_per_seq),
        out_shape=(jax.ShapeDtypeStruct((b * s, d), F32), jax.ShapeDtypeStruct((b * s, d), BF16)),
        grid=(b * tiles_per_seq, ff // tf),
        in_specs=[rows, rows, modspec, modspec,
                  pl.BlockSpec((d, tf), lambda i, j: (0, j)), pl.BlockSpec((d, tf), lambda i, j: (0, j)),
                  pl.BlockSpec((tf, d), lambda i, j: (j, 0)), vec, vec],
        out_specs=(rows, rows),
        scratch_shapes=[pltpu.VMEM((tm, d), F32)],
        compiler_params=_cp(("parallel", "arbitrary")),
        name="dense_swiglu_ln",
    )(u2.reshape(b * s, d), x1.reshape(b * s, d), mod_sel, mod_next, w_gate, w_up, w_down,
      ln_g.reshape(1, d), ln_b.reshape(1, d))
    return xs.reshape(b, s, d), un.reshape(b, s, d)


MOE_TM = 512
MOE_TF = 512


def _moe_kernel(te_ref, nu_ref, dst_ref, u_hbm, wg_ref, wu_ref, wd_ref, out_hbm,
                xbuf, xbf, acc, gsem, ssem, *, n_rows):
    tm = MOE_TM
    n_tok = n_rows // TOP_K
    i = pl.program_id(0)
    j = pl.program_id(1)
    nt = pl.num_programs(0)
    nj = pl.num_programs(1)
    n_used = nu_ref[0]
    slot = i % 2

    def gather_start(tile, sl):
        base = tile * tm

        def body(r, carry):
            p = dst_ref[base + r]
            row = jnp.where(p >= n_tok, p - n_tok, jnp.maximum(p, 0))
            pltpu.make_async_copy(u_hbm.at[pl.ds(row, 1)], xbuf.at[sl, pl.ds(r, 1)], gsem.at[sl]).start()
            return carry

        lax.fori_loop(0, tm, body, 0, unroll=8)

    def gather_wait(sl):
        pltpu.make_async_copy(u_hbm.at[pl.ds(0, tm)], xbuf.at[sl], gsem.at[sl]).wait()

    def scatter_wait():
        pltpu.make_async_copy(acc.at[0], out_hbm.at[pl.ds(0, tm)], ssem.at[0]).wait()

    used = i < n_used

    @pl.when(used & (j == 0))
    def _():
        @pl.when(i == 0)
        def _():
            gather_start(0, 0)
            acc[1] = jnp.zeros((tm, acc.shape[2]), F32)
            dump = pltpu.make_async_copy(acc.at[1], out_hbm.at[pl.ds(n_rows, tm)], ssem.at[0])
            dump.start()
            dump.wait()

        gather_wait(slot)

        @pl.when(i + 1 < n_used)
        def _():
            gather_start(i + 1, 1 - slot)

        xbf[...] = xbuf[slot].astype(BF16)
        acc[slot] = jnp.zeros((tm, acc.shape[2]), F32)

    @pl.when(used)
    def _():
        x = xbf[...]
        g = jnp.dot(x, wg_ref[0], preferred_element_type=F32)
        u = jnp.dot(x, wu_ref[0], preferred_element_type=F32)
        h = (g * _sigmoid(g)) * u
        acc[slot] += jnp.dot(h.astype(BF16), wd_ref[0], preferred_element_type=F32)

    @pl.when(used & (j == nj - 1))
    def _():
        @pl.when(i > 0)
        def _():
            scatter_wait()

        base = i * tm

        def body(r, carry):
            p = dst_ref[base + r]
            row = jnp.where(p >= 0, p, n_rows + r)
            pltpu.make_async_copy(acc.at[slot, pl.ds(r, 1)], out_hbm.at[pl.ds(row, 1)], ssem.at[0]).start()
            return carry

        lax.fori_loop(0, tm, body, 0, unroll=8)

    @pl.when((i == nt - 1) & (j == nj - 1))
    def _():
        scatter_wait()


def _moe_experts(u_rows, dst, tile_expert, n_used, w_gate, w_up, w_down):
    r, d = u_rows.shape
    ff = w_gate.shape[2]
    tm, tf = MOE_TM, MOE_TF
    nt = dst.shape[0] // tm
    nj = ff // tf
    n_rows = TOP_K * r

    def jmap(i, j, nu):
        return jnp.where(i < nu[0], j, nj - 1)

    grid_spec = pltpu.PrefetchScalarGridSpec(
        num_scalar_prefetch=3,
        grid=(nt, nj),
        in_specs=[pl.BlockSpec(memory_space=pl.ANY),
                  pl.BlockSpec((1, d, tf), lambda i, j, te, nu, ds: (te[i], 0, jmap(i, j, nu))),
                  pl.BlockSpec((1, d, tf), lambda i, j, te, nu, ds: (te[i], 0, jmap(i, j, nu))),
                  pl.BlockSpec((1, tf, d), lambda i, j, te, nu, ds: (te[i], jmap(i, j, nu), 0))],
        out_specs=pl.BlockSpec(memory_space=pl.ANY),
        scratch_shapes=[pltpu.VMEM((2, tm, d), F32), pltpu.VMEM((tm, d), BF16), pltpu.VMEM((2, tm, d), F32),
                        pltpu.SemaphoreType.DMA((2,)), pltpu.SemaphoreType.DMA((1,))],
    )
    return pl.pallas_call(
        functools.partial(_moe_kernel, n_rows=n_rows),
        out_shape=jax.ShapeDtypeStruct((n_rows + tm, d), F32),
        grid_spec=grid_spec,
        compiler_params=_cp(("arbitrary", "arbitrary")),
        name="moe_experts",
    )(tile_expert, n_used, dst, u_rows, w_gate, w_up, w_down)


def _moe_combine_kernel(x_ref, y1_ref, y2_ref, w_ref, mod_ref, lng_ref, lnb_ref, o_ref, *, alpha):
    mod = mod_ref[0, 0]
    w = w_ref[0]
    f = w[:, 2:3] * y1_ref[...] + w[:, 3:4] * y2_ref[...]
    o_ref[0] = _layer_norm_rows(alpha * x_ref[0] + mod[5:6, :] * f) * lng_ref[...] + lnb_ref[...]


def _moe_ffn(x_lat, u_lat, route, mod_sel, lp_moe, ln_g, ln_b, alpha):
    b, l, d = x_lat.shape
    t = b * l
    tm = MOE_TM
    ids = route[..., 0:TOP_K].astype(jnp.int32).reshape(t * TOP_K)
    onehot = (ids[:, None] == jnp.arange(N_EXPERTS)[None, :]).astype(jnp.int32)
    csum = jnp.cumsum(onehot, axis=0)
    rank = jnp.take_along_axis(csum, ids[:, None], axis=1)[:, 0] - 1
    counts = csum[-1]
    padded = ((counts + tm - 1) // tm) * tm
    ends = jnp.cumsum(padded)
    starts = ends - padded
    slot = starts[ids] + rank
    n_slots = t * TOP_K + N_EXPERTS * tm
    nt = n_slots // tm
    pair = jnp.arange(t * TOP_K, dtype=jnp.int32)
    dst = jnp.full((n_slots,), -1, jnp.int32).at[slot].set((pair % TOP_K) * t + pair // TOP_K)
    tile_start = jnp.arange(nt, dtype=jnp.int32) * tm
    tile_expert = jnp.minimum(jnp.sum(tile_start[:, None] >= ends[None, :], axis=1), N_EXPERTS - 1).astype(jnp.int32)
    n_used = (ends[-1] // tm).astype(jnp.int32).reshape(1)
    last_e = tile_expert[jnp.maximum(n_used[0] - 1, 0)]
    tile_expert = jnp.where(jnp.arange(nt) < n_used[0], tile_expert, last_e)
    y = _moe_experts(u_lat.reshape(t, d), dst, tile_expert, n_used, lp_moe['w_gate'], lp_moe['w_up'],
                     lp_moe['w_down'])
    tmc = 256
    nl = l // tmc
    vec = pl.BlockSpec((1, d), lambda i, j: (0, 0))
    return pl.pallas_call(
        functools.partial(_moe_combine_kernel, alpha=alpha),
        out_shape=jax.ShapeDtypeStruct((b, l, d), F32),
        grid=(b, nl),
        in_specs=[pl.BlockSpec((1, tmc, d), lambda i, j: (i, j, 0)),
                  pl.BlockSpec((tmc, d), lambda i, j: (i * nl + j, 0)),
                  pl.BlockSpec((tmc, d), lambda i, j: (b * nl + i * nl + j, 0)),
                  pl.BlockSpec((1, tmc, LANES), lambda i, j: (i, j, 0)),
                  pl.BlockSpec((1, 1, 6, d), lambda i, j: (i, 1, 0, 0)), vec, vec],
        out_specs=pl.BlockSpec((1, tmc, d), lambda i, j: (i, j, 0)),
        compiler_params=_cp(("parallel", "parallel")),
        name="moe_combine_ln",
    )(x_lat, y, y, route, mod_sel, ln_g.reshape(1, d), ln_b.reshape(1, d))


def kernel(x, c, ctx, c_ctx, w_mod, b_mod, w_in, lru_conv_w, lru_conv_b, lru_w_r, lru_b_r, lru_w_i, lru_b_i, lru_lambda, hy_conv_w, hy_conv_b, hy_w1, hy_b1, hy_w2, hy_b2, hy_w3, hy_b3, hy_freq, hy_bias, na_rpb, s5_a_re, s5_a_im, s5_log_dt, s5_b_re, s5_b_im, s5_c_re, s5_c_im, s5_d, s5_w_glu, s5_b_glu, w_br_a, w_br_b, w_br_c, w_br_d, w_out, ln1_g, ln1_b, ln2_g, ln2_b, ff_w_gate, ff_w_up, ff_w_down, moe_router, moe_w_gate, moe_w_up, moe_w_down):
    bsz, l, d = x.shape
    ctx_len = ctx.shape[1]
    depth = w_in.shape[0]
    s = ctx_len + l
    alpha = (2.0 * depth) ** 0.25
    xs = jnp.concatenate([ctx, x], axis=1)
    c_rows = jnp.zeros((SUBLANES, d), F32).at[0:bsz].set(c).at[bsz].set(c_ctx)
    fft_lat = _fft_tables(l)
    dft_ctx = _dft_matrices(ctx_len) if depth > 1 else None
    assert depth == 2
    mod_all = _mod_vectors(c_rows, w_mod, b_mod).reshape(depth, SUBLANES, 6, d)
    mods = [jnp.stack([jnp.broadcast_to(mod_all[li, bsz], (bsz, 6, d)), mod_all[li, 0:bsz]], axis=1)
            for li in range(depth)]
    mods.append(mods[-1])
    u1 = _ln_mod(xs, mods[0], 0, 1, ctx_len)

    for li in range(depth):
        with_ctx = li < depth - 1
        lp = {
            'lru_conv_w': lru_conv_w[li], 'lru_conv_b': lru_conv_b[li], 'lru_w_r': lru_w_r[li],
            'lru_b_r': lru_b_r[li], 'lru_w_i': lru_w_i[li], 'lru_b_i': lru_b_i[li], 'lru_lambda': lru_lambda[li],
            'hy_conv_w': hy_conv_w[li], 'hy_conv_b': hy_conv_b[li], 'hy_w1': hy_w1[li], 'hy_b1': hy_b1[li],
            'hy_w2': hy_w2[li], 'hy_b2': hy_b2[li], 'hy_w3': hy_w3[li], 'hy_b3': hy_b3[li],
            'hy_freq': hy_freq[li], 'hy_bias': hy_bias[li],
            's5_a_re': s5_a_re[li], 's5_a_im': s5_a_im[li], 's5_log_dt': s5_log_dt[li], 's5_b_re': s5_b_re[li],
            's5_b_im': s5_b_im[li], 's5_c_re': s5_c_re[li], 's5_c_im': s5_c_im[li], 's5_d': s5_d[li],
            's5_w_glu': s5_w_glu[li], 's5_b_glu': s5_b_glu[li], 'w_br_a': w_br_a[li], 'w_br_b': w_br_b[li],
            'w_br_c': w_br_c[li], 'w_br_d': w_br_d[li], 'w_out': w_out[li],
        }
        mod_sel = mods[li]

        u1f = u1.reshape(bsz * s, d)
        wi = w_in[li].astype(BF16)
        p_a = _matmul(u1f, wi[:, OFF_A:OFF_B], F32, name="proj_lru").reshape(bsz, s, OFF_B - OFF_A)
        p_c = _matmul(u1f, wi[:, OFF_C:OFF_D], BF16, name="proj_natten").reshape(bsz, s, OFF_D - OFF_C)
        p_d = _matmul(u1f, wi[:, OFF_D:OFF_G], F32, name="proj_s5").reshape(bsz, s, OFF_G - OFF_D)
        p_g = _matmul(u1f, wi[:, OFF_G:], BF16, name="proj_gates").reshape(bsz, s, N_BRANCH * d)
        p_b = _matmul(u1f, wi[:, OFF_B:OFF_C], F32, name="proj_hyena").reshape(bsz, s, OFF_C - OFF_B)

        ya = _lru_mixer(p_a, lp, ctx_len)
        yb_a, yb_b = _hyena_latent(p_b, lp, fft_lat, ctx_len)
        if with_ctx:
            yb_ctx = _hyena_sequence(_matmul_nt(wi[:, OFF_B:OFF_C].T, u1, F32, 0, ctx_len), lp, dft_ctx)
        else:
            yb_ctx = jnp.zeros((bsz, HY_WIDTH, ctx_len), BF16)
        yc_l, yc_c = _natten_mixer(p_c, na_rpb[li], ctx_len, with_ctx)
        if yc_c is None:
            yc_c = jnp.zeros((bsz, ctx_len, NA_WIDTH), BF16)
        yd = _s5_mixer(p_d, lp, ctx_len)
        e = li // 2
        if li % 2 == 0:
            assert with_ctx
            x1, u2 = _merge(ya, yb_a, yb_b, yb_ctx, yc_l, yc_c, yd, p_g, xs, mod_sel, lp, ln1_g[li], ln1_b[li],
                            alpha, ctx_len, True, BF16)
            xs, u1 = _dense_ffn(u2, x1, mod_sel, mods[li + 1], ff_w_gate[e].astype(BF16), ff_w_up[e].astype(BF16),
                                ff_w_down[e].astype(BF16), ln2_g[li], ln2_b[li], alpha, ctx_len)
        else:
            assert not with_ctx
            x1, u2, route = _merge(ya, yb_a, yb_b, yb_ctx, yc_l, yc_c, yd, p_g, xs, mod_sel, lp, ln1_g[li],
                                   ln1_b[li], alpha, ctx_len, False, F32, moe_router[e])
            lp_moe = {'w_gate': moe_w_gate[e].astype(BF16), 'w_up': moe_w_up[e].astype(BF16),
                      'w_down': moe_w_down[e].astype(BF16)}
            return _moe_ffn(x1, u2, route, mod_sel, lp_moe, ln2_g[li], ln2_b[li], alpha)
```

```python
import functools
import math

import numpy as np
import jax
import jax.numpy as jnp
from jax import lax
from jax.experimental import pallas as pl
from jax.experimental.pallas import tpu as pltpu

F32 = jnp.float32
BF16 = jnp.bfloat16
HIGHEST = lax.Precision.HIGHEST

LRU_WIDTH = 384
LRU_BLOCK = 64
LRU_CONV = 4
LRU_C = 8.0
HY_WIDTH = 256
HY_ORDER = 2
HY_SHORT = 3
HY_BANDS = 16
HY_FILT_HID = 64
HY_MAX_DECAY = math.log(1e-2) / 0.3
HY_MIN_DECAY = math.log(1e-2) / 1.5
NA_HEADS = 6
NA_HEAD_DIM = 64
NA_WIDTH = NA_HEADS * NA_HEAD_DIM
NA_WIN_R = 8
NA_WIN_C = 16
GRID_W = 64
S5_WIDTH = 256
S5_GROUP = 16
S5_GROUPS = 16
S5_STATE = 64
N_BRANCH = 4
OFF_A = 0
OFF_B = OFF_A + 2 * LRU_WIDTH
OFF_C = OFF_B + 3 * HY_WIDTH
OFF_D = OFF_C + 3 * NA_WIDTH
OFF_G = OFF_D + S5_WIDTH
N_EXPERTS = 8
TOP_K = 2
LN_EPS = 1e-5
MASK_VALUE = -1e30

LANES = 128
SUBLANES = 8
VMEM_LIMIT = 56 * 1024 * 1024


def _cp(sem, vmem=VMEM_LIMIT):
    return pltpu.CompilerParams(dimension_semantics=sem, vmem_limit_bytes=vmem)


def _gelu(x):
    return 0.5 * x * (1.0 + jnp.tanh(math.sqrt(2.0 / math.pi) * (x + 0.044715 * (x * x * x))))


def _sigmoid(x):
    return 0.5 + 0.5 * jnp.tanh(0.5 * x)


def _layer_norm_rows(x):
    mu = jnp.mean(x, axis=-1, keepdims=True)
    xc = x - mu
    var = jnp.mean(xc * xc, axis=-1, keepdims=True)
    return xc * lax.rsqrt(var + LN_EPS)


def _mod_kernel(c_ref, w_ref, b_ref, o_ref):
    c = c_ref[...]
    a = c * _sigmoid(c)
    o_ref[0] = jnp.dot(a, w_ref[0], preferred_element_type=F32, precision=HIGHEST) + b_ref[0]


def _mod_vectors(c_rows, w_mod, b_mod):
    d = c_rows.shape[1]
    depth, _, n = w_mod.shape
    tn = 1536
    return pl.pallas_call(
        _mod_kernel,
        out_shape=jax.ShapeDtypeStruct((depth, SUBLANES, n), F32),
        grid=(depth, n // tn),
        in_specs=[pl.BlockSpec((SUBLANES, d), lambda l, j: (0, 0)),
                  pl.BlockSpec((1, d, tn), lambda l, j: (l, 0, j)),
                  pl.BlockSpec((1, 1, tn), lambda l, j: (l, 0, j))],
        out_specs=pl.BlockSpec((1, SUBLANES, tn), lambda l, j: (l, 0, j)),
        compiler_params=_cp(("arbitrary", "arbitrary")),
        name="mod_vectors",
    )(c_rows, w_mod, b_mod.reshape(depth, 1, n))


def _ln_mod_kernel(x_ref, mod_ref, o_ref, *, shift_idx, scale_idx):
    y = _layer_norm_rows(x_ref[0])
    m = mod_ref[0, 0]
    o = y * (1.0 + m[scale_idx:scale_idx + 1, :]) + m[shift_idx:shift_idx + 1, :]
    o_ref[0] = o.astype(o_ref.dtype)


def _ln_mod(xs, mod_sel, shift_idx, scale_idx, ctx_len):
    b, s, d = xs.shape
    tm = ctx_len
    return pl.pallas_call(
        functools.partial(_ln_mod_kernel, shift_idx=shift_idx, scale_idx=scale_idx),
        out_shape=jax.ShapeDtypeStruct((b, s, d), BF16),
        grid=(b, s // tm),
        in_specs=[pl.BlockSpec((1, tm, d), lambda i, j: (i, j, 0)),
                  pl.BlockSpec((1, 1, 6, d), lambda i, j: (i, jnp.minimum(j, 1), 0, 0))],
        out_specs=pl.BlockSpec((1, tm, d), lambda i, j: (i, j, 0)),
        compiler_params=_cp(("parallel", "parallel")),
        name="ln_mod",
    )(xs, mod_sel)


def _mm_kernel(a_ref, w_ref, o_ref):
    o_ref[...] = jnp.dot(a_ref[...], w_ref[...], preferred_element_type=F32).astype(o_ref.dtype)


def _pick_tile(n, prefs):
    for t in prefs:
        if n % t == 0:
            return t
    return n


def _matmul(a, w, out_dtype, tm=None, tn=None, name="matmul"):
    m, k = a.shape
    n = w.shape[1]
    tm = tm or _pick_tile(m, (1024, 512, 256, 128))
    tn = tn or (n if n <= 1536 else _pick_tile(n, (1024, 768, 512, 384, 256, 128)))
    return pl.pallas_call(
        _mm_kernel,
        out_shape=jax.ShapeDtypeStruct((m, n), out_dtype),
        grid=(m // tm, n // tn),
        in_specs=[pl.BlockSpec((tm, k), lambda i, j: (i, 0)),
                  pl.BlockSpec((k, tn), lambda i, j: (0, j))],
        out_specs=pl.BlockSpec((tm, tn), lambda i, j: (i, j)),
        compiler_params=_cp(("parallel", "parallel")),
        name=name,
    )(a, w)


def _mm_nt_kernel(w_ref, u_ref, o_ref):
    o_ref[0] = lax.dot_general(w_ref[...], u_ref[0], (((1,), (1,)), ((), ())),
                               preferred_element_type=F32).astype(o_ref.dtype)


def _matmul_nt(w_t, u, out_dtype, tok0, ntok, tn=256):
    c, k = w_t.shape
    b = u.shape[0]
    j0 = tok0 // tn
    return pl.pallas_call(
        _mm_nt_kernel,
        out_shape=jax.ShapeDtypeStruct((b, c, ntok), out_dtype),
        grid=(b, ntok // tn),
        in_specs=[pl.BlockSpec((c, k), lambda i, j: (0, 0)),
                  pl.BlockSpec((1, tn, k), lambda i, j: (i, j + j0, 0))],
        out_specs=pl.BlockSpec((1, c, tn), lambda i, j: (i, 0, j)),
        compiler_params=_cp(("parallel", "parallel")),
        name="matmul_nt",
    )(w_t, u)


LRU_CHUNK = 128


def _tile_scan(a, b, row, reverse):
    for s in (1, 2, 4):
        if reverse:
            keep = row < SUBLANES - s
            shift = SUBLANES - s
        else:
            keep = row >= s
            shift = s
        a_sh = pltpu.roll(a, shift, 0)
        b_sh = pltpu.roll(b, shift, 0)
        b = jnp.where(keep, a * b_sh, 0.0) + b
        a = jnp.where(keep, a * a_sh, a)
    return a, b


def _lru_kernel(pg_ref, px_ref, cw_ref, cb_ref, wg_ref, bg_ref, lam_ref, y_ref,
                xpad, a_f, b_f, a_b, b_b, *, s_len, ctx_len):
    ch = LRU_CHUNK
    n_chunks = s_len // ch
    zeros8 = jnp.zeros((SUBLANES, LANES), F32)
    xpad[0:SUBLANES, :] = zeros8
    xpad[ctx_len + SUBLANES:ctx_len + 2 * SUBLANES, :] = zeros8
    xpad[s_len + 2 * SUBLANES:s_len + 3 * SUBLANES, :] = zeros8

    def pad_row(r):
        return pl.multiple_of(r + jnp.where(r >= ctx_len, 2 * SUBLANES, SUBLANES), SUBLANES)

    def copy_body(i, carry):
        r = pl.multiple_of(i * ch, ch)
        xpad[pl.ds(pad_row(r), ch), :] = px_ref[0, pl.ds(r, ch), :]
        return carry

    lax.fori_loop(0, n_chunks, copy_body, 0)

    lam = lam_ref[...]
    sp = jnp.log(1.0 + jnp.exp(-lam))

    def gates_body(i, carry):
        r = pl.multiple_of(i * ch, ch)
        win = xpad[pl.ds(pad_row(r) - SUBLANES, ch + 2 * SUBLANES), :]
        xc = jnp.zeros((ch, LANES), F32) + cb_ref[...]
        for k in range(LRU_CONV):
            off = k - LRU_CONV // 2
            xc = xc + win[SUBLANES + off:SUBLANES + off + ch, :] * cw_ref[k:k + 1, :]
        gl = jnp.dot(xc.astype(BF16), wg_ref[0], preferred_element_type=F32) + bg_ref[0]
        for d, (a_s, b_s) in enumerate(((a_f, b_f), (a_b, b_b))):
            g_r = _sigmoid(gl[:, d * 2 * LANES:d * 2 * LANES + LANES])
            g_i = _sigmoid(gl[:, d * 2 * LANES + LANES:(d + 1) * 2 * LANES])
            log_a = (-LRU_C) * g_r * sp[d:d + 1, :]
            a = jnp.exp(log_a)
            bb = jnp.sqrt(1.0 - a * a) * g_i * xc
            a_s[pl.ds(r, ch), :] = a
            b_s[pl.ds(r, ch), :] = bb
        return carry

    lax.fori_loop(0, n_chunks, gates_body, 0)

    row = lax.broadcasted_iota(jnp.int32, (SUBLANES, LANES), 0)

    n_ctx_tiles = ctx_len // SUBLANES
    n_tiles = s_len // SUBLANES

    def scan_body(i, carry):
        h_f, h_b = carry
        r = pl.multiple_of(i * SUBLANES, SUBLANES)
        a, b = _tile_scan(a_f[pl.ds(r, SUBLANES), :], b_f[pl.ds(r, SUBLANES), :], row, False)
        hf = b + a * h_f
        b_f[pl.ds(r, SUBLANES), :] = hf
        t = jnp.where(i < n_ctx_tiles, n_ctx_tiles - 1 - i, n_tiles + n_ctx_tiles - 1 - i)
        rb = pl.multiple_of(t * SUBLANES, SUBLANES)
        a, b = _tile_scan(a_b[pl.ds(rb, SUBLANES), :], b_b[pl.ds(rb, SUBLANES), :], row, True)
        hb = b + a * h_b
        b_b[pl.ds(rb, SUBLANES), :] = hb
        return (jnp.broadcast_to(hf[SUBLANES - 1:SUBLANES, :], (SUBLANES, LANES)),
                jnp.broadcast_to(hb[0:1, :], (SUBLANES, LANES)))

    lax.fori_loop(0, n_tiles, scan_body, (zeros8, zeros8), unroll=2)

    def out_body(i, carry):
        r = pl.multiple_of(i * ch, ch)
        g = _gelu(pg_ref[0, pl.ds(r, ch), :])
        y = g * (b_f[pl.ds(r, ch), :] + b_b[pl.ds(r, ch), :])
        y_ref[0, pl.ds(r, ch), :] = y.astype(y_ref.dtype)
        return carry

    lax.fori_loop(0, n_chunks, out_body, 0)


def _lru_gate_weights(w_r, w_i, b_r, b_i):
    n_grp = LRU_WIDTH // LANES
    per = LANES // LRU_BLOCK

    def bd(w):
        w = w.reshape(n_grp, per, LRU_BLOCK, LRU_BLOCK)
        z = jnp.zeros((n_grp, LRU_BLOCK, LRU_BLOCK), w.dtype)
        top = jnp.concatenate([w[:, 0], z], axis=2)
        bot = jnp.concatenate([z, w[:, 1]], axis=2)
        return jnp.concatenate([top, bot], axis=1)

    wg = jnp.concatenate([bd(w_r[0]), bd(w_i[0]), bd(w_r[1]), bd(w_i[1])], axis=2).astype(BF16)
    bg = jnp.stack([b_r[0], b_i[0], b_r[1], b_i[1]], axis=0).reshape(4, n_grp, LANES)
    bg = jnp.transpose(bg, (1, 0, 2)).reshape(n_grp, 1, 4 * LANES)
    return wg, bg


def _lru_mixer(p_a, lp, ctx_len):
    b, s, _ = p_a.shape
    n_grp = LRU_WIDTH // LANES
    wg, bg = _lru_gate_weights(lp['lru_w_r'], lp['lru_w_i'], lp['lru_b_r'], lp['lru_b_i'])
    scr = pltpu.VMEM((s, LANES), F32)
    return pl.pallas_call(
        functools.partial(_lru_kernel, s_len=s, ctx_len=ctx_len),
        out_shape=jax.ShapeDtypeStruct((b, s, LRU_WIDTH), BF16),
        grid=(b, n_grp),
        in_specs=[pl.BlockSpec((1, s, LANES), lambda i, g: (i, 0, g)),
                  pl.BlockSpec((1, s, LANES), lambda i, g: (i, 0, n_grp + g)),
                  pl.BlockSpec((LRU_CONV, LANES), lambda i, g: (0, g)),
                  pl.BlockSpec((1, LANES), lambda i, g: (0, g)),
                  pl.BlockSpec((1, LANES, 4 * LANES), lambda i, g: (g, 0, 0)),
                  pl.BlockSpec((1, 1, 4 * LANES), lambda i, g: (g, 0, 0)),
                  pl.BlockSpec((2, LANES), lambda i, g: (0, g))],
        out_specs=pl.BlockSpec((1, s, LANES), lambda i, g: (i, 0, g)),
        scratch_shapes=[pltpu.VMEM((s + 3 * SUBLANES, LANES), F32), scr, scr, scr, scr],
        compiler_params=_cp(("parallel", "parallel")),
        name="rglru",
    )(p_a, p_a, lp['lru_conv_w'], lp['lru_conv_b'].reshape(1, LRU_WIDTH), wg, bg, lp['lru_lambda'])


S5_R = 4
S5_NSTATE = S5_GROUPS * S5_STATE


def _s5_kernel(xa_ref, xb_ref, winj_ref, wloc_ref, wro_ref, ap_ref, ya_ref, yb_ref, g_ref, *, reverse, n_ctx_tiles):
    n = S5_NSTATE
    nr = g_ref.shape[0]
    x = jnp.concatenate([h[0, pl.ds(i, nr, stride=S5_R), :] for i in range(S5_R) for h in (xa_ref, xb_ref)],
                        axis=1).astype(BF16)
    g_ref[...] = jnp.dot(x, winj_ref[...], preferred_element_type=F32)
    n_tiles = g_ref.shape[0] // SUBLANES
    row = lax.broadcasted_iota(jnp.int32, (SUBLANES, n), 0)
    zeros = jnp.zeros((SUBLANES, n), F32)
    if reverse:
        shift1, e_in, e_out = SUBLANES - 1, SUBLANES - 1, 0
    else:
        shift1, e_in, e_out = 1, 0, SUBLANES - 1

    def make_body(first_tile):
        def body(i, carry):
            hr, hi = carry
            t = (first_tile - i) if reverse else (first_tile + i)
            r = pl.multiple_of(t * SUBLANES, SUBLANES)
            br = g_ref[pl.ds(r, SUBLANES), 0:n]
            bi = g_ref[pl.ds(r, SUBLANES), n:2 * n]
            for k, s in enumerate((1, 2, 4)):
                ar = ap_ref[SUBLANES + k:SUBLANES + k + 1, 0:n]
                ai = ap_ref[SUBLANES + k:SUBLANES + k + 1, n:2 * n]
                if reverse:
                    keep = row < SUBLANES - s
                    shift = SUBLANES - s
                else:
                    keep = row >= s
                    shift = s
                brs = pltpu.roll(br, shift, 0)
                bis = pltpu.roll(bi, shift, 0)
                nr = ar * brs - ai * bis
                ni = ar * bis + ai * brs
                br = br + jnp.where(keep, nr, 0.0)
                bi = bi + jnp.where(keep, ni, 0.0)
            cr = ap_ref[0:SUBLANES, 0:n]
            ci = ap_ref[0:SUBLANES, n:2 * n]
            out_r = br + (cr * hr - ci * hi)
            out_i = bi + (cr * hi + ci * hr)
            g_ref[pl.ds(r, SUBLANES), 0:n] = jnp.where(row == e_in, hr, pltpu.roll(out_r, shift1, 0))
            g_ref[pl.ds(r, SUBLANES), n:2 * n] = jnp.where(row == e_in, hi, pltpu.roll(out_i, shift1, 0))
            return (jnp.broadcast_to(out_r[e_out:e_out + 1, :], (SUBLANES, n)),
                    jnp.broadcast_to(out_i[e_out:e_out + 1, :], (SUBLANES, n)))
        return body

    if reverse:
        carry = lax.fori_loop(0, n_ctx_tiles, make_body(n_ctx_tiles - 1), (zeros, zeros))
        lax.fori_loop(0, n_tiles - n_ctx_tiles, make_body(n_tiles - 1), carry)
    else:
        lax.fori_loop(0, n_tiles, make_body(0), (zeros, zeros))
    y = (jnp.dot(x, wloc_ref[...], preferred_element_type=F32)
         + jnp.dot(g_ref[...].astype(BF16), wro_ref[...], preferred_element_type=F32))
    for i in range(S5_R):
        for k, h in enumerate((ya_ref, yb_ref)):
            h[0, pl.ds(i, nr, stride=S5_R), :] = y[:, (2 * i + k) * LANES:(2 * i + k + 1) * LANES]


def _s5_params(a_re, a_im, log_dt, b_re, b_im, c_re, c_im, reverse):
    rr = S5_R
    dt = jnp.exp(log_dt)[:, None]
    den = a_re * a_re + a_im * a_im
    mag = jnp.exp(dt * a_re)
    ab_re = mag * jnp.cos(dt * a_im)
    ab_im = mag * jnp.sin(dt * a_im)
    f_re = ((ab_re - 1.0) * a_re + ab_im * a_im) / den
    f_im = (ab_im * a_re - (ab_re - 1.0) * a_im) / den
    bb_re = f_re[..., None] * b_re - f_im[..., None] * b_im
    bb_im = f_re[..., None] * b_im + f_im[..., None] * b_re
    grp_tok = (np.arange(rr * S5_WIDTH) // S5_GROUP) % S5_GROUPS
    grp_state = np.arange(S5_NSTATE) // S5_STATE

    def block_diag(t, lead, grp_rows, grp_cols):
        n_lead, minor, ncols = t.shape
        full = jnp.broadcast_to(t[:, None], (n_lead, S5_GROUPS, minor, ncols)).reshape(-1, ncols)
        return jnp.where(jnp.asarray(grp_rows[:, None] == grp_cols[None, :]), full, 0.0)

    def apow(k):
        k = k.astype(F32)[:, None, None]
        m = jnp.exp(k * dt[None] * a_re[None])
        return m * jnp.cos(k * dt[None] * a_im[None]), m * jnp.sin(k * dt[None] * a_im[None])

    steps = jnp.arange(rr)
    rows = rr * S5_WIDTH
    er, ei = apow(steps if reverse else (rr - 1 - steps))
    inj_re = er[..., None] * bb_re[None] - ei[..., None] * bb_im[None]
    inj_im = er[..., None] * bb_im[None] + ei[..., None] * bb_re[None]
    def inj_map(t):
        t = jnp.transpose(t, (0, 3, 1, 2)).reshape(rr, S5_GROUP, S5_NSTATE)
        return block_diag(t, rr, grp_tok, grp_state)

    winj = jnp.concatenate([inj_map(inj_re), inj_map(inj_im)], axis=1)
    fr, fi = apow((rr - steps) if reverse else (steps + 1))
    ro_re = c_re[None] * fr[:, :, None, :] - c_im[None] * fi[:, :, None, :]
    ro_im = c_re[None] * fi[:, :, None, :] + c_im[None] * fr[:, :, None, :]
    def ro_map(t):
        t = jnp.transpose(t, (3, 0, 1, 2)).reshape(1, S5_STATE, rows)
        return block_diag(t, 1, grp_state, grp_tok)

    wro = jnp.concatenate([ro_map(ro_re), -ro_map(ro_im)], axis=0)
    kr, ki = apow(steps)
    ab_r = kr[..., None] * bb_re[None] - ki[..., None] * bb_im[None]
    ab_i = kr[..., None] * bb_im[None] + ki[..., None] * bb_re[None]
    kk = jnp.einsum('gop,kgpc->kgoc', c_re, ab_r) - jnp.einsum('gop,kgpc->kgoc', c_im, ab_i)
    src = jnp.arange(rr)[:, None]
    tgt = jnp.arange(rr)[None, :]
    lag = (src - tgt) if reverse else (tgt - src)
    kmat = jnp.where((lag >= 0)[:, :, None, None, None], kk[jnp.clip(lag, 0, rr - 1)], 0.0)
    wloc = block_diag(jnp.transpose(kmat, (0, 4, 1, 2, 3)).reshape(rr, S5_GROUP, rows), rr, grp_tok, grp_tok)
    i8 = jnp.arange(SUBLANES)
    dist = (SUBLANES - i8) if reverse else (i8 + 1)
    ks = jnp.concatenate([dist, jnp.array([1, 2, 4]), jnp.zeros((5,), dist.dtype)]) * rr
    pr, pi = apow(ks)
    ap = jnp.concatenate([pr.reshape(16, S5_NSTATE), pi.reshape(16, S5_NSTATE)], axis=1)
    return winj.astype(BF16), wloc.astype(BF16), wro.astype(BF16), ap


def _s5_scan(p_d, lp, d, ctx_len):
    b, s, w = p_d.shape
    nr = s // S5_R
    wr = S5_R * w
    reverse = d == 1
    n_ctx_tiles = ctx_len // (S5_R * SUBLANES)
    winj, wloc, wro, ap = _s5_params(lp['s5_a_re'][d], lp['s5_a_im'][d], lp['s5_log_dt'][d], lp['s5_b_re'][d],
                                     lp['s5_b_im'][d], lp['s5_c_re'][d], lp['s5_c_im'][d], reverse)
    assert w == 2 * LANES
    full = lambda shape: pl.BlockSpec(shape, lambda i: (0, 0))
    half = jax.ShapeDtypeStruct((b, s, LANES), F32)
    return pl.pallas_call(
        functools.partial(_s5_kernel, reverse=reverse, n_ctx_tiles=n_ctx_tiles),
        out_shape=(half, half),
        grid=(b,),
        in_specs=[pl.BlockSpec((1, s, LANES), lambda i: (i, 0, 0)), pl.BlockSpec((1, s, LANES), lambda i: (i, 0, 1)),
                  full((wr, 2 * S5_NSTATE)), full((wr, wr)), full((2 * S5_NSTATE, wr)), full((16, 2 * S5_NSTATE))],
        out_specs=(pl.BlockSpec((1, s, LANES), lambda i: (i, 0, 0)), pl.BlockSpec((1, s, LANES), lambda i: (i, 0, 0))),
        scratch_shapes=[pltpu.VMEM((nr, 2 * S5_NSTATE), F32)],
        compiler_params=_cp(("parallel",)),
        name="s5_scan_bwd" if reverse else "s5_scan_fwd",
    )(p_d, p_d, winj, wloc, wro, ap)


def _s5_out_kernel(yfa_ref, yfb_ref, yba_ref, ybb_ref, u_ref, d_ref, w_ref, b_ref, o_ref):
    y = (jnp.concatenate([yfa_ref[...] + yba_ref[...], yfb_ref[...] + ybb_ref[...]], axis=1)
         + d_ref[...] * u_ref[...])
    g = _gelu(y)
    z = jnp.dot(g.astype(BF16), w_ref[...], preferred_element_type=F32) + b_ref[...]
    o_ref[...] = (g * _sigmoid(z)).astype(o_ref.dtype)


def _s5_mixer(p_d, lp, ctx_len):
    b, s, w = p_d.shape
    yfa, yfb = _s5_scan(p_d, lp, 0, ctx_len)
    yba, ybb = _s5_scan(p_d, lp, 1, ctx_len)
    m = b * s
    tm = _pick_tile(m, (1024, 512, 256))
    row = pl.BlockSpec((tm, w), lambda i: (i, 0))
    hrow = pl.BlockSpec((tm, LANES), lambda i: (i, 0))
    vec = pl.BlockSpec((1, w), lambda i: (0, 0))
    flat = lambda a: a.reshape(m, LANES)
    out = pl.pallas_call(
        _s5_out_kernel,
        out_shape=jax.ShapeDtypeStruct((m, w), BF16),
        grid=(m // tm,),
        in_specs=[hrow, hrow, hrow, hrow, row, vec, pl.BlockSpec((w, w), lambda i: (0, 0)), vec],
        out_specs=row,
        compiler_params=_cp(("parallel",)),
        name="s5_out",
    )(flat(yfa), flat(yfb), flat(yba), flat(ybb), p_d.reshape(m, w), lp['s5_d'].reshape(1, w),
      lp['s5_w_glu'].astype(BF16), lp['s5_b_glu'].reshape(1, w))
    return out.reshape(b, s, w)


NA_QROWS = 4


def _natten_plan(rows):
    kr = min(NA_WIN_R, rows)
    span = kr + NA_QROWS - 1
    variants, index, blk_var = [], {}, []
    for blk in range(rows // NA_QROWS):
        r0 = blk * NA_QROWS
        ws = int(np.clip(r0 - kr // 2, 0, rows - span))
        dr = np.zeros((NA_QROWS, span), np.int32)
        ok = np.zeros((NA_QROWS, span), bool)
        for q in range(NA_QROWS):
            r = r0 + q
            rs = int(np.clip(r - kr // 2, 0, rows - kr))
            for i in range(span):
                ok[q, i] = rs <= ws + i < rs + kr
                dr[q, i] = (ws + i - r + (NA_WIN_R - 1)) if ok[q, i] else 0
        key = dr.tobytes() + ok.tobytes()
        if key not in index:
            index[key] = len(variants)
            variants.append((dr, ok))
        blk_var.append(index[key])
    return (np.stack([v[0] for v in variants]), np.stack([v[1] for v in variants]),
            np.asarray(blk_var, np.int32))


def _natten_bias(rpb, dr, ok):
    nv, nq, span = dr.shape
    n_dr, n_dc = 2 * NA_WIN_R - 1, 2 * NA_WIN_C - 1
    w = np.arange(GRID_W)
    cs = np.clip(w - NA_WIN_C // 2, 0, GRID_W - NA_WIN_C)
    ok_col = (w[None, :] >= cs[:, None]) & (w[None, :] < cs[:, None] + NA_WIN_C)
    dc = w[None, :] - w[:, None] + (NA_WIN_C - 1)
    e_dc = (dc[None] == np.arange(n_dc)[:, None, None]).astype(np.float32)
    e_dr = ((dr[..., None] == np.arange(n_dr)) & ok[..., None]).astype(np.float32)
    g = jnp.einsum('vqir,hrk,kwc->vhqwic', e_dr, rpb, e_dc, precision=HIGHEST)
    ok_all = ok[:, None, :, None, :, None] & ok_col[None, None, None, :, None, :]
    g = jnp.where(jnp.asarray(ok_all), g, MASK_VALUE)
    return g.reshape(nv, NA_HEADS, nq * GRID_W, span * GRID_W).astype(F32)


def _attend(q2, keys, vals, biases, lane):
    nq = q2.shape[0]
    sels = [(lane >= hh * NA_HEAD_DIM) & (lane < (hh + 1) * NA_HEAD_DIM) for hh in range(2)]
    qs = jnp.concatenate([jnp.where(sel, q2, jnp.zeros_like(q2)) for sel in sels], axis=0)
    ss = []
    for k_i, b0, b1 in zip(keys, biases[0], biases[1]):
        s_i = lax.dot_general(qs, k_i, (((1,), (1,)), ((), ())), preferred_element_type=F32)
        if b0 is not None:
            s_i = jnp.concatenate([s_i[:nq] + b0, s_i[nq:] + b1], axis=0)
        ss.append(s_i)
    m = ss[0].max(axis=-1, keepdims=True)
    for s_i in ss[1:]:
        m = jnp.maximum(m, s_i.max(axis=-1, keepdims=True))
    ps = [jnp.exp(s_i - m) for s_i in ss]
    den = ps[0].sum(axis=-1, keepdims=True)
    for p_i in ps[1:]:
        den = den + p_i.sum(axis=-1, keepdims=True)
    o = jnp.dot(ps[0].astype(BF16), vals[0], preferred_element_type=F32)
    for p_i, v_i in zip(ps[1:], vals[1:]):
        o = o + jnp.dot(p_i.astype(BF16), v_i, preferred_element_type=F32)
    o = o / den
    return jnp.where(sels[0], o[:nq], o[nq:])


def _natten_kernel(var_ref, q_ref, k_ref, v_ref, bias_ref, o_ref, *, rows, ctx_len):
    kr = min(NA_WIN_R, rows)
    span = kr + NA_QROWS - 1
    r0 = pl.program_id(1) * NA_QROWS
    ws = jnp.clip(r0 - kr // 2, 0, rows - span)
    base = pl.multiple_of(ctx_len + ws * GRID_W, GRID_W)
    lane = lax.broadcasted_iota(jnp.int32, (NA_QROWS * GRID_W, LANES), 1)
    scale = NA_HEAD_DIM ** -0.5
    for hp in range(NA_HEADS // 2):
        ls = slice(hp * LANES, (hp + 1) * LANES)
        q2 = q_ref[0, :, ls] * scale
        kw = k_ref[0, pl.ds(base, span * GRID_W), ls]
        vw = v_ref[0, pl.ds(base, span * GRID_W), ls]
        kc = k_ref[0, 0:ctx_len, ls]
        vc = v_ref[0, 0:ctx_len, ls]
        biases = [[bias_ref[0, 2 * hp + hh], None] for hh in range(2)]
        out = _attend(q2, [kw, kc], [vw, vc], biases, lane)
        o_ref[0, :, ls] = out.astype(o_ref.dtype)


def _ctx_attn_kernel(q_ref, k_ref, v_ref, o_ref, *, ctx_len):
    lane = lax.broadcasted_iota(jnp.int32, (ctx_len, LANES), 1)
    scale = NA_HEAD_DIM ** -0.5
    for hp in range(NA_HEADS // 2):
        ls = slice(hp * LANES, (hp + 1) * LANES)
        out = _attend(q_ref[0, :, ls] * scale, [k_ref[0, :, ls]], [v_ref[0, :, ls]], [[None], [None]], lane)
        o_ref[0, :, ls] = out.astype(o_ref.dtype)


def _natten_mixer(p_c, rpb, ctx_len, with_ctx):
    b, s, _ = p_c.shape
    l = s - ctx_len
    rows = l // GRID_W
    kr = min(NA_WIN_R, rows)
    span = kr + NA_QROWS - 1
    nq = NA_QROWS * GRID_W
    dr, ok, blk_var = _natten_plan(rows)
    bias = _natten_bias(rpb, dr, ok)
    cb = ctx_len // nq
    grid_spec = pltpu.PrefetchScalarGridSpec(
        num_scalar_prefetch=1,
        grid=(b, rows // NA_QROWS),
        in_specs=[pl.BlockSpec((1, nq, NA_WIDTH), lambda i, r, var: (i, cb + r, 0)),
                  pl.BlockSpec((1, s, NA_WIDTH), lambda i, r, var: (i, 0, 1)),
                  pl.BlockSpec((1, s, NA_WIDTH), lambda i, r, var: (i, 0, 2)),
                  pl.BlockSpec((1, NA_HEADS, nq, span * GRID_W), lambda i, r, var: (var[r], 0, 0, 0))],
        out_specs=pl.BlockSpec((1, nq, NA_WIDTH), lambda i, r, var: (i, r, 0)),
    )
    y_l = pl.pallas_call(
        functools.partial(_natten_kernel, rows=rows, ctx_len=ctx_len),
        out_shape=jax.ShapeDtypeStruct((b, l, NA_WIDTH), BF16),
        grid_spec=grid_spec,
        compiler_params=_cp(("parallel", "arbitrary")),
        name="natten",
    )(jnp.asarray(blk_var), p_c, p_c, p_c, bias)
    if not with_ctx:
        return y_l, None
    y_c = pl.pallas_call(
        functools.partial(_ctx_attn_kernel, ctx_len=ctx_len),
        out_shape=jax.ShapeDtypeStruct((b, ctx_len, NA_WIDTH), BF16),
        grid=(b,),
        in_specs=[pl.BlockSpec((1, ctx_len, NA_WIDTH), lambda i: (i, 0, 0)),
                  pl.BlockSpec((1, ctx_len, NA_WIDTH), lambda i: (i, 0, 1)),
                  pl.BlockSpec((1, ctx_len, NA_WIDTH), lambda i: (i, 0, 2))],
        out_specs=pl.BlockSpec((1, ctx_len, NA_WIDTH), lambda i: (i, 0, 0)),
        compiler_params=_cp(("parallel",)),
        name="ctx_attn",
    )(p_c, p_c, p_c)
    return y_l, y_c


DFT_ROWS = 64


def _dft_gen_kernel(ca_ref, sa_ref, cb_ref, sb_ref, fwd_ref, inv_ref, *, l):
    i = pl.program_id(0)
    ca = ca_ref[0]
    sa = sa_ref[0]
    cb = cb_ref[...]
    sb = sb_ref[...]
    gc = ca * cb - sa * sb
    gs = sa * cb + ca * sb
    x = i * DFT_ROWS + lax.broadcasted_iota(jnp.int32, (DFT_ROWS, l), 0)
    y = lax.broadcasted_iota(jnp.int32, (DFT_ROWS, l), 1)
    n = 2.0 * l
    nyq_x = jnp.where((x & 1) == 0, 1.0, -1.0)
    fwd_ref[:, 0:l] = gc.astype(fwd_ref.dtype)
    fwd_ref[:, l:2 * l] = jnp.where(y == 0, nyq_x, -gs).astype(fwd_ref.dtype)
    scale = jnp.where(x == 0, 1.0 / n, 2.0 / n)
    nyq_y = jnp.where((y & 1) == 0, 1.0, -1.0)
    inv_ref[0] = (scale * gc).astype(inv_ref.dtype)
    inv_ref[1] = (scale * jnp.where(x == 0, nyq_y, -gs)).astype(inv_ref.dtype)


def _dft_matrices(l):
    n = 2 * l
    k1 = l // DFT_ROWS
    y = np.arange(l, dtype=np.int64)
    xa = (DFT_ROWS * np.arange(k1, dtype=np.int64))[:, None]
    xb = np.arange(DFT_ROWS, dtype=np.int64)[:, None]
    ang_a = jnp.asarray(((xa * y[None, :]) % n).astype(np.float32)) * F32(2.0 * math.pi / n)
    ang_b = jnp.asarray(((xb * y[None, :]) % n).astype(np.float32)) * F32(2.0 * math.pi / n)
    ca, sa = jnp.cos(ang_a).reshape(k1, 1, l), jnp.sin(ang_a).reshape(k1, 1, l)
    cb, sb = jnp.cos(ang_b), jnp.sin(ang_b)
    row = pl.BlockSpec((1, 1, l), lambda i: (i, 0, 0))
    tab = pl.BlockSpec((DFT_ROWS, l), lambda i: (0, 0))
    fwd, inv = pl.pallas_call(
        functools.partial(_dft_gen_kernel, l=l),
        out_shape=(jax.ShapeDtypeStruct((l, 2 * l), BF16), jax.ShapeDtypeStruct((2, l, l), BF16)),
        grid=(k1,),
        in_specs=[row, row, tab, tab],
        out_specs=(pl.BlockSpec((DFT_ROWS, 2 * l), lambda i: (i, 0)),
                   pl.BlockSpec((2, DFT_ROWS, l), lambda i: (0, i, 0))),
        compiler_params=_cp(("parallel",)),
        name="dft_gen",
    )(ca, sa, cb, sb)
    return fwd, inv.reshape(2 * l, l)


def _hy_filter_kernel(w1t_ref, w1c_ref, w1s_ref, b1_ref, w2_ref, b2_ref, w3_ref, b3_ref, fr_ref, dl_ref,
                      h_ref, asum_ref, *, l, tl):
    j = pl.program_id(0)
    t = (j * tl + lax.broadcasted_iota(jnp.int32, (1, tl), 1)).astype(F32)
    t_norm = t / l
    bands = (1 + lax.broadcasted_iota(jnp.int32, (HY_BANDS, 1), 0)).astype(F32)
    ang = (2.0 * math.pi / l) * t * bands
    fr = fr_ref[...]
    lin = (w1t_ref[...] * t_norm
           + jnp.dot(w1c_ref[...], jnp.cos(ang), preferred_element_type=F32, precision=HIGHEST)
           + jnp.dot(w1s_ref[...], jnp.sin(ang), preferred_element_type=F32, precision=HIGHEST))
    h = jnp.sin(fr * (lin + b1_ref[...]))
    h = jnp.sin(fr * (jnp.dot(w2_ref[...], h, preferred_element_type=F32, precision=HIGHEST) + b2_ref[...]))
    h = jnp.dot(w3_ref[...], h, preferred_element_type=F32, precision=HIGHEST) + b3_ref[...]
    window = jnp.exp(-t_norm * dl_ref[...])
    first = (j * tl + lax.broadcasted_iota(jnp.int32, (HY_WIDTH, tl), 1)) == 0

    @pl.when(j == 0)
    def _():
        asum_ref[...] = jnp.zeros_like(asum_ref)

    for blk in range(2 * HY_ORDER):
        rs = slice(blk * HY_WIDTH, (blk + 1) * HY_WIDTH)
        hb = h[rs, :] * window
        if blk >= HY_ORDER:
            hb = jnp.where(first, 0.0, hb)
        h_ref[rs, :] = hb
        asum_ref[rs, :] += jnp.sum(jnp.abs(hb), axis=1, keepdims=True)


def _hy_filters(lp, l):
    tl = min(l, 512)
    hid = HY_FILT_HID
    w1 = lp['hy_w1']
    col = lambda v: v.reshape(-1, 1)
    deltas = np.abs(np.linspace(HY_MIN_DECAY, HY_MAX_DECAY, HY_WIDTH, dtype=np.float32)).reshape(-1, 1)
    full = lambda shape: pl.BlockSpec(shape, lambda j: (0, 0))
    n_out = 2 * HY_ORDER * HY_WIDTH
    return pl.pallas_call(
        functools.partial(_hy_filter_kernel, l=l, tl=tl),
        out_shape=(jax.ShapeDtypeStruct((n_out, l), F32), jax.ShapeDtypeStruct((n_out, 1), F32)),
        grid=(l // tl,),
        in_specs=[full((hid, 1)), full((hid, HY_BANDS)), full((hid, HY_BANDS)), full((hid, 1)),
                  full((hid, hid)), full((hid, 1)), full((n_out, hid)), full((n_out, 1)),
                  full((hid, 1)), full((HY_WIDTH, 1))],
        out_specs=(pl.BlockSpec((n_out, tl), lambda j: (0, j)), pl.BlockSpec((n_out, 1), lambda j: (0, 0))),
        compiler_params=_cp(("arbitrary",)),
        name="hyena_filters",
    )(w1[0:1].T, w1[1:1 + HY_BANDS].T, w1[1 + HY_BANDS:].T, col(lp['hy_b1']), lp['hy_w2'].T, col(lp['hy_b2']),
      lp['hy_w3'].T, col(lp['hy_b3']), col(lp['hy_freq']), jnp.asarray(deltas))


def _hy_short_conv_kernel(x_ref, w_ref, b_ref, z_ref, zbf_ref, *, l):
    x = x_ref[0]
    t = lax.broadcasted_iota(jnp.int32, x.shape, 1)
    left = HY_SHORT // 2
    z = jnp.zeros(x.shape, F32) + b_ref[...]
    for k in range(HY_SHORT):
        off = k - left
        if off == 0:
            sh = x
        else:
            sh = pltpu.roll(x, (-off) % l, 1)
            sh = jnp.where((t + off >= 0) & (t + off < l), sh, 0.0)
        z = z + sh * w_ref[:, k:k + 1]
    z_ref[0] = z
    zbf_ref[0] = z.astype(zbf_ref.dtype)


def _hy_conv_kernel(y_ref, fc_ref, fs_ref, ic_ref, is_ref, kfc_ref, kfs_ref, kbc_ref, kbs_ref, n_ref,
                    o_ref, *, nb):
    j = pl.program_id(1)

    @pl.when(j == 0)
    def _():
        o_ref[...] = jnp.zeros_like(o_ref)

    y = y_ref[...]
    zr = jnp.dot(y, fc_ref[...], preferred_element_type=F32)
    zi = jnp.dot(y, fs_ref[...], preferred_element_type=F32)
    inv_n = 1.0 / (n_ref[...] + 1e-6)
    kr = (kfc_ref[...] + kbc_ref[...]) * inv_n
    ki = (kfs_ref[...] - kbs_ref[...]) * inv_n
    tn = kr.shape[1]
    f0 = (j * tn + lax.broadcasted_iota(jnp.int32, kr.shape, 1)) == 0
    ki = jnp.where(f0, (kfs_ref[...] + kbs_ref[...]) * inv_n, ki)
    prs, pis = [], []
    for bb in range(nb):
        rs = slice(bb * HY_WIDTH, (bb + 1) * HY_WIDTH)
        a, b = zr[rs], zi[rs]
        prs.append(a * kr - jnp.where(f0, 0.0, b * ki))
        pis.append(jnp.where(f0, b * ki, a * ki + b * kr))
    pr = jnp.concatenate(prs, axis=0).astype(BF16)
    pi = jnp.concatenate(pis, axis=0).astype(BF16)
    o_ref[...] += (jnp.dot(pr, ic_ref[...], preferred_element_type=F32)
                   + jnp.dot(pi, is_ref[...], preferred_element_type=F32))


def _hy_long_conv(ybf, n_tiles, nb, row_stride, fwd, inv, kf, asum, order, l):
    tmh = nb * HY_WIDTH
    m = n_tiles * tmh
    tn = min(l, 256)
    jn = l // tn
    o_f = order
    o_b = HY_ORDER + order
    return pl.pallas_call(
        functools.partial(_hy_conv_kernel, nb=nb),
        out_shape=jax.ShapeDtypeStruct((m, l), F32),
        grid=(n_tiles, jn),
        in_specs=[pl.BlockSpec((tmh, l), lambda i, j: (i * row_stride, 0)),
                  pl.BlockSpec((l, tn), lambda i, j: (0, j)),
                  pl.BlockSpec((l, tn), lambda i, j: (0, jn + j)),
                  pl.BlockSpec((tn, l), lambda i, j: (j, 0)),
                  pl.BlockSpec((tn, l), lambda i, j: (jn + j, 0)),
                  pl.BlockSpec((HY_WIDTH, tn), lambda i, j: (o_f, j)),
                  pl.BlockSpec((HY_WIDTH, tn), lambda i, j: (o_f, jn + j)),
                  pl.BlockSpec((HY_WIDTH, tn), lambda i, j: (o_b, j)),
                  pl.BlockSpec((HY_WIDTH, tn), lambda i, j: (o_b, jn + j)),
                  pl.BlockSpec((HY_WIDTH, 1), lambda i, j: (order, 0))],
        out_specs=pl.BlockSpec((tmh, l), lambda i, j: (i, 0)),
        compiler_params=_cp(("parallel", "arbitrary")),
        name="hyena_long_conv",
    )(ybf, fwd, fwd, inv, inv, kf, kf, kf, kf, asum)


def _hy_gate_kernel(g_ref, c_ref, y_ref, bias_ref, o_ref, obf_ref):
    o = g_ref[0] * (c_ref[0] + y_ref[0] * bias_ref[...])
    o_ref[0] = o
    obf_ref[0] = o.astype(obf_ref.dtype)


def _hy_gate(z, conv, y, bias_col, gate_blk, y_blk):
    b, _, l = z.shape
    tl = min(l, 1024)
    spec = lambda blk: pl.BlockSpec((1, HY_WIDTH, tl), lambda i, j: (i, blk, j))
    return pl.pallas_call(
        _hy_gate_kernel,
        out_shape=(jax.ShapeDtypeStruct((b, HY_WIDTH, l), F32), jax.ShapeDtypeStruct((b, HY_WIDTH, l), BF16)),
        grid=(b, l // tl),
        in_specs=[spec(gate_blk), spec(0), spec(y_blk), pl.BlockSpec((HY_WIDTH, 1), lambda i, j: (0, 0))],
        out_specs=(spec(0), spec(0)),
        compiler_params=_cp(("parallel", "parallel")),
        name="hyena_gate",
    )(z, conv, y, bias_col)


def _hyena_sequence(p_bt, lp, dft):
    b, c3, l = p_bt.shape
    fwd, inv = dft
    h, asum = _hy_filters(lp, l)
    kf = _matmul(h.astype(BF16), fwd, F32, name="hyena_filter_dft")
    asum2 = asum.reshape(2, HY_ORDER * HY_WIDTH).sum(axis=0).reshape(HY_ORDER * HY_WIDTH, 1)
    blk = pl.BlockSpec((1, LANES, l), lambda i, g: (i, g, 0))
    z, z_bf = pl.pallas_call(
        functools.partial(_hy_short_conv_kernel, l=l),
        out_shape=(jax.ShapeDtypeStruct((b, c3, l), F32), jax.ShapeDtypeStruct((b, c3, l), BF16)),
        grid=(b, c3 // LANES),
        in_specs=[blk,
                  pl.BlockSpec((LANES, HY_SHORT), lambda i, g: (g, 0)),
                  pl.BlockSpec((LANES, 1), lambda i, g: (g, 0))],
        out_specs=(blk, blk),
        compiler_params=_cp(("parallel", "parallel")),
        name="hyena_short_conv",
    )(p_bt, lp['hy_conv_w'].T, lp['hy_conv_b'].reshape(c3, 1))
    n_blk = c3 // HY_WIDTH
    nb2 = 2 if b % 2 == 0 else 1
    conv1 = _hy_long_conv(z_bf.reshape(b * c3, l), b, 1, n_blk, fwd, inv, kf, asum2, 0, l).reshape(b, HY_WIDTH, l)
    y1, y1_bf = _hy_gate(z, conv1, z, lp['hy_bias'][0].reshape(HY_WIDTH, 1), 1, 0)
    conv2 = _hy_long_conv(y1_bf.reshape(b * HY_WIDTH, l), b // nb2, nb2, 1, fwd, inv, kf, asum2, 1, l)
    _, y2_bf = _hy_gate(z, conv2.reshape(b, HY_WIDTH, l), y1, lp['hy_bias'][1].reshape(HY_WIDTH, 1), 2, 0)
    return y2_bf


FFT_N2 = 128
FFT_KB = 4
FFT_NB = 8
FFT_PITCH = 136


def _fft_stage1(xa_ref, xb_ref, f1_ref, xpa, xpb, za, zb, n_in, n1):
    for blk in range(n_in):
        xpa[blk * FFT_PITCH:blk * FFT_PITCH + FFT_N2, :] = xa_ref[0, blk * FFT_N2:(blk + 1) * FFT_N2, :]
        xpb[blk * FFT_PITCH:blk * FFT_PITCH + FFT_N2, :] = xb_ref[0, blk * FFT_N2:(blk + 1) * FFT_N2, :]
    f1 = f1_ref[...]
    for c0 in range(0, FFT_N2, FFT_NB):
        cols = []
        for q in range(FFT_NB):
            cols.append(xpa[pl.ds(c0 + q, n_in, stride=FFT_PITCH), :])
            cols.append(xpb[pl.ds(c0 + q, n_in, stride=FFT_PITCH), :])
        z = jnp.dot(f1, jnp.concatenate(cols, axis=1).astype(BF16), preferred_element_type=F32)
        for q in range(FFT_NB):
            za[pl.ds(c0 + q, 2 * n1, stride=FFT_PITCH), :] = z[:, (2 * q) * LANES:(2 * q + 1) * LANES]
            zb[pl.ds(c0 + q, 2 * n1, stride=FFT_PITCH), :] = z[:, (2 * q + 1) * LANES:(2 * q + 2) * LANES]


def _fft_load_z(za, zb, k1, n1):
    r_re = pl.multiple_of(k1 * FFT_PITCH, SUBLANES)
    r_im = pl.multiple_of((n1 + k1) * FFT_PITCH, SUBLANES)
    z = jnp.concatenate(
        [jnp.concatenate([za[pl.ds(r_re, FFT_N2), :], zb[pl.ds(r_re, FFT_N2), :]], axis=1),
         jnp.concatenate([za[pl.ds(r_im, FFT_N2), :], zb[pl.ds(r_im, FFT_N2), :]], axis=1)], axis=0)
    return z.astype(BF16), r_re, r_im


def _hy_fft_filter_kernel(xa_ref, xb_ref, f1_ref, ef_ref, y_ref, xpa, xpb, za, zb, *, n1):
    j = pl.program_id(1)

    @pl.when(j == 0)
    def _():
        _fft_stage1(xa_ref, xb_ref, f1_ref, xpa, xpb, za, zb, n1, n1)

    for q in range(FFT_KB):
        z, _, _ = _fft_load_z(za, zb, j * FFT_KB + q, n1)
        y = jnp.dot(ef_ref[q], z, preferred_element_type=F32)
        y_ref[0, 0, q] = y[:FFT_N2]
        y_ref[0, 1, q] = y[FFT_N2:]


def _hy_fft_conv_kernel(xa_ref, xb_ref, ga_ref, gb_ref, bias_ref, f1_ref, f1i_ref, ef_ref, ei_ref, kf_ref, n_ref,
                        oa_ref, ob_ref, xpa, xpb, za, zb, *, n1):
    n_in = n1 // 2
    j = pl.program_id(1)

    @pl.when(j == 0)
    def _():
        _fft_stage1(xa_ref, xb_ref, f1_ref, xpa, xpb, za, zb, n_in, n1)

    inv_n = 1.0 / (n_ref[0] + 1e-6)
    for q in range(FFT_KB):
        z, r_re, r_im = _fft_load_z(za, zb, j * FFT_KB + q, n1)
        y = jnp.dot(ef_ref[q], z, preferred_element_type=F32)
        kr = kf_ref[0, 0, q] * inv_n
        ki = kf_ref[0, 1, q] * inv_n
        yr, yi = y[:FFT_N2], y[FFT_N2:]
        p = jnp.concatenate([yr * kr - yi * ki, yr * ki + yi * kr], axis=0).astype(BF16)
        u = jnp.dot(ei_ref[q], p, preferred_element_type=F32)
        za[pl.ds(r_re, FFT_N2), :] = u[:FFT_N2, :LANES]
        zb[pl.ds(r_re, FFT_N2), :] = u[:FFT_N2, LANES:]
        za[pl.ds(r_im, FFT_N2), :] = u[FFT_N2:, :LANES]
        zb[pl.ds(r_im, FFT_N2), :] = u[FFT_N2:, LANES:]

    @pl.when(j == pl.num_programs(1) - 1)
    def _():
        f1i = f1i_ref[...]
        bias = bias_ref[0]
        for c0 in range(0, FFT_N2, FFT_NB):
            cols = []
            for q in range(FFT_NB):
                cols.append(za[pl.ds(c0 + q, 2 * n1, stride=FFT_PITCH), :])
                cols.append(zb[pl.ds(c0 + q, 2 * n1, stride=FFT_PITCH), :])
            yv = jnp.dot(f1i, jnp.concatenate(cols, axis=1).astype(BF16), preferred_element_type=F32)
            for q in range(FFT_NB):
                za[pl.ds(c0 + q, n_in, stride=FFT_PITCH), :] = yv[:, (2 * q) * LANES:(2 * q + 1) * LANES]
                zb[pl.ds(c0 + q, n_in, stride=FFT_PITCH), :] = yv[:, (2 * q + 1) * LANES:(2 * q + 2) * LANES]
        for blk in range(n_in):
            rows = slice(blk * FFT_N2, (blk + 1) * FFT_N2)
            prow = slice(blk * FFT_PITCH, blk * FFT_PITCH + FFT_N2)
            oa_ref[0, rows, :] = ga_ref[0, rows, :] * (za[prow, :] + xa_ref[0, rows, :] * bias[:, :LANES])
            ob_ref[0, rows, :] = gb_ref[0, rows, :] * (zb[prow, :] + xb_ref[0, rows, :] * bias[:, LANES:])


def _fft_tables(l):
    n = 2 * l
    n1 = n // FFT_N2
    n_in = n1 // 2
    k1 = np.arange(n1)
    phi = 2.0 * np.pi * np.outer(k1, np.arange(n1)) / n1
    f1_full = np.concatenate([np.cos(phi), -np.sin(phi)], axis=0).astype(np.float32)
    f1 = f1_full[:, :n_in]
    f1i = (np.concatenate([np.cos(phi[:, :n_in]), -np.sin(phi[:, :n_in])], axis=0).T / n).astype(np.float32)
    n2 = np.arange(FFT_N2)
    alpha = 2.0 * np.pi * np.outer(k1, n2) / n
    beta = 2.0 * np.pi * np.outer(np.arange(FFT_N2), n2) / FFT_N2
    ca, sa = jnp.asarray(np.cos(alpha), F32)[:, None, :], jnp.asarray(np.sin(alpha), F32)[:, None, :]
    cb, sb = jnp.asarray(np.cos(beta), F32)[None], jnp.asarray(np.sin(beta), F32)[None]
    er = ca * cb - sa * sb
    ei = -(sa * cb + ca * sb)
    ef = jnp.concatenate([jnp.concatenate([er, -ei], axis=2), jnp.concatenate([ei, er], axis=2)], axis=1)
    ert, eit = jnp.swapaxes(er, 1, 2), jnp.swapaxes(ei, 1, 2)
    einv = jnp.concatenate([jnp.concatenate([ert, eit], axis=2), jnp.concatenate([-eit, ert], axis=2)], axis=1)
    return {'n1': n1, 'f1': jnp.asarray(f1, BF16), 'f1_full': jnp.asarray(f1_full, BF16), 'f1i': jnp.asarray(f1i, BF16),
            'ef': ef.astype(BF16), 'ei': einv.astype(BF16)}


def _hy_filter_tm_kernel(w1t_ref, w1c_ref, w1s_ref, b1_ref, w2_ref, b2_ref, w3_ref, b3_ref, fr_ref, dl_ref,
                         k_ref, asum_ref, *, l, tl):
    j = pl.program_id(0)
    n = j * tl + lax.broadcasted_iota(jnp.int32, (1, tl), 1)
    t = jnp.where(n < l, n, 2 * l - n).astype(F32)
    t_norm = t / l
    bands = (1 + lax.broadcasted_iota(jnp.int32, (HY_BANDS, 1), 0)).astype(F32)
    ang = (2.0 * math.pi / l) * t * bands
    fr = fr_ref[...]
    lin = (w1t_ref[...] * t_norm
           + jnp.dot(w1c_ref[...], jnp.cos(ang), preferred_element_type=F32, precision=HIGHEST)
           + jnp.dot(w1s_ref[...], jnp.sin(ang), preferred_element_type=F32, precision=HIGHEST))
    h = jnp.sin(fr * (lin + b1_ref[...]))
    h = jnp.sin(fr * (jnp.dot(w2_ref[...], h, preferred_element_type=F32, precision=HIGHEST) + b2_ref[...]))
    h = jnp.dot(w3_ref[...], h, preferred_element_type=F32, precision=HIGHEST) + b3_ref[...]
    window = jnp.exp(-t_norm * dl_ref[...])

    @pl.when(j == 0)
    def _():
        asum_ref[...] = jnp.zeros_like(asum_ref)

    for o in range(HY_ORDER):
        hf = h[o * HY_WIDTH:(o + 1) * HY_WIDTH, :]
        hb = h[(HY_ORDER + o) * HY_WIDTH:(HY_ORDER + o + 1) * HY_WIDTH, :]
        k = jnp.where(n > l, hb, hf) * window
        k = jnp.where(n == l, 0.0, k).T
        k_ref[o] = k
        asum_ref[o] += jnp.sum(jnp.abs(k), axis=0, keepdims=True)


def _hyena_latent(p_b, lp, tabs, ctx_len):
    b, s, c3 = p_b.shape
    l = s - ctx_len
    n = 2 * l
    n1 = tabs['n1']
    n_in = n1 // 2
    hid = HY_FILT_HID
    w = HY_WIDTH
    full = lambda shape: pl.BlockSpec(shape, lambda *idx: (0,) * len(shape))
    tl = 512
    w1 = lp['hy_w1']
    col = lambda v: v.reshape(-1, 1)
    deltas = np.abs(np.linspace(HY_MIN_DECAY, HY_MAX_DECAY, w, dtype=np.float32)).reshape(-1, 1)
    n_out = 2 * HY_ORDER * w
    k_tm, asum = pl.pallas_call(
        functools.partial(_hy_filter_tm_kernel, l=l, tl=tl),
        out_shape=(jax.ShapeDtypeStruct((HY_ORDER, n, w), F32), jax.ShapeDtypeStruct((HY_ORDER, 1, w), F32)),
        grid=(n // tl,),
        in_specs=[full((hid, 1)), full((hid, HY_BANDS)), full((hid, HY_BANDS)), full((hid, 1)), full((hid, hid)),
                  full((hid, 1)), full((n_out, hid)), full((n_out, 1)), full((hid, 1)), full((w, 1))],
        out_specs=(pl.BlockSpec((HY_ORDER, tl, w), lambda j: (0, j, 0)),
                   pl.BlockSpec((HY_ORDER, 1, w), lambda j: (0, 0, 0))),
        compiler_params=_cp(("arbitrary",)),
        name="hyena_filters_tm",
    )(w1[0:1].T, w1[1:1 + HY_BANDS].T, w1[1 + HY_BANDS:].T, col(lp['hy_b1']), lp['hy_w2'].T, col(lp['hy_b2']),
      lp['hy_w3'].T, col(lp['hy_b3']), col(lp['hy_freq']), jnp.asarray(deltas))
    nj = n1 // FFT_KB
    scr = pltpu.VMEM((2 * n1 * FFT_PITCH, LANES), F32)
    xp_full = pltpu.VMEM((n1 * FFT_PITCH, LANES), F32)
    xp_half = pltpu.VMEM((n_in * FFT_PITCH, LANES), F32)
    half = lambda blk: pl.BlockSpec((1, n, LANES), lambda i, j: (i, 0, blk))
    kf = pl.pallas_call(
        functools.partial(_hy_fft_filter_kernel, n1=n1),
        out_shape=jax.ShapeDtypeStruct((HY_ORDER, 2, n1, FFT_N2, w), F32),
        grid=(HY_ORDER, nj),
        in_specs=[half(0), half(1), full((2 * n1, n1)),
                  pl.BlockSpec((FFT_KB, 2 * FFT_N2, 2 * FFT_N2), lambda i, j: (j, 0, 0))],
        out_specs=pl.BlockSpec((1, 2, FFT_KB, FFT_N2, w), lambda i, j: (i, 0, j, 0, 0)),
        scratch_shapes=[xp_full, xp_full, scr, scr],
        compiler_params=_cp(("parallel", "arbitrary")),
        name="hyena_filter_fft",
    )(k_tm, k_tm, tabs['f1_full'], tabs['ef'])
    z = pl.pallas_call(
        functools.partial(_hy_short_conv_tm_kernel, l=l, ctx_len=ctx_len),
        out_shape=jax.ShapeDtypeStruct((b, l, c3), F32),
        grid=(b, c3 // LANES),
        in_specs=[pl.BlockSpec((1, s, LANES), lambda i, g: (i, 0, g)),
                  pl.BlockSpec((HY_SHORT, LANES), lambda i, g: (0, g)),
                  pl.BlockSpec((1, LANES), lambda i, g: (0, g))],
        out_specs=pl.BlockSpec((1, l, LANES), lambda i, g: (i, 0, g)),
        scratch_shapes=[pltpu.VMEM((l + 2 * SUBLANES, LANES), F32)],
        compiler_params=_cp(("parallel", "parallel")),
        name="hyena_short_conv_tm",
    )(p_b, lp['hy_conv_w'], lp['hy_conv_b'].reshape(1, c3))

    def conv(xa, xb, xblk, gate_blk, order):
        lane = lambda arr, blk: pl.BlockSpec((1, l, LANES), lambda i, j: (i, 0, blk))
        half_out = jax.ShapeDtypeStruct((b, l, LANES), F32)
        xb_blk = xblk + 1 if xa is xb else xblk
        return pl.pallas_call(
            functools.partial(_hy_fft_conv_kernel, n1=n1),
            out_shape=(half_out, half_out),
            grid=(b, nj),
            in_specs=[lane(xa, xblk), lane(xb, xb_blk), lane(z, gate_blk), lane(z, gate_blk + 1),
                      pl.BlockSpec((1, 1, w), lambda i, j: (order, 0, 0)),
                      full((2 * n1, n_in)), full((n_in, 2 * n1)),
                      pl.BlockSpec((FFT_KB, 2 * FFT_N2, 2 * FFT_N2), lambda i, j: (j, 0, 0)),
                      pl.BlockSpec((FFT_KB, 2 * FFT_N2, 2 * FFT_N2), lambda i, j: (j, 0, 0)),
                      pl.BlockSpec((1, 2, FFT_KB, FFT_N2, w), lambda i, j: (order, 0, j, 0, 0)),
                      pl.BlockSpec((1, 1, w), lambda i, j: (order, 0, 0))],
            out_specs=(pl.BlockSpec((1, l, LANES), lambda i, j: (i, 0, 0)),
                       pl.BlockSpec((1, l, LANES), lambda i, j: (i, 0, 0))),
            scratch_shapes=[xp_half, xp_half, scr, scr],
            compiler_params=_cp(("parallel", "arbitrary")),
            name="hyena_fft_conv",
        )(xa, xb, z, z, lp['hy_bias'].reshape(HY_ORDER, 1, w), tabs['f1'], tabs['f1i'], tabs['ef'], tabs['ei'],
          kf, asum)

    y1a, y1b = conv(z, z, 0, 2, 0)
    return conv(y1a, y1b, 0, 4, 1)


def _hy_short_conv_tm_kernel(x_ref, w_ref, b_ref, z_ref, xpad, *, l, ctx_len):
    ch = LANES
    zeros8 = jnp.zeros((SUBLANES, LANES), F32)
    xpad[0:SUBLANES, :] = zeros8
    xpad[l + SUBLANES:l + 2 * SUBLANES, :] = zeros8

    def copy_body(i, carry):
        r = pl.multiple_of(i * ch, ch)
        xpad[pl.ds(r + SUBLANES, ch), :] = x_ref[0, pl.ds(r + ctx_len, ch), :]
        return carry

    lax.fori_loop(0, l // ch, copy_body, 0)

    def conv_body(i, carry):
        r = pl.multiple_of(i * ch, ch)
        win = xpad[pl.ds(r, ch + 2 * SUBLANES), :]
        acc = jnp.zeros((ch, LANES), F32) + b_ref[...]
        for k in range(HY_SHORT):
            off = k - HY_SHORT // 2
            acc = acc + win[SUBLANES + off:SUBLANES + off + ch, :] * w_ref[k:k + 1, :]
        z_ref[0, pl.ds(r, ch), :] = acc
        return carry

    lax.fori_loop(0, l // ch, conv_body, 0)


def _top2_route(logits):
    lane = lax.broadcasted_iota(jnp.int32, logits.shape, 1)
    lg = jnp.where(lane < N_EXPERTS, logits, -jnp.inf)
    v1 = lg.max(axis=-1, keepdims=True)
    i1 = jnp.min(jnp.where(lg == v1, lane, LANES), axis=-1, keepdims=True)
    lg2 = jnp.where(lane == i1, -jnp.inf, lg)
    v2 = lg2.max(axis=-1, keepdims=True)
    i2 = jnp.min(jnp.where(lg2 == v2, lane, LANES), axis=-1, keepdims=True)
    e2 = jnp.exp(v2 - v1)
    w1 = 1.0 / (1.0 + e2)
    w2 = e2 / (1.0 + e2)
    out = jnp.where(lane == 0, i1.astype(F32), 0.0)
    out = jnp.where(lane == 1, i2.astype(F32), out)
    out = jnp.where(lane == 2, w1, out)
    return jnp.where(lane == 3, w2, out)


def _merge_kernel(ya_ref, yba_ref, ybb_ref, ybc_ref, ycl_ref, ycc_ref, yd_ref, pg_ref, x_ref, mod_ref, wa_ref, wb_ref,
                  wc_ref, wd_ref, wo_ref, lng_ref, lnb_ref, *rest, alpha, first_tile, with_router):
    if with_router:
        wr_ref, x_out_ref, u_out_ref, route_ref = rest
    else:
        x_out_ref, u_out_ref = rest
    j = pl.program_id(1) + first_tile
    d = x_ref.shape[-1]
    yb = jnp.concatenate([yba_ref[0], ybb_ref[0]], axis=1)
    yc = ycl_ref[0]
    if first_tile == 0:
        yb = jnp.where(j == 0, ybc_ref[0].astype(F32).T, yb)
        yc = jnp.where(j == 0, ycc_ref[0], yc)
    yb = yb.astype(BF16)
    projs = [jnp.dot(ya_ref[0], wa_ref[...], preferred_element_type=F32),
             jnp.dot(yb, wb_ref[...], preferred_element_type=F32),
             jnp.dot(yc, wc_ref[...], preferred_element_type=F32),
             jnp.dot(yd_ref[0], wd_ref[...], preferred_element_type=F32)]
    merged = None
    for i, pr in enumerate(projs):
        term = _sigmoid(pg_ref[0, :, i * d:(i + 1) * d].astype(F32)) * pr
        merged = term if merged is None else merged + term
    m = jnp.dot(merged.astype(BF16), wo_ref[...], preferred_element_type=F32)
    mod = mod_ref[0, 0]
    xn = _layer_norm_rows(alpha * x_ref[0] + mod[2:3, :] * m) * lng_ref[...] + lnb_ref[...]
    x_out_ref[0] = xn
    u = _layer_norm_rows(xn) * (1.0 + mod[4:5, :]) + mod[3:4, :]
    u_out_ref[0] = u.astype(u_out_ref.dtype)
    if with_router:
        u_hi = u.astype(BF16)
        u_lo = (u - u_hi.astype(F32)).astype(BF16)
        wr = wr_ref[...]
        w_hi = wr.astype(BF16)
        w_lo = (wr - w_hi.astype(F32)).astype(BF16)
        logits = (jnp.dot(u_hi, w_hi, preferred_element_type=F32) + jnp.dot(u_lo, w_hi, preferred_element_type=F32)
                  + jnp.dot(u_hi, w_lo, preferred_element_type=F32))
        route_ref[0] = _top2_route(logits)


def _merge(ya, yb_lat_a, yb_lat_b, yb_ctx, yc_lat, yc_ctx, yd, pg, xs, mod_sel, lp, ln_g, ln_b, alpha, ctx_len,
           with_ctx, u_dtype, w_router=None):
    b, s, d = xs.shape
    tm = ctx_len
    first = 0 if with_ctx else 1
    nt = s // tm - first
    tok = lambda w: pl.BlockSpec((1, tm, w), lambda i, j: (i, j + first, 0))
    lat = lambda j: jnp.maximum(j + first - 1, 0)
    out = lambda w: pl.BlockSpec((1, tm, w), lambda i, j: (i, j, 0))
    full = lambda shape: pl.BlockSpec(shape, lambda i, j: (0,) * len(shape))
    in_specs = [tok(LRU_WIDTH),
                pl.BlockSpec((1, tm, LANES), lambda i, j: (i, lat(j), 0)),
                pl.BlockSpec((1, tm, LANES), lambda i, j: (i, lat(j), 0)),
                pl.BlockSpec((1, HY_WIDTH, tm), lambda i, j: (i, 0, 0)),
                pl.BlockSpec((1, tm, NA_WIDTH), lambda i, j: (i, lat(j), 0)),
                pl.BlockSpec((1, tm, NA_WIDTH), lambda i, j: (i, 0, 0)),
                tok(S5_WIDTH), tok(N_BRANCH * d), tok(d),
                pl.BlockSpec((1, 1, 6, d), lambda i, j: (i, jnp.minimum(j + first, 1), 0, 0)),
                full((LRU_WIDTH, d)), full((HY_WIDTH, d)), full((NA_WIDTH, d)), full((S5_WIDTH, d)),
                full((d, d)), full((1, d)), full((1, d))]
    args = [ya, yb_lat_a, yb_lat_b, yb_ctx, yc_lat, yc_ctx, yd, pg, xs, mod_sel,
            lp['w_br_a'].astype(BF16), lp['w_br_b'].astype(BF16), lp['w_br_c'].astype(BF16),
            lp['w_br_d'].astype(BF16), lp['w_out'].astype(BF16), ln_g.reshape(1, d), ln_b.reshape(1, d)]
    out_shape = [jax.ShapeDtypeStruct((b, nt * tm, d), F32), jax.ShapeDtypeStruct((b, nt * tm, d), u_dtype)]
    out_specs = [out(d), out(d)]
    if w_router is not None:
        in_specs.append(full((d, LANES)))
        args.append(jnp.zeros((d, LANES), F32).at[:, :N_EXPERTS].set(w_router))
        out_shape.append(jax.ShapeDtypeStruct((b, nt * tm, LANES), F32))
        out_specs.append(out(LANES))
    return pl.pallas_call(
        functools.partial(_merge_kernel, alpha=alpha, first_tile=first, with_router=w_router is not None),
        out_shape=tuple(out_shape),
        grid=(b, nt),
        in_specs=in_specs,
        out_specs=tuple(out_specs),
        compiler_params=_cp(("parallel", "parallel")),
        name="merge",
    )(*args)


FFN_TF = 256


def _dense_ffn_kernel(u_ref, x_ref, mod_ref, modn_ref, wg_ref, wu_ref, wd_ref, lng_ref, lnb_ref,
                      xs_ref, un_ref, acc_ref, *, alpha, ctx_len, tiles_per_seq):
    i = pl.program_id(0)
    j = pl.program_id(1)

    @pl.when(j == 0)
    def _():
        acc_ref[...] = jnp.zeros_like(acc_ref)

    u = u_ref[...]
    g = jnp.dot(u, wg_ref[...], preferred_element_type=F32)
    v = jnp.dot(u, wu_ref[...], preferred_element_type=F32)
    h = (g * _sigmoid(g)) * v
    acc_ref[...] += jnp.dot(h.astype(BF16), wd_ref[...], preferred_element_type=F32)

    @pl.when(j == pl.num_programs(1) - 1)
    def _():
        row = lax.broadcasted_iota(jnp.int32, (acc_ref.shape[0], 1), 0)
        is_ctx = (row < ctx_len) & ((i % tiles_per_seq) == 0)
        pick = lambda m, k: jnp.where(is_ctx, m[0, 0, k:k + 1, :], m[0, 1, k:k + 1, :])
        x2 = (_layer_norm_rows(alpha * x_ref[...] + pick(mod_ref, 5) * acc_ref[...]) * lng_ref[...]
              + lnb_ref[...])
        xs_ref[...] = x2
        un_ref[...] = (_layer_norm_rows(x2) * (1.0 + pick(modn_ref, 1)) + pick(modn_ref, 0)).astype(un_ref.dtype)


def _dense_ffn(u2, x1, mod_sel, mod_next, w_gate, w_up, w_down, ln_g, ln_b, alpha, ctx_len):
    b, s, d = x1.shape
    ff = w_gate.shape[1]
    tiles_per_seq = 4
    tm = s // tiles_per_seq
    assert tm % 16 == 0 and tm >= ctx_len
    tf = FFN_TF
    rows = pl.BlockSpec((tm, d), lambda i, j: (i, 0))
    modspec = pl.BlockSpec((1, 2, 6, d), lambda i, j: (i // tiles_per_seq, 0, 0, 0))
    vec = pl.BlockSpec((1, d), lambda i, j: (0, 0))
    xs, un = pl.pallas_call(
        functools.partial(_dense_ffn_kernel, alpha=alpha, ctx_len=ctx_len, tiles_per_seq=tiles_per_seq),
        out_shape=(jax.ShapeDtypeStruct((b * s, d), F32), jax.ShapeDtypeStruct((b * s, d), BF16)),
        grid=(b * tiles_per_seq, ff // tf),
        in_specs=[rows, rows, modspec, modspec,
                  pl.BlockSpec((d, tf), lambda i, j: (0, j)), pl.BlockSpec((d, tf), lambda i, j: (0, j)),
                  pl.BlockSpec((tf, d), lambda i, j: (j, 0)), vec, vec],
        out_specs=(rows, rows),
        scratch_shapes=[pltpu.VMEM((tm, d), F32)],
        compiler_params=_cp(("parallel", "arbitrary")),
        name="dense_swiglu_ln",
    )(u2.reshape(b * s, d), x1.reshape(b * s, d), mod_sel, mod_next, w_gate, w_up, w_down,
      ln_g.reshape(1, d), ln_b.reshape(1, d))
    return xs.reshape(b, s, d), un.reshape(b, s, d)


MOE_TM = 512
MOE_TF = 512


def _moe_kernel(te_ref, nu_ref, dst_ref, u_hbm, wg_ref, wu_ref, wd_ref, out_hbm,
                xbuf, xbf, acc, gsem, ssem, *, n_rows):
    tm = MOE_TM
    n_tok = n_rows // TOP_K
    i = pl.program_id(0)
    j = pl.program_id(1)
    nt = pl.num_programs(0)
    nj = pl.num_programs(1)
    n_used = nu_ref[0]
    slot = i % 2

    def gather_start(tile, sl):
        base = tile * tm

        def body(r, carry):
            p = dst_ref[base + r]
            row = jnp.where(p >= n_tok, p - n_tok, jnp.maximum(p, 0))
            pltpu.make_async_copy(u_hbm.at[pl.ds(row, 1)], xbuf.at[sl, pl.ds(r, 1)], gsem.at[sl]).start()
            return carry

        lax.fori_loop(0, tm, body, 0, unroll=8)

    def gather_wait(sl):
        pltpu.make_async_copy(u_hbm.at[pl.ds(0, tm)], xbuf.at[sl], gsem.at[sl]).wait()

    def scatter_wait():
        pltpu.make_async_copy(acc.at[0], out_hbm.at[pl.ds(0, tm)], ssem.at[0]).wait()

    used = i < n_used

    @pl.when(used & (j == 0))
    def _():
        @pl.when(i == 0)
        def _():
            gather_start(0, 0)
            acc[1] = jnp.zeros((tm, acc.shape[2]), F32)
            dump = pltpu.make_async_copy(acc.at[1], out_hbm.at[pl.ds(n_rows, tm)], ssem.at[0])
            dump.start()
            dump.wait()

        gather_wait(slot)

        @pl.when(i + 1 < n_used)
        def _():
            gather_start(i + 1, 1 - slot)

        xbf[...] = xbuf[slot].astype(BF16)
        acc[slot] = jnp.zeros((tm, acc.shape[2]), F32)

    @pl.when(used)
    def _():
        x = xbf[...]
        g = jnp.dot(x, wg_ref[0], preferred_element_type=F32)
        u = jnp.dot(x, wu_ref[0], preferred_element_type=F32)
        h = (g * _sigmoid(g)) * u
        acc[slot] += jnp.dot(h.astype(BF16), wd_ref[0], preferred_element_type=F32)

    @pl.when(used & (j == nj - 1))
    def _():
        @pl.when(i > 0)
        def _():
            scatter_wait()

        base = i * tm

        def body(r, carry):
            p = dst_ref[base + r]
            row = jnp.where(p >= 0, p, n_rows + r)
            pltpu.make_async_copy(acc.at[slot, pl.ds(r, 1)], out_hbm.at[pl.ds(row, 1)], ssem.at[0]).start()
            return carry

        lax.fori_loop(0, tm, body, 0, unroll=8)

    @pl.when((i == nt - 1) & (j == nj - 1))
    def _():
        scatter_wait()


def _moe_experts(u_rows, dst, tile_expert, n_used, w_gate, w_up, w_down):
    r, d = u_rows.shape
    ff = w_gate.shape[2]
    tm, tf = MOE_TM, MOE_TF
    nt = dst.shape[0] // tm
    nj = ff // tf
    n_rows = TOP_K * r

    def jmap(i, j, nu):
        return jnp.where(i < nu[0], j, nj - 1)

    grid_spec = pltpu.PrefetchScalarGridSpec(
        num_scalar_prefetch=3,
        grid=(nt, nj),
        in_specs=[pl.BlockSpec(memory_space=pl.ANY),
                  pl.BlockSpec((1, d, tf), lambda i, j, te, nu, ds: (te[i], 0, jmap(i, j, nu))),
                  pl.BlockSpec((1, d, tf), lambda i, j, te, nu, ds: (te[i], 0, jmap(i, j, nu))),
                  pl.BlockSpec((1, tf, d), lambda i, j, te, nu, ds: (te[i], jmap(i, j, nu), 0))],
        out_specs=pl.BlockSpec(memory_space=pl.ANY),
        scratch_shapes=[pltpu.VMEM((2, tm, d), F32), pltpu.VMEM((tm, d), BF16), pltpu.VMEM((2, tm, d), F32),
                        pltpu.SemaphoreType.DMA((2,)), pltpu.SemaphoreType.DMA((1,))],
    )
    return pl.pallas_call(
        functools.partial(_moe_kernel, n_rows=n_rows),
        out_shape=jax.ShapeDtypeStruct((n_rows + tm, d), F32),
        grid_spec=grid_spec,
        compiler_params=_cp(("arbitrary", "arbitrary")),
        name="moe_experts",
    )(tile_expert, n_used, dst, u_rows, w_gate, w_up, w_down)


def _moe_combine_kernel(x_ref, y1_ref, y2_ref, w_ref, mod_ref, lng_ref, lnb_ref, o_ref, *, alpha):
    mod = mod_ref[0, 0]
    w = w_ref[0]
    f = w[:, 2:3] * y1_ref[...] + w[:, 3:4] * y2_ref[...]
    o_ref[0] = _layer_norm_rows(alpha * x_ref[0] + mod[5:6, :] * f) * lng_ref[...] + lnb_ref[...]


def _moe_ffn(x_lat, u_lat, route, mod_sel, lp_moe, ln_g, ln_b, alpha):
    b, l, d = x_lat.shape
    t = b * l
    tm = MOE_TM
    ids = route[..., 0:TOP_K].astype(jnp.int32).reshape(t * TOP_K)
    onehot = (ids[:, None] == jnp.arange(N_EXPERTS)[None, :]).astype(jnp.int32)
    csum = jnp.cumsum(onehot, axis=0)
    rank = jnp.take_along_axis(csum, ids[:, None], axis=1)[:, 0] - 1
    counts = csum[-1]
    padded = ((counts + tm - 1) // tm) * tm
    ends = jnp.cumsum(padded)
    starts = ends - padded
    slot = starts[ids] + rank
    n_slots = t * TOP_K + N_EXPERTS * tm
    nt = n_slots // tm
    pair = jnp.arange(t * TOP_K, dtype=jnp.int32)
    dst = jnp.full((n_slots,), -1, jnp.int32).at[slot].set((pair % TOP_K) * t + pair // TOP_K)
    tile_start = jnp.arange(nt, dtype=jnp.int32) * tm
    tile_expert = jnp.minimum(jnp.sum(tile_start[:, None] >= ends[None, :], axis=1), N_EXPERTS - 1).astype(jnp.int32)
    n_used = (ends[-1] // tm).astype(jnp.int32).reshape(1)
    last_e = tile_expert[jnp.maximum(n_used[0] - 1, 0)]
    tile_expert = jnp.where(jnp.arange(nt) < n_used[0], tile_expert, last_e)
    y = _moe_experts(u_lat.reshape(t, d), dst, tile_expert, n_used, lp_moe['w_gate'], lp_moe['w_up'],
                     lp_moe['w_down'])
    tmc = 256
    nl = l // tmc
    vec = pl.BlockSpec((1, d), lambda i, j: (0, 0))
    return pl.pallas_call(
        functools.partial(_moe_combine_kernel, alpha=alpha),
        out_shape=jax.ShapeDtypeStruct((b, l, d), F32),
        grid=(b, nl),
        in_specs=[pl.BlockSpec((1, tmc, d), lambda i, j: (i, j, 0)),
                  pl.BlockSpec((tmc, d), lambda i, j: (i * nl + j, 0)),
                  pl.BlockSpec((tmc, d), lambda i, j: (b * nl + i * nl + j, 0)),
                  pl.BlockSpec((1, tmc, LANES), lambda i, j: (i, j, 0)),
                  pl.BlockSpec((1, 1, 6, d), lambda i, j: (i, 1, 0, 0)), vec, vec],
        out_specs=pl.BlockSpec((1, tmc, d), lambda i, j: (i, j, 0)),
        compiler_params=_cp(("parallel", "parallel")),
        name="moe_combine_ln",
    )(x_lat, y, y, route, mod_sel, ln_g.reshape(1, d), ln_b.reshape(1, d))


def kernel(x, c, ctx, c_ctx, w_mod, b_mod, w_in, lru_conv_w, lru_conv_b, lru_w_r, lru_b_r, lru_w_i, lru_b_i, lru_lambda, hy_conv_w, hy_conv_b, hy_w1, hy_b1, hy_w2, hy_b2, hy_w3, hy_b3, hy_freq, hy_bias, na_rpb, s5_a_re, s5_a_im, s5_log_dt, s5_b_re, s5_b_im, s5_c_re, s5_c_im, s5_d, s5_w_glu, s5_b_glu, w_br_a, w_br_b, w_br_c, w_br_d, w_out, ln1_g, ln1_b, ln2_g, ln2_b, ff_w_gate, ff_w_up, ff_w_down, moe_router, moe_w_gate, moe_w_up, moe_w_down):
    bsz, l, d = x.shape
    ctx_len = ctx.shape[1]
    depth = w_in.shape[0]
    s = ctx_len + l
    alpha = (2.0 * depth) ** 0.25
    xs = jnp.concatenate([ctx, x], axis=1)
    c_rows = jnp.zeros((SUBLANES, d), F32).at[0:bsz].set(c).at[bsz].set(c_ctx)
    fft_lat = _fft_tables(l)
    dft_ctx = _dft_matrices(ctx_len) if depth > 1 else None
    assert depth == 2
    mod_all = _mod_vectors(c_rows, w_mod, b_mod).reshape(depth, SUBLANES, 6, d)
    mods = [jnp.stack([jnp.broadcast_to(mod_all[li, bsz], (bsz, 6, d)), mod_all[li, 0:bsz]], axis=1)
            for li in range(depth)]
    mods.append(mods[-1])
    u1 = _ln_mod(xs, mods[0], 0, 1, ctx_len)

    for li in range(depth):
        with_ctx = li < depth - 1
        lp = {
            'lru_conv_w': lru_conv_w[li], 'lru_conv_b': lru_conv_b[li], 'lru_w_r': lru_w_r[li],
            'lru_b_r': lru_b_r[li], 'lru_w_i': lru_w_i[li], 'lru_b_i': lru_b_i[li], 'lru_lambda': lru_lambda[li],
            'hy_conv_w': hy_conv_w[li], 'hy_conv_b': hy_conv_b[li], 'hy_w1': hy_w1[li], 'hy_b1': hy_b1[li],
            'hy_w2': hy_w2[li], 'hy_b2': hy_b2[li], 'hy_w3': hy_w3[li], 'hy_b3': hy_b3[li],
            'hy_freq': hy_freq[li], 'hy_bias': hy_bias[li],
            's5_a_re': s5_a_re[li], 's5_a_im': s5_a_im[li], 's5_log_dt': s5_log_dt[li], 's5_b_re': s5_b_re[li],
            's5_b_im': s5_b_im[li], 's5_c_re': s5_c_re[li], 's5_c_im': s5_c_im[li], 's5_d': s5_d[li],
            's5_w_glu': s5_w_glu[li], 's5_b_glu': s5_b_glu[li], 'w_br_a': w_br_a[li], 'w_br_b': w_br_b[li],
            'w_br_c': w_br_c[li], 'w_br_d': w_br_d[li], 'w_out': w_out[li],
        }
        mod_sel = mods[li]

        u1f = u1.reshape(bsz * s, d)
        wi = w_in[li].astype(BF16)
        p_a = _matmul(u1f, wi[:, OFF_A:OFF_B], F32, name="proj_lru").reshape(bsz, s, OFF_B - OFF_A)
        p_c = _matmul(u1f, wi[:, OFF_C:OFF_D], BF16, name="proj_natten").reshape(bsz, s, OFF_D - OFF_C)
        p_d = _matmul(u1f, wi[:, OFF_D:OFF_G], F32, name="proj_s5").reshape(bsz, s, OFF_G - OFF_D)
        p_g = _matmul(u1f, wi[:, OFF_G:], BF16, name="proj_gates").reshape(bsz, s, N_BRANCH * d)
        p_b = _matmul(u1f, wi[:, OFF_B:OFF_C], F32, name="proj_hyena").reshape(bsz, s, OFF_C - OFF_B)

        ya = _lru_mixer(p_a, lp, ctx_len)
        yb_a, yb_b = _hyena_latent(p_b, lp, fft_lat, ctx_len)
        if with_ctx:
            yb_ctx = _hyena_sequence(_matmul_nt(wi[:, OFF_B:OFF_C].T, u1, F32, 0, ctx_len), lp, dft_ctx)
        else:
            yb_ctx = jnp.zeros((bsz, HY_WIDTH, ctx_len), BF16)
        yc_l, yc_c = _natten_mixer(p_c, na_rpb[li], ctx_len, with_ctx)
        if yc_c is None:
            yc_c = jnp.zeros((bsz, ctx_len, NA_WIDTH), BF16)
        yd = _s5_mixer(p_d, lp, ctx_len)
        e = li // 2
        if li % 2 == 0:
            assert with_ctx
            x1, u2 = _merge(ya, yb_a, yb_b, yb_ctx, yc_l, yc_c, yd, p_g, xs, mod_sel, lp, ln1_g[li], ln1_b[li],
                            alpha, ctx_len, True, BF16)
            xs, u1 = _dense_ffn(u2, x1, mod_sel, mods[li + 1], ff_w_gate[e].astype(BF16), ff_w_up[e].astype(BF16),
                                ff_w_down[e].astype(BF16), ln2_g[li], ln2_b[li], alpha, ctx_len)
        else:
            assert not with_ctx
            x1, u2, route = _merge(ya, yb_a, yb_b, yb_ctx, yc_l, yc_c, yd, p_g, xs, mod_sel, lp, ln1_g[li],
                                   ln1_b[li], alpha, ctx_len, False, F32, moe_router[e])
            lp_moe = {'w_gate': moe_w_gate[e].astype(BF16), 'w_up': moe_w_up[e].astype(BF16),
                      'w_down': moe_w_down[e].astype(BF16)}
            return _moe_ffn(x1, u2, route, mod_sel, lp_moe, ln2_g[li], ln2_b[li], alpha)
```

```python
import functools
import math

import numpy as np
import jax
import jax.numpy as jnp
from jax import lax
from jax.experimental import pallas as pl
from jax.experimental.pallas import tpu as pltpu

F32 = jnp.float32
BF16 = jnp.bfloat16
HIGHEST = lax.Precision.HIGHEST

LRU_WIDTH = 384
LRU_BLOCK = 64
LRU_CONV = 4
LRU_C = 8.0
HY_WIDTH = 256
HY_ORDER = 2
HY_SHORT = 3
HY_BANDS = 16
HY_FILT_HID = 64
HY_MAX_DECAY = math.log(1e-2) / 0.3
HY_MIN_DECAY = math.log(1e-2) / 1.5
NA_HEADS = 6
NA_HEAD_DIM = 64
NA_WIDTH = NA_HEADS * NA_HEAD_DIM
NA_WIN_R = 8
NA_WIN_C = 16
GRID_W = 64
S5_WIDTH = 256
S5_GROUP = 16
S5_GROUPS = 16
S5_STATE = 64
N_BRANCH = 4
OFF_A = 0
OFF_B = OFF_A + 2 * LRU_WIDTH
OFF_C = OFF_B + 3 * HY_WIDTH
OFF_D = OFF_C + 3 * NA_WIDTH
OFF_G = OFF_D + S5_WIDTH
N_EXPERTS = 8
TOP_K = 2
LN_EPS = 1e-5
MASK_VALUE = -1e30

LANES = 128
SUBLANES = 8
VMEM_LIMIT = 56 * 1024 * 1024


def _cp(sem, vmem=VMEM_LIMIT):
    return pltpu.CompilerParams(dimension_semantics=sem, vmem_limit_bytes=vmem)


def _gelu(x):
    return 0.5 * x * (1.0 + jnp.tanh(math.sqrt(2.0 / math.pi) * (x + 0.044715 * (x * x * x))))


def _sigmoid(x):
    return 0.5 + 0.5 * jnp.tanh(0.5 * x)


def _layer_norm_rows(x):
    mu = jnp.mean(x, axis=-1, keepdims=True)
    xc = x - mu
    var = jnp.mean(xc * xc, axis=-1, keepdims=True)
    return xc * lax.rsqrt(var + LN_EPS)


def _mod_kernel(c_ref, w_ref, b_ref, o_ref):
    c = c_ref[...]
    a = c * _sigmoid(c)
    o_ref[0] = jnp.dot(a, w_ref[0], preferred_element_type=F32, precision=HIGHEST) + b_ref[0]


def _mod_vectors(c_rows, w_mod, b_mod):
    d = c_rows.shape[1]
    depth, _, n = w_mod.shape
    tn = 1536
    return pl.pallas_call(
        _mod_kernel,
        out_shape=jax.ShapeDtypeStruct((depth, SUBLANES, n), F32),
        grid=(depth, n // tn),
        in_specs=[pl.BlockSpec((SUBLANES, d), lambda l, j: (0, 0)),
                  pl.BlockSpec((1, d, tn), lambda l, j: (l, 0, j)),
                  pl.BlockSpec((1, 1, tn), lambda l, j: (l, 0, j))],
        out_specs=pl.BlockSpec((1, SUBLANES, tn), lambda l, j: (l, 0, j)),
        compiler_params=_cp(("arbitrary", "arbitrary")),
        name="mod_vectors",
    )(c_rows, w_mod, b_mod.reshape(depth, 1, n))


def _ln_mod_kernel(xc_ref, x_ref, mod_ref, o_ref, *, shift_idx, scale_idx):
    x = jnp.where(pl.program_id(1) == 0, xc_ref[0], x_ref[0])
    y = _layer_norm_rows(x)
    m = mod_ref[0, 0]
    o = y * (1.0 + m[scale_idx:scale_idx + 1, :]) + m[shift_idx:shift_idx + 1, :]
    o_ref[0] = o.astype(o_ref.dtype)


def _ln_mod(ctx, x, mod_sel, shift_idx, scale_idx):
    b, ctx_len, d = ctx.shape
    s = ctx_len + x.shape[1]
    tm = ctx_len
    return pl.pallas_call(
        functools.partial(_ln_mod_kernel, shift_idx=shift_idx, scale_idx=scale_idx),
        out_shape=jax.ShapeDtypeStruct((b, s, d), BF16),
        grid=(b, s // tm),
        in_specs=[pl.BlockSpec((1, tm, d), lambda i, j: (i, 0, 0)),
                  pl.BlockSpec((1, tm, d), lambda i, j: (i, jnp.maximum(j - 1, 0), 0)),
                  pl.BlockSpec((1, 1, 6, d), lambda i, j: (i, jnp.minimum(j, 1), 0, 0))],
        out_specs=pl.BlockSpec((1, tm, d), lambda i, j: (i, j, 0)),
        compiler_params=_cp(("parallel", "parallel")),
        name="ln_mod",
    )(ctx, x, mod_sel)


def _mm_kernel(a_ref, w_ref, o_ref):
    o_ref[...] = jnp.dot(a_ref[...], w_ref[...], preferred_element_type=F32).astype(o_ref.dtype)


def _pick_tile(n, prefs):
    for t in prefs:
        if n % t == 0:
            return t
    return n


def _matmul(a, w, out_dtype, tm=None, tn=None, name="matmul"):
    m, k = a.shape
    n = w.shape[1]
    tm = tm or _pick_tile(m, (1024, 512, 256, 128))
    tn = tn or (n if n <= 1536 else _pick_tile(n, (1024, 768, 512, 384, 256, 128)))
    return pl.pallas_call(
        _mm_kernel,
        out_shape=jax.ShapeDtypeStruct((m, n), out_dtype),
        grid=(m // tm, n // tn),
        in_specs=[pl.BlockSpec((tm, k), lambda i, j: (i, 0)),
                  pl.BlockSpec((k, tn), lambda i, j: (0, j))],
        out_specs=pl.BlockSpec((tm, tn), lambda i, j: (i, j)),
        compiler_params=_cp(("parallel", "parallel")),
        name=name,
    )(a, w)


def _mm_nt_kernel(w_ref, u_ref, o_ref):
    o_ref[0] = lax.dot_general(w_ref[...], u_ref[0], (((1,), (1,)), ((), ())),
                               preferred_element_type=F32).astype(o_ref.dtype)


def _matmul_nt(w_t, u, out_dtype, tok0, ntok, tn=256):
    c, k = w_t.shape
    b = u.shape[0]
    j0 = tok0 // tn
    return pl.pallas_call(
        _mm_nt_kernel,
        out_shape=jax.ShapeDtypeStruct((b, c, ntok), out_dtype),
        grid=(b, ntok // tn),
        in_specs=[pl.BlockSpec((c, k), lambda i, j: (0, 0)),
                  pl.BlockSpec((1, tn, k), lambda i, j: (i, j + j0, 0))],
        out_specs=pl.BlockSpec((1, c, tn), lambda i, j: (i, 0, j)),
        compiler_params=_cp(("parallel", "parallel")),
        name="matmul_nt",
    )(w_t, u)


LRU_CHUNK = 128


def _tile_scan(a, b, row, reverse):
    for s in (1, 2, 4):
        if reverse:
            keep = row < SUBLANES - s
            shift = SUBLANES - s
        else:
            keep = row >= s
            shift = s
        a_sh = pltpu.roll(a, shift, 0)
        b_sh = pltpu.roll(b, shift, 0)
        b = jnp.where(keep, a * b_sh, 0.0) + b
        a = jnp.where(keep, a * a_sh, a)
    return a, b


def _lru_kernel(pg_ref, px_ref, cw_ref, cb_ref, wg_ref, bg_ref, lam_ref, y_ref,
                xpad, a_f, b_f, a_b, b_b, *, s_len, ctx_len):
    ch = LRU_CHUNK
    n_chunks = s_len // ch
    zeros8 = jnp.zeros((SUBLANES, LANES), F32)
    xpad[0:SUBLANES, :] = zeros8
    xpad[ctx_len + SUBLANES:ctx_len + 2 * SUBLANES, :] = zeros8
    xpad[s_len + 2 * SUBLANES:s_len + 3 * SUBLANES, :] = zeros8

    def pad_row(r):
        return pl.multiple_of(r + jnp.where(r >= ctx_len, 2 * SUBLANES, SUBLANES), SUBLANES)

    def copy_body(i, carry):
        r = pl.multiple_of(i * ch, ch)
        xpad[pl.ds(pad_row(r), ch), :] = px_ref[0, pl.ds(r, ch), :]
        return carry

    lax.fori_loop(0, n_chunks, copy_body, 0)

    lam = lam_ref[...]
    sp = jnp.log(1.0 + jnp.exp(-lam))

    def gates_body(i, carry):
        r = pl.multiple_of(i * ch, ch)
        win = xpad[pl.ds(pad_row(r) - SUBLANES, ch + 2 * SUBLANES), :]
        xc = jnp.zeros((ch, LANES), F32) + cb_ref[...]
        for k in range(LRU_CONV):
            off = k - LRU_CONV // 2
            xc = xc + win[SUBLANES + off:SUBLANES + off + ch, :] * cw_ref[k:k + 1, :]
        gl = jnp.dot(xc.astype(BF16), wg_ref[0], preferred_element_type=F32) + bg_ref[0]
        for d, (a_s, b_s) in enumerate(((a_f, b_f), (a_b, b_b))):
            g_r = _sigmoid(gl[:, d * 2 * LANES:d * 2 * LANES + LANES])
            g_i = _sigmoid(gl[:, d * 2 * LANES + LANES:(d + 1) * 2 * LANES])
            log_a = (-LRU_C) * g_r * sp[d:d + 1, :]
            a = jnp.exp(log_a)
            bb = jnp.sqrt(1.0 - a * a) * g_i * xc
            a_s[pl.ds(r, ch), :] = a
            b_s[pl.ds(r, ch), :] = bb
        return carry

    lax.fori_loop(0, n_chunks, gates_body, 0)

    row = lax.broadcasted_iota(jnp.int32, (SUBLANES, LANES), 0)

    n_ctx_tiles = ctx_len // SUBLANES
    n_tiles = s_len // SUBLANES

    def scan_body(i, carry):
        h_f, h_b = carry
        r = pl.multiple_of(i * SUBLANES, SUBLANES)
        a, b = _tile_scan(a_f[pl.ds(r, SUBLANES), :], b_f[pl.ds(r, SUBLANES), :], row, False)
        hf = b + a * h_f
        b_f[pl.ds(r, SUBLANES), :] = hf
        t = jnp.where(i < n_ctx_tiles, n_ctx_tiles - 1 - i, n_tiles + n_ctx_tiles - 1 - i)
        rb = pl.multiple_of(t * SUBLANES, SUBLANES)
        a, b = _tile_scan(a_b[pl.ds(rb, SUBLANES), :], b_b[pl.ds(rb, SUBLANES), :], row, True)
        hb = b + a * h_b
        b_b[pl.ds(rb, SUBLANES), :] = hb
        return (jnp.broadcast_to(hf[SUBLANES - 1:SUBLANES, :], (SUBLANES, LANES)),
                jnp.broadcast_to(hb[0:1, :], (SUBLANES, LANES)))

    lax.fori_loop(0, n_tiles, scan_body, (zeros8, zeros8), unroll=2)

    def out_body(i, carry):
        r = pl.multiple_of(i * ch, ch)
        g = _gelu(pg_ref[0, pl.ds(r, ch), :])
        y = g * (b_f[pl.ds(r, ch), :] + b_b[pl.ds(r, ch), :])
        y_ref[0, pl.ds(r, ch), :] = y.astype(y_ref.dtype)
        return carry

    lax.fori_loop(0, n_chunks, out_body, 0)


def _lru_gate_weights(w_r, w_i, b_r, b_i):
    n_grp = LRU_WIDTH // LANES
    per = LANES // LRU_BLOCK

    def bd(w):
        w = w.reshape(n_grp, per, LRU_BLOCK, LRU_BLOCK)
        z = jnp.zeros((n_grp, LRU_BLOCK, LRU_BLOCK), w.dtype)
        top = jnp.concatenate([w[:, 0], z], axis=2)
        bot = jnp.concatenate([z, w[:, 1]], axis=2)
        return jnp.concatenate([top, bot], axis=1)

    wg = jnp.concatenate([bd(w_r[0]), bd(w_i[0]), bd(w_r[1]), bd(w_i[1])], axis=2).astype(BF16)
    bg = jnp.stack([b_r[0], b_i[0], b_r[1], b_i[1]], axis=0).reshape(4, n_grp, LANES)
    bg = jnp.transpose(bg, (1, 0, 2)).reshape(n_grp, 1, 4 * LANES)
    return wg, bg


def _lru_mixer(p_a, lp, ctx_len):
    b, s, _ = p_a.shape
    n_grp = LRU_WIDTH // LANES
    wg, bg = _lru_gate_weights(lp['lru_w_r'], lp['lru_w_i'], lp['lru_b_r'], lp['lru_b_i'])
    scr = pltpu.VMEM((s, LANES), F32)
    return pl.pallas_call(
        functools.partial(_lru_kernel, s_len=s, ctx_len=ctx_len),
        out_shape=jax.ShapeDtypeStruct((b, s, LRU_WIDTH), BF16),
        grid=(b, n_grp),
        in_specs=[pl.BlockSpec((1, s, LANES), lambda i, g: (i, 0, g)),
                  pl.BlockSpec((1, s, LANES), lambda i, g: (i, 0, n_grp + g)),
                  pl.BlockSpec((LRU_CONV, LANES), lambda i, g: (0, g)),
                  pl.BlockSpec((1, LANES), lambda i, g: (0, g)),
                  pl.BlockSpec((1, LANES, 4 * LANES), lambda i, g: (g, 0, 0)),
                  pl.BlockSpec((1, 1, 4 * LANES), lambda i, g: (g, 0, 0)),
                  pl.BlockSpec((2, LANES), lambda i, g: (0, g))],
        out_specs=pl.BlockSpec((1, s, LANES), lambda i, g: (i, 0, g)),
        scratch_shapes=[pltpu.VMEM((s + 3 * SUBLANES, LANES), F32), scr, scr, scr, scr],
        compiler_params=_cp(("parallel", "parallel")),
        name="rglru",
    )(p_a, p_a, lp['lru_conv_w'], lp['lru_conv_b'].reshape(1, LRU_WIDTH), wg, bg, lp['lru_lambda'])


S5_R = 4
S5_NSTATE = S5_GROUPS * S5_STATE


def _s5_kernel(xa_ref, xb_ref, winj_ref, wloc_ref, wro_ref, ap_ref, ya_ref, yb_ref, g_ref, *, reverse, n_ctx_tiles):
    n = S5_NSTATE
    nr = g_ref.shape[0]
    x = jnp.concatenate([h[0, pl.ds(i, nr, stride=S5_R), :] for i in range(S5_R) for h in (xa_ref, xb_ref)],
                        axis=1).astype(BF16)
    g_ref[...] = jnp.dot(x, winj_ref[...], preferred_element_type=F32)
    n_tiles = g_ref.shape[0] // SUBLANES
    row = lax.broadcasted_iota(jnp.int32, (SUBLANES, n), 0)
    zeros = jnp.zeros((SUBLANES, n), F32)
    if reverse:
        shift1, e_in, e_out = SUBLANES - 1, SUBLANES - 1, 0
    else:
        shift1, e_in, e_out = 1, 0, SUBLANES - 1

    def make_body(first_tile):
        def body(i, carry):
            hr, hi = carry
            t = (first_tile - i) if reverse else (first_tile + i)
            r = pl.multiple_of(t * SUBLANES, SUBLANES)
            br = g_ref[pl.ds(r, SUBLANES), 0:n]
            bi = g_ref[pl.ds(r, SUBLANES), n:2 * n]
            for k, s in enumerate((1, 2, 4)):
                ar = ap_ref[SUBLANES + k:SUBLANES + k + 1, 0:n]
                ai = ap_ref[SUBLANES + k:SUBLANES + k + 1, n:2 * n]
                if reverse:
                    keep = row < SUBLANES - s
                    shift = SUBLANES - s
                else:
                    keep = row >= s
                    shift = s
                brs = pltpu.roll(br, shift, 0)
                bis = pltpu.roll(bi, shift, 0)
                nr = ar * brs - ai * bis
                ni = ar * bis + ai * brs
                br = br + jnp.where(keep, nr, 0.0)
                bi = bi + jnp.where(keep, ni, 0.0)
            cr = ap_ref[0:SUBLANES, 0:n]
            ci = ap_ref[0:SUBLANES, n:2 * n]
            out_r = br + (cr * hr - ci * hi)
            out_i = bi + (cr * hi + ci * hr)
            g_ref[pl.ds(r, SUBLANES), 0:n] = jnp.where(row == e_in, hr, pltpu.roll(out_r, shift1, 0))
            g_ref[pl.ds(r, SUBLANES), n:2 * n] = jnp.where(row == e_in, hi, pltpu.roll(out_i, shift1, 0))
            return (jnp.broadcast_to(out_r[e_out:e_out + 1, :], (SUBLANES, n)),
                    jnp.broadcast_to(out_i[e_out:e_out + 1, :], (SUBLANES, n)))
        return body

    if reverse:
        carry = lax.fori_loop(0, n_ctx_tiles, make_body(n_ctx_tiles - 1), (zeros, zeros))
        lax.fori_loop(0, n_tiles - n_ctx_tiles, make_body(n_tiles - 1), carry)
    else:
        lax.fori_loop(0, n_tiles, make_body(0), (zeros, zeros))
    y = (jnp.dot(x, wloc_ref[...], preferred_element_type=F32)
         + jnp.dot(g_ref[...].astype(BF16), wro_ref[...], preferred_element_type=F32))
    for i in range(S5_R):
        for k, h in enumerate((ya_ref, yb_ref)):
            h[0, pl.ds(i, nr, stride=S5_R), :] = y[:, (2 * i + k) * LANES:(2 * i + k + 1) * LANES]


def _s5_params(a_re, a_im, log_dt, b_re, b_im, c_re, c_im, reverse):
    rr = S5_R
    dt = jnp.exp(log_dt)[:, None]
    den = a_re * a_re + a_im * a_im
    mag = jnp.exp(dt * a_re)
    ab_re = mag * jnp.cos(dt * a_im)
    ab_im = mag * jnp.sin(dt * a_im)
    f_re = ((ab_re - 1.0) * a_re + ab_im * a_im) / den
    f_im = (ab_im * a_re - (ab_re - 1.0) * a_im) / den
    bb_re = f_re[..., None] * b_re - f_im[..., None] * b_im
    bb_im = f_re[..., None] * b_im + f_im[..., None] * b_re
    grp_tok = (np.arange(rr * S5_WIDTH) // S5_GROUP) % S5_GROUPS
    grp_state = np.arange(S5_NSTATE) // S5_STATE

    def block_diag(t, lead, grp_rows, grp_cols):
        n_lead, minor, ncols = t.shape
        full = jnp.broadcast_to(t[:, None], (n_lead, S5_GROUPS, minor, ncols)).reshape(-1, ncols)
        return jnp.where(jnp.asarray(grp_rows[:, None] == grp_cols[None, :]), full, 0.0)

    def apow(k):
        k = k.astype(F32)[:, None, None]
        m = jnp.exp(k * dt[None] * a_re[None])
        return m * jnp.cos(k * dt[None] * a_im[None]), m * jnp.sin(k * dt[None] * a_im[None])

    steps = jnp.arange(rr)
    rows = rr * S5_WIDTH
    er, ei = apow(steps if reverse else (rr - 1 - steps))
    inj_re = er[..., None] * bb_re[None] - ei[..., None] * bb_im[None]
    inj_im = er[..., None] * bb_im[None] + ei[..., None] * bb_re[None]
    def inj_map(t):
        t = jnp.transpose(t, (0, 3, 1, 2)).reshape(rr, S5_GROUP, S5_NSTATE)
        return block_diag(t, rr, grp_tok, grp_state)

    winj = jnp.concatenate([inj_map(inj_re), inj_map(inj_im)], axis=1)
    fr, fi = apow((rr - steps) if reverse else (steps + 1))
    ro_re = c_re[None] * fr[:, :, None, :] - c_im[None] * fi[:, :, None, :]
    ro_im = c_re[None] * fi[:, :, None, :] + c_im[None] * fr[:, :, None, :]
    def ro_map(t):
        t = jnp.transpose(t, (3, 0, 1, 2)).reshape(1, S5_STATE, rows)
        return block_diag(t, 1, grp_state, grp_tok)

    wro = jnp.concatenate([ro_map(ro_re), -ro_map(ro_im)], axis=0)
    kr, ki = apow(steps)
    ab_r = kr[..., None] * bb_re[None] - ki[..., None] * bb_im[None]
    ab_i = kr[..., None] * bb_im[None] + ki[..., None] * bb_re[None]
    kk = jnp.einsum('gop,kgpc->kgoc', c_re, ab_r) - jnp.einsum('gop,kgpc->kgoc', c_im, ab_i)
    src = jnp.arange(rr)[:, None]
    tgt = jnp.arange(rr)[None, :]
    lag = (src - tgt) if reverse else (tgt - src)
    kmat = jnp.where((lag >= 0)[:, :, None, None, None], kk[jnp.clip(lag, 0, rr - 1)], 0.0)
    wloc = block_diag(jnp.transpose(kmat, (0, 4, 1, 2, 3)).reshape(rr, S5_GROUP, rows), rr, grp_tok, grp_tok)
    i8 = jnp.arange(SUBLANES)
    dist = (SUBLANES - i8) if reverse else (i8 + 1)
    ks = jnp.concatenate([dist, jnp.array([1, 2, 4]), jnp.zeros((5,), dist.dtype)]) * rr
    pr, pi = apow(ks)
    ap = jnp.concatenate([pr.reshape(16, S5_NSTATE), pi.reshape(16, S5_NSTATE)], axis=1)
    return winj.astype(BF16), wloc.astype(BF16), wro.astype(BF16), ap


def _s5_scan(p_d, lp, d, ctx_len):
    b, s, w = p_d.shape
    nr = s // S5_R
    wr = S5_R * w
    reverse = d == 1
    n_ctx_tiles = ctx_len // (S5_R * SUBLANES)
    winj, wloc, wro, ap = _s5_params(lp['s5_a_re'][d], lp['s5_a_im'][d], lp['s5_log_dt'][d], lp['s5_b_re'][d],
                                     lp['s5_b_im'][d], lp['s5_c_re'][d], lp['s5_c_im'][d], reverse)
    assert w == 2 * LANES
    full = lambda shape: pl.BlockSpec(shape, lambda i: (0, 0))
    half = jax.ShapeDtypeStruct((b, s, LANES), F32)
    return pl.pallas_call(
        functools.partial(_s5_kernel, reverse=reverse, n_ctx_tiles=n_ctx_tiles),
        out_shape=(half, half),
        grid=(b,),
        in_specs=[pl.BlockSpec((1, s, LANES), lambda i: (i, 0, 0)), pl.BlockSpec((1, s, LANES), lambda i: (i, 0, 1)),
                  full((wr, 2 * S5_NSTATE)), full((wr, wr)), full((2 * S5_NSTATE, wr)), full((16, 2 * S5_NSTATE))],
        out_specs=(pl.BlockSpec((1, s, LANES), lambda i: (i, 0, 0)), pl.BlockSpec((1, s, LANES), lambda i: (i, 0, 0))),
        scratch_shapes=[pltpu.VMEM((nr, 2 * S5_NSTATE), F32)],
        compiler_params=_cp(("parallel",)),
        name="s5_scan_bwd" if reverse else "s5_scan_fwd",
    )(p_d, p_d, winj, wloc, wro, ap)


def _s5_out_kernel(yfa_ref, yfb_ref, yba_ref, ybb_ref, u_ref, d_ref, w_ref, b_ref, o_ref):
    y = (jnp.concatenate([yfa_ref[...] + yba_ref[...], yfb_ref[...] + ybb_ref[...]], axis=1)
         + d_ref[...] * u_ref[...])
    g = _gelu(y)
    z = jnp.dot(g.astype(BF16), w_ref[...], preferred_element_type=F32) + b_ref[...]
    o_ref[...] = (g * _sigmoid(z)).astype(o_ref.dtype)


def _s5_mixer(p_d, lp, ctx_len):
    b, s, w = p_d.shape
    yfa, yfb = _s5_scan(p_d, lp, 0, ctx_len)
    yba, ybb = _s5_scan(p_d, lp, 1, ctx_len)
    m = b * s
    tm = _pick_tile(m, (1024, 512, 256))
    row = pl.BlockSpec((tm, w), lambda i: (i, 0))
    hrow = pl.BlockSpec((tm, LANES), lambda i: (i, 0))
    vec = pl.BlockSpec((1, w), lambda i: (0, 0))
    flat = lambda a: a.reshape(m, LANES)
    out = pl.pallas_call(
        _s5_out_kernel,
        out_shape=jax.ShapeDtypeStruct((m, w), BF16),
        grid=(m // tm,),
        in_specs=[hrow, hrow, hrow, hrow, row, vec, pl.BlockSpec((w, w), lambda i: (0, 0)), vec],
        out_specs=row,
        compiler_params=_cp(("parallel",)),
        name="s5_out",
    )(flat(yfa), flat(yfb), flat(yba), flat(ybb), p_d.reshape(m, w), lp['s5_d'].reshape(1, w),
      lp['s5_w_glu'].astype(BF16), lp['s5_b_glu'].reshape(1, w))
    return out.reshape(b, s, w)


NA_QROWS = 4


def _natten_plan(rows):
    kr = min(NA_WIN_R, rows)
    span = kr + NA_QROWS - 1
    variants, index, blk_var = [], {}, []
    for blk in range(rows // NA_QROWS):
        r0 = blk * NA_QROWS
        ws = int(np.clip(r0 - kr // 2, 0, rows - span))
        dr = np.zeros((NA_QROWS, span), np.int32)
        ok = np.zeros((NA_QROWS, span), bool)
        for q in range(NA_QROWS):
            r = r0 + q
            rs = int(np.clip(r - kr // 2, 0, rows - kr))
            for i in range(span):
                ok[q, i] = rs <= ws + i < rs + kr
                dr[q, i] = (ws + i - r + (NA_WIN_R - 1)) if ok[q, i] else 0
        key = dr.tobytes() + ok.tobytes()
        if key not in index:
            index[key] = len(variants)
            variants.append((dr, ok))
        blk_var.append(index[key])
    return (np.stack([v[0] for v in variants]), np.stack([v[1] for v in variants]),
            np.asarray(blk_var, np.int32))


def _natten_bias(rpb, dr, ok):
    nv, nq, span = dr.shape
    n_dr, n_dc = 2 * NA_WIN_R - 1, 2 * NA_WIN_C - 1
    w = np.arange(GRID_W)
    cs = np.clip(w - NA_WIN_C // 2, 0, GRID_W - NA_WIN_C)
    ok_col = (w[None, :] >= cs[:, None]) & (w[None, :] < cs[:, None] + NA_WIN_C)
    dc = w[None, :] - w[:, None] + (NA_WIN_C - 1)
    e_dc = (dc[None] == np.arange(n_dc)[:, None, None]).astype(np.float32)
    e_dr = ((dr[..., None] == np.arange(n_dr)) & ok[..., None]).astype(np.float32)
    g = jnp.einsum('vqir,hrk,kwc->vhqwic', e_dr, rpb, e_dc, precision=HIGHEST)
    ok_all = ok[:, None, :, None, :, None] & ok_col[None, None, None, :, None, :]
    g = jnp.where(jnp.asarray(ok_all), g, MASK_VALUE)
    return g.reshape(nv, NA_HEADS, nq * GRID_W, span * GRID_W).astype(F32)


def _attend(q2, keys, vals, biases, lane):
    nq = q2.shape[0]
    sels = [(lane >= hh * NA_HEAD_DIM) & (lane < (hh + 1) * NA_HEAD_DIM) for hh in range(2)]
    qs = jnp.concatenate([jnp.where(sel, q2, jnp.zeros_like(q2)) for sel in sels], axis=0)
    ss = []
    for k_i, b0, b1 in zip(keys, biases[0], biases[1]):
        s_i = lax.dot_general(qs, k_i, (((1,), (1,)), ((), ())), preferred_element_type=F32)
        if b0 is not None:
            s_i = jnp.concatenate([s_i[:nq] + b0, s_i[nq:] + b1], axis=0)
        ss.append(s_i)
    m = ss[0].max(axis=-1, keepdims=True)
    for s_i in ss[1:]:
        m = jnp.maximum(m, s_i.max(axis=-1, keepdims=True))
    ps = [jnp.exp(s_i - m) for s_i in ss]
    den = ps[0].sum(axis=-1, keepdims=True)
    for p_i in ps[1:]:
        den = den + p_i.sum(axis=-1, keepdims=True)
    o = jnp.dot(ps[0].astype(BF16), vals[0], preferred_element_type=F32)
    for p_i, v_i in zip(ps[1:], vals[1:]):
        o = o + jnp.dot(p_i.astype(BF16), v_i, preferred_element_type=F32)
    o = o / den
    return jnp.where(sels[0], o[:nq], o[nq:])


def _natten_kernel(var_ref, q_ref, k_ref, v_ref, bias_ref, o_ref, *, rows, ctx_len):
    kr = min(NA_WIN_R, rows)
    span = kr + NA_QROWS - 1
    r0 = pl.program_id(1) * NA_QROWS
    ws = jnp.clip(r0 - kr // 2, 0, rows - span)
    base = pl.multiple_of(ctx_len + ws * GRID_W, GRID_W)
    lane = lax.broadcasted_iota(jnp.int32, (NA_QROWS * GRID_W, LANES), 1)
    scale = NA_HEAD_DIM ** -0.5
    for hp in range(NA_HEADS // 2):
        ls = slice(hp * LANES, (hp + 1) * LANES)
        q2 = q_ref[0, :, ls] * scale
        kw = k_ref[0, pl.ds(base, span * GRID_W), ls]
        vw = v_ref[0, pl.ds(base, span * GRID_W), ls]
        kc = k_ref[0, 0:ctx_len, ls]
        vc = v_ref[0, 0:ctx_len, ls]
        biases = [[bias_ref[0, 2 * hp + hh], None] for hh in range(2)]
        out = _attend(q2, [kw, kc], [vw, vc], biases, lane)
        o_ref[0, :, ls] = out.astype(o_ref.dtype)


def _ctx_attn_kernel(q_ref, k_ref, v_ref, o_ref, *, ctx_len):
    lane = lax.broadcasted_iota(jnp.int32, (ctx_len, LANES), 1)
    scale = NA_HEAD_DIM ** -0.5
    for hp in range(NA_HEADS // 2):
        ls = slice(hp * LANES, (hp + 1) * LANES)
        out = _attend(q_ref[0, :, ls] * scale, [k_ref[0, :, ls]], [v_ref[0, :, ls]], [[None], [None]], lane)
        o_ref[0, :, ls] = out.astype(o_ref.dtype)


def _natten_mixer(p_c, rpb, ctx_len, with_ctx):
    b, s, _ = p_c.shape
    l = s - ctx_len
    rows = l // GRID_W
    kr = min(NA_WIN_R, rows)
    span = kr + NA_QROWS - 1
    nq = NA_QROWS * GRID_W
    dr, ok, blk_var = _natten_plan(rows)
    bias = _natten_bias(rpb, dr, ok)
    cb = ctx_len // nq
    grid_spec = pltpu.PrefetchScalarGridSpec(
        num_scalar_prefetch=1,
        grid=(b, rows // NA_QROWS),
        in_specs=[pl.BlockSpec((1, nq, NA_WIDTH), lambda i, r, var: (i, cb + r, 0)),
                  pl.BlockSpec((1, s, NA_WIDTH), lambda i, r, var: (i, 0, 1)),
                  pl.BlockSpec((1, s, NA_WIDTH), lambda i, r, var: (i, 0, 2)),
                  pl.BlockSpec((1, NA_HEADS, nq, span * GRID_W), lambda i, r, var: (var[r], 0, 0, 0))],
        out_specs=pl.BlockSpec((1, nq, NA_WIDTH), lambda i, r, var: (i, r, 0)),
    )
    y_l = pl.pallas_call(
        functools.partial(_natten_kernel, rows=rows, ctx_len=ctx_len),
        out_shape=jax.ShapeDtypeStruct((b, l, NA_WIDTH), BF16),
        grid_spec=grid_spec,
        compiler_params=_cp(("parallel", "arbitrary")),
        name="natten",
    )(jnp.asarray(blk_var), p_c, p_c, p_c, bias)
    if not with_ctx:
        return y_l, None
    y_c = pl.pallas_call(
        functools.partial(_ctx_attn_kernel, ctx_len=ctx_len),
        out_shape=jax.ShapeDtypeStruct((b, ctx_len, NA_WIDTH), BF16),
        grid=(b,),
        in_specs=[pl.BlockSpec((1, ctx_len, NA_WIDTH), lambda i: (i, 0, 0)),
                  pl.BlockSpec((1, ctx_len, NA_WIDTH), lambda i: (i, 0, 1)),
                  pl.BlockSpec((1, ctx_len, NA_WIDTH), lambda i: (i, 0, 2))],
        out_specs=pl.BlockSpec((1, ctx_len, NA_WIDTH), lambda i: (i, 0, 0)),
        compiler_params=_cp(("parallel",)),
        name="ctx_attn",
    )(p_c, p_c, p_c)
    return y_l, y_c


DFT_ROWS = 64


def _dft_gen_kernel(ca_ref, sa_ref, cb_ref, sb_ref, fwd_ref, inv_ref, *, l):
    i = pl.program_id(0)
    ca = ca_ref[0]
    sa = sa_ref[0]
    cb = cb_ref[...]
    sb = sb_ref[...]
    gc = ca * cb - sa * sb
    gs = sa * cb + ca * sb
    x = i * DFT_ROWS + lax.broadcasted_iota(jnp.int32, (DFT_ROWS, l), 0)
    y = lax.broadcasted_iota(jnp.int32, (DFT_ROWS, l), 1)
    n = 2.0 * l
    nyq_x = jnp.where((x & 1) == 0, 1.0, -1.0)
    fwd_ref[:, 0:l] = gc.astype(fwd_ref.dtype)
    fwd_ref[:, l:2 * l] = jnp.where(y == 0, nyq_x, -gs).astype(fwd_ref.dtype)
    scale = jnp.where(x == 0, 1.0 / n, 2.0 / n)
    nyq_y = jnp.where((y & 1) == 0, 1.0, -1.0)
    inv_ref[0] = (scale * gc).astype(inv_ref.dtype)
    inv_ref[1] = (scale * jnp.where(x == 0, nyq_y, -gs)).astype(inv_ref.dtype)


def _dft_matrices(l):
    n = 2 * l
    k1 = l // DFT_ROWS
    y = np.arange(l, dtype=np.int64)
    xa = (DFT_ROWS * np.arange(k1, dtype=np.int64))[:, None]
    xb = np.arange(DFT_ROWS, dtype=np.int64)[:, None]
    ang_a = jnp.asarray(((xa * y[None, :]) % n).astype(np.float32)) * F32(2.0 * math.pi / n)
    ang_b = jnp.asarray(((xb * y[None, :]) % n).astype(np.float32)) * F32(2.0 * math.pi / n)
    ca, sa = jnp.cos(ang_a).reshape(k1, 1, l), jnp.sin(ang_a).reshape(k1, 1, l)
    cb, sb = jnp.cos(ang_b), jnp.sin(ang_b)
    row = pl.BlockSpec((1, 1, l), lambda i: (i, 0, 0))
    tab = pl.BlockSpec((DFT_ROWS, l), lambda i: (0, 0))
    fwd, inv = pl.pallas_call(
        functools.partial(_dft_gen_kernel, l=l),
        out_shape=(jax.ShapeDtypeStruct((l, 2 * l), BF16), jax.ShapeDtypeStruct((2, l, l), BF16)),
        grid=(k1,),
        in_specs=[row, row, tab, tab],
        out_specs=(pl.BlockSpec((DFT_ROWS, 2 * l), lambda i: (i, 0)),
                   pl.BlockSpec((2, DFT_ROWS, l), lambda i: (0, i, 0))),
        compiler_params=_cp(("parallel",)),
        name="dft_gen",
    )(ca, sa, cb, sb)
    return fwd, inv.reshape(2 * l, l)


def _hy_filter_kernel(w1t_ref, w1c_ref, w1s_ref, b1_ref, w2_ref, b2_ref, w3_ref, b3_ref, fr_ref, dl_ref,
                      h_ref, asum_ref, *, l, tl):
    j = pl.program_id(0)
    t = (j * tl + lax.broadcasted_iota(jnp.int32, (1, tl), 1)).astype(F32)
    t_norm = t / l
    bands = (1 + lax.broadcasted_iota(jnp.int32, (HY_BANDS, 1), 0)).astype(F32)
    ang = (2.0 * math.pi / l) * t * bands
    fr = fr_ref[...]
    lin = (w1t_ref[...] * t_norm
           + jnp.dot(w1c_ref[...], jnp.cos(ang), preferred_element_type=F32, precision=HIGHEST)
           + jnp.dot(w1s_ref[...], jnp.sin(ang), preferred_element_type=F32, precision=HIGHEST))
    h = jnp.sin(fr * (lin + b1_ref[...]))
    h = jnp.sin(fr * (jnp.dot(w2_ref[...], h, preferred_element_type=F32, precision=HIGHEST) + b2_ref[...]))
    h = jnp.dot(w3_ref[...], h, preferred_element_type=F32, precision=HIGHEST) + b3_ref[...]
    window = jnp.exp(-t_norm * dl_ref[...])
    first = (j * tl + lax.broadcasted_iota(jnp.int32, (HY_WIDTH, tl), 1)) == 0

    @pl.when(j == 0)
    def _():
        asum_ref[...] = jnp.zeros_like(asum_ref)

    for blk in range(2 * HY_ORDER):
        rs = slice(blk * HY_WIDTH, (blk + 1) * HY_WIDTH)
        hb = h[rs, :] * window
        if blk >= HY_ORDER:
            hb = jnp.where(first, 0.0, hb)
        h_ref[rs, :] = hb
        asum_ref[rs, :] += jnp.sum(jnp.abs(hb), axis=1, keepdims=True)


def _hy_filters(lp, l):
    tl = min(l, 512)
    hid = HY_FILT_HID
    w1 = lp['hy_w1']
    col = lambda v: v.reshape(-1, 1)
    deltas = np.abs(np.linspace(HY_MIN_DECAY, HY_MAX_DECAY, HY_WIDTH, dtype=np.float32)).reshape(-1, 1)
    full = lambda shape: pl.BlockSpec(shape, lambda j: (0, 0))
    n_out = 2 * HY_ORDER * HY_WIDTH
    return pl.pallas_call(
        functools.partial(_hy_filter_kernel, l=l, tl=tl),
        out_shape=(jax.ShapeDtypeStruct((n_out, l), F32), jax.ShapeDtypeStruct((n_out, 1), F32)),
        grid=(l // tl,),
        in_specs=[full((hid, 1)), full((hid, HY_BANDS)), full((hid, HY_BANDS)), full((hid, 1)),
                  full((hid, hid)), full((hid, 1)), full((n_out, hid)), full((n_out, 1)),
                  full((hid, 1)), full((HY_WIDTH, 1))],
        out_specs=(pl.BlockSpec((n_out, tl), lambda j: (0, j)), pl.BlockSpec((n_out, 1), lambda j: (0, 0))),
        compiler_params=_cp(("arbitrary",)),
        name="hyena_filters",
    )(w1[0:1].T, w1[1:1 + HY_BANDS].T, w1[1 + HY_BANDS:].T, col(lp['hy_b1']), lp['hy_w2'].T, col(lp['hy_b2']),
      lp['hy_w3'].T, col(lp['hy_b3']), col(lp['hy_freq']), jnp.asarray(deltas))


def _hy_short_conv_kernel(x_ref, w_ref, b_ref, z_ref, zbf_ref, *, l):
    x = x_ref[0]
    t = lax.broadcasted_iota(jnp.int32, x.shape, 1)
    left = HY_SHORT // 2
    z = jnp.zeros(x.shape, F32) + b_ref[...]
    for k in range(HY_SHORT):
        off = k - left
        if off == 0:
            sh = x
        else:
            sh = pltpu.roll(x, (-off) % l, 1)
            sh = jnp.where((t + off >= 0) & (t + off < l), sh, 0.0)
        z = z + sh * w_ref[:, k:k + 1]
    z_ref[0] = z
    zbf_ref[0] = z.astype(zbf_ref.dtype)


def _hy_conv_kernel(y_ref, fc_ref, fs_ref, ic_ref, is_ref, kfc_ref, kfs_ref, kbc_ref, kbs_ref, n_ref,
                    o_ref, *, nb):
    j = pl.program_id(1)

    @pl.when(j == 0)
    def _():
        o_ref[...] = jnp.zeros_like(o_ref)

    y = y_ref[...]
    zr = jnp.dot(y, fc_ref[...], preferred_element_type=F32)
    zi = jnp.dot(y, fs_ref[...], preferred_element_type=F32)
    inv_n = 1.0 / (n_ref[...] + 1e-6)
    kr = (kfc_ref[...] + kbc_ref[...]) * inv_n
    ki = (kfs_ref[...] - kbs_ref[...]) * inv_n
    tn = kr.shape[1]
    f0 = (j * tn + lax.broadcasted_iota(jnp.int32, kr.shape, 1)) == 0
    ki = jnp.where(f0, (kfs_ref[...] + kbs_ref[...]) * inv_n, ki)
    prs, pis = [], []
    for bb in range(nb):
        rs = slice(bb * HY_WIDTH, (bb + 1) * HY_WIDTH)
        a, b = zr[rs], zi[rs]
        prs.append(a * kr - jnp.where(f0, 0.0, b * ki))
        pis.append(jnp.where(f0, b * ki, a * ki + b * kr))
    pr = jnp.concatenate(prs, axis=0).astype(BF16)
    pi = jnp.concatenate(pis, axis=0).astype(BF16)
    o_ref[...] += (jnp.dot(pr, ic_ref[...], preferred_element_type=F32)
                   + jnp.dot(pi, is_ref[...], preferred_element_type=F32))


def _hy_long_conv(ybf, n_tiles, nb, row_stride, fwd, inv, kf, asum, order, l):
    tmh = nb * HY_WIDTH
    m = n_tiles * tmh
    tn = min(l, 256)
    jn = l // tn
    o_f = order
    o_b = HY_ORDER + order
    return pl.pallas_call(
        functools.partial(_hy_conv_kernel, nb=nb),
        out_shape=jax.ShapeDtypeStruct((m, l), F32),
        grid=(n_tiles, jn),
        in_specs=[pl.BlockSpec((tmh, l), lambda i, j: (i * row_stride, 0)),
                  pl.BlockSpec((l, tn), lambda i, j: (0, j)),
                  pl.BlockSpec((l, tn), lambda i, j: (0, jn + j)),
                  pl.BlockSpec((tn, l), lambda i, j: (j, 0)),
                  pl.BlockSpec((tn, l), lambda i, j: (jn + j, 0)),
                  pl.BlockSpec((HY_WIDTH, tn), lambda i, j: (o_f, j)),
                  pl.BlockSpec((HY_WIDTH, tn), lambda i, j: (o_f, jn + j)),
                  pl.BlockSpec((HY_WIDTH, tn), lambda i, j: (o_b, j)),
                  pl.BlockSpec((HY_WIDTH, tn), lambda i, j: (o_b, jn + j)),
                  pl.BlockSpec((HY_WIDTH, 1), lambda i, j: (order, 0))],
        out_specs=pl.BlockSpec((tmh, l), lambda i, j: (i, 0)),
        compiler_params=_cp(("parallel", "arbitrary")),
        name="hyena_long_conv",
    )(ybf, fwd, fwd, inv, inv, kf, kf, kf, kf, asum)


def _hy_gate_kernel(g_ref, c_ref, y_ref, bias_ref, o_ref, obf_ref):
    o = g_ref[0] * (c_ref[0] + y_ref[0] * bias_ref[...])
    o_ref[0] = o
    obf_ref[0] = o.astype(obf_ref.dtype)


def _hy_gate(z, conv, y, bias_col, gate_blk, y_blk):
    b, _, l = z.shape
    tl = min(l, 1024)
    spec = lambda blk: pl.BlockSpec((1, HY_WIDTH, tl), lambda i, j: (i, blk, j))
    return pl.pallas_call(
        _hy_gate_kernel,
        out_shape=(jax.ShapeDtypeStruct((b, HY_WIDTH, l), F32), jax.ShapeDtypeStruct((b, HY_WIDTH, l), BF16)),
        grid=(b, l // tl),
        in_specs=[spec(gate_blk), spec(0), spec(y_blk), pl.BlockSpec((HY_WIDTH, 1), lambda i, j: (0, 0))],
        out_specs=(spec(0), spec(0)),
        compiler_params=_cp(("parallel", "parallel")),
        name="hyena_gate",
    )(z, conv, y, bias_col)


def _hyena_sequence(p_bt, lp, dft):
    b, c3, l = p_bt.shape
    fwd, inv = dft
    h, asum = _hy_filters(lp, l)
    kf = _matmul(h.astype(BF16), fwd, F32, name="hyena_filter_dft")
    asum2 = asum.reshape(2, HY_ORDER * HY_WIDTH).sum(axis=0).reshape(HY_ORDER * HY_WIDTH, 1)
    blk = pl.BlockSpec((1, LANES, l), lambda i, g: (i, g, 0))
    z, z_bf = pl.pallas_call(
        functools.partial(_hy_short_conv_kernel, l=l),
        out_shape=(jax.ShapeDtypeStruct((b, c3, l), F32), jax.ShapeDtypeStruct((b, c3, l), BF16)),
        grid=(b, c3 // LANES),
        in_specs=[blk,
                  pl.BlockSpec((LANES, HY_SHORT), lambda i, g: (g, 0)),
                  pl.BlockSpec((LANES, 1), lambda i, g: (g, 0))],
        out_specs=(blk, blk),
        compiler_params=_cp(("parallel", "parallel")),
        name="hyena_short_conv",
    )(p_bt, lp['hy_conv_w'].T, lp['hy_conv_b'].reshape(c3, 1))
    n_blk = c3 // HY_WIDTH
    nb2 = 2 if b % 2 == 0 else 1
    conv1 = _hy_long_conv(z_bf.reshape(b * c3, l), b, 1, n_blk, fwd, inv, kf, asum2, 0, l).reshape(b, HY_WIDTH, l)
    y1, y1_bf = _hy_gate(z, conv1, z, lp['hy_bias'][0].reshape(HY_WIDTH, 1), 1, 0)
    conv2 = _hy_long_conv(y1_bf.reshape(b * HY_WIDTH, l), b // nb2, nb2, 1, fwd, inv, kf, asum2, 1, l)
    _, y2_bf = _hy_gate(z, conv2.reshape(b, HY_WIDTH, l), y1, lp['hy_bias'][1].reshape(HY_WIDTH, 1), 2, 0)
    return y2_bf


FFT_N2 = 128
FFT_KB = 4
FFT_NB = 8
FFT_PITCH = 136


def _fft_stage1(xa_ref, xb_ref, f1_ref, xpa, xpb, za, zb, n_in, n1):
    for blk in range(n_in):
        xpa[blk * FFT_PITCH:blk * FFT_PITCH + FFT_N2, :] = xa_ref[0, blk * FFT_N2:(blk + 1) * FFT_N2, :]
        xpb[blk * FFT_PITCH:blk * FFT_PITCH + FFT_N2, :] = xb_ref[0, blk * FFT_N2:(blk + 1) * FFT_N2, :]
    f1 = f1_ref[...]
    for c0 in range(0, FFT_N2, FFT_NB):
        cols = []
        for q in range(FFT_NB):
            cols.append(xpa[pl.ds(c0 + q, n_in, stride=FFT_PITCH), :])
            cols.append(xpb[pl.ds(c0 + q, n_in, stride=FFT_PITCH), :])
        z = jnp.dot(f1, jnp.concatenate(cols, axis=1).astype(BF16), preferred_element_type=F32)
        for q in range(FFT_NB):
            za[pl.ds(c0 + q, 2 * n1, stride=FFT_PITCH), :] = z[:, (2 * q) * LANES:(2 * q + 1) * LANES]
            zb[pl.ds(c0 + q, 2 * n1, stride=FFT_PITCH), :] = z[:, (2 * q + 1) * LANES:(2 * q + 2) * LANES]


def _fft_load_z(za, zb, k1, n1):
    r_re = pl.multiple_of(k1 * FFT_PITCH, SUBLANES)
    r_im = pl.multiple_of((n1 + k1) * FFT_PITCH, SUBLANES)
    z = jnp.concatenate(
        [jnp.concatenate([za[pl.ds(r_re, FFT_N2), :], zb[pl.ds(r_re, FFT_N2), :]], axis=1),
         jnp.concatenate([za[pl.ds(r_im, FFT_N2), :], zb[pl.ds(r_im, FFT_N2), :]], axis=1)], axis=0)
    return z.astype(BF16), r_re, r_im


def _hy_fft_filter_kernel(xa_ref, xb_ref, f1_ref, ef_ref, y_ref, xpa, xpb, za, zb, *, n1):
    j = pl.program_id(1)

    @pl.when(j == 0)
    def _():
        _fft_stage1(xa_ref, xb_ref, f1_ref, xpa, xpb, za, zb, n1, n1)

    for q in range(FFT_KB):
        z, _, _ = _fft_load_z(za, zb, j * FFT_KB + q, n1)
        y = jnp.dot(ef_ref[q], z, preferred_element_type=F32)
        y_ref[0, 0, q] = y[:FFT_N2]
        y_ref[0, 1, q] = y[FFT_N2:]


def _hy_fft_conv_kernel(xa_ref, xb_ref, ga_ref, gb_ref, bias_ref, f1_ref, f1i_ref, ef_ref, ei_ref, kf_ref, n_ref,
                        oa_ref, ob_ref, xpa, xpb, za, zb, *, n1):
    n_in = n1 // 2
    j = pl.program_id(1)

    @pl.when(j == 0)
    def _():
        _fft_stage1(xa_ref, xb_ref, f1_ref, xpa, xpb, za, zb, n_in, n1)

    inv_n = 1.0 / (n_ref[0] + 1e-6)
    for q in range(FFT_KB):
        z, r_re, r_im = _fft_load_z(za, zb, j * FFT_KB + q, n1)
        y = jnp.dot(ef_ref[q], z, preferred_element_type=F32)
        kr = kf_ref[0, 0, q] * inv_n
        ki = kf_ref[0, 1, q] * inv_n
        yr, yi = y[:FFT_N2], y[FFT_N2:]
        p = jnp.concatenate([yr * kr - yi * ki, yr * ki + yi * kr], axis=0).astype(BF16)
        u = jnp.dot(ei_ref[q], p, preferred_element_type=F32)
        za[pl.ds(r_re, FFT_N2), :] = u[:FFT_N2, :LANES]
        zb[pl.ds(r_re, FFT_N2), :] = u[:FFT_N2, LANES:]
        za[pl.ds(r_im, FFT_N2), :] = u[FFT_N2:, :LANES]
        zb[pl.ds(r_im, FFT_N2), :] = u[FFT_N2:, LANES:]

    @pl.when(j == pl.num_programs(1) - 1)
    def _():
        f1i = f1i_ref[...]
        bias = bias_ref[0]
        for c0 in range(0, FFT_N2, FFT_NB):
            cols = []
            for q in range(FFT_NB):
                cols.append(za[pl.ds(c0 + q, 2 * n1, stride=FFT_PITCH), :])
                cols.append(zb[pl.ds(c0 + q, 2 * n1, stride=FFT_PITCH), :])
            yv = jnp.dot(f1i, jnp.concatenate(cols, axis=1).astype(BF16), preferred_element_type=F32)
            for q in range(FFT_NB):
                za[pl.ds(c0 + q, n_in, stride=FFT_PITCH), :] = yv[:, (2 * q) * LANES:(2 * q + 1) * LANES]
                zb[pl.ds(c0 + q, n_in, stride=FFT_PITCH), :] = yv[:, (2 * q + 1) * LANES:(2 * q + 2) * LANES]
        for blk in range(n_in):
            rows = slice(blk * FFT_N2, (blk + 1) * FFT_N2)
            prow = slice(blk * FFT_PITCH, blk * FFT_PITCH + FFT_N2)
            oa_ref[0, rows, :] = ga_ref[0, rows, :] * (za[prow, :] + xa_ref[0, rows, :] * bias[:, :LANES])
            ob_ref[0, rows, :] = gb_ref[0, rows, :] * (zb[prow, :] + xb_ref[0, rows, :] * bias[:, LANES:])


def _fft_tables(l):
    n = 2 * l
    n1 = n // FFT_N2
    n_in = n1 // 2
    k1 = np.arange(n1)
    phi = 2.0 * np.pi * np.outer(k1, np.arange(n1)) / n1
    f1_full = np.concatenate([np.cos(phi), -np.sin(phi)], axis=0).astype(np.float32)
    f1 = f1_full[:, :n_in]
    f1i = (np.concatenate([np.cos(phi[:, :n_in]), -np.sin(phi[:, :n_in])], axis=0).T / n).astype(np.float32)
    n2 = np.arange(FFT_N2)
    alpha = 2.0 * np.pi * np.outer(k1, n2) / n
    beta = 2.0 * np.pi * np.outer(np.arange(FFT_N2), n2) / FFT_N2
    ca, sa = jnp.asarray(np.cos(alpha), F32)[:, None, :], jnp.asarray(np.sin(alpha), F32)[:, None, :]
    cb, sb = jnp.asarray(np.cos(beta), F32)[None], jnp.asarray(np.sin(beta), F32)[None]
    er = ca * cb - sa * sb
    ei = -(sa * cb + ca * sb)
    ef = jnp.concatenate([jnp.concatenate([er, -ei], axis=2), jnp.concatenate([ei, er], axis=2)], axis=1)
    ert, eit = jnp.swapaxes(er, 1, 2), jnp.swapaxes(ei, 1, 2)
    einv = jnp.concatenate([jnp.concatenate([ert, eit], axis=2), jnp.concatenate([-eit, ert], axis=2)], axis=1)
    return {'n1': n1, 'f1': jnp.asarray(f1, BF16), 'f1_full': jnp.asarray(f1_full, BF16), 'f1i': jnp.asarray(f1i, BF16),
            'ef': ef.astype(BF16), 'ei': einv.astype(BF16)}


def _hy_filter_tm_kernel(w1t_ref, w1c_ref, w1s_ref, b1_ref, w2_ref, b2_ref, w3_ref, b3_ref, fr_ref, dl_ref,
                         k_ref, asum_ref, *, l, tl):
    j = pl.program_id(0)
    n = j * tl + lax.broadcasted_iota(jnp.int32, (1, tl), 1)
    t = jnp.where(n < l, n, 2 * l - n).astype(F32)
    t_norm = t / l
    bands = (1 + lax.broadcasted_iota(jnp.int32, (HY_BANDS, 1), 0)).astype(F32)
    ang = (2.0 * math.pi / l) * t * bands
    fr = fr_ref[...]
    lin = (w1t_ref[...] * t_norm
           + jnp.dot(w1c_ref[...], jnp.cos(ang), preferred_element_type=F32, precision=HIGHEST)
           + jnp.dot(w1s_ref[...], jnp.sin(ang), preferred_element_type=F32, precision=HIGHEST))
    h = jnp.sin(fr * (lin + b1_ref[...]))
    h = jnp.sin(fr * (jnp.dot(w2_ref[...], h, preferred_element_type=F32, precision=HIGHEST) + b2_ref[...]))
    h = jnp.dot(w3_ref[...], h, preferred_element_type=F32, precision=HIGHEST) + b3_ref[...]
    window = jnp.exp(-t_norm * dl_ref[...])

    @pl.when(j == 0)
    def _():
        asum_ref[...] = jnp.zeros_like(asum_ref)

    for o in range(HY_ORDER):
        hf = h[o * HY_WIDTH:(o + 1) * HY_WIDTH, :]
        hb = h[(HY_ORDER + o) * HY_WIDTH:(HY_ORDER + o + 1) * HY_WIDTH, :]
        k = jnp.where(n > l, hb, hf) * window
        k = jnp.where(n == l, 0.0, k).T
        k_ref[o] = k
        asum_ref[o] += jnp.sum(jnp.abs(k), axis=0, keepdims=True)


def _hyena_latent(p_b, lp, tabs, ctx_len):
    b, s, c3 = p_b.shape
    l = s - ctx_len
    n = 2 * l
    n1 = tabs['n1']
    n_in = n1 // 2
    hid = HY_FILT_HID
    w = HY_WIDTH
    full = lambda shape: pl.BlockSpec(shape, lambda *idx: (0,) * len(shape))
    tl = 512
    w1 = lp['hy_w1']
    col = lambda v: v.reshape(-1, 1)
    deltas = np.abs(np.linspace(HY_MIN_DECAY, HY_MAX_DECAY, w, dtype=np.float32)).reshape(-1, 1)
    n_out = 2 * HY_ORDER * w
    k_tm, asum = pl.pallas_call(
        functools.partial(_hy_filter_tm_kernel, l=l, tl=tl),
        out_shape=(jax.ShapeDtypeStruct((HY_ORDER, n, w), F32), jax.ShapeDtypeStruct((HY_ORDER, 1, w), F32)),
        grid=(n // tl,),
        in_specs=[full((hid, 1)), full((hid, HY_BANDS)), full((hid, HY_BANDS)), full((hid, 1)), full((hid, hid)),
                  full((hid, 1)), full((n_out, hid)), full((n_out, 1)), full((hid, 1)), full((w, 1))],
        out_specs=(pl.BlockSpec((HY_ORDER, tl, w), lambda j: (0, j, 0)),
                   pl.BlockSpec((HY_ORDER, 1, w), lambda j: (0, 0, 0))),
        compiler_params=_cp(("arbitrary",)),
        name="hyena_filters_tm",
    )(w1[0:1].T, w1[1:1 + HY_BANDS].T, w1[1 + HY_BANDS:].T, col(lp['hy_b1']), lp['hy_w2'].T, col(lp['hy_b2']),
      lp['hy_w3'].T, col(lp['hy_b3']), col(lp['hy_freq']), jnp.asarray(deltas))
    nj = n1 // FFT_KB
    scr = pltpu.VMEM((2 * n1 * FFT_PITCH, LANES), F32)
    xp_full = pltpu.VMEM((n1 * FFT_PITCH, LANES), F32)
    xp_half = pltpu.VMEM((n_in * FFT_PITCH, LANES), F32)
    half = lambda blk: pl.BlockSpec((1, n, LANES), lambda i, j: (i, 0, blk))
    kf = pl.pallas_call(
        functools.partial(_hy_fft_filter_kernel, n1=n1),
        out_shape=jax.ShapeDtypeStruct((HY_ORDER, 2, n1, FFT_N2, w), F32),
        grid=(HY_ORDER, nj),
        in_specs=[half(0), half(1), full((2 * n1, n1)),
                  pl.BlockSpec((FFT_KB, 2 * FFT_N2, 2 * FFT_N2), lambda i, j: (j, 0, 0))],
        out_specs=pl.BlockSpec((1, 2, FFT_KB, FFT_N2, w), lambda i, j: (i, 0, j, 0, 0)),
        scratch_shapes=[xp_full, xp_full, scr, scr],
        compiler_params=_cp(("parallel", "arbitrary")),
        name="hyena_filter_fft",
    )(k_tm, k_tm, tabs['f1_full'], tabs['ef'])
    z = pl.pallas_call(
        functools.partial(_hy_short_conv_tm_kernel, l=l, ctx_len=ctx_len),
        out_shape=jax.ShapeDtypeStruct((b, l, c3), F32),
        grid=(b, c3 // LANES),
        in_specs=[pl.BlockSpec((1, s, LANES), lambda i, g: (i, 0, g)),
                  pl.BlockSpec((HY_SHORT, LANES), lambda i, g: (0, g)),
                  pl.BlockSpec((1, LANES), lambda i, g: (0, g))],
        out_specs=pl.BlockSpec((1, l, LANES), lambda i, g: (i, 0, g)),
        scratch_shapes=[pltpu.VMEM((l + 2 * SUBLANES, LANES), F32)],
        compiler_params=_cp(("parallel", "parallel")),
        name="hyena_short_conv_tm",
    )(p_b, lp['hy_conv_w'], lp['hy_conv_b'].reshape(1, c3))

    def conv(xa, xb, xblk, gate_blk, order):
        lane = lambda arr, blk: pl.BlockSpec((1, l, LANES), lambda i, j: (i, 0, blk))
        half_out = jax.ShapeDtypeStruct((b, l, LANES), F32)
        xb_blk = xblk + 1 if xa is xb else xblk
        return pl.pallas_call(
            functools.partial(_hy_fft_conv_kernel, n1=n1),
            out_shape=(half_out, half_out),
            grid=(b, nj),
            in_specs=[lane(xa, xblk), lane(xb, xb_blk), lane(z, gate_blk), lane(z, gate_blk + 1),
                      pl.BlockSpec((1, 1, w), lambda i, j: (order, 0, 0)),
                      full((2 * n1, n_in)), full((n_in, 2 * n1)),
                      pl.BlockSpec((FFT_KB, 2 * FFT_N2, 2 * FFT_N2), lambda i, j: (j, 0, 0)),
                      pl.BlockSpec((FFT_KB, 2 * FFT_N2, 2 * FFT_N2), lambda i, j: (j, 0, 0)),
                      pl.BlockSpec((1, 2, FFT_KB, FFT_N2, w), lambda i, j: (order, 0, j, 0, 0)),
                      pl.BlockSpec((1, 1, w), lambda i, j: (order, 0, 0))],
            out_specs=(pl.BlockSpec((1, l, LANES), lambda i, j: (i, 0, 0)),
                       pl.BlockSpec((1, l, LANES), lambda i, j: (i, 0, 0))),
            scratch_shapes=[xp_half, xp_half, scr, scr],
            compiler_params=_cp(("parallel", "arbitrary")),
            name="hyena_fft_conv",
        )(xa, xb, z, z, lp['hy_bias'].reshape(HY_ORDER, 1, w), tabs['f1'], tabs['f1i'], tabs['ef'], tabs['ei'],
          kf, asum)

    y1a, y1b = conv(z, z, 0, 2, 0)
    return conv(y1a, y1b, 0, 4, 1)


def _hy_short_conv_tm_kernel(x_ref, w_ref, b_ref, z_ref, xpad, *, l, ctx_len):
    ch = LANES
    zeros8 = jnp.zeros((SUBLANES, LANES), F32)
    xpad[0:SUBLANES, :] = zeros8
    xpad[l + SUBLANES:l + 2 * SUBLANES, :] = zeros8

    def copy_body(i, carry):
        r = pl.multiple_of(i * ch, ch)
        xpad[pl.ds(r + SUBLANES, ch), :] = x_ref[0, pl.ds(r + ctx_len, ch), :]
        return carry

    lax.fori_loop(0, l // ch, copy_body, 0)

    def conv_body(i, carry):
        r = pl.multiple_of(i * ch, ch)
        win = xpad[pl.ds(r, ch + 2 * SUBLANES), :]
        acc = jnp.zeros((ch, LANES), F32) + b_ref[...]
        for k in range(HY_SHORT):
            off = k - HY_SHORT // 2
            acc = acc + win[SUBLANES + off:SUBLANES + off + ch, :] * w_ref[k:k + 1, :]
        z_ref[0, pl.ds(r, ch), :] = acc
        return carry

    lax.fori_loop(0, l // ch, conv_body, 0)


def _top2_route(logits):
    lane = lax.broadcasted_iota(jnp.int32, logits.shape, 1)
    lg = jnp.where(lane < N_EXPERTS, logits, -jnp.inf)
    v1 = lg.max(axis=-1, keepdims=True)
    i1 = jnp.min(jnp.where(lg == v1, lane, LANES), axis=-1, keepdims=True)
    lg2 = jnp.where(lane == i1, -jnp.inf, lg)
    v2 = lg2.max(axis=-1, keepdims=True)
    i2 = jnp.min(jnp.where(lg2 == v2, lane, LANES), axis=-1, keepdims=True)
    e2 = jnp.exp(v2 - v1)
    w1 = 1.0 / (1.0 + e2)
    w2 = e2 / (1.0 + e2)
    out = jnp.where(lane == 0, i1.astype(F32), 0.0)
    out = jnp.where(lane == 1, i2.astype(F32), out)
    out = jnp.where(lane == 2, w1, out)
    return jnp.where(lane == 3, w2, out)


def _merge_kernel(ya_ref, yba_ref, ybb_ref, ybc_ref, ycl_ref, ycc_ref, yd_ref, pg_ref, xc_ref, x_ref, mod_ref, wa_ref,
                  wb_ref, wc_ref, wd_ref, wo_ref, lng_ref, lnb_ref, *rest, alpha, first_tile, with_router, u_tiles):
    if with_router:
        wr_ref, x_out_ref, u_out_ref, route_ref = rest
    else:
        x_out_ref, u_out_ref = rest
    j = pl.program_id(1) + first_tile
    d = x_ref.shape[-1]
    yb = jnp.concatenate([yba_ref[0], ybb_ref[0]], axis=1)
    yc = ycl_ref[0]
    if first_tile == 0:
        yb = jnp.where(j == 0, ybc_ref[0].astype(F32).T, yb)
        yc = jnp.where(j == 0, ycc_ref[0], yc)
    yb = yb.astype(BF16)
    projs = [jnp.dot(ya_ref[0], wa_ref[...], preferred_element_type=F32),
             jnp.dot(yb, wb_ref[...], preferred_element_type=F32),
             jnp.dot(yc, wc_ref[...], preferred_element_type=F32),
             jnp.dot(yd_ref[0], wd_ref[...], preferred_element_type=F32)]
    merged = None
    for i, pr in enumerate(projs):
        term = _sigmoid(pg_ref[0, :, i * d:(i + 1) * d]).astype(F32) * pr
        merged = term if merged is None else merged + term
    m = jnp.dot(merged.astype(BF16), wo_ref[...], preferred_element_type=F32)
    mod = mod_ref[0, 0]
    x_res = x_ref[0]
    if first_tile == 0:
        x_res = jnp.where(j == 0, xc_ref[0], x_res)
    xn = _layer_norm_rows(alpha * x_res + mod[2:3, :] * m) * lng_ref[...] + lnb_ref[...]
    x_out_ref[0] = xn
    u = _layer_norm_rows(xn) * (1.0 + mod[4:5, :]) + mod[3:4, :]
    if u_tiles:
        for sub in range(d // LANES):
            u_out_ref[0, :, sub, :] = u[:, sub * LANES:(sub + 1) * LANES]
    else:
        u_out_ref[0] = u.astype(u_out_ref.dtype)
    if with_router:
        u_hi = u.astype(BF16)
        u_lo = (u - u_hi.astype(F32)).astype(BF16)
        wr = wr_ref[...]
        w_hi = wr.astype(BF16)
        w_lo = (wr - w_hi.astype(F32)).astype(BF16)
        logits = (jnp.dot(u_hi, w_hi, preferred_element_type=F32) + jnp.dot(u_lo, w_hi, preferred_element_type=F32)
                  + jnp.dot(u_hi, w_lo, preferred_element_type=F32))
        route_ref[0] = _top2_route(logits)


def _merge(ya, yb_lat_a, yb_lat_b, yb_ctx, yc_lat, yc_ctx, yd, pg, resid, mod_sel, lp, ln_g, ln_b, alpha, ctx_len,
           with_ctx, u_dtype, w_router=None, u_tiles=False):
    if isinstance(resid, tuple):
        x_ctx, x_lat = resid
        lat_tile = lambda j, first: jnp.maximum(j + first - 1, 0)
    else:
        x_ctx = x_lat = resid
        lat_tile = lambda j, first: j + first
    b, d = x_lat.shape[0], x_lat.shape[2]
    s = ya.shape[1]
    tm = ctx_len
    first = 0 if with_ctx else 1
    nt = s // tm - first
    tok = lambda w: pl.BlockSpec((1, tm, w), lambda i, j: (i, j + first, 0))
    lat = lambda j: jnp.maximum(j + first - 1, 0)
    out = lambda w: pl.BlockSpec((1, tm, w), lambda i, j: (i, j, 0))
    full = lambda shape: pl.BlockSpec(shape, lambda i, j: (0,) * len(shape))
    in_specs = [tok(LRU_WIDTH),
                pl.BlockSpec((1, tm, LANES), lambda i, j: (i, lat(j), 0)),
                pl.BlockSpec((1, tm, LANES), lambda i, j: (i, lat(j), 0)),
                pl.BlockSpec((1, HY_WIDTH, tm), lambda i, j: (i, 0, 0)),
                pl.BlockSpec((1, tm, NA_WIDTH), lambda i, j: (i, lat(j), 0)),
                pl.BlockSpec((1, tm, NA_WIDTH), lambda i, j: (i, 0, 0)),
                tok(S5_WIDTH), tok(N_BRANCH * d),
                pl.BlockSpec((1, tm, d), lambda i, j: (i, 0, 0)),
                pl.BlockSpec((1, tm, d), lambda i, j: (i, lat_tile(j, first), 0)),
                pl.BlockSpec((1, 1, 6, d), lambda i, j: (i, jnp.minimum(j + first, 1), 0, 0)),
                full((LRU_WIDTH, d)), full((HY_WIDTH, d)), full((NA_WIDTH, d)), full((S5_WIDTH, d)),
                full((d, d)), full((1, d)), full((1, d))]
    args = [ya, yb_lat_a, yb_lat_b, yb_ctx, yc_lat, yc_ctx, yd, pg, x_ctx, x_lat, mod_sel,
            lp['w_br_a'].astype(BF16), lp['w_br_b'].astype(BF16), lp['w_br_c'].astype(BF16),
            lp['w_br_d'].astype(BF16), lp['w_out'].astype(BF16), ln_g.reshape(1, d), ln_b.reshape(1, d)]
    if u_tiles:
        u_shape = jax.ShapeDtypeStruct((b, nt * tm, d // LANES, LANES), u_dtype)
        u_spec = pl.BlockSpec((1, tm, d // LANES, LANES), lambda i, j: (i, j, 0, 0))
    else:
        u_shape = jax.ShapeDtypeStruct((b, nt * tm, d), u_dtype)
        u_spec = out(d)
    out_shape = [jax.ShapeDtypeStruct((b, nt * tm, d), F32), u_shape]
    out_specs = [out(d), u_spec]
    if w_router is not None:
        in_specs.append(full((d, LANES)))
        args.append(jnp.zeros((d, LANES), F32).at[:, :N_EXPERTS].set(w_router))
        out_shape.append(jax.ShapeDtypeStruct((b, nt * tm, LANES), F32))
        out_specs.append(out(LANES))
    return pl.pallas_call(
        functools.partial(_merge_kernel, alpha=alpha, first_tile=first, with_router=w_router is not None,
                          u_tiles=u_tiles),
        out_shape=tuple(out_shape),
        grid=(b, nt),
        in_specs=in_specs,
        out_specs=tuple(out_specs),
        compiler_params=_cp(("parallel", "parallel")),
        name="merge",
    )(*args)


FFN_TF = 256


def _dense_ffn_kernel(u_ref, x_ref, mod_ref, modn_ref, wg_ref, wu_ref, wd_ref, lng_ref, lnb_ref,
                      xs_ref, un_ref, acc_ref, *, alpha, ctx_len, tiles_per_seq):
    i = pl.program_id(0)
    j = pl.program_id(1)

    @pl.when(j == 0)
    def _():
        acc_ref[...] = jnp.zeros_like(acc_ref)

    u = u_ref[...]
    g = jnp.dot(u, wg_ref[...], preferred_element_type=F32)
    v = jnp.dot(u, wu_ref[...], preferred_element_type=F32)
    h = (g * _sigmoid(g)) * v
    acc_ref[...] += jnp.dot(h.astype(BF16), wd_ref[...], preferred_element_type=F32)

    @pl.when(j == pl.num_programs(1) - 1)
    def _():
        row = lax.broadcasted_iota(jnp.int32, (acc_ref.shape[0], 1), 0)
        is_ctx = (row < ctx_len) & ((i % tiles_per_seq) == 0)
        pick = lambda m, k: jnp.where(is_ctx, m[0, 0, k:k + 1, :], m[0, 1, k:k + 1, :])
        x2 = (_layer_norm_rows(alpha * x_ref[...] + pick(mod_ref, 5) * acc_ref[...]) * lng_ref[...]
              + lnb_ref[...])
        xs_ref[...] = x2
        un_ref[...] = (_layer_norm_rows(x2) * (1.0 + pick(modn_ref, 1)) + pick(modn_ref, 0)).astype(un_ref.dtype)


def _dense_ffn(u2, x1, mod_sel, mod_next, w_gate, w_up, w_down, ln_g, ln_b, alpha, ctx_len):
    b, s, d = x1.shape
    ff = w_gate.shape[1]
    tiles_per_seq = 4
    tm = s // tiles_per_seq
    assert tm % 16 == 0 and tm >= ctx_len
    tf = FFN_TF
    rows = pl.BlockSpec((tm, d), lambda i, j: (i, 0))
    modspec = pl.BlockSpec((1, 2, 6, d), lambda i, j: (i // tiles_per_seq, 0, 0, 0))
    vec = pl.BlockSpec((1, d), lambda i, j: (0, 0))
    xs, un = pl.pallas_call(
        functools.partial(_dense_ffn_kernel, alpha=alpha, ctx_len=ctx_len, tiles_per_seq=tiles_per_seq),
        out_shape=(jax.ShapeDtypeStruct((b * s, d), F32), jax.ShapeDtypeStruct((b * s, d), BF16)),
        grid=(b * tiles_per_seq, ff // tf),
        in_specs=[rows, rows, modspec, modspec,
                  pl.BlockSpec((d, tf), lambda i, j: (0, j)), pl.BlockSpec((d, tf), lambda i, j: (0, j)),
                  pl.BlockSpec((tf, d), lambda i, j: (j, 0)), vec, vec],
        out_specs=(rows, rows),
        scratch_shapes=[pltpu.VMEM((tm, d), F32)],
        compiler_params=_cp(("parallel", "arbitrary")),
        name="dense_swiglu_ln",
    )(u2.reshape(b * s, d), x1.reshape(b * s, d), mod_sel, mod_next, w_gate, w_up, w_down,
      ln_g.reshape(1, d), ln_b.reshape(1, d))
    return xs.reshape(b, s, d), un.reshape(b, s, d)


MOE_TM = 512
MOE_TF = 896


def _moe_kernel(te_ref, nu_ref, dst_ref, u_hbm, wg_ref, wu_ref, wd_ref, out_hbm,
                xbuf, xbf, acc2, acc, gsem, ssem, *, n_rows):
    tm = MOE_TM
    n_sub = acc.shape[2]
    n_tok = n_rows // TOP_K
    i = pl.program_id(0)
    j = pl.program_id(1)
    nt = pl.num_programs(0)
    nj = pl.num_programs(1)
    n_used = nu_ref[0]
    slot = i % 2

    def gather_start(tile, sl):
        base = tile * tm

        def body(r, carry):
            p = dst_ref[base + r]
            row = jnp.where(p >= n_tok, p - n_tok, jnp.maximum(p, 0))
            pltpu.make_async_copy(u_hbm.at[pl.ds(row, 1)], xbuf.at[sl, pl.ds(r, 1)], gsem.at[sl]).start()
            return carry

        lax.fori_loop(0, tm, body, 0, unroll=8)

    def gather_wait(sl):
        pltpu.make_async_copy(u_hbm.at[pl.ds(0, tm)], xbuf.at[sl], gsem.at[sl]).wait()

    def scatter_wait():
        pltpu.make_async_copy(acc.at[0], out_hbm.at[pl.ds(0, tm)], ssem.at[0]).wait()

    used = i < n_used

    @pl.when(used & (j == 0))
    def _():
        @pl.when(i == 0)
        def _():
            gather_start(0, 0)
            acc[1] = jnp.zeros(acc.shape[1:], F32)
            dump = pltpu.make_async_copy(acc.at[1], out_hbm.at[pl.ds(n_rows, tm)], ssem.at[0])
            dump.start()
            dump.wait()

        gather_wait(slot)

        @pl.when(i + 1 < n_used)
        def _():
            gather_start(i + 1, 1 - slot)

        xbf[...] = jnp.concatenate([xbuf[slot, :, sub, :] for sub in range(n_sub)], axis=1).astype(BF16)
        acc2[...] = jnp.zeros_like(acc2)

    @pl.when(used)
    def _():
        x = xbf[...]
        g = jnp.dot(x, wg_ref[0], preferred_element_type=F32)
        u = jnp.dot(x, wu_ref[0], preferred_element_type=F32)
        h = (g * _sigmoid(g)) * u
        acc2[...] += jnp.dot(h.astype(BF16), wd_ref[0], preferred_element_type=F32)

    @pl.when(used & (j == nj - 1))
    def _():
        @pl.when(i > 0)
        def _():
            scatter_wait()

        for sub in range(n_sub):
            acc[slot, :, sub, :] = acc2[:, sub * LANES:(sub + 1) * LANES]
        base = i * tm

        def body(r, carry):
            p = dst_ref[base + r]
            row = jnp.where(p >= 0, p, n_rows + r)
            pltpu.make_async_copy(acc.at[slot, pl.ds(r, 1)], out_hbm.at[pl.ds(row, 1)], ssem.at[0]).start()
            return carry

        lax.fori_loop(0, tm, body, 0, unroll=8)

    @pl.when((i == nt - 1) & (j == nj - 1))
    def _():
        scatter_wait()


def _moe_experts(u_rows, dst, tile_expert, n_used, w_gate, w_up, w_down):
    r, n_sub, _ = u_rows.shape
    d = n_sub * LANES
    ff = w_gate.shape[2]
    tm, tf = MOE_TM, MOE_TF
    nt = dst.shape[0] // tm
    nj = ff // tf
    n_rows = TOP_K * r

    def jmap(i, j, nu):
        return jnp.where(i < nu[0], j, nj - 1)

    grid_spec = pltpu.PrefetchScalarGridSpec(
        num_scalar_prefetch=3,
        grid=(nt, nj),
        in_specs=[pl.BlockSpec(memory_space=pl.ANY),
                  pl.BlockSpec((1, d, tf), lambda i, j, te, nu, ds: (te[i], 0, jmap(i, j, nu))),
                  pl.BlockSpec((1, d, tf), lambda i, j, te, nu, ds: (te[i], 0, jmap(i, j, nu))),
                  pl.BlockSpec((1, tf, d), lambda i, j, te, nu, ds: (te[i], jmap(i, j, nu), 0))],
        out_specs=pl.BlockSpec(memory_space=pl.ANY),
        scratch_shapes=[pltpu.VMEM((2, tm, n_sub, LANES), F32), pltpu.VMEM((tm, d), BF16), pltpu.VMEM((tm, d), F32),
                        pltpu.VMEM((2, tm, n_sub, LANES), F32),
                        pltpu.SemaphoreType.DMA((2,)), pltpu.SemaphoreType.DMA((1,))],
    )
    return pl.pallas_call(
        functools.partial(_moe_kernel, n_rows=n_rows),
        out_shape=jax.ShapeDtypeStruct((n_rows + tm, n_sub, LANES), F32),
        grid_spec=grid_spec,
        compiler_params=_cp(("arbitrary", "arbitrary")),
        name="moe_experts",
    )(tile_expert, n_used, dst, u_rows, w_gate, w_up, w_down)


def _moe_combine_kernel(x_ref, y1_ref, y2_ref, w_ref, mod_ref, lng_ref, lnb_ref, o_ref, *, alpha):
    mod = mod_ref[0, 0]
    w = w_ref[0]
    untile = lambda r: jnp.concatenate([r[:, sub, :] for sub in range(r.shape[1])], axis=1)
    f = w[:, 2:3] * untile(y1_ref) + w[:, 3:4] * untile(y2_ref)
    o_ref[0] = _layer_norm_rows(alpha * x_ref[0] + mod[5:6, :] * f) * lng_ref[...] + lnb_ref[...]


def _moe_ffn(x_lat, u_lat, route, mod_sel, lp_moe, ln_g, ln_b, alpha):
    b, l, d = x_lat.shape
    n_sub = d // LANES
    t = b * l
    tm = MOE_TM
    ids = route[..., 0:TOP_K].astype(jnp.int32).reshape(t * TOP_K)
    onehot = (ids[:, None] == jnp.arange(N_EXPERTS)[None, :]).astype(jnp.int32)
    csum = jnp.cumsum(onehot, axis=0)
    rank = jnp.take_along_axis(csum, ids[:, None], axis=1)[:, 0] - 1
    counts = csum[-1]
    padded = ((counts + tm - 1) // tm) * tm
    ends = jnp.cumsum(padded)
    starts = ends - padded
    slot = starts[ids] + rank
    n_slots = t * TOP_K + N_EXPERTS * tm
    nt = n_slots // tm
    pair = jnp.arange(t * TOP_K, dtype=jnp.int32)
    dst = jnp.full((n_slots,), -1, jnp.int32).at[slot].set((pair % TOP_K) * t + pair // TOP_K)
    tile_start = jnp.arange(nt, dtype=jnp.int32) * tm
    tile_expert = jnp.minimum(jnp.sum(tile_start[:, None] >= ends[None, :], axis=1), N_EXPERTS - 1).astype(jnp.int32)
    n_used = (ends[-1] // tm).astype(jnp.int32).reshape(1)
    last_e = tile_expert[jnp.maximum(n_used[0] - 1, 0)]
    tile_expert = jnp.where(jnp.arange(nt) < n_used[0], tile_expert, last_e)
    y = _moe_experts(u_lat.reshape(t, n_sub, LANES), dst, tile_expert, n_used, lp_moe['w_gate'], lp_moe['w_up'],
                     lp_moe['w_down'])
    tmc = 256
    nl = l // tmc
    vec = pl.BlockSpec((1, d), lambda i, j: (0, 0))
    return pl.pallas_call(
        functools.partial(_moe_combine_kernel, alpha=alpha),
        out_shape=jax.ShapeDtypeStruct((b, l, d), F32),
        grid=(b, nl),
        in_specs=[pl.BlockSpec((1, tmc, d), lambda i, j: (i, j, 0)),
                  pl.BlockSpec((tmc, n_sub, LANES), lambda i, j: (i * nl + j, 0, 0)),
                  pl.BlockSpec((tmc, n_sub, LANES), lambda i, j: (b * nl + i * nl + j, 0, 0)),
                  pl.BlockSpec((1, tmc, LANES), lambda i, j: (i, j, 0)),
                  pl.BlockSpec((1, 1, 6, d), lambda i, j: (i, 1, 0, 0)), vec, vec],
        out_specs=pl.BlockSpec((1, tmc, d), lambda i, j: (i, j, 0)),
        compiler_params=_cp(("parallel", "parallel")),
        name="moe_combine_ln",
    )(x_lat, y, y, route, mod_sel, ln_g.reshape(1, d), ln_b.reshape(1, d))


def kernel(x, c, ctx, c_ctx, w_mod, b_mod, w_in, lru_conv_w, lru_conv_b, lru_w_r, lru_b_r, lru_w_i, lru_b_i, lru_lambda, hy_conv_w, hy_conv_b, hy_w1, hy_b1, hy_w2, hy_b2, hy_w3, hy_b3, hy_freq, hy_bias, na_rpb, s5_a_re, s5_a_im, s5_log_dt, s5_b_re, s5_b_im, s5_c_re, s5_c_im, s5_d, s5_w_glu, s5_b_glu, w_br_a, w_br_b, w_br_c, w_br_d, w_out, ln1_g, ln1_b, ln2_g, ln2_b, ff_w_gate, ff_w_up, ff_w_down, moe_router, moe_w_gate, moe_w_up, moe_w_down):
    bsz, l, d = x.shape
    ctx_len = ctx.shape[1]
    depth = w_in.shape[0]
    s = ctx_len + l
    alpha = (2.0 * depth) ** 0.25
    xs = (ctx, x)
    c_rows = jnp.zeros((SUBLANES, d), F32).at[0:bsz].set(c).at[bsz].set(c_ctx)
    fft_lat = _fft_tables(l)
    dft_ctx = _dft_matrices(ctx_len) if depth > 1 else None
    assert depth == 2
    mod_all = _mod_vectors(c_rows, w_mod, b_mod).reshape(depth, SUBLANES, 6, d)
    mods = [jnp.stack([jnp.broadcast_to(mod_all[li, bsz], (bsz, 6, d)), mod_all[li, 0:bsz]], axis=1)
            for li in range(depth)]
    mods.append(mods[-1])
    u1 = _ln_mod(ctx, x, mods[0], 0, 1)

    for li in range(depth):
        with_ctx = li < depth - 1
        lp = {
            'lru_conv_w': lru_conv_w[li], 'lru_conv_b': lru_conv_b[li], 'lru_w_r': lru_w_r[li],
            'lru_b_r': lru_b_r[li], 'lru_w_i': lru_w_i[li], 'lru_b_i': lru_b_i[li], 'lru_lambda': lru_lambda[li],
            'hy_conv_w': hy_conv_w[li], 'hy_conv_b': hy_conv_b[li], 'hy_w1': hy_w1[li], 'hy_b1': hy_b1[li],
            'hy_w2': hy_w2[li], 'hy_b2': hy_b2[li], 'hy_w3': hy_w3[li], 'hy_b3': hy_b3[li],
            'hy_freq': hy_freq[li], 'hy_bias': hy_bias[li],
            's5_a_re': s5_a_re[li], 's5_a_im': s5_a_im[li], 's5_log_dt': s5_log_dt[li], 's5_b_re': s5_b_re[li],
            's5_b_im': s5_b_im[li], 's5_c_re': s5_c_re[li], 's5_c_im': s5_c_im[li], 's5_d': s5_d[li],
            's5_w_glu': s5_w_glu[li], 's5_b_glu': s5_b_glu[li], 'w_br_a': w_br_a[li], 'w_br_b': w_br_b[li],
            'w_br_c': w_br_c[li], 'w_br_d': w_br_d[li], 'w_out': w_out[li],
        }
        mod_sel = mods[li]

        u1f = u1.reshape(bsz * s, d)
        wi = w_in[li].astype(BF16)
        p_a = _matmul(u1f, wi[:, OFF_A:OFF_B], F32, name="proj_lru").reshape(bsz, s, OFF_B - OFF_A)
        p_c = _matmul(u1f, wi[:, OFF_C:OFF_D], BF16, name="proj_natten").reshape(bsz, s, OFF_D - OFF_C)
        p_d = _matmul(u1f, wi[:, OFF_D:OFF_G], F32, name="proj_s5").reshape(bsz, s, OFF_G - OFF_D)
        p_g = _matmul(u1f, wi[:, OFF_G:], BF16, name="proj_gates").reshape(bsz, s, N_BRANCH * d)
        p_b = _matmul(u1f, wi[:, OFF_B:OFF_C], F32, name="proj_hyena").reshape(bsz, s, OFF_C - OFF_B)

        ya = _lru_mixer(p_a, lp, ctx_len)
        yb_a, yb_b = _hyena_latent(p_b, lp, fft_lat, ctx_len)
        if with_ctx:
            yb_ctx = _hyena_sequence(_matmul_nt(wi[:, OFF_B:OFF_C].T, u1, F32, 0, ctx_len), lp, dft_ctx)
        else:
            yb_ctx = jnp.zeros((bsz, HY_WIDTH, ctx_len), BF16)
        yc_l, yc_c = _natten_mixer(p_c, na_rpb[li], ctx_len, with_ctx)
        if yc_c is None:
            yc_c = jnp.zeros((bsz, ctx_len, NA_WIDTH), BF16)
        yd = _s5_mixer(p_d, lp, ctx_len)
        e = li // 2
        if li % 2 == 0:
            assert with_ctx
            x1, u2 = _merge(ya, yb_a, yb_b, yb_ctx, yc_l, yc_c, yd, p_g, xs, mod_sel, lp, ln1_g[li], ln1_b[li],
                            alpha, ctx_len, True, BF16)
            xs, u1 = _dense_ffn(u2, x1, mod_sel, mods[li + 1], ff_w_gate[e].astype(BF16), ff_w_up[e].astype(BF16),
                                ff_w_down[e].astype(BF16), ln2_g[li], ln2_b[li], alpha, ctx_len)
        else:
            assert not with_ctx
            x1, u2, route = _merge(ya, yb_a, yb_b, yb_ctx, yc_l, yc_c, yd, p_g, xs, mod_sel, lp, ln1_g[li],
                                   ln1_b[li], alpha, ctx_len, False, F32, moe_router[e], u_tiles=True)
            lp_moe = {'w_gate': moe_w_gate[e].astype(BF16), 'w_up': moe_w_up[e].astype(BF16),
                      'w_down': moe_w_down[e].astype(BF16)}
            return _moe_ffn(x1, u2, route, mod_sel, lp_moe, ln2_g[li], ln2_b[li], alpha)
```

```python
import functools
import math

import numpy as np
import jax
import jax.numpy as jnp
from jax import lax
from jax.experimental import pallas as pl
from jax.experimental.pallas import tpu as pltpu

F32 = jnp.float32
BF16 = jnp.bfloat16
HIGHEST = lax.Precision.HIGHEST

LRU_WIDTH = 384
LRU_BLOCK = 64
LRU_CONV = 4
LRU_C = 8.0
HY_WIDTH = 256
HY_ORDER = 2
HY_SHORT = 3
HY_BANDS = 16
HY_FILT_HID = 64
HY_MAX_DECAY = math.log(1e-2) / 0.3
HY_MIN_DECAY = math.log(1e-2) / 1.5
NA_HEADS = 6
NA_HEAD_DIM = 64
NA_WIDTH = NA_HEADS * NA_HEAD_DIM
NA_WIN_R = 8
NA_WIN_C = 16
GRID_W = 64
S5_WIDTH = 256
S5_GROUP = 16
S5_GROUPS = 16
S5_STATE = 64
N_BRANCH = 4
OFF_A = 0
OFF_B = OFF_A + 2 * LRU_WIDTH
OFF_C = OFF_B + 3 * HY_WIDTH
OFF_D = OFF_C + 3 * NA_WIDTH
OFF_G = OFF_D + S5_WIDTH
N_EXPERTS = 8
TOP_K = 2
LN_EPS = 1e-5
MASK_VALUE = -1e30

LANES = 128
SUBLANES = 8
VMEM_LIMIT = 56 * 1024 * 1024


def _cp(sem, vmem=VMEM_LIMIT):
    return pltpu.CompilerParams(dimension_semantics=sem, vmem_limit_bytes=vmem)


def _gelu(x):
    return 0.5 * x * (1.0 + jnp.tanh(math.sqrt(2.0 / math.pi) * (x + 0.044715 * (x * x * x))))


def _sigmoid(x):
    return 0.5 + 0.5 * jnp.tanh(0.5 * x)


def _layer_norm_rows(x):
    mu = jnp.mean(x, axis=-1, keepdims=True)
    xc = x - mu
    var = jnp.mean(xc * xc, axis=-1, keepdims=True)
    return xc * lax.rsqrt(var + LN_EPS)


def _mod_kernel(c_ref, w_ref, b_ref, o_ref):
    c = c_ref[...]
    a = c * _sigmoid(c)
    o_ref[0] = jnp.dot(a, w_ref[0], preferred_element_type=F32, precision=HIGHEST) + b_ref[0]


def _mod_vectors(c_rows, w_mod, b_mod):
    d = c_rows.shape[1]
    depth, _, n = w_mod.shape
    tn = 1536
    return pl.pallas_call(
        _mod_kernel,
        out_shape=jax.ShapeDtypeStruct((depth, SUBLANES, n), F32),
        grid=(depth, n // tn),
        in_specs=[pl.BlockSpec((SUBLANES, d), lambda l, j: (0, 0)),
                  pl.BlockSpec((1, d, tn), lambda l, j: (l, 0, j)),
                  pl.BlockSpec((1, 1, tn), lambda l, j: (l, 0, j))],
        out_specs=pl.BlockSpec((1, SUBLANES, tn), lambda l, j: (l, 0, j)),
        compiler_params=_cp(("arbitrary", "arbitrary")),
        name="mod_vectors",
    )(c_rows, w_mod, b_mod.reshape(depth, 1, n))


def _ln_mod_kernel(xc_ref, x_ref, mod_ref, o_ref, *, shift_idx, scale_idx):
    x = jnp.where(pl.program_id(1) == 0, xc_ref[0], x_ref[0])
    y = _layer_norm_rows(x)
    m = mod_ref[0, 0]
    o = y * (1.0 + m[scale_idx:scale_idx + 1, :]) + m[shift_idx:shift_idx + 1, :]
    o_ref[0] = o.astype(o_ref.dtype)


def _ln_mod(ctx, x, mod_sel, shift_idx, scale_idx):
    b, ctx_len, d = ctx.shape
    s = ctx_len + x.shape[1]
    tm = ctx_len
    return pl.pallas_call(
        functools.partial(_ln_mod_kernel, shift_idx=shift_idx, scale_idx=scale_idx),
        out_shape=jax.ShapeDtypeStruct((b, s, d), BF16),
        grid=(b, s // tm),
        in_specs=[pl.BlockSpec((1, tm, d), lambda i, j: (i, 0, 0)),
                  pl.BlockSpec((1, tm, d), lambda i, j: (i, jnp.maximum(j - 1, 0), 0)),
                  pl.BlockSpec((1, 1, 6, d), lambda i, j: (i, jnp.minimum(j, 1), 0, 0))],
        out_specs=pl.BlockSpec((1, tm, d), lambda i, j: (i, j, 0)),
        compiler_params=_cp(("parallel", "parallel")),
        name="ln_mod",
    )(ctx, x, mod_sel)


def _mm_kernel(a_ref, w_ref, o_ref):
    o_ref[...] = jnp.dot(a_ref[...], w_ref[...], preferred_element_type=F32).astype(o_ref.dtype)


def _pick_tile(n, prefs):
    for t in prefs:
        if n % t == 0:
            return t
    return n


def _matmul(a, w, out_dtype, tm=None, tn=None, name="matmul"):
    m, k = a.shape
    n = w.shape[1]
    tm = tm or _pick_tile(m, (1024, 512, 256, 128))
    tn = tn or (n if n <= 1536 else _pick_tile(n, (1024, 768, 512, 384, 256, 128)))
    return pl.pallas_call(
        _mm_kernel,
        out_shape=jax.ShapeDtypeStruct((m, n), out_dtype),
        grid=(m // tm, n // tn),
        in_specs=[pl.BlockSpec((tm, k), lambda i, j: (i, 0)),
                  pl.BlockSpec((k, tn), lambda i, j: (0, j))],
        out_specs=pl.BlockSpec((tm, tn), lambda i, j: (i, j)),
        compiler_params=_cp(("parallel", "parallel")),
        name=name,
    )(a, w)


def _mm_nt_kernel(w_ref, u_ref, o_ref):
    o_ref[0] = lax.dot_general(w_ref[...], u_ref[0], (((1,), (1,)), ((), ())),
                               preferred_element_type=F32).astype(o_ref.dtype)


def _matmul_nt(w_t, u, out_dtype, tok0, ntok, tn=256):
    c, k = w_t.shape
    b = u.shape[0]
    j0 = tok0 // tn
    return pl.pallas_call(
        _mm_nt_kernel,
        out_shape=jax.ShapeDtypeStruct((b, c, ntok), out_dtype),
        grid=(b, ntok // tn),
        in_specs=[pl.BlockSpec((c, k), lambda i, j: (0, 0)),
                  pl.BlockSpec((1, tn, k), lambda i, j: (i, j + j0, 0))],
        out_specs=pl.BlockSpec((1, c, tn), lambda i, j: (i, 0, j)),
        compiler_params=_cp(("parallel", "parallel")),
        name="matmul_nt",
    )(w_t, u)


LRU_CHUNK = 128


def _tile_scan(a, b, row, reverse):
    for s in (1, 2, 4):
        if reverse:
            keep = row < SUBLANES - s
            shift = SUBLANES - s
        else:
            keep = row >= s
            shift = s
        a_sh = pltpu.roll(a, shift, 0)
        b_sh = pltpu.roll(b, shift, 0)
        b = jnp.where(keep, a * b_sh, 0.0) + b
        a = jnp.where(keep, a * a_sh, a)
    return a, b


def _lru_kernel(pg_ref, px_ref, cw_ref, cb_ref, wg_ref, bg_ref, lam_ref, y_ref,
                xpad, a_f, b_f, a_b, b_b, *, s_len, ctx_len):
    ch = LRU_CHUNK
    n_chunks = s_len // ch
    zeros8 = jnp.zeros((SUBLANES, LANES), F32)
    xpad[0:SUBLANES, :] = zeros8
    xpad[ctx_len + SUBLANES:ctx_len + 2 * SUBLANES, :] = zeros8
    xpad[s_len + 2 * SUBLANES:s_len + 3 * SUBLANES, :] = zeros8

    def pad_row(r):
        return pl.multiple_of(r + jnp.where(r >= ctx_len, 2 * SUBLANES, SUBLANES), SUBLANES)

    def copy_body(i, carry):
        r = pl.multiple_of(i * ch, ch)
        xpad[pl.ds(pad_row(r), ch), :] = px_ref[0, pl.ds(r, ch), :]
        return carry

    lax.fori_loop(0, n_chunks, copy_body, 0)

    lam = lam_ref[...]
    sp = jnp.log(1.0 + jnp.exp(-lam))

    def gates_body(i, carry):
        r = pl.multiple_of(i * ch, ch)
        win = xpad[pl.ds(pad_row(r) - SUBLANES, ch + 2 * SUBLANES), :]
        xc = jnp.zeros((ch, LANES), F32) + cb_ref[...]
        for k in range(LRU_CONV):
            off = k - LRU_CONV // 2
            xc = xc + win[SUBLANES + off:SUBLANES + off + ch, :] * cw_ref[k:k + 1, :]
        gl = jnp.dot(xc.astype(BF16), wg_ref[0], preferred_element_type=F32) + bg_ref[0]
        for d, (a_s, b_s) in enumerate(((a_f, b_f), (a_b, b_b))):
            g_r = _sigmoid(gl[:, d * 2 * LANES:d * 2 * LANES + LANES])
            g_i = _sigmoid(gl[:, d * 2 * LANES + LANES:(d + 1) * 2 * LANES])
            log_a = (-LRU_C) * g_r * sp[d:d + 1, :]
            a = jnp.exp(log_a)
            bb = jnp.sqrt(1.0 - a * a) * g_i * xc
            a_s[pl.ds(r, ch), :] = a
            b_s[pl.ds(r, ch), :] = bb
        return carry

    lax.fori_loop(0, n_chunks, gates_body, 0)

    row = lax.broadcasted_iota(jnp.int32, (SUBLANES, LANES), 0)

    n_ctx_tiles = ctx_len // SUBLANES
    n_tiles = s_len // SUBLANES

    def scan_body(i, carry):
        h_f, h_b = carry
        r = pl.multiple_of(i * SUBLANES, SUBLANES)
        a, b = _tile_scan(a_f[pl.ds(r, SUBLANES), :], b_f[pl.ds(r, SUBLANES), :], row, False)
        hf = b + a * h_f
        b_f[pl.ds(r, SUBLANES), :] = hf
        t = jnp.where(i < n_ctx_tiles, n_ctx_tiles - 1 - i, n_tiles + n_ctx_tiles - 1 - i)
        rb = pl.multiple_of(t * SUBLANES, SUBLANES)
        a, b = _tile_scan(a_b[pl.ds(rb, SUBLANES), :], b_b[pl.ds(rb, SUBLANES), :], row, True)
        hb = b + a * h_b
        b_b[pl.ds(rb, SUBLANES), :] = hb
        return (jnp.broadcast_to(hf[SUBLANES - 1:SUBLANES, :], (SUBLANES, LANES)),
                jnp.broadcast_to(hb[0:1, :], (SUBLANES, LANES)))

    lax.fori_loop(0, n_tiles, scan_body, (zeros8, zeros8), unroll=2)

    def out_body(i, carry):
        r = pl.multiple_of(i * ch, ch)
        g = _gelu(pg_ref[0, pl.ds(r, ch), :])
        y = g * (b_f[pl.ds(r, ch), :] + b_b[pl.ds(r, ch), :])
        y_ref[0, pl.ds(r, ch), :] = y.astype(y_ref.dtype)
        return carry

    lax.fori_loop(0, n_chunks, out_body, 0)


def _lru_gate_weights(w_r, w_i, b_r, b_i):
    n_grp = LRU_WIDTH // LANES
    per = LANES // LRU_BLOCK

    def bd(w):
        w = w.reshape(n_grp, per, LRU_BLOCK, LRU_BLOCK)
        z = jnp.zeros((n_grp, LRU_BLOCK, LRU_BLOCK), w.dtype)
        top = jnp.concatenate([w[:, 0], z], axis=2)
        bot = jnp.concatenate([z, w[:, 1]], axis=2)
        return jnp.concatenate([top, bot], axis=1)

    wg = jnp.concatenate([bd(w_r[0]), bd(w_i[0]), bd(w_r[1]), bd(w_i[1])], axis=2).astype(BF16)
    bg = jnp.stack([b_r[0], b_i[0], b_r[1], b_i[1]], axis=0).reshape(4, n_grp, LANES)
    bg = jnp.transpose(bg, (1, 0, 2)).reshape(n_grp, 1, 4 * LANES)
    return wg, bg


def _lru_mixer(p_a, lp, ctx_len):
    b, s, _ = p_a.shape
    n_grp = LRU_WIDTH // LANES
    wg, bg = _lru_gate_weights(lp['lru_w_r'], lp['lru_w_i'], lp['lru_b_r'], lp['lru_b_i'])
    scr = pltpu.VMEM((s, LANES), F32)
    return pl.pallas_call(
        functools.partial(_lru_kernel, s_len=s, ctx_len=ctx_len),
        out_shape=jax.ShapeDtypeStruct((b, s, LRU_WIDTH), BF16),
        grid=(b, n_grp),
        in_specs=[pl.BlockSpec((1, s, LANES), lambda i, g: (i, 0, g)),
                  pl.BlockSpec((1, s, LANES), lambda i, g: (i, 0, n_grp + g)),
                  pl.BlockSpec((LRU_CONV, LANES), lambda i, g: (0, g)),
                  pl.BlockSpec((1, LANES), lambda i, g: (0, g)),
                  pl.BlockSpec((1, LANES, 4 * LANES), lambda i, g: (g, 0, 0)),
                  pl.BlockSpec((1, 1, 4 * LANES), lambda i, g: (g, 0, 0)),
                  pl.BlockSpec((2, LANES), lambda i, g: (0, g))],
        out_specs=pl.BlockSpec((1, s, LANES), lambda i, g: (i, 0, g)),
        scratch_shapes=[pltpu.VMEM((s + 3 * SUBLANES, LANES), F32), scr, scr, scr, scr],
        compiler_params=_cp(("parallel", "parallel")),
        name="rglru",
    )(p_a, p_a, lp['lru_conv_w'], lp['lru_conv_b'].reshape(1, LRU_WIDTH), wg, bg, lp['lru_lambda'])


S5_R = 4
S5_NSTATE = S5_GROUPS * S5_STATE


def _s5_kernel(xa_ref, xb_ref, winj_ref, wloc_ref, wro_ref, ap_ref, ya_ref, yb_ref, g_ref, *, reverse, n_ctx_tiles):
    n = S5_NSTATE
    nr = g_ref.shape[0]
    x = jnp.concatenate([h[0, pl.ds(i, nr, stride=S5_R), :] for i in range(S5_R) for h in (xa_ref, xb_ref)],
                        axis=1).astype(BF16)
    g_ref[...] = jnp.dot(x, winj_ref[...], preferred_element_type=F32)
    n_tiles = g_ref.shape[0] // SUBLANES
    row = lax.broadcasted_iota(jnp.int32, (SUBLANES, n), 0)
    zeros = jnp.zeros((SUBLANES, n), F32)
    if reverse:
        shift1, e_in, e_out = SUBLANES - 1, SUBLANES - 1, 0
    else:
        shift1, e_in, e_out = 1, 0, SUBLANES - 1

    def make_body(first_tile):
        def body(i, carry):
            hr, hi = carry
            t = (first_tile - i) if reverse else (first_tile + i)
            r = pl.multiple_of(t * SUBLANES, SUBLANES)
            br = g_ref[pl.ds(r, SUBLANES), 0:n]
            bi = g_ref[pl.ds(r, SUBLANES), n:2 * n]
            for k, s in enumerate((1, 2, 4)):
                ar = ap_ref[SUBLANES + k:SUBLANES + k + 1, 0:n]
                ai = ap_ref[SUBLANES + k:SUBLANES + k + 1, n:2 * n]
                if reverse:
                    keep = row < SUBLANES - s
                    shift = SUBLANES - s
                else:
                    keep = row >= s
                    shift = s
                brs = pltpu.roll(br, shift, 0)
                bis = pltpu.roll(bi, shift, 0)
                nr = ar * brs - ai * bis
                ni = ar * bis + ai * brs
                br = br + jnp.where(keep, nr, 0.0)
                bi = bi + jnp.where(keep, ni, 0.0)
            cr = ap_ref[0:SUBLANES, 0:n]
            ci = ap_ref[0:SUBLANES, n:2 * n]
            out_r = br + (cr * hr - ci * hi)
            out_i = bi + (cr * hi + ci * hr)
            g_ref[pl.ds(r, SUBLANES), 0:n] = jnp.where(row == e_in, hr, pltpu.roll(out_r, shift1, 0))
            g_ref[pl.ds(r, SUBLANES), n:2 * n] = jnp.where(row == e_in, hi, pltpu.roll(out_i, shift1, 0))
            return (jnp.broadcast_to(out_r[e_out:e_out + 1, :], (SUBLANES, n)),
                    jnp.broadcast_to(out_i[e_out:e_out + 1, :], (SUBLANES, n)))
        return body

    if reverse:
        carry = lax.fori_loop(0, n_ctx_tiles, make_body(n_ctx_tiles - 1), (zeros, zeros))
        lax.fori_loop(0, n_tiles - n_ctx_tiles, make_body(n_tiles - 1), carry)
    else:
        lax.fori_loop(0, n_tiles, make_body(0), (zeros, zeros))
    y = (jnp.dot(x, wloc_ref[...], preferred_element_type=F32)
         + jnp.dot(g_ref[...].astype(BF16), wro_ref[...], preferred_element_type=F32))
    for i in range(S5_R):
        for k, h in enumerate((ya_ref, yb_ref)):
            h[0, pl.ds(i, nr, stride=S5_R), :] = y[:, (2 * i + k) * LANES:(2 * i + k + 1) * LANES]


def _s5_params(a_re, a_im, log_dt, b_re, b_im, c_re, c_im, reverse):
    rr = S5_R
    dt = jnp.exp(log_dt)[:, None]
    den = a_re * a_re + a_im * a_im
    mag = jnp.exp(dt * a_re)
    ab_re = mag * jnp.cos(dt * a_im)
    ab_im = mag * jnp.sin(dt * a_im)
    f_re = ((ab_re - 1.0) * a_re + ab_im * a_im) / den
    f_im = (ab_im * a_re - (ab_re - 1.0) * a_im) / den
    bb_re = f_re[..., None] * b_re - f_im[..., None] * b_im
    bb_im = f_re[..., None] * b_im + f_im[..., None] * b_re
    grp_tok = (np.arange(rr * S5_WIDTH) // S5_GROUP) % S5_GROUPS
    grp_state = np.arange(S5_NSTATE) // S5_STATE

    def block_diag(t, lead, grp_rows, grp_cols):
        n_lead, minor, ncols = t.shape
        full = jnp.broadcast_to(t[:, None], (n_lead, S5_GROUPS, minor, ncols)).reshape(-1, ncols)
        return jnp.where(jnp.asarray(grp_rows[:, None] == grp_cols[None, :]), full, 0.0)

    def apow(k):
        k = k.astype(F32)[:, None, None]
        m = jnp.exp(k * dt[None] * a_re[None])
        return m * jnp.cos(k * dt[None] * a_im[None]), m * jnp.sin(k * dt[None] * a_im[None])

    steps = jnp.arange(rr)
    rows = rr * S5_WIDTH
    er, ei = apow(steps if reverse else (rr - 1 - steps))
    inj_re = er[..., None] * bb_re[None] - ei[..., None] * bb_im[None]
    inj_im = er[..., None] * bb_im[None] + ei[..., None] * bb_re[None]
    def inj_map(t):
        t = jnp.transpose(t, (0, 3, 1, 2)).reshape(rr, S5_GROUP, S5_NSTATE)
        return block_diag(t, rr, grp_tok, grp_state)

    winj = jnp.concatenate([inj_map(inj_re), inj_map(inj_im)], axis=1)
    fr, fi = apow((rr - steps) if reverse else (steps + 1))
    ro_re = c_re[None] * fr[:, :, None, :] - c_im[None] * fi[:, :, None, :]
    ro_im = c_re[None] * fi[:, :, None, :] + c_im[None] * fr[:, :, None, :]
    def ro_map(t):
        t = jnp.transpose(t, (3, 0, 1, 2)).reshape(1, S5_STATE, rows)
        return block_diag(t, 1, grp_state, grp_tok)

    wro = jnp.concatenate([ro_map(ro_re), -ro_map(ro_im)], axis=0)
    kr, ki = apow(steps)
    ab_r = kr[..., None] * bb_re[None] - ki[..., None] * bb_im[None]
    ab_i = kr[..., None] * bb_im[None] + ki[..., None] * bb_re[None]
    kk = jnp.einsum('gop,kgpc->kgoc', c_re, ab_r) - jnp.einsum('gop,kgpc->kgoc', c_im, ab_i)
    src = jnp.arange(rr)[:, None]
    tgt = jnp.arange(rr)[None, :]
    lag = (src - tgt) if reverse else (tgt - src)
    kmat = jnp.where((lag >= 0)[:, :, None, None, None], kk[jnp.clip(lag, 0, rr - 1)], 0.0)
    wloc = block_diag(jnp.transpose(kmat, (0, 4, 1, 2, 3)).reshape(rr, S5_GROUP, rows), rr, grp_tok, grp_tok)
    i8 = jnp.arange(SUBLANES)
    dist = (SUBLANES - i8) if reverse else (i8 + 1)
    ks = jnp.concatenate([dist, jnp.array([1, 2, 4]), jnp.zeros((5,), dist.dtype)]) * rr
    pr, pi = apow(ks)
    ap = jnp.concatenate([pr.reshape(16, S5_NSTATE), pi.reshape(16, S5_NSTATE)], axis=1)
    return winj.astype(BF16), wloc.astype(BF16), wro.astype(BF16), ap


def _s5_scan(p_d, lp, d, ctx_len):
    b, s, w = p_d.shape
    nr = s // S5_R
    wr = S5_R * w
    reverse = d == 1
    n_ctx_tiles = ctx_len // (S5_R * SUBLANES)
    winj, wloc, wro, ap = _s5_params(lp['s5_a_re'][d], lp['s5_a_im'][d], lp['s5_log_dt'][d], lp['s5_b_re'][d],
                                     lp['s5_b_im'][d], lp['s5_c_re'][d], lp['s5_c_im'][d], reverse)
    assert w == 2 * LANES
    full = lambda shape: pl.BlockSpec(shape, lambda i: (0, 0))
    half = jax.ShapeDtypeStruct((b, s, LANES), F32)
    return pl.pallas_call(
        functools.partial(_s5_kernel, reverse=reverse, n_ctx_tiles=n_ctx_tiles),
        out_shape=(half, half),
        grid=(b,),
        in_specs=[pl.BlockSpec((1, s, LANES), lambda i: (i, 0, 0)), pl.BlockSpec((1, s, LANES), lambda i: (i, 0, 1)),
                  full((wr, 2 * S5_NSTATE)), full((wr, wr)), full((2 * S5_NSTATE, wr)), full((16, 2 * S5_NSTATE))],
        out_specs=(pl.BlockSpec((1, s, LANES), lambda i: (i, 0, 0)), pl.BlockSpec((1, s, LANES), lambda i: (i, 0, 0))),
        scratch_shapes=[pltpu.VMEM((nr, 2 * S5_NSTATE), F32)],
        compiler_params=_cp(("parallel",)),
        name="s5_scan_bwd" if reverse else "s5_scan_fwd",
    )(p_d, p_d, winj, wloc, wro, ap)


def _s5_out_kernel(yfa_ref, yfb_ref, yba_ref, ybb_ref, u_ref, d_ref, w_ref, b_ref, o_ref):
    y = (jnp.concatenate([yfa_ref[...] + yba_ref[...], yfb_ref[...] + ybb_ref[...]], axis=1)
         + d_ref[...] * u_ref[...])
    g = _gelu(y)
    z = jnp.dot(g.astype(BF16), w_ref[...], preferred_element_type=F32) + b_ref[...]
    o_ref[...] = (g * _sigmoid(z)).astype(o_ref.dtype)


def _s5_mixer(p_d, lp, ctx_len):
    b, s, w = p_d.shape
    yfa, yfb = _s5_scan(p_d, lp, 0, ctx_len)
    yba, ybb = _s5_scan(p_d, lp, 1, ctx_len)
    m = b * s
    tm = _pick_tile(m, (1024, 512, 256))
    row = pl.BlockSpec((tm, w), lambda i: (i, 0))
    hrow = pl.BlockSpec((tm, LANES), lambda i: (i, 0))
    vec = pl.BlockSpec((1, w), lambda i: (0, 0))
    flat = lambda a: a.reshape(m, LANES)
    out = pl.pallas_call(
        _s5_out_kernel,
        out_shape=jax.ShapeDtypeStruct((m, w), BF16),
        grid=(m // tm,),
        in_specs=[hrow, hrow, hrow, hrow, row, vec, pl.BlockSpec((w, w), lambda i: (0, 0)), vec],
        out_specs=row,
        compiler_params=_cp(("parallel",)),
        name="s5_out",
    )(flat(yfa), flat(yfb), flat(yba), flat(ybb), p_d.reshape(m, w), lp['s5_d'].reshape(1, w),
      lp['s5_w_glu'].astype(BF16), lp['s5_b_glu'].reshape(1, w))
    return out.reshape(b, s, w)


NA_QROWS = 4


def _natten_plan(rows):
    kr = min(NA_WIN_R, rows)
    span = kr + NA_QROWS - 1
    variants, index, blk_var = [], {}, []
    for blk in range(rows // NA_QROWS):
        r0 = blk * NA_QROWS
        ws = int(np.clip(r0 - kr // 2, 0, rows - span))
        dr = np.zeros((NA_QROWS, span), np.int32)
        ok = np.zeros((NA_QROWS, span), bool)
        for q in range(NA_QROWS):
            r = r0 + q
            rs = int(np.clip(r - kr // 2, 0, rows - kr))
            for i in range(span):
                ok[q, i] = rs <= ws + i < rs + kr
                dr[q, i] = (ws + i - r + (NA_WIN_R - 1)) if ok[q, i] else 0
        key = dr.tobytes() + ok.tobytes()
        if key not in index:
            index[key] = len(variants)
            variants.append((dr, ok))
        blk_var.append(index[key])
    return (np.stack([v[0] for v in variants]), np.stack([v[1] for v in variants]),
            np.asarray(blk_var, np.int32))


def _natten_bias(rpb, dr, ok):
    nv, nq, span = dr.shape
    n_dr, n_dc = 2 * NA_WIN_R - 1, 2 * NA_WIN_C - 1
    w = np.arange(GRID_W)
    cs = np.clip(w - NA_WIN_C // 2, 0, GRID_W - NA_WIN_C)
    ok_col = (w[None, :] >= cs[:, None]) & (w[None, :] < cs[:, None] + NA_WIN_C)
    dc = w[None, :] - w[:, None] + (NA_WIN_C - 1)
    e_dc = (dc[None] == np.arange(n_dc)[:, None, None]).astype(np.float32)
    e_dr = ((dr[..., None] == np.arange(n_dr)) & ok[..., None]).astype(np.float32)
    g = jnp.einsum('vqir,hrk,kwc->vhqwic', e_dr, rpb, e_dc, precision=HIGHEST)
    ok_all = ok[:, None, :, None, :, None] & ok_col[None, None, None, :, None, :]
    g = jnp.where(jnp.asarray(ok_all), g, MASK_VALUE)
    return g.reshape(nv, NA_HEADS, nq * GRID_W, span * GRID_W).astype(F32)


def _attend(q2, keys, vals, biases, lane):
    nq = q2.shape[0]
    sels = [(lane >= hh * NA_HEAD_DIM) & (lane < (hh + 1) * NA_HEAD_DIM) for hh in range(2)]
    qs = jnp.concatenate([jnp.where(sel, q2, jnp.zeros_like(q2)) for sel in sels], axis=0)
    ss = []
    for k_i, b0, b1 in zip(keys, biases[0], biases[1]):
        s_i = lax.dot_general(qs, k_i, (((1,), (1,)), ((), ())), preferred_element_type=F32)
        if b0 is not None:
            s_i = jnp.concatenate([s_i[:nq] + b0, s_i[nq:] + b1], axis=0)
        ss.append(s_i)
    m = ss[0].max(axis=-1, keepdims=True)
    for s_i in ss[1:]:
        m = jnp.maximum(m, s_i.max(axis=-1, keepdims=True))
    ps = [jnp.exp(s_i - m) for s_i in ss]
    den = ps[0].sum(axis=-1, keepdims=True)
    for p_i in ps[1:]:
        den = den + p_i.sum(axis=-1, keepdims=True)
    o = jnp.dot(ps[0].astype(BF16), vals[0], preferred_element_type=F32)
    for p_i, v_i in zip(ps[1:], vals[1:]):
        o = o + jnp.dot(p_i.astype(BF16), v_i, preferred_element_type=F32)
    o = o / den
    return jnp.where(sels[0], o[:nq], o[nq:])


def _natten_kernel(var_ref, q_ref, k_ref, v_ref, bias_ref, o_ref, *, rows, ctx_len):
    kr = min(NA_WIN_R, rows)
    span = kr + NA_QROWS - 1
    r0 = pl.program_id(1) * NA_QROWS
    ws = jnp.clip(r0 - kr // 2, 0, rows - span)
    base = pl.multiple_of(ctx_len + ws * GRID_W, GRID_W)
    lane = lax.broadcasted_iota(jnp.int32, (NA_QROWS * GRID_W, LANES), 1)
    scale = NA_HEAD_DIM ** -0.5
    for hp in range(NA_HEADS // 2):
        ls = slice(hp * LANES, (hp + 1) * LANES)
        q2 = q_ref[0, :, ls] * scale
        kw = k_ref[0, pl.ds(base, span * GRID_W), ls]
        vw = v_ref[0, pl.ds(base, span * GRID_W), ls]
        kc = k_ref[0, 0:ctx_len, ls]
        vc = v_ref[0, 0:ctx_len, ls]
        biases = [[bias_ref[0, 2 * hp + hh], None] for hh in range(2)]
        out = _attend(q2, [kw, kc], [vw, vc], biases, lane)
        o_ref[0, :, ls] = out.astype(o_ref.dtype)


def _ctx_attn_kernel(q_ref, k_ref, v_ref, o_ref, *, ctx_len):
    lane = lax.broadcasted_iota(jnp.int32, (ctx_len, LANES), 1)
    scale = NA_HEAD_DIM ** -0.5
    for hp in range(NA_HEADS // 2):
        ls = slice(hp * LANES, (hp + 1) * LANES)
        out = _attend(q_ref[0, :, ls] * scale, [k_ref[0, :, ls]], [v_ref[0, :, ls]], [[None], [None]], lane)
        o_ref[0, :, ls] = out.astype(o_ref.dtype)


def _natten_mixer(p_c, rpb, ctx_len, with_ctx):
    b, s, _ = p_c.shape
    l = s - ctx_len
    rows = l // GRID_W
    kr = min(NA_WIN_R, rows)
    span = kr + NA_QROWS - 1
    nq = NA_QROWS * GRID_W
    dr, ok, blk_var = _natten_plan(rows)
    bias = _natten_bias(rpb, dr, ok)
    cb = ctx_len // nq
    grid_spec = pltpu.PrefetchScalarGridSpec(
        num_scalar_prefetch=1,
        grid=(b, rows // NA_QROWS),
        in_specs=[pl.BlockSpec((1, nq, NA_WIDTH), lambda i, r, var: (i, cb + r, 0)),
                  pl.BlockSpec((1, s, NA_WIDTH), lambda i, r, var: (i, 0, 1)),
                  pl.BlockSpec((1, s, NA_WIDTH), lambda i, r, var: (i, 0, 2)),
                  pl.BlockSpec((1, NA_HEADS, nq, span * GRID_W), lambda i, r, var: (var[r], 0, 0, 0))],
        out_specs=pl.BlockSpec((1, nq, NA_WIDTH), lambda i, r, var: (i, r, 0)),
    )
    y_l = pl.pallas_call(
        functools.partial(_natten_kernel, rows=rows, ctx_len=ctx_len),
        out_shape=jax.ShapeDtypeStruct((b, l, NA_WIDTH), BF16),
        grid_spec=grid_spec,
        compiler_params=_cp(("parallel", "arbitrary")),
        name="natten",
    )(jnp.asarray(blk_var), p_c, p_c, p_c, bias)
    if not with_ctx:
        return y_l, None
    y_c = pl.pallas_call(
        functools.partial(_ctx_attn_kernel, ctx_len=ctx_len),
        out_shape=jax.ShapeDtypeStruct((b, ctx_len, NA_WIDTH), BF16),
        grid=(b,),
        in_specs=[pl.BlockSpec((1, ctx_len, NA_WIDTH), lambda i: (i, 0, 0)),
                  pl.BlockSpec((1, ctx_len, NA_WIDTH), lambda i: (i, 0, 1)),
                  pl.BlockSpec((1, ctx_len, NA_WIDTH), lambda i: (i, 0, 2))],
        out_specs=pl.BlockSpec((1, ctx_len, NA_WIDTH), lambda i: (i, 0, 0)),
        compiler_params=_cp(("parallel",)),
        name="ctx_attn",
    )(p_c, p_c, p_c)
    return y_l, y_c


DFT_ROWS = 64


def _dft_gen_kernel(ca_ref, sa_ref, cb_ref, sb_ref, fwd_ref, inv_ref, *, l):
    i = pl.program_id(0)
    ca = ca_ref[0]
    sa = sa_ref[0]
    cb = cb_ref[...]
    sb = sb_ref[...]
    gc = ca * cb - sa * sb
    gs = sa * cb + ca * sb
    x = i * DFT_ROWS + lax.broadcasted_iota(jnp.int32, (DFT_ROWS, l), 0)
    y = lax.broadcasted_iota(jnp.int32, (DFT_ROWS, l), 1)
    n = 2.0 * l
    nyq_x = jnp.where((x & 1) == 0, 1.0, -1.0)
    fwd_ref[:, 0:l] = gc.astype(fwd_ref.dtype)
    fwd_ref[:, l:2 * l] = jnp.where(y == 0, nyq_x, -gs).astype(fwd_ref.dtype)
    scale = jnp.where(x == 0, 1.0 / n, 2.0 / n)
    nyq_y = jnp.where((y & 1) == 0, 1.0, -1.0)
    inv_ref[0] = (scale * gc).astype(inv_ref.dtype)
    inv_ref[1] = (scale * jnp.where(x == 0, nyq_y, -gs)).astype(inv_ref.dtype)


def _dft_matrices(l):
    n = 2 * l
    k1 = l // DFT_ROWS
    y = np.arange(l, dtype=np.int64)
    xa = (DFT_ROWS * np.arange(k1, dtype=np.int64))[:, None]
    xb = np.arange(DFT_ROWS, dtype=np.int64)[:, None]
    ang_a = jnp.asarray(((xa * y[None, :]) % n).astype(np.float32)) * F32(2.0 * math.pi / n)
    ang_b = jnp.asarray(((xb * y[None, :]) % n).astype(np.float32)) * F32(2.0 * math.pi / n)
    ca, sa = jnp.cos(ang_a).reshape(k1, 1, l), jnp.sin(ang_a).reshape(k1, 1, l)
    cb, sb = jnp.cos(ang_b), jnp.sin(ang_b)
    row = pl.BlockSpec((1, 1, l), lambda i: (i, 0, 0))
    tab = pl.BlockSpec((DFT_ROWS, l), lambda i: (0, 0))
    fwd, inv = pl.pallas_call(
        functools.partial(_dft_gen_kernel, l=l),
        out_shape=(jax.ShapeDtypeStruct((l, 2 * l), BF16), jax.ShapeDtypeStruct((2, l, l), BF16)),
        grid=(k1,),
        in_specs=[row, row, tab, tab],
        out_specs=(pl.BlockSpec((DFT_ROWS, 2 * l), lambda i: (i, 0)),
                   pl.BlockSpec((2, DFT_ROWS, l), lambda i: (0, i, 0))),
        compiler_params=_cp(("parallel",)),
        name="dft_gen",
    )(ca, sa, cb, sb)
    return fwd, inv.reshape(2 * l, l)


def _hy_filter_kernel(w1t_ref, w1c_ref, w1s_ref, b1_ref, w2_ref, b2_ref, w3_ref, b3_ref, fr_ref, dl_ref,
                      h_ref, asum_ref, *, l, tl):
    j = pl.program_id(0)
    t = (j * tl + lax.broadcasted_iota(jnp.int32, (1, tl), 1)).astype(F32)
    t_norm = t / l
    bands = (1 + lax.broadcasted_iota(jnp.int32, (HY_BANDS, 1), 0)).astype(F32)
    ang = (2.0 * math.pi / l) * t * bands
    fr = fr_ref[...]
    lin = (w1t_ref[...] * t_norm
           + jnp.dot(w1c_ref[...], jnp.cos(ang), preferred_element_type=F32, precision=HIGHEST)
           + jnp.dot(w1s_ref[...], jnp.sin(ang), preferred_element_type=F32, precision=HIGHEST))
    h = jnp.sin(fr * (lin + b1_ref[...]))
    h = jnp.sin(fr * (jnp.dot(w2_ref[...], h, preferred_element_type=F32, precision=HIGHEST) + b2_ref[...]))
    h = jnp.dot(w3_ref[...], h, preferred_element_type=F32, precision=HIGHEST) + b3_ref[...]
    window = jnp.exp(-t_norm * dl_ref[...])
    first = (j * tl + lax.broadcasted_iota(jnp.int32, (HY_WIDTH, tl), 1)) == 0

    @pl.when(j == 0)
    def _():
        asum_ref[...] = jnp.zeros_like(asum_ref)

    for blk in range(2 * HY_ORDER):
        rs = slice(blk * HY_WIDTH, (blk + 1) * HY_WIDTH)
        hb = h[rs, :] * window
        if blk >= HY_ORDER:
            hb = jnp.where(first, 0.0, hb)
        h_ref[rs, :] = hb
        asum_ref[rs, :] += jnp.sum(jnp.abs(hb), axis=1, keepdims=True)


def _hy_filters(lp, l):
    tl = min(l, 512)
    hid = HY_FILT_HID
    w1 = lp['hy_w1']
    col = lambda v: v.reshape(-1, 1)
    deltas = np.abs(np.linspace(HY_MIN_DECAY, HY_MAX_DECAY, HY_WIDTH, dtype=np.float32)).reshape(-1, 1)
    full = lambda shape: pl.BlockSpec(shape, lambda j: (0, 0))
    n_out = 2 * HY_ORDER * HY_WIDTH
    return pl.pallas_call(
        functools.partial(_hy_filter_kernel, l=l, tl=tl),
        out_shape=(jax.ShapeDtypeStruct((n_out, l), F32), jax.ShapeDtypeStruct((n_out, 1), F32)),
        grid=(l // tl,),
        in_specs=[full((hid, 1)), full((hid, HY_BANDS)), full((hid, HY_BANDS)), full((hid, 1)),
                  full((hid, hid)), full((hid, 1)), full((n_out, hid)), full((n_out, 1)),
                  full((hid, 1)), full((HY_WIDTH, 1))],
        out_specs=(pl.BlockSpec((n_out, tl), lambda j: (0, j)), pl.BlockSpec((n_out, 1), lambda j: (0, 0))),
        compiler_params=_cp(("arbitrary",)),
        name="hyena_filters",
    )(w1[0:1].T, w1[1:1 + HY_BANDS].T, w1[1 + HY_BANDS:].T, col(lp['hy_b1']), lp['hy_w2'].T, col(lp['hy_b2']),
      lp['hy_w3'].T, col(lp['hy_b3']), col(lp['hy_freq']), jnp.asarray(deltas))


def _hy_short_conv_kernel(x_ref, w_ref, b_ref, z_ref, zbf_ref, *, l):
    x = x_ref[0]
    t = lax.broadcasted_iota(jnp.int32, x.shape, 1)
    left = HY_SHORT // 2
    z = jnp.zeros(x.shape, F32) + b_ref[...]
    for k in range(HY_SHORT):
        off = k - left
        if off == 0:
            sh = x
        else:
            sh = pltpu.roll(x, (-off) % l, 1)
            sh = jnp.where((t + off >= 0) & (t + off < l), sh, 0.0)
        z = z + sh * w_ref[:, k:k + 1]
    z_ref[0] = z
    zbf_ref[0] = z.astype(zbf_ref.dtype)


def _hy_conv_kernel(y_ref, fc_ref, fs_ref, ic_ref, is_ref, kfc_ref, kfs_ref, kbc_ref, kbs_ref, n_ref,
                    o_ref, *, nb):
    j = pl.program_id(1)

    @pl.when(j == 0)
    def _():
        o_ref[...] = jnp.zeros_like(o_ref)

    y = y_ref[...]
    zr = jnp.dot(y, fc_ref[...], preferred_element_type=F32)
    zi = jnp.dot(y, fs_ref[...], preferred_element_type=F32)
    inv_n = 1.0 / (n_ref[...] + 1e-6)
    kr = (kfc_ref[...] + kbc_ref[...]) * inv_n
    ki = (kfs_ref[...] - kbs_ref[...]) * inv_n
    tn = kr.shape[1]
    f0 = (j * tn + lax.broadcasted_iota(jnp.int32, kr.shape, 1)) == 0
    ki = jnp.where(f0, (kfs_ref[...] + kbs_ref[...]) * inv_n, ki)
    prs, pis = [], []
    for bb in range(nb):
        rs = slice(bb * HY_WIDTH, (bb + 1) * HY_WIDTH)
        a, b = zr[rs], zi[rs]
        prs.append(a * kr - jnp.where(f0, 0.0, b * ki))
        pis.append(jnp.where(f0, b * ki, a * ki + b * kr))
    pr = jnp.concatenate(prs, axis=0).astype(BF16)
    pi = jnp.concatenate(pis, axis=0).astype(BF16)
    o_ref[...] += (jnp.dot(pr, ic_ref[...], preferred_element_type=F32)
                   + jnp.dot(pi, is_ref[...], preferred_element_type=F32))


def _hy_long_conv(ybf, n_tiles, nb, row_stride, fwd, inv, kf, asum, order, l):
    tmh = nb * HY_WIDTH
    m = n_tiles * tmh
    tn = min(l, 256)
    jn = l // tn
    o_f = order
    o_b = HY_ORDER + order
    return pl.pallas_call(
        functools.partial(_hy_conv_kernel, nb=nb),
        out_shape=jax.ShapeDtypeStruct((m, l), F32),
        grid=(n_tiles, jn),
        in_specs=[pl.BlockSpec((tmh, l), lambda i, j: (i * row_stride, 0)),
                  pl.BlockSpec((l, tn), lambda i, j: (0, j)),
                  pl.BlockSpec((l, tn), lambda i, j: (0, jn + j)),
                  pl.BlockSpec((tn, l), lambda i, j: (j, 0)),
                  pl.BlockSpec((tn, l), lambda i, j: (jn + j, 0)),
                  pl.BlockSpec((HY_WIDTH, tn), lambda i, j: (o_f, j)),
                  pl.BlockSpec((HY_WIDTH, tn), lambda i, j: (o_f, jn + j)),
                  pl.BlockSpec((HY_WIDTH, tn), lambda i, j: (o_b, j)),
                  pl.BlockSpec((HY_WIDTH, tn), lambda i, j: (o_b, jn + j)),
                  pl.BlockSpec((HY_WIDTH, 1), lambda i, j: (order, 0))],
        out_specs=pl.BlockSpec((tmh, l), lambda i, j: (i, 0)),
        compiler_params=_cp(("parallel", "arbitrary")),
        name="hyena_long_conv",
    )(ybf, fwd, fwd, inv, inv, kf, kf, kf, kf, asum)


def _hy_gate_kernel(g_ref, c_ref, y_ref, bias_ref, o_ref, obf_ref):
    o = g_ref[0] * (c_ref[0] + y_ref[0] * bias_ref[...])
    o_ref[0] = o
    obf_ref[0] = o.astype(obf_ref.dtype)


def _hy_gate(z, conv, y, bias_col, gate_blk, y_blk):
    b, _, l = z.shape
    tl = min(l, 1024)
    spec = lambda blk: pl.BlockSpec((1, HY_WIDTH, tl), lambda i, j: (i, blk, j))
    return pl.pallas_call(
        _hy_gate_kernel,
        out_shape=(jax.ShapeDtypeStruct((b, HY_WIDTH, l), F32), jax.ShapeDtypeStruct((b, HY_WIDTH, l), BF16)),
        grid=(b, l // tl),
        in_specs=[spec(gate_blk), spec(0), spec(y_blk), pl.BlockSpec((HY_WIDTH, 1), lambda i, j: (0, 0))],
        out_specs=(spec(0), spec(0)),
        compiler_params=_cp(("parallel", "parallel")),
        name="hyena_gate",
    )(z, conv, y, bias_col)


def _hyena_sequence(p_bt, lp, dft):
    b, c3, l = p_bt.shape
    fwd, inv = dft
    h, asum = _hy_filters(lp, l)
    kf = _matmul(h.astype(BF16), fwd, F32, name="hyena_filter_dft")
    asum2 = asum.reshape(2, HY_ORDER * HY_WIDTH).sum(axis=0).reshape(HY_ORDER * HY_WIDTH, 1)
    blk = pl.BlockSpec((1, LANES, l), lambda i, g: (i, g, 0))
    z, z_bf = pl.pallas_call(
        functools.partial(_hy_short_conv_kernel, l=l),
        out_shape=(jax.ShapeDtypeStruct((b, c3, l), F32), jax.ShapeDtypeStruct((b, c3, l), BF16)),
        grid=(b, c3 // LANES),
        in_specs=[blk,
                  pl.BlockSpec((LANES, HY_SHORT), lambda i, g: (g, 0)),
                  pl.BlockSpec((LANES, 1), lambda i, g: (g, 0))],
        out_specs=(blk, blk),
        compiler_params=_cp(("parallel", "parallel")),
        name="hyena_short_conv",
    )(p_bt, lp['hy_conv_w'].T, lp['hy_conv_b'].reshape(c3, 1))
    n_blk = c3 // HY_WIDTH
    nb2 = 2 if b % 2 == 0 else 1
    conv1 = _hy_long_conv(z_bf.reshape(b * c3, l), b, 1, n_blk, fwd, inv, kf, asum2, 0, l).reshape(b, HY_WIDTH, l)
    y1, y1_bf = _hy_gate(z, conv1, z, lp['hy_bias'][0].reshape(HY_WIDTH, 1), 1, 0)
    conv2 = _hy_long_conv(y1_bf.reshape(b * HY_WIDTH, l), b // nb2, nb2, 1, fwd, inv, kf, asum2, 1, l)
    _, y2_bf = _hy_gate(z, conv2.reshape(b, HY_WIDTH, l), y1, lp['hy_bias'][1].reshape(HY_WIDTH, 1), 2, 0)
    return y2_bf


FFT_N2 = 128
FFT_KB = 4
FFT_NB = 8
FFT_PITCH = 136


def _fft_stage1(xa_ref, xb_ref, f1_ref, xpa, xpb, za, zb, n_in, n1):
    for blk in range(n_in):
        xpa[blk * FFT_PITCH:blk * FFT_PITCH + FFT_N2, :] = xa_ref[0, blk * FFT_N2:(blk + 1) * FFT_N2, :]
        xpb[blk * FFT_PITCH:blk * FFT_PITCH + FFT_N2, :] = xb_ref[0, blk * FFT_N2:(blk + 1) * FFT_N2, :]
    f1 = f1_ref[...]
    for c0 in range(0, FFT_N2, FFT_NB):
        cols = []
        for q in range(FFT_NB):
            cols.append(xpa[pl.ds(c0 + q, n_in, stride=FFT_PITCH), :])
            cols.append(xpb[pl.ds(c0 + q, n_in, stride=FFT_PITCH), :])
        z = jnp.dot(f1, jnp.concatenate(cols, axis=1).astype(BF16), preferred_element_type=F32)
        for q in range(FFT_NB):
            za[pl.ds(c0 + q, 2 * n1, stride=FFT_PITCH), :] = z[:, (2 * q) * LANES:(2 * q + 1) * LANES]
            zb[pl.ds(c0 + q, 2 * n1, stride=FFT_PITCH), :] = z[:, (2 * q + 1) * LANES:(2 * q + 2) * LANES]


def _fft_load_z(za, zb, k1, n1):
    r_re = pl.multiple_of(k1 * FFT_PITCH, SUBLANES)
    r_im = pl.multiple_of((n1 + k1) * FFT_PITCH, SUBLANES)
    z = jnp.concatenate(
        [jnp.concatenate([za[pl.ds(r_re, FFT_N2), :], zb[pl.ds(r_re, FFT_N2), :]], axis=1),
         jnp.concatenate([za[pl.ds(r_im, FFT_N2), :], zb[pl.ds(r_im, FFT_N2), :]], axis=1)], axis=0)
    return z.astype(BF16), r_re, r_im


def _hy_fft_filter_kernel(xa_ref, xb_ref, f1_ref, ef_ref, y_ref, xpa, xpb, za, zb, *, n1):
    j = pl.program_id(1)

    @pl.when(j == 0)
    def _():
        _fft_stage1(xa_ref, xb_ref, f1_ref, xpa, xpb, za, zb, n1, n1)

    for q in range(FFT_KB):
        z, _, _ = _fft_load_z(za, zb, j * FFT_KB + q, n1)
        y = jnp.dot(ef_ref[q], z, preferred_element_type=F32)
        y_ref[0, 0, q] = y[:FFT_N2]
        y_ref[0, 1, q] = y[FFT_N2:]


def _hy_fft_conv_kernel(xa_ref, xb_ref, ga_ref, gb_ref, bias_ref, f1_ref, f1i_ref, ef_ref, ei_ref, kf_ref, n_ref,
                        oa_ref, ob_ref, xpa, xpb, za, zb, *, n1):
    n_in = n1 // 2
    j = pl.program_id(1)

    @pl.when(j == 0)
    def _():
        _fft_stage1(xa_ref, xb_ref, f1_ref, xpa, xpb, za, zb, n_in, n1)

    inv_n = 1.0 / (n_ref[0] + 1e-6)
    for q in range(FFT_KB):
        z, r_re, r_im = _fft_load_z(za, zb, j * FFT_KB + q, n1)
        y = jnp.dot(ef_ref[q], z, preferred_element_type=F32)
        kr = kf_ref[0, 0, q] * inv_n
        ki = kf_ref[0, 1, q] * inv_n
        yr, yi = y[:FFT_N2], y[FFT_N2:]
        p = jnp.concatenate([yr * kr - yi * ki, yr * ki + yi * kr], axis=0).astype(BF16)
        u = jnp.dot(ei_ref[q], p, preferred_element_type=F32)
        za[pl.ds(r_re, FFT_N2), :] = u[:FFT_N2, :LANES]
        zb[pl.ds(r_re, FFT_N2), :] = u[:FFT_N2, LANES:]
        za[pl.ds(r_im, FFT_N2), :] = u[FFT_N2:, :LANES]
        zb[pl.ds(r_im, FFT_N2), :] = u[FFT_N2:, LANES:]

    @pl.when(j == pl.num_programs(1) - 1)
    def _():
        f1i = f1i_ref[...]
        bias = bias_ref[0]
        for c0 in range(0, FFT_N2, FFT_NB):
            cols = []
            for q in range(FFT_NB):
                cols.append(za[pl.ds(c0 + q, 2 * n1, stride=FFT_PITCH), :])
                cols.append(zb[pl.ds(c0 + q, 2 * n1, stride=FFT_PITCH), :])
            yv = jnp.dot(f1i, jnp.concatenate(cols, axis=1).astype(BF16), preferred_element_type=F32)
            for q in range(FFT_NB):
                za[pl.ds(c0 + q, n_in, stride=FFT_PITCH), :] = yv[:, (2 * q) * LANES:(2 * q + 1) * LANES]
                zb[pl.ds(c0 + q, n_in, stride=FFT_PITCH), :] = yv[:, (2 * q + 1) * LANES:(2 * q + 2) * LANES]
        for blk in range(n_in):
            rows = slice(blk * FFT_N2, (blk + 1) * FFT_N2)
            prow = slice(blk * FFT_PITCH, blk * FFT_PITCH + FFT_N2)
            oa_ref[0, rows, :] = ga_ref[0, rows, :] * (za[prow, :] + xa_ref[0, rows, :] * bias[:, :LANES])
            ob_ref[0, rows, :] = gb_ref[0, rows, :] * (zb[prow, :] + xb_ref[0, rows, :] * bias[:, LANES:])


def _fft_tables(l):
    n = 2 * l
    n1 = n // FFT_N2
    n_in = n1 // 2
    k1 = np.arange(n1)
    phi = 2.0 * np.pi * np.outer(k1, np.arange(n1)) / n1
    f1_full = np.concatenate([np.cos(phi), -np.sin(phi)], axis=0).astype(np.float32)
    f1 = f1_full[:, :n_in]
    f1i = (np.concatenate([np.cos(phi[:, :n_in]), -np.sin(phi[:, :n_in])], axis=0).T / n).astype(np.float32)
    n2 = np.arange(FFT_N2)
    alpha = 2.0 * np.pi * np.outer(k1, n2) / n
    beta = 2.0 * np.pi * np.outer(np.arange(FFT_N2), n2) / FFT_N2
    ca, sa = jnp.asarray(np.cos(alpha), F32)[:, None, :], jnp.asarray(np.sin(alpha), F32)[:, None, :]
    cb, sb = jnp.asarray(np.cos(beta), F32)[None], jnp.asarray(np.sin(beta), F32)[None]
    er = ca * cb - sa * sb
    ei = -(sa * cb + ca * sb)
    ef = jnp.concatenate([jnp.concatenate([er, -ei], axis=2), jnp.concatenate([ei, er], axis=2)], axis=1)
    ert, eit = jnp.swapaxes(er, 1, 2), jnp.swapaxes(ei, 1, 2)
    einv = jnp.concatenate([jnp.concatenate([ert, eit], axis=2), jnp.concatenate([-eit, ert], axis=2)], axis=1)
    return {'n1': n1, 'f1': jnp.asarray(f1, BF16), 'f1_full': jnp.asarray(f1_full, BF16), 'f1i': jnp.asarray(f1i, BF16),
            'ef': ef.astype(BF16), 'ei': einv.astype(BF16)}


def _hy_filter_tm_kernel(w1t_ref, w1c_ref, w1s_ref, b1_ref, w2_ref, b2_ref, w3_ref, b3_ref, fr_ref, dl_ref,
                         k_ref, asum_ref, *, l, tl):
    j = pl.program_id(0)
    n = j * tl + lax.broadcasted_iota(jnp.int32, (1, tl), 1)
    t = jnp.where(n < l, n, 2 * l - n).astype(F32)
    t_norm = t / l
    bands = (1 + lax.broadcasted_iota(jnp.int32, (HY_BANDS, 1), 0)).astype(F32)
    ang = (2.0 * math.pi / l) * t * bands
    fr = fr_ref[...]
    lin = (w1t_ref[...] * t_norm
           + jnp.dot(w1c_ref[...], jnp.cos(ang), preferred_element_type=F32, precision=HIGHEST)
           + jnp.dot(w1s_ref[...], jnp.sin(ang), preferred_element_type=F32, precision=HIGHEST))
    h = jnp.sin(fr * (lin + b1_ref[...]))
    h = jnp.sin(fr * (jnp.dot(w2_ref[...], h, preferred_element_type=F32, precision=HIGHEST) + b2_ref[...]))
    h = jnp.dot(w3_ref[...], h, preferred_element_type=F32, precision=HIGHEST) + b3_ref[...]
    window = jnp.exp(-t_norm * dl_ref[...])

    @pl.when(j == 0)
    def _():
        asum_ref[...] = jnp.zeros_like(asum_ref)

    for o in range(HY_ORDER):
        hf = h[o * HY_WIDTH:(o + 1) * HY_WIDTH, :]
        hb = h[(HY_ORDER + o) * HY_WIDTH:(HY_ORDER + o + 1) * HY_WIDTH, :]
        k = jnp.where(n > l, hb, hf) * window
        k = jnp.where(n == l, 0.0, k).T
        k_ref[o] = k
        asum_ref[o] += jnp.sum(jnp.abs(k), axis=0, keepdims=True)


def _hyena_latent(p_b, lp, tabs, ctx_len):
    b, s, c3 = p_b.shape
    l = s - ctx_len
    n = 2 * l
    n1 = tabs['n1']
    n_in = n1 // 2
    hid = HY_FILT_HID
    w = HY_WIDTH
    full = lambda shape: pl.BlockSpec(shape, lambda *idx: (0,) * len(shape))
    tl = 512
    w1 = lp['hy_w1']
    col = lambda v: v.reshape(-1, 1)
    deltas = np.abs(np.linspace(HY_MIN_DECAY, HY_MAX_DECAY, w, dtype=np.float32)).reshape(-1, 1)
    n_out = 2 * HY_ORDER * w
    k_tm, asum = pl.pallas_call(
        functools.partial(_hy_filter_tm_kernel, l=l, tl=tl),
        out_shape=(jax.ShapeDtypeStruct((HY_ORDER, n, w), F32), jax.ShapeDtypeStruct((HY_ORDER, 1, w), F32)),
        grid=(n // tl,),
        in_specs=[full((hid, 1)), full((hid, HY_BANDS)), full((hid, HY_BANDS)), full((hid, 1)), full((hid, hid)),
                  full((hid, 1)), full((n_out, hid)), full((n_out, 1)), full((hid, 1)), full((w, 1))],
        out_specs=(pl.BlockSpec((HY_ORDER, tl, w), lambda j: (0, j, 0)),
                   pl.BlockSpec((HY_ORDER, 1, w), lambda j: (0, 0, 0))),
        compiler_params=_cp(("arbitrary",)),
        name="hyena_filters_tm",
    )(w1[0:1].T, w1[1:1 + HY_BANDS].T, w1[1 + HY_BANDS:].T, col(lp['hy_b1']), lp['hy_w2'].T, col(lp['hy_b2']),
      lp['hy_w3'].T, col(lp['hy_b3']), col(lp['hy_freq']), jnp.asarray(deltas))
    nj = n1 // FFT_KB
    scr = pltpu.VMEM((2 * n1 * FFT_PITCH, LANES), F32)
    xp_full = pltpu.VMEM((n1 * FFT_PITCH, LANES), F32)
    xp_half = pltpu.VMEM((n_in * FFT_PITCH, LANES), F32)
    half = lambda blk: pl.BlockSpec((1, n, LANES), lambda i, j: (i, 0, blk))
    kf = pl.pallas_call(
        functools.partial(_hy_fft_filter_kernel, n1=n1),
        out_shape=jax.ShapeDtypeStruct((HY_ORDER, 2, n1, FFT_N2, w), F32),
        grid=(HY_ORDER, nj),
        in_specs=[half(0), half(1), full((2 * n1, n1)),
                  pl.BlockSpec((FFT_KB, 2 * FFT_N2, 2 * FFT_N2), lambda i, j: (j, 0, 0))],
        out_specs=pl.BlockSpec((1, 2, FFT_KB, FFT_N2, w), lambda i, j: (i, 0, j, 0, 0)),
        scratch_shapes=[xp_full, xp_full, scr, scr],
        compiler_params=_cp(("parallel", "arbitrary")),
        name="hyena_filter_fft",
    )(k_tm, k_tm, tabs['f1_full'], tabs['ef'])
    z = pl.pallas_call(
        functools.partial(_hy_short_conv_tm_kernel, l=l, ctx_len=ctx_len),
        out_shape=jax.ShapeDtypeStruct((b, l, c3), F32),
        grid=(b, c3 // LANES),
        in_specs=[pl.BlockSpec((1, s, LANES), lambda i, g: (i, 0, g)),
                  pl.BlockSpec((HY_SHORT, LANES), lambda i, g: (0, g)),
                  pl.BlockSpec((1, LANES), lambda i, g: (0, g))],
        out_specs=pl.BlockSpec((1, l, LANES), lambda i, g: (i, 0, g)),
        scratch_shapes=[pltpu.VMEM((l + 2 * SUBLANES, LANES), F32)],
        compiler_params=_cp(("parallel", "parallel")),
        name="hyena_short_conv_tm",
    )(p_b, lp['hy_conv_w'], lp['hy_conv_b'].reshape(1, c3))

    def conv(xa, xb, xblk, gate_blk, order):
        lane = lambda arr, blk: pl.BlockSpec((1, l, LANES), lambda i, j: (i, 0, blk))
        half_out = jax.ShapeDtypeStruct((b, l, LANES), F32)
        xb_blk = xblk + 1 if xa is xb else xblk
        return pl.pallas_call(
            functools.partial(_hy_fft_conv_kernel, n1=n1),
            out_shape=(half_out, half_out),
            grid=(b, nj),
            in_specs=[lane(xa, xblk), lane(xb, xb_blk), lane(z, gate_blk), lane(z, gate_blk + 1),
                      pl.BlockSpec((1, 1, w), lambda i, j: (order, 0, 0)),
                      full((2 * n1, n_in)), full((n_in, 2 * n1)),
                      pl.BlockSpec((FFT_KB, 2 * FFT_N2, 2 * FFT_N2), lambda i, j: (j, 0, 0)),
                      pl.BlockSpec((FFT_KB, 2 * FFT_N2, 2 * FFT_N2), lambda i, j: (j, 0, 0)),
                      pl.BlockSpec((1, 2, FFT_KB, FFT_N2, w), lambda i, j: (order, 0, j, 0, 0)),
                      pl.BlockSpec((1, 1, w), lambda i, j: (order, 0, 0))],
            out_specs=(pl.BlockSpec((1, l, LANES), lambda i, j: (i, 0, 0)),
                       pl.BlockSpec((1, l, LANES), lambda i, j: (i, 0, 0))),
            scratch_shapes=[xp_half, xp_half, scr, scr],
            compiler_params=_cp(("parallel", "arbitrary")),
            name="hyena_fft_conv",
        )(xa, xb, z, z, lp['hy_bias'].reshape(HY_ORDER, 1, w), tabs['f1'], tabs['f1i'], tabs['ef'], tabs['ei'],
          kf, asum)

    y1a, y1b = conv(z, z, 0, 2, 0)
    return conv(y1a, y1b, 0, 4, 1)


def _hy_short_conv_tm_kernel(x_ref, w_ref, b_ref, z_ref, xpad, *, l, ctx_len):
    ch = LANES
    zeros8 = jnp.zeros((SUBLANES, LANES), F32)
    xpad[0:SUBLANES, :] = zeros8
    xpad[l + SUBLANES:l + 2 * SUBLANES, :] = zeros8

    def copy_body(i, carry):
        r = pl.multiple_of(i * ch, ch)
        xpad[pl.ds(r + SUBLANES, ch), :] = x_ref[0, pl.ds(r + ctx_len, ch), :]
        return carry

    lax.fori_loop(0, l // ch, copy_body, 0)

    def conv_body(i, carry):
        r = pl.multiple_of(i * ch, ch)
        win = xpad[pl.ds(r, ch + 2 * SUBLANES), :]
        acc = jnp.zeros((ch, LANES), F32) + b_ref[...]
        for k in range(HY_SHORT):
            off = k - HY_SHORT // 2
            acc = acc + win[SUBLANES + off:SUBLANES + off + ch, :] * w_ref[k:k + 1, :]
        z_ref[0, pl.ds(r, ch), :] = acc
        return carry

    lax.fori_loop(0, l // ch, conv_body, 0)


def _top2_route(logits):
    lane = lax.broadcasted_iota(jnp.int32, logits.shape, 1)
    lg = jnp.where(lane < N_EXPERTS, logits, -jnp.inf)
    v1 = lg.max(axis=-1, keepdims=True)
    i1 = jnp.min(jnp.where(lg == v1, lane, LANES), axis=-1, keepdims=True)
    lg2 = jnp.where(lane == i1, -jnp.inf, lg)
    v2 = lg2.max(axis=-1, keepdims=True)
    i2 = jnp.min(jnp.where(lg2 == v2, lane, LANES), axis=-1, keepdims=True)
    e2 = jnp.exp(v2 - v1)
    w1 = 1.0 / (1.0 + e2)
    w2 = e2 / (1.0 + e2)
    out = jnp.where(lane == 0, i1.astype(F32), 0.0)
    out = jnp.where(lane == 1, i2.astype(F32), out)
    out = jnp.where(lane == 2, w1, out)
    return jnp.where(lane == 3, w2, out)


def _merge_kernel(ya_ref, yba_ref, ybb_ref, ybc_ref, ycl_ref, ycc_ref, yd_ref, pg_ref, xc_ref, x_ref, mod_ref, wa_ref,
                  wb_ref, wc_ref, wd_ref, wo_ref, lng_ref, lnb_ref, *rest, alpha, first_tile, with_router, u_tiles):
    if with_router:
        wr_ref, x_out_ref, u_out_ref, route_ref = rest
    else:
        x_out_ref, u_out_ref = rest
    j = pl.program_id(1) + first_tile
    d = x_ref.shape[-1]
    yb = jnp.concatenate([yba_ref[0], ybb_ref[0]], axis=1)
    yc = ycl_ref[0]
    if first_tile == 0:
        yb = jnp.where(j == 0, ybc_ref[0].astype(F32).T, yb)
        yc = jnp.where(j == 0, ycc_ref[0], yc)
    yb = yb.astype(BF16)
    projs = [jnp.dot(ya_ref[0], wa_ref[...], preferred_element_type=F32),
             jnp.dot(yb, wb_ref[...], preferred_element_type=F32),
             jnp.dot(yc, wc_ref[...], preferred_element_type=F32),
             jnp.dot(yd_ref[0], wd_ref[...], preferred_element_type=F32)]
    merged = None
    for i, pr in enumerate(projs):
        term = _sigmoid(pg_ref[0, :, i * d:(i + 1) * d]).astype(F32) * pr
        merged = term if merged is None else merged + term
    m = jnp.dot(merged.astype(BF16), wo_ref[...], preferred_element_type=F32)
    mod = mod_ref[0, 0]
    x_res = x_ref[0]
    if first_tile == 0:
        x_res = jnp.where(j == 0, xc_ref[0], x_res)
    xn = _layer_norm_rows(alpha * x_res + mod[2:3, :] * m) * lng_ref[...] + lnb_ref[...]
    x_out_ref[0] = xn
    u = _layer_norm_rows(xn) * (1.0 + mod[4:5, :]) + mod[3:4, :]
    if u_tiles:
        for sub in range(d // LANES):
            u_out_ref[0, :, sub, :] = u[:, sub * LANES:(sub + 1) * LANES]
    else:
        u_out_ref[0] = u.astype(u_out_ref.dtype)
    if with_router:
        u_hi = u.astype(BF16)
        u_lo = (u - u_hi.astype(F32)).astype(BF16)
        wr = wr_ref[...]
        w_hi = wr.astype(BF16)
        w_lo = (wr - w_hi.astype(F32)).astype(BF16)
        logits = (jnp.dot(u_hi, w_hi, preferred_element_type=F32) + jnp.dot(u_lo, w_hi, preferred_element_type=F32)
                  + jnp.dot(u_hi, w_lo, preferred_element_type=F32))
        route_ref[0] = _top2_route(logits)


def _merge(ya, yb_lat_a, yb_lat_b, yb_ctx, yc_lat, yc_ctx, yd, pg, resid, mod_sel, lp, ln_g, ln_b, alpha, ctx_len,
           with_ctx, u_dtype, w_router=None, u_tiles=False):
    if isinstance(resid, tuple):
        x_ctx, x_lat = resid
        lat_tile = lambda j, first: jnp.maximum(j + first - 1, 0)
    else:
        x_ctx = x_lat = resid
        lat_tile = lambda j, first: j + first
    b, d = x_lat.shape[0], x_lat.shape[2]
    s = ya.shape[1]
    tm = ctx_len
    first = 0 if with_ctx else 1
    nt = s // tm - first
    tok = lambda w: pl.BlockSpec((1, tm, w), lambda i, j: (i, j + first, 0))
    lat = lambda j: jnp.maximum(j + first - 1, 0)
    out = lambda w: pl.BlockSpec((1, tm, w), lambda i, j: (i, j, 0))
    full = lambda shape: pl.BlockSpec(shape, lambda i, j: (0,) * len(shape))
    in_specs = [tok(LRU_WIDTH),
                pl.BlockSpec((1, tm, LANES), lambda i, j: (i, lat(j), 0)),
                pl.BlockSpec((1, tm, LANES), lambda i, j: (i, lat(j), 0)),
                pl.BlockSpec((1, HY_WIDTH, tm), lambda i, j: (i, 0, 0)),
                pl.BlockSpec((1, tm, NA_WIDTH), lambda i, j: (i, lat(j), 0)),
                pl.BlockSpec((1, tm, NA_WIDTH), lambda i, j: (i, 0, 0)),
                tok(S5_WIDTH), tok(N_BRANCH * d),
                pl.BlockSpec((1, tm, d), lambda i, j: (i, 0, 0)),
                pl.BlockSpec((1, tm, d), lambda i, j: (i, lat_tile(j, first), 0)),
                pl.BlockSpec((1, 1, 6, d), lambda i, j: (i, jnp.minimum(j + first, 1), 0, 0)),
                full((LRU_WIDTH, d)), full((HY_WIDTH, d)), full((NA_WIDTH, d)), full((S5_WIDTH, d)),
                full((d, d)), full((1, d)), full((1, d))]
    args = [ya, yb_lat_a, yb_lat_b, yb_ctx, yc_lat, yc_ctx, yd, pg, x_ctx, x_lat, mod_sel,
            lp['w_br_a'].astype(BF16), lp['w_br_b'].astype(BF16), lp['w_br_c'].astype(BF16),
            lp['w_br_d'].astype(BF16), lp['w_out'].astype(BF16), ln_g.reshape(1, d), ln_b.reshape(1, d)]
    if u_tiles:
        u_shape = jax.ShapeDtypeStruct((b, nt * tm, d // LANES, LANES), u_dtype)
        u_spec = pl.BlockSpec((1, tm, d // LANES, LANES), lambda i, j: (i, j, 0, 0))
    else:
        u_shape = jax.ShapeDtypeStruct((b, nt * tm, d), u_dtype)
        u_spec = out(d)
    out_shape = [jax.ShapeDtypeStruct((b, nt * tm, d), F32), u_shape]
    out_specs = [out(d), u_spec]
    if w_router is not None:
        in_specs.append(full((d, LANES)))
        args.append(jnp.zeros((d, LANES), F32).at[:, :N_EXPERTS].set(w_router))
        out_shape.append(jax.ShapeDtypeStruct((b, nt * tm, LANES), F32))
        out_specs.append(out(LANES))
    return pl.pallas_call(
        functools.partial(_merge_kernel, alpha=alpha, first_tile=first, with_router=w_router is not None,
                          u_tiles=u_tiles),
        out_shape=tuple(out_shape),
        grid=(b, nt),
        in_specs=in_specs,
        out_specs=tuple(out_specs),
        compiler_params=_cp(("parallel", "parallel")),
        name="merge",
    )(*args)


FFN_TF = 1408
FFN_TILES_PER_SEQ = 8


def _dense_ffn_kernel(u_ref, x_ref, mod_ref, modn_ref, wg_ref, wu_ref, wd_ref, lng_ref, lnb_ref,
                      xs_ref, un_ref, acc_ref, *, alpha, ctx_len, tiles_per_seq):
    i = pl.program_id(0)
    j = pl.program_id(1)

    @pl.when(j == 0)
    def _():
        acc_ref[...] = jnp.zeros_like(acc_ref)

    u = u_ref[...]
    g = jnp.dot(u, wg_ref[...], preferred_element_type=F32)
    v = jnp.dot(u, wu_ref[...], preferred_element_type=F32)
    h = (g * _sigmoid(g)) * v
    acc_ref[...] += jnp.dot(h.astype(BF16), wd_ref[...], preferred_element_type=F32)

    @pl.when(j == pl.num_programs(1) - 1)
    def _():
        row = lax.broadcasted_iota(jnp.int32, (acc_ref.shape[0], 1), 0)
        is_ctx = (row < ctx_len) & ((i % tiles_per_seq) == 0)
        pick = lambda m, k: jnp.where(is_ctx, m[0, 0, k:k + 1, :], m[0, 1, k:k + 1, :])
        x2 = (_layer_norm_rows(alpha * x_ref[...] + pick(mod_ref, 5) * acc_ref[...]) * lng_ref[...]
              + lnb_ref[...])
        xs_ref[...] = x2
        un_ref[...] = (_layer_norm_rows(x2) * (1.0 + pick(modn_ref, 1)) + pick(modn_ref, 0)).astype(un_ref.dtype)


def _dense_ffn(u2, x1, mod_sel, mod_next, w_gate, w_up, w_down, ln_g, ln_b, alpha, ctx_len):
    b, s, d = x1.shape
    ff = w_gate.shape[1]
    tiles_per_seq = FFN_TILES_PER_SEQ
    tm = s // tiles_per_seq
    assert tm % 16 == 0 and tm >= ctx_len
    tf = FFN_TF
    rows = pl.BlockSpec((tm, d), lambda i, j: (i, 0))
    modspec = pl.BlockSpec((1, 2, 6, d), lambda i, j: (i // tiles_per_seq, 0, 0, 0))
    vec = pl.BlockSpec((1, d), lambda i, j: (0, 0))
    xs, un = pl.pallas_call(
        functools.partial(_dense_ffn_kernel, alpha=alpha, ctx_len=ctx_len, tiles_per_seq=tiles_per_seq),
        out_shape=(jax.ShapeDtypeStruct((b * s, d), F32), jax.ShapeDtypeStruct((b * s, d), BF16)),
        grid=(b * tiles_per_seq, ff // tf),
        in_specs=[rows, rows, modspec, modspec,
                  pl.BlockSpec((d, tf), lambda i, j: (0, j)), pl.BlockSpec((d, tf), lambda i, j: (0, j)),
                  pl.BlockSpec((tf, d), lambda i, j: (j, 0)), vec, vec],
        out_specs=(rows, rows),
        scratch_shapes=[pltpu.VMEM((tm, d), F32)],
        compiler_params=_cp(("parallel", "arbitrary")),
        name="dense_swiglu_ln",
    )(u2.reshape(b * s, d), x1.reshape(b * s, d), mod_sel, mod_next, w_gate, w_up, w_down,
      ln_g.reshape(1, d), ln_b.reshape(1, d))
    return xs.reshape(b, s, d), un.reshape(b, s, d)


MOE_TM = 512
MOE_TF = 896


def _moe_kernel(te_ref, nu_ref, dst_ref, u_hbm, wg_ref, wu_ref, wd_ref, out_hbm,
                xbuf, xbf, acc2, acc, gsem, ssem, *, n_rows):
    tm = MOE_TM
    n_sub = acc.shape[2]
    n_tok = n_rows // TOP_K
    i = pl.program_id(0)
    j = pl.program_id(1)
    nt = pl.num_programs(0)
    nj = pl.num_programs(1)
    n_used = nu_ref[0]
    slot = i % 2

    def gather_start(tile, sl):
        base = tile * tm

        def body(r, carry):
            p = dst_ref[base + r]
            row = jnp.where(p >= n_tok, p - n_tok, jnp.maximum(p, 0))
            pltpu.make_async_copy(u_hbm.at[pl.ds(row, 1)], xbuf.at[sl, pl.ds(r, 1)], gsem.at[sl]).start()
            return carry

        lax.fori_loop(0, tm, body, 0, unroll=8)

    def gather_wait(sl):
        pltpu.make_async_copy(u_hbm.at[pl.ds(0, tm)], xbuf.at[sl], gsem.at[sl]).wait()

    def scatter_wait():
        pltpu.make_async_copy(acc.at[0], out_hbm.at[pl.ds(0, tm)], ssem.at[0]).wait()

    used = i < n_used

    @pl.when(used & (j == 0))
    def _():
        @pl.when(i == 0)
        def _():
            gather_start(0, 0)
            acc[1] = jnp.zeros(acc.shape[1:], F32)
            dump = pltpu.make_async_copy(acc.at[1], out_hbm.at[pl.ds(n_rows, tm)], ssem.at[0])
            dump.start()
            dump.wait()

        gather_wait(slot)

        @pl.when(i + 1 < n_used)
        def _():
            gather_start(i + 1, 1 - slot)

        xbf[...] = jnp.concatenate([xbuf[slot, :, sub, :] for sub in range(n_sub)], axis=1).astype(BF16)
        acc2[...] = jnp.zeros_like(acc2)

    @pl.when(used)
    def _():
        x = xbf[...]
        g = jnp.dot(x, wg_ref[0], preferred_element_type=F32)
        u = jnp.dot(x, wu_ref[0], preferred_element_type=F32)
        h = (g * _sigmoid(g)) * u
        acc2[...] += jnp.dot(h.astype(BF16), wd_ref[0], preferred_element_type=F32)

    @pl.when(used & (j == nj - 1))
    def _():
        @pl.when(i > 0)
        def _():
            scatter_wait()

        for sub in range(n_sub):
            acc[slot, :, sub, :] = acc2[:, sub * LANES:(sub + 1) * LANES]
        base = i * tm

        def body(r, carry):
            p = dst_ref[base + r]
            row = jnp.where(p >= 0, p, n_rows + r)
            pltpu.make_async_copy(acc.at[slot, pl.ds(r, 1)], out_hbm.at[pl.ds(row, 1)], ssem.at[0]).start()
            return carry

        lax.fori_loop(0, tm, body, 0, unroll=8)

    @pl.when((i == nt - 1) & (j == nj - 1))
    def _():
        scatter_wait()


def _moe_experts(u_rows, dst, tile_expert, n_used, w_gate, w_up, w_down):
    r, n_sub, _ = u_rows.shape
    d = n_sub * LANES
    ff = w_gate.shape[2]
    tm, tf = MOE_TM, MOE_TF
    nt = dst.shape[0] // tm
    nj = ff // tf
    n_rows = TOP_K * r

    def jmap(i, j, nu):
        return jnp.where(i < nu[0], j, nj - 1)

    grid_spec = pltpu.PrefetchScalarGridSpec(
        num_scalar_prefetch=3,
        grid=(nt, nj),
        in_specs=[pl.BlockSpec(memory_space=pl.ANY),
                  pl.BlockSpec((1, d, tf), lambda i, j, te, nu, ds: (te[i], 0, jmap(i, j, nu))),
                  pl.BlockSpec((1, d, tf), lambda i, j, te, nu, ds: (te[i], 0, jmap(i, j, nu))),
                  pl.BlockSpec((1, tf, d), lambda i, j, te, nu, ds: (te[i], jmap(i, j, nu), 0))],
        out_specs=pl.BlockSpec(memory_space=pl.ANY),
        scratch_shapes=[pltpu.VMEM((2, tm, n_sub, LANES), F32), pltpu.VMEM((tm, d), BF16), pltpu.VMEM((tm, d), F32),
                        pltpu.VMEM((2, tm, n_sub, LANES), F32),
                        pltpu.SemaphoreType.DMA((2,)), pltpu.SemaphoreType.DMA((1,))],
    )
    return pl.pallas_call(
        functools.partial(_moe_kernel, n_rows=n_rows),
        out_shape=jax.ShapeDtypeStruct((n_rows + tm, n_sub, LANES), F32),
        grid_spec=grid_spec,
        compiler_params=_cp(("arbitrary", "arbitrary")),
        name="moe_experts",
    )(tile_expert, n_used, dst, u_rows, w_gate, w_up, w_down)


def _moe_combine_kernel(x_ref, y1_ref, y2_ref, w_ref, mod_ref, lng_ref, lnb_ref, o_ref, *, alpha):
    mod = mod_ref[0, 0]
    w = w_ref[0]
    untile = lambda r: jnp.concatenate([r[:, sub, :] for sub in range(r.shape[1])], axis=1)
    f = w[:, 2:3] * untile(y1_ref) + w[:, 3:4] * untile(y2_ref)
    o_ref[0] = _layer_norm_rows(alpha * x_ref[0] + mod[5:6, :] * f) * lng_ref[...] + lnb_ref[...]


def _moe_ffn(x_lat, u_lat, route, mod_sel, lp_moe, ln_g, ln_b, alpha):
    b, l, d = x_lat.shape
    n_sub = d // LANES
    t = b * l
    tm = MOE_TM
    ids = route[..., 0:TOP_K].astype(jnp.int32).reshape(t * TOP_K)
    onehot = (ids[:, None] == jnp.arange(N_EXPERTS)[None, :]).astype(jnp.int32)
    csum = jnp.cumsum(onehot, axis=0)
    rank = jnp.take_along_axis(csum, ids[:, None], axis=1)[:, 0] - 1
    counts = csum[-1]
    padded = ((counts + tm - 1) // tm) * tm
    ends = jnp.cumsum(padded)
    starts = ends - padded
    slot = starts[ids] + rank
    n_slots = t * TOP_K + N_EXPERTS * tm
    nt = n_slots // tm
    pair = jnp.arange(t * TOP_K, dtype=jnp.int32)
    dst = jnp.full((n_slots,), -1, jnp.int32).at[slot].set((pair % TOP_K) * t + pair // TOP_K)
    tile_start = jnp.arange(nt, dtype=jnp.int32) * tm
    tile_expert = jnp.minimum(jnp.sum(tile_start[:, None] >= ends[None, :], axis=1), N_EXPERTS - 1).astype(jnp.int32)
    n_used = (ends[-1] // tm).astype(jnp.int32).reshape(1)
    last_e = tile_expert[jnp.maximum(n_used[0] - 1, 0)]
    tile_expert = jnp.where(jnp.arange(nt) < n_used[0], tile_expert, last_e)
    y = _moe_experts(u_lat.reshape(t, n_sub, LANES), dst, tile_expert, n_used, lp_moe['w_gate'], lp_moe['w_up'],
                     lp_moe['w_down'])
    tmc = 256
    nl = l // tmc
    vec = pl.BlockSpec((1, d), lambda i, j: (0, 0))
    return pl.pallas_call(
        functools.partial(_moe_combine_kernel, alpha=alpha),
        out_shape=jax.ShapeDtypeStruct((b, l, d), F32),
        grid=(b, nl),
        in_specs=[pl.BlockSpec((1, tmc, d), lambda i, j: (i, j, 0)),
                  pl.BlockSpec((tmc, n_sub, LANES), lambda i, j: (i * nl + j, 0, 0)),
                  pl.BlockSpec((tmc, n_sub, LANES), lambda i, j: (b * nl + i * nl + j, 0, 0)),
                  pl.BlockSpec((1, tmc, LANES), lambda i, j: (i, j, 0)),
                  pl.BlockSpec((1, 1, 6, d), lambda i, j: (i, 1, 0, 0)), vec, vec],
        out_specs=pl.BlockSpec((1, tmc, d), lambda i, j: (i, j, 0)),
        compiler_params=_cp(("parallel", "parallel")),
        name="moe_combine_ln",
    )(x_lat, y, y, route, mod_sel, ln_g.reshape(1, d), ln_b.reshape(1, d))


def kernel(x, c, ctx, c_ctx, w_mod, b_mod, w_in, lru_conv_w, lru_conv_b, lru_w_r, lru_b_r, lru_w_i, lru_b_i, lru_lambda, hy_conv_w, hy_conv_b, hy_w1, hy_b1, hy_w2, hy_b2, hy_w3, hy_b3, hy_freq, hy_bias, na_rpb, s5_a_re, s5_a_im, s5_log_dt, s5_b_re, s5_b_im, s5_c_re, s5_c_im, s5_d, s5_w_glu, s5_b_glu, w_br_a, w_br_b, w_br_c, w_br_d, w_out, ln1_g, ln1_b, ln2_g, ln2_b, ff_w_gate, ff_w_up, ff_w_down, moe_router, moe_w_gate, moe_w_up, moe_w_down):
    bsz, l, d = x.shape
    ctx_len = ctx.shape[1]
    depth = w_in.shape[0]
    s = ctx_len + l
    alpha = (2.0 * depth) ** 0.25
    xs = (ctx, x)
    c_rows = jnp.zeros((SUBLANES, d), F32).at[0:bsz].set(c).at[bsz].set(c_ctx)
    fft_lat = _fft_tables(l)
    dft_ctx = _dft_matrices(ctx_len) if depth > 1 else None
    assert depth == 2
    mod_all = _mod_vectors(c_rows, w_mod, b_mod).reshape(depth, SUBLANES, 6, d)
    mods = [jnp.stack([jnp.broadcast_to(mod_all[li, bsz], (bsz, 6, d)), mod_all[li, 0:bsz]], axis=1)
            for li in range(depth)]
    mods.append(mods[-1])
    u1 = _ln_mod(ctx, x, mods[0], 0, 1)

    for li in range(depth):
        with_ctx = li < depth - 1
        lp = {
            'lru_conv_w': lru_conv_w[li], 'lru_conv_b': lru_conv_b[li], 'lru_w_r': lru_w_r[li],
            'lru_b_r': lru_b_r[li], 'lru_w_i': lru_w_i[li], 'lru_b_i': lru_b_i[li], 'lru_lambda': lru_lambda[li],
            'hy_conv_w': hy_conv_w[li], 'hy_conv_b': hy_conv_b[li], 'hy_w1': hy_w1[li], 'hy_b1': hy_b1[li],
            'hy_w2': hy_w2[li], 'hy_b2': hy_b2[li], 'hy_w3': hy_w3[li], 'hy_b3': hy_b3[li],
            'hy_freq': hy_freq[li], 'hy_bias': hy_bias[li],
            's5_a_re': s5_a_re[li], 's5_a_im': s5_a_im[li], 's5_log_dt': s5_log_dt[li], 's5_b_re': s5_b_re[li],
            's5_b_im': s5_b_im[li], 's5_c_re': s5_c_re[li], 's5_c_im': s5_c_im[li], 's5_d': s5_d[li],
            's5_w_glu': s5_w_glu[li], 's5_b_glu': s5_b_glu[li], 'w_br_a': w_br_a[li], 'w_br_b': w_br_b[li],
            'w_br_c': w_br_c[li], 'w_br_d': w_br_d[li], 'w_out': w_out[li],
        }
        mod_sel = mods[li]

        u1f = u1.reshape(bsz * s, d)
        wi = w_in[li].astype(BF16)
        p_a = _matmul(u1f, wi[:, OFF_A:OFF_B], F32, name="proj_lru").reshape(bsz, s, OFF_B - OFF_A)
        p_c = _matmul(u1f, wi[:, OFF_C:OFF_D], BF16, name="proj_natten").reshape(bsz, s, OFF_D - OFF_C)
        p_d = _matmul(u1f, wi[:, OFF_D:OFF_G], F32, name="proj_s5").reshape(bsz, s, OFF_G - OFF_D)
        p_g = _matmul(u1f, wi[:, OFF_G:], BF16, name="proj_gates").reshape(bsz, s, N_BRANCH * d)
        p_b = _matmul(u1f, wi[:, OFF_B:OFF_C], F32, name="proj_hyena").reshape(bsz, s, OFF_C - OFF_B)

        ya = _lru_mixer(p_a, lp, ctx_len)
        yb_a, yb_b = _hyena_latent(p_b, lp, fft_lat, ctx_len)
        if with_ctx:
            yb_ctx = _hyena_sequence(_matmul_nt(wi[:, OFF_B:OFF_C].T, u1, F32, 0, ctx_len), lp, dft_ctx)
        else:
            yb_ctx = jnp.zeros((bsz, HY_WIDTH, ctx_len), BF16)
        yc_l, yc_c = _natten_mixer(p_c, na_rpb[li], ctx_len, with_ctx)
        if yc_c is None:
            yc_c = jnp.zeros((bsz, ctx_len, NA_WIDTH), BF16)
        yd = _s5_mixer(p_d, lp, ctx_len)
        e = li // 2
        if li % 2 == 0:
            assert with_ctx
            x1, u2 = _merge(ya, yb_a, yb_b, yb_ctx, yc_l, yc_c, yd, p_g, xs, mod_sel, lp, ln1_g[li], ln1_b[li],
                            alpha, ctx_len, True, BF16)
            xs, u1 = _dense_ffn(u2, x1, mod_sel, mods[li + 1], ff_w_gate[e].astype(BF16), ff_w_up[e].astype(BF16),
                                ff_w_down[e].astype(BF16), ln2_g[li], ln2_b[li], alpha, ctx_len)
        else:
            assert not with_ctx
            x1, u2, route = _merge(ya, yb_a, yb_b, yb_ctx, yc_l, yc_c, yd, p_g, xs, mod_sel, lp, ln1_g[li],
                                   ln1_b[li], alpha, ctx_len, False, F32, moe_router[e], u_tiles=True)
            lp_moe = {'w_gate': moe_w_gate[e].astype(BF16), 'w_up': moe_w_up[e].astype(BF16),
                      'w_down': moe_w_down[e].astype(BF16)}
            return _moe_ffn(x1, u2, route, mod_sel, lp_moe, ln2_g[li], ln2_b[li], alpha)
```

```python
import functools
import math

import numpy as np
import jax
import jax.numpy as jnp
from jax import lax
from jax.experimental import pallas as pl
from jax.experimental.pallas import tpu as pltpu

F32 = jnp.float32
BF16 = jnp.bfloat16
HIGHEST = lax.Precision.HIGHEST

LRU_WIDTH = 384
LRU_BLOCK = 64
LRU_CONV = 4
LRU_C = 8.0
HY_WIDTH = 256
HY_ORDER = 2
HY_SHORT = 3
HY_BANDS = 16
HY_FILT_HID = 64
HY_MAX_DECAY = math.log(1e-2) / 0.3
HY_MIN_DECAY = math.log(1e-2) / 1.5
NA_HEADS = 6
NA_HEAD_DIM = 64
NA_WIDTH = NA_HEADS * NA_HEAD_DIM
NA_WIN_R = 8
NA_WIN_C = 16
GRID_W = 64
S5_WIDTH = 256
S5_GROUP = 16
S5_GROUPS = 16
S5_STATE = 64
N_BRANCH = 4
OFF_A = 0
OFF_B = OFF_A + 2 * LRU_WIDTH
OFF_C = OFF_B + 3 * HY_WIDTH
OFF_D = OFF_C + 3 * NA_WIDTH
OFF_G = OFF_D + S5_WIDTH
N_EXPERTS = 8
TOP_K = 2
LN_EPS = 1e-5
MASK_VALUE = -1e30

LANES = 128
SUBLANES = 8
VMEM_LIMIT = 56 * 1024 * 1024


def _cp(sem, vmem=VMEM_LIMIT):
    return pltpu.CompilerParams(dimension_semantics=sem, vmem_limit_bytes=vmem)


def _gelu(x):
    return 0.5 * x * (1.0 + jnp.tanh(math.sqrt(2.0 / math.pi) * (x + 0.044715 * (x * x * x))))


def _sigmoid(x):
    return 0.5 + 0.5 * jnp.tanh(0.5 * x)


def _layer_norm_rows(x):
    mu = jnp.mean(x, axis=-1, keepdims=True)
    xc = x - mu
    var = jnp.mean(xc * xc, axis=-1, keepdims=True)
    return xc * lax.rsqrt(var + LN_EPS)


def _mod_kernel(c_ref, w_ref, b_ref, o_ref):
    c = c_ref[...]
    a = c * _sigmoid(c)
    o_ref[0] = jnp.dot(a, w_ref[0], preferred_element_type=F32, precision=HIGHEST) + b_ref[0]


def _mod_vectors(c_rows, w_mod, b_mod):
    d = c_rows.shape[1]
    depth, _, n = w_mod.shape
    tn = 1536
    return pl.pallas_call(
        _mod_kernel,
        out_shape=jax.ShapeDtypeStruct((depth, SUBLANES, n), F32),
        grid=(depth, n // tn),
        in_specs=[pl.BlockSpec((SUBLANES, d), lambda l, j: (0, 0)),
                  pl.BlockSpec((1, d, tn), lambda l, j: (l, 0, j)),
                  pl.BlockSpec((1, 1, tn), lambda l, j: (l, 0, j))],
        out_specs=pl.BlockSpec((1, SUBLANES, tn), lambda l, j: (l, 0, j)),
        compiler_params=_cp(("arbitrary", "arbitrary")),
        name="mod_vectors",
    )(c_rows, w_mod, b_mod.reshape(depth, 1, n))


def _ln_mod_kernel(xc_ref, x_ref, mod_ref, o_ref, *, shift_idx, scale_idx):
    x = jnp.where(pl.program_id(1) == 0, xc_ref[0], x_ref[0])
    y = _layer_norm_rows(x)
    m = mod_ref[0, 0]
    o = y * (1.0 + m[scale_idx:scale_idx + 1, :]) + m[shift_idx:shift_idx + 1, :]
    o_ref[0] = o.astype(o_ref.dtype)


def _ln_mod(ctx, x, mod_sel, shift_idx, scale_idx):
    b, ctx_len, d = ctx.shape
    s = ctx_len + x.shape[1]
    tm = ctx_len
    return pl.pallas_call(
        functools.partial(_ln_mod_kernel, shift_idx=shift_idx, scale_idx=scale_idx),
        out_shape=jax.ShapeDtypeStruct((b, s, d), BF16),
        grid=(b, s // tm),
        in_specs=[pl.BlockSpec((1, tm, d), lambda i, j: (i, 0, 0)),
                  pl.BlockSpec((1, tm, d), lambda i, j: (i, jnp.maximum(j - 1, 0), 0)),
                  pl.BlockSpec((1, 1, 6, d), lambda i, j: (i, jnp.minimum(j, 1), 0, 0))],
        out_specs=pl.BlockSpec((1, tm, d), lambda i, j: (i, j, 0)),
        compiler_params=_cp(("parallel", "parallel")),
        name="ln_mod",
    )(ctx, x, mod_sel)


def _mm_kernel(a_ref, w_ref, o_ref):
    o_ref[...] = jnp.dot(a_ref[...], w_ref[...], preferred_element_type=F32).astype(o_ref.dtype)


def _pick_tile(n, prefs):
    for t in prefs:
        if n % t == 0:
            return t
    return n


def _matmul(a, w, out_dtype, tm=None, tn=None, name="matmul"):
    m, k = a.shape
    n = w.shape[1]
    tm = tm or _pick_tile(m, (1024, 512, 256, 128))
    tn = tn or (n if n <= 1536 else _pick_tile(n, (1024, 768, 512, 384, 256, 128)))
    return pl.pallas_call(
        _mm_kernel,
        out_shape=jax.ShapeDtypeStruct((m, n), out_dtype),
        grid=(m // tm, n // tn),
        in_specs=[pl.BlockSpec((tm, k), lambda i, j: (i, 0)),
                  pl.BlockSpec((k, tn), lambda i, j: (0, j))],
        out_specs=pl.BlockSpec((tm, tn), lambda i, j: (i, j)),
        compiler_params=_cp(("parallel", "parallel")),
        name=name,
    )(a, w)


def _mm_nt_kernel(w_ref, u_ref, o_ref):
    o_ref[0] = lax.dot_general(w_ref[...], u_ref[0], (((1,), (1,)), ((), ())),
                               preferred_element_type=F32).astype(o_ref.dtype)


def _matmul_nt(w_t, u, out_dtype, tok0, ntok, tn=256):
    c, k = w_t.shape
    b = u.shape[0]
    j0 = tok0 // tn
    return pl.pallas_call(
        _mm_nt_kernel,
        out_shape=jax.ShapeDtypeStruct((b, c, ntok), out_dtype),
        grid=(b, ntok // tn),
        in_specs=[pl.BlockSpec((c, k), lambda i, j: (0, 0)),
                  pl.BlockSpec((1, tn, k), lambda i, j: (i, j + j0, 0))],
        out_specs=pl.BlockSpec((1, c, tn), lambda i, j: (i, 0, j)),
        compiler_params=_cp(("parallel", "parallel")),
        name="matmul_nt",
    )(w_t, u)


LRU_CHUNK = 128


def _tile_scan(a, b, row, reverse):
    for s in (1, 2, 4):
        if reverse:
            keep = row < SUBLANES - s
            shift = SUBLANES - s
        else:
            keep = row >= s
            shift = s
        a_sh = pltpu.roll(a, shift, 0)
        b_sh = pltpu.roll(b, shift, 0)
        b = jnp.where(keep, a * b_sh, 0.0) + b
        a = jnp.where(keep, a * a_sh, a)
    return a, b


def _lru_kernel(pg_ref, px_ref, cw_ref, cb_ref, wg_ref, bg_ref, lam_ref, y_ref,
                xpad, a_f, b_f, a_b, b_b, *, s_len, ctx_len):
    ch = LRU_CHUNK
    n_chunks = s_len // ch
    zeros8 = jnp.zeros((SUBLANES, LANES), F32)
    xpad[0:SUBLANES, :] = zeros8
    xpad[ctx_len + SUBLANES:ctx_len + 2 * SUBLANES, :] = zeros8
    xpad[s_len + 2 * SUBLANES:s_len + 3 * SUBLANES, :] = zeros8

    def pad_row(r):
        return pl.multiple_of(r + jnp.where(r >= ctx_len, 2 * SUBLANES, SUBLANES), SUBLANES)

    def copy_body(i, carry):
        r = pl.multiple_of(i * ch, ch)
        xpad[pl.ds(pad_row(r), ch), :] = px_ref[0, pl.ds(r, ch), :]
        return carry

    lax.fori_loop(0, n_chunks, copy_body, 0)

    lam = lam_ref[...]
    sp = jnp.log(1.0 + jnp.exp(-lam))

    def gates_body(i, carry):
        r = pl.multiple_of(i * ch, ch)
        win = xpad[pl.ds(pad_row(r) - SUBLANES, ch + 2 * SUBLANES), :]
        xc = jnp.zeros((ch, LANES), F32) + cb_ref[...]
        for k in range(LRU_CONV):
            off = k - LRU_CONV // 2
            xc = xc + win[SUBLANES + off:SUBLANES + off + ch, :] * cw_ref[k:k + 1, :]
        gl = jnp.dot(xc.astype(BF16), wg_ref[0], preferred_element_type=F32) + bg_ref[0]
        for d, (a_s, b_s) in enumerate(((a_f, b_f), (a_b, b_b))):
            g_r = _sigmoid(gl[:, d * 2 * LANES:d * 2 * LANES + LANES])
            g_i = _sigmoid(gl[:, d * 2 * LANES + LANES:(d + 1) * 2 * LANES])
            log_a = (-LRU_C) * g_r * sp[d:d + 1, :]
            a = jnp.exp(log_a)
            bb = jnp.sqrt(1.0 - a * a) * g_i * xc
            a_s[pl.ds(r, ch), :] = a
            b_s[pl.ds(r, ch), :] = bb
        return carry

    lax.fori_loop(0, n_chunks, gates_body, 0)

    row = lax.broadcasted_iota(jnp.int32, (SUBLANES, LANES), 0)

    n_ctx_tiles = ctx_len // SUBLANES
    n_tiles = s_len // SUBLANES

    def scan_body(i, carry):
        h_f, h_b = carry
        r = pl.multiple_of(i * SUBLANES, SUBLANES)
        a, b = _tile_scan(a_f[pl.ds(r, SUBLANES), :], b_f[pl.ds(r, SUBLANES), :], row, False)
        hf = b + a * h_f
        b_f[pl.ds(r, SUBLANES), :] = hf
        t = jnp.where(i < n_ctx_tiles, n_ctx_tiles - 1 - i, n_tiles + n_ctx_tiles - 1 - i)
        rb = pl.multiple_of(t * SUBLANES, SUBLANES)
        a, b = _tile_scan(a_b[pl.ds(rb, SUBLANES), :], b_b[pl.ds(rb, SUBLANES), :], row, True)
        hb = b + a * h_b
        b_b[pl.ds(rb, SUBLANES), :] = hb
        return (jnp.broadcast_to(hf[SUBLANES - 1:SUBLANES, :], (SUBLANES, LANES)),
                jnp.broadcast_to(hb[0:1, :], (SUBLANES, LANES)))

    lax.fori_loop(0, n_tiles, scan_body, (zeros8, zeros8), unroll=2)

    def out_body(i, carry):
        r = pl.multiple_of(i * ch, ch)
        g = _gelu(pg_ref[0, pl.ds(r, ch), :])
        y = g * (b_f[pl.ds(r, ch), :] + b_b[pl.ds(r, ch), :])
        y_ref[0, pl.ds(r, ch), :] = y.astype(y_ref.dtype)
        return carry

    lax.fori_loop(0, n_chunks, out_body, 0)


def _lru_gate_weights(w_r, w_i, b_r, b_i):
    n_grp = LRU_WIDTH // LANES
    per = LANES // LRU_BLOCK

    def bd(w):
        w = w.reshape(n_grp, per, LRU_BLOCK, LRU_BLOCK)
        z = jnp.zeros((n_grp, LRU_BLOCK, LRU_BLOCK), w.dtype)
        top = jnp.concatenate([w[:, 0], z], axis=2)
        bot = jnp.concatenate([z, w[:, 1]], axis=2)
        return jnp.concatenate([top, bot], axis=1)

    wg = jnp.concatenate([bd(w_r[0]), bd(w_i[0]), bd(w_r[1]), bd(w_i[1])], axis=2).astype(BF16)
    bg = jnp.stack([b_r[0], b_i[0], b_r[1], b_i[1]], axis=0).reshape(4, n_grp, LANES)
    bg = jnp.transpose(bg, (1, 0, 2)).reshape(n_grp, 1, 4 * LANES)
    return wg, bg


def _lru_mixer(p_a, lp, ctx_len):
    b, s, _ = p_a.shape
    n_grp = LRU_WIDTH // LANES
    wg, bg = _lru_gate_weights(lp['lru_w_r'], lp['lru_w_i'], lp['lru_b_r'], lp['lru_b_i'])
    scr = pltpu.VMEM((s, LANES), F32)
    return pl.pallas_call(
        functools.partial(_lru_kernel, s_len=s, ctx_len=ctx_len),
        out_shape=jax.ShapeDtypeStruct((b, s, LRU_WIDTH), BF16),
        grid=(b, n_grp),
        in_specs=[pl.BlockSpec((1, s, LANES), lambda i, g: (i, 0, g)),
                  pl.BlockSpec((1, s, LANES), lambda i, g: (i, 0, n_grp + g)),
                  pl.BlockSpec((LRU_CONV, LANES), lambda i, g: (0, g)),
                  pl.BlockSpec((1, LANES), lambda i, g: (0, g)),
                  pl.BlockSpec((1, LANES, 4 * LANES), lambda i, g: (g, 0, 0)),
                  pl.BlockSpec((1, 1, 4 * LANES), lambda i, g: (g, 0, 0)),
                  pl.BlockSpec((2, LANES), lambda i, g: (0, g))],
        out_specs=pl.BlockSpec((1, s, LANES), lambda i, g: (i, 0, g)),
        scratch_shapes=[pltpu.VMEM((s + 3 * SUBLANES, LANES), F32), scr, scr, scr, scr],
        compiler_params=_cp(("parallel", "parallel")),
        name="rglru",
    )(p_a, p_a, lp['lru_conv_w'], lp['lru_conv_b'].reshape(1, LRU_WIDTH), wg, bg, lp['lru_lambda'])


S5_R = 4
S5_NSTATE = S5_GROUPS * S5_STATE


def _s5_kernel(xa_ref, xb_ref, winj_ref, wloc_ref, wro_ref, ap_ref, ya_ref, yb_ref, g_ref, *, reverse, n_ctx_tiles):
    n = S5_NSTATE
    nr = g_ref.shape[0]
    x = jnp.concatenate([h[0, pl.ds(i, nr, stride=S5_R), :] for i in range(S5_R) for h in (xa_ref, xb_ref)],
                        axis=1).astype(BF16)
    g_ref[...] = jnp.dot(x, winj_ref[...], preferred_element_type=F32)
    n_tiles = g_ref.shape[0] // SUBLANES
    row = lax.broadcasted_iota(jnp.int32, (SUBLANES, n), 0)
    zeros = jnp.zeros((SUBLANES, n), F32)
    if reverse:
        shift1, e_in, e_out = SUBLANES - 1, SUBLANES - 1, 0
    else:
        shift1, e_in, e_out = 1, 0, SUBLANES - 1

    def make_body(first_tile):
        def body(i, carry):
            hr, hi = carry
            t = (first_tile - i) if reverse else (first_tile + i)
            r = pl.multiple_of(t * SUBLANES, SUBLANES)
            br = g_ref[pl.ds(r, SUBLANES), 0:n]
            bi = g_ref[pl.ds(r, SUBLANES), n:2 * n]
            for k, s in enumerate((1, 2, 4)):
                ar = ap_ref[SUBLANES + k:SUBLANES + k + 1, 0:n]
                ai = ap_ref[SUBLANES + k:SUBLANES + k + 1, n:2 * n]
                if reverse:
                    keep = row < SUBLANES - s
                    shift = SUBLANES - s
                else:
                    keep = row >= s
                    shift = s
                brs = pltpu.roll(br, shift, 0)
                bis = pltpu.roll(bi, shift, 0)
                nr = ar * brs - ai * bis
                ni = ar * bis + ai * brs
                br = br + jnp.where(keep, nr, 0.0)
                bi = bi + jnp.where(keep, ni, 0.0)
            cr = ap_ref[0:SUBLANES, 0:n]
            ci = ap_ref[0:SUBLANES, n:2 * n]
            out_r = br + (cr * hr - ci * hi)
            out_i = bi + (cr * hi + ci * hr)
            g_ref[pl.ds(r, SUBLANES), 0:n] = jnp.where(row == e_in, hr, pltpu.roll(out_r, shift1, 0))
            g_ref[pl.ds(r, SUBLANES), n:2 * n] = jnp.where(row == e_in, hi, pltpu.roll(out_i, shift1, 0))
            return (jnp.broadcast_to(out_r[e_out:e_out + 1, :], (SUBLANES, n)),
                    jnp.broadcast_to(out_i[e_out:e_out + 1, :], (SUBLANES, n)))
        return body

    if reverse:
        carry = lax.fori_loop(0, n_ctx_tiles, make_body(n_ctx_tiles - 1), (zeros, zeros))
        lax.fori_loop(0, n_tiles - n_ctx_tiles, make_body(n_tiles - 1), carry)
    else:
        lax.fori_loop(0, n_tiles, make_body(0), (zeros, zeros))
    y = (jnp.dot(x, wloc_ref[...], preferred_element_type=F32)
         + jnp.dot(g_ref[...].astype(BF16), wro_ref[...], preferred_element_type=F32))
    for i in range(S5_R):
        for k, h in enumerate((ya_ref, yb_ref)):
            h[0, pl.ds(i, nr, stride=S5_R), :] = y[:, (2 * i + k) * LANES:(2 * i + k + 1) * LANES]


def _s5_params(a_re, a_im, log_dt, b_re, b_im, c_re, c_im, reverse):
    rr = S5_R
    dt = jnp.exp(log_dt)[:, None]
    den = a_re * a_re + a_im * a_im
    mag = jnp.exp(dt * a_re)
    ab_re = mag * jnp.cos(dt * a_im)
    ab_im = mag * jnp.sin(dt * a_im)
    f_re = ((ab_re - 1.0) * a_re + ab_im * a_im) / den
    f_im = (ab_im * a_re - (ab_re - 1.0) * a_im) / den
    bb_re = f_re[..., None] * b_re - f_im[..., None] * b_im
    bb_im = f_re[..., None] * b_im + f_im[..., None] * b_re
    grp_tok = (np.arange(rr * S5_WIDTH) // S5_GROUP) % S5_GROUPS
    grp_state = np.arange(S5_NSTATE) // S5_STATE

    def block_diag(t, lead, grp_rows, grp_cols):
        n_lead, minor, ncols = t.shape
        full = jnp.broadcast_to(t[:, None], (n_lead, S5_GROUPS, minor, ncols)).reshape(-1, ncols)
        return jnp.where(jnp.asarray(grp_rows[:, None] == grp_cols[None, :]), full, 0.0)

    def apow(k):
        k = k.astype(F32)[:, None, None]
        m = jnp.exp(k * dt[None] * a_re[None])
        return m * jnp.cos(k * dt[None] * a_im[None]), m * jnp.sin(k * dt[None] * a_im[None])

    steps = jnp.arange(rr)
    rows = rr * S5_WIDTH
    er, ei = apow(steps if reverse else (rr - 1 - steps))
    inj_re = er[..., None] * bb_re[None] - ei[..., None] * bb_im[None]
    inj_im = er[..., None] * bb_im[None] + ei[..., None] * bb_re[None]
    def inj_map(t):
        t = jnp.transpose(t, (0, 3, 1, 2)).reshape(rr, S5_GROUP, S5_NSTATE)
        return block_diag(t, rr, grp_tok, grp_state)

    winj = jnp.concatenate([inj_map(inj_re), inj_map(inj_im)], axis=1)
    fr, fi = apow((rr - steps) if reverse else (steps + 1))
    ro_re = c_re[None] * fr[:, :, None, :] - c_im[None] * fi[:, :, None, :]
    ro_im = c_re[None] * fi[:, :, None, :] + c_im[None] * fr[:, :, None, :]
    def ro_map(t):
        t = jnp.transpose(t, (3, 0, 1, 2)).reshape(1, S5_STATE, rows)
        return block_diag(t, 1, grp_state, grp_tok)

    wro = jnp.concatenate([ro_map(ro_re), -ro_map(ro_im)], axis=0)
    kr, ki = apow(steps)
    ab_r = kr[..., None] * bb_re[None] - ki[..., None] * bb_im[None]
    ab_i = kr[..., None] * bb_im[None] + ki[..., None] * bb_re[None]
    kk = jnp.einsum('gop,kgpc->kgoc', c_re, ab_r) - jnp.einsum('gop,kgpc->kgoc', c_im, ab_i)
    src = jnp.arange(rr)[:, None]
    tgt = jnp.arange(rr)[None, :]
    lag = (src - tgt) if reverse else (tgt - src)
    kmat = jnp.where((lag >= 0)[:, :, None, None, None], kk[jnp.clip(lag, 0, rr - 1)], 0.0)
    wloc = block_diag(jnp.transpose(kmat, (0, 4, 1, 2, 3)).reshape(rr, S5_GROUP, rows), rr, grp_tok, grp_tok)
    i8 = jnp.arange(SUBLANES)
    dist = (SUBLANES - i8) if reverse else (i8 + 1)
    ks = jnp.concatenate([dist, jnp.array([1, 2, 4]), jnp.zeros((5,), dist.dtype)]) * rr
    pr, pi = apow(ks)
    ap = jnp.concatenate([pr.reshape(16, S5_NSTATE), pi.reshape(16, S5_NSTATE)], axis=1)
    return winj.astype(BF16), wloc.astype(BF16), wro.astype(BF16), ap


def _s5_scan(p_d, lp, d, ctx_len):
    b, s, w = p_d.shape
    nr = s // S5_R
    wr = S5_R * w
    reverse = d == 1
    n_ctx_tiles = ctx_len // (S5_R * SUBLANES)
    winj, wloc, wro, ap = _s5_params(lp['s5_a_re'][d], lp['s5_a_im'][d], lp['s5_log_dt'][d], lp['s5_b_re'][d],
                                     lp['s5_b_im'][d], lp['s5_c_re'][d], lp['s5_c_im'][d], reverse)
    assert w == 2 * LANES
    full = lambda shape: pl.BlockSpec(shape, lambda i: (0, 0))
    half = jax.ShapeDtypeStruct((b, s, LANES), F32)
    return pl.pallas_call(
        functools.partial(_s5_kernel, reverse=reverse, n_ctx_tiles=n_ctx_tiles),
        out_shape=(half, half),
        grid=(b,),
        in_specs=[pl.BlockSpec((1, s, LANES), lambda i: (i, 0, 0)), pl.BlockSpec((1, s, LANES), lambda i: (i, 0, 1)),
                  full((wr, 2 * S5_NSTATE)), full((wr, wr)), full((2 * S5_NSTATE, wr)), full((16, 2 * S5_NSTATE))],
        out_specs=(pl.BlockSpec((1, s, LANES), lambda i: (i, 0, 0)), pl.BlockSpec((1, s, LANES), lambda i: (i, 0, 0))),
        scratch_shapes=[pltpu.VMEM((nr, 2 * S5_NSTATE), F32)],
        compiler_params=_cp(("parallel",)),
        name="s5_scan_bwd" if reverse else "s5_scan_fwd",
    )(p_d, p_d, winj, wloc, wro, ap)


def _s5_out_kernel(yfa_ref, yfb_ref, yba_ref, ybb_ref, u_ref, d_ref, w_ref, b_ref, o_ref):
    y = (jnp.concatenate([yfa_ref[...] + yba_ref[...], yfb_ref[...] + ybb_ref[...]], axis=1)
         + d_ref[...] * u_ref[...])
    g = _gelu(y)
    z = jnp.dot(g.astype(BF16), w_ref[...], preferred_element_type=F32) + b_ref[...]
    o_ref[...] = (g * _sigmoid(z)).astype(o_ref.dtype)


def _s5_mixer(p_d, lp, ctx_len):
    b, s, w = p_d.shape
    yfa, yfb = _s5_scan(p_d, lp, 0, ctx_len)
    yba, ybb = _s5_scan(p_d, lp, 1, ctx_len)
    m = b * s
    tm = _pick_tile(m, (1024, 512, 256))
    row = pl.BlockSpec((tm, w), lambda i: (i, 0))
    hrow = pl.BlockSpec((tm, LANES), lambda i: (i, 0))
    vec = pl.BlockSpec((1, w), lambda i: (0, 0))
    flat = lambda a: a.reshape(m, LANES)
    out = pl.pallas_call(
        _s5_out_kernel,
        out_shape=jax.ShapeDtypeStruct((m, w), BF16),
        grid=(m // tm,),
        in_specs=[hrow, hrow, hrow, hrow, row, vec, pl.BlockSpec((w, w), lambda i: (0, 0)), vec],
        out_specs=row,
        compiler_params=_cp(("parallel",)),
        name="s5_out",
    )(flat(yfa), flat(yfb), flat(yba), flat(ybb), p_d.reshape(m, w), lp['s5_d'].reshape(1, w),
      lp['s5_w_glu'].astype(BF16), lp['s5_b_glu'].reshape(1, w))
    return out.reshape(b, s, w)


NA_QROWS = 4


def _natten_plan(rows):
    kr = min(NA_WIN_R, rows)
    span = kr + NA_QROWS - 1
    variants, index, blk_var = [], {}, []
    for blk in range(rows // NA_QROWS):
        r0 = blk * NA_QROWS
        ws = int(np.clip(r0 - kr // 2, 0, rows - span))
        dr = np.zeros((NA_QROWS, span), np.int32)
        ok = np.zeros((NA_QROWS, span), bool)
        for q in range(NA_QROWS):
            r = r0 + q
            rs = int(np.clip(r - kr // 2, 0, rows - kr))
            for i in range(span):
                ok[q, i] = rs <= ws + i < rs + kr
                dr[q, i] = (ws + i - r + (NA_WIN_R - 1)) if ok[q, i] else 0
        key = dr.tobytes() + ok.tobytes()
        if key not in index:
            index[key] = len(variants)
            variants.append((dr, ok))
        blk_var.append(index[key])
    return (np.stack([v[0] for v in variants]), np.stack([v[1] for v in variants]),
            np.asarray(blk_var, np.int32))


def _natten_bias(rpb, dr, ok):
    nv, nq, span = dr.shape
    n_dr, n_dc = 2 * NA_WIN_R - 1, 2 * NA_WIN_C - 1
    w = np.arange(GRID_W)
    cs = np.clip(w - NA_WIN_C // 2, 0, GRID_W - NA_WIN_C)
    ok_col = (w[None, :] >= cs[:, None]) & (w[None, :] < cs[:, None] + NA_WIN_C)
    dc = w[None, :] - w[:, None] + (NA_WIN_C - 1)
    e_dc = (dc[None] == np.arange(n_dc)[:, None, None]).astype(np.float32)
    e_dr = ((dr[..., None] == np.arange(n_dr)) & ok[..., None]).astype(np.float32)
    g = jnp.einsum('vqir,hrk,kwc->vhqwic', e_dr, rpb, e_dc, precision=HIGHEST)
    ok_all = ok[:, None, :, None, :, None] & ok_col[None, None, None, :, None, :]
    g = jnp.where(jnp.asarray(ok_all), g, MASK_VALUE)
    return g.reshape(nv, NA_HEADS, nq * GRID_W, span * GRID_W).astype(F32)


def _attend(q2, keys, vals, biases, lane):
    nq = q2.shape[0]
    sels = [(lane >= hh * NA_HEAD_DIM) & (lane < (hh + 1) * NA_HEAD_DIM) for hh in range(2)]
    qs = jnp.concatenate([jnp.where(sel, q2, jnp.zeros_like(q2)) for sel in sels], axis=0)
    ss = []
    for k_i, b0, b1 in zip(keys, biases[0], biases[1]):
        s_i = lax.dot_general(qs, k_i, (((1,), (1,)), ((), ())), preferred_element_type=F32)
        if b0 is not None:
            s_i = jnp.concatenate([s_i[:nq] + b0, s_i[nq:] + b1], axis=0)
        ss.append(s_i)
    m = ss[0].max(axis=-1, keepdims=True)
    for s_i in ss[1:]:
        m = jnp.maximum(m, s_i.max(axis=-1, keepdims=True))
    ps = [jnp.exp(s_i - m) for s_i in ss]
    den = ps[0].sum(axis=-1, keepdims=True)
    for p_i in ps[1:]:
        den = den + p_i.sum(axis=-1, keepdims=True)
    o = jnp.dot(ps[0].astype(BF16), vals[0], preferred_element_type=F32)
    for p_i, v_i in zip(ps[1:], vals[1:]):
        o = o + jnp.dot(p_i.astype(BF16), v_i, preferred_element_type=F32)
    o = o / den
    return jnp.where(sels[0], o[:nq], o[nq:])


def _natten_kernel(var_ref, q_ref, k_ref, v_ref, bias_ref, o_ref, *, rows, ctx_len):
    kr = min(NA_WIN_R, rows)
    span = kr + NA_QROWS - 1
    r0 = pl.program_id(1) * NA_QROWS
    ws = jnp.clip(r0 - kr // 2, 0, rows - span)
    base = pl.multiple_of(ctx_len + ws * GRID_W, GRID_W)
    lane = lax.broadcasted_iota(jnp.int32, (NA_QROWS * GRID_W, LANES), 1)
    scale = NA_HEAD_DIM ** -0.5
    for hp in range(NA_HEADS // 2):
        ls = slice(hp * LANES, (hp + 1) * LANES)
        q2 = q_ref[0, :, ls] * scale
        kw = k_ref[0, pl.ds(base, span * GRID_W), ls]
        vw = v_ref[0, pl.ds(base, span * GRID_W), ls]
        kc = k_ref[0, 0:ctx_len, ls]
        vc = v_ref[0, 0:ctx_len, ls]
        biases = [[bias_ref[0, 2 * hp + hh], None] for hh in range(2)]
        out = _attend(q2, [kw, kc], [vw, vc], biases, lane)
        o_ref[0, :, ls] = out.astype(o_ref.dtype)


def _ctx_attn_kernel(q_ref, k_ref, v_ref, o_ref, *, ctx_len):
    lane = lax.broadcasted_iota(jnp.int32, (ctx_len, LANES), 1)
    scale = NA_HEAD_DIM ** -0.5
    for hp in range(NA_HEADS // 2):
        ls = slice(hp * LANES, (hp + 1) * LANES)
        out = _attend(q_ref[0, :, ls] * scale, [k_ref[0, :, ls]], [v_ref[0, :, ls]], [[None], [None]], lane)
        o_ref[0, :, ls] = out.astype(o_ref.dtype)


def _natten_mixer(p_c, rpb, ctx_len, with_ctx):
    b, s, _ = p_c.shape
    l = s - ctx_len
    rows = l // GRID_W
    kr = min(NA_WIN_R, rows)
    span = kr + NA_QROWS - 1
    nq = NA_QROWS * GRID_W
    dr, ok, blk_var = _natten_plan(rows)
    bias = _natten_bias(rpb, dr, ok)
    cb = ctx_len // nq
    grid_spec = pltpu.PrefetchScalarGridSpec(
        num_scalar_prefetch=1,
        grid=(b, rows // NA_QROWS),
        in_specs=[pl.BlockSpec((1, nq, NA_WIDTH), lambda i, r, var: (i, cb + r, 0)),
                  pl.BlockSpec((1, s, NA_WIDTH), lambda i, r, var: (i, 0, 1)),
                  pl.BlockSpec((1, s, NA_WIDTH), lambda i, r, var: (i, 0, 2)),
                  pl.BlockSpec((1, NA_HEADS, nq, span * GRID_W), lambda i, r, var: (var[r], 0, 0, 0))],
        out_specs=pl.BlockSpec((1, nq, NA_WIDTH), lambda i, r, var: (i, r, 0)),
    )
    y_l = pl.pallas_call(
        functools.partial(_natten_kernel, rows=rows, ctx_len=ctx_len),
        out_shape=jax.ShapeDtypeStruct((b, l, NA_WIDTH), BF16),
        grid_spec=grid_spec,
        compiler_params=_cp(("parallel", "arbitrary")),
        name="natten",
    )(jnp.asarray(blk_var), p_c, p_c, p_c, bias)
    if not with_ctx:
        return y_l, None
    y_c = pl.pallas_call(
        functools.partial(_ctx_attn_kernel, ctx_len=ctx_len),
        out_shape=jax.ShapeDtypeStruct((b, ctx_len, NA_WIDTH), BF16),
        grid=(b,),
        in_specs=[pl.BlockSpec((1, ctx_len, NA_WIDTH), lambda i: (i, 0, 0)),
                  pl.BlockSpec((1, ctx_len, NA_WIDTH), lambda i: (i, 0, 1)),
                  pl.BlockSpec((1, ctx_len, NA_WIDTH), lambda i: (i, 0, 2))],
        out_specs=pl.BlockSpec((1, ctx_len, NA_WIDTH), lambda i: (i, 0, 0)),
        compiler_params=_cp(("parallel",)),
        name="ctx_attn",
    )(p_c, p_c, p_c)
    return y_l, y_c


DFT_ROWS = 64


def _dft_gen_kernel(ca_ref, sa_ref, cb_ref, sb_ref, fwd_ref, inv_ref, *, l):
    i = pl.program_id(0)
    ca = ca_ref[0]
    sa = sa_ref[0]
    cb = cb_ref[...]
    sb = sb_ref[...]
    gc = ca * cb - sa * sb
    gs = sa * cb + ca * sb
    x = i * DFT_ROWS + lax.broadcasted_iota(jnp.int32, (DFT_ROWS, l), 0)
    y = lax.broadcasted_iota(jnp.int32, (DFT_ROWS, l), 1)
    n = 2.0 * l
    nyq_x = jnp.where((x & 1) == 0, 1.0, -1.0)
    fwd_ref[:, 0:l] = gc.astype(fwd_ref.dtype)
    fwd_ref[:, l:2 * l] = jnp.where(y == 0, nyq_x, -gs).astype(fwd_ref.dtype)
    scale = jnp.where(x == 0, 1.0 / n, 2.0 / n)
    nyq_y = jnp.where((y & 1) == 0, 1.0, -1.0)
    inv_ref[0] = (scale * gc).astype(inv_ref.dtype)
    inv_ref[1] = (scale * jnp.where(x == 0, nyq_y, -gs)).astype(inv_ref.dtype)


def _dft_matrices(l):
    n = 2 * l
    k1 = l // DFT_ROWS
    y = np.arange(l, dtype=np.int64)
    xa = (DFT_ROWS * np.arange(k1, dtype=np.int64))[:, None]
    xb = np.arange(DFT_ROWS, dtype=np.int64)[:, None]
    ang_a = jnp.asarray(((xa * y[None, :]) % n).astype(np.float32)) * F32(2.0 * math.pi / n)
    ang_b = jnp.asarray(((xb * y[None, :]) % n).astype(np.float32)) * F32(2.0 * math.pi / n)
    ca, sa = jnp.cos(ang_a).reshape(k1, 1, l), jnp.sin(ang_a).reshape(k1, 1, l)
    cb, sb = jnp.cos(ang_b), jnp.sin(ang_b)
    row = pl.BlockSpec((1, 1, l), lambda i: (i, 0, 0))
    tab = pl.BlockSpec((DFT_ROWS, l), lambda i: (0, 0))
    fwd, inv = pl.pallas_call(
        functools.partial(_dft_gen_kernel, l=l),
        out_shape=(jax.ShapeDtypeStruct((l, 2 * l), BF16), jax.ShapeDtypeStruct((2, l, l), BF16)),
        grid=(k1,),
        in_specs=[row, row, tab, tab],
        out_specs=(pl.BlockSpec((DFT_ROWS, 2 * l), lambda i: (i, 0)),
                   pl.BlockSpec((2, DFT_ROWS, l), lambda i: (0, i, 0))),
        compiler_params=_cp(("parallel",)),
        name="dft_gen",
    )(ca, sa, cb, sb)
    return fwd, inv.reshape(2 * l, l)


def _hy_filter_kernel(w1t_ref, w1c_ref, w1s_ref, b1_ref, w2_ref, b2_ref, w3_ref, b3_ref, fr_ref, dl_ref,
                      h_ref, asum_ref, *, l, tl):
    j = pl.program_id(0)
    t = (j * tl + lax.broadcasted_iota(jnp.int32, (1, tl), 1)).astype(F32)
    t_norm = t / l
    bands = (1 + lax.broadcasted_iota(jnp.int32, (HY_BANDS, 1), 0)).astype(F32)
    ang = (2.0 * math.pi / l) * t * bands
    fr = fr_ref[...]
    lin = (w1t_ref[...] * t_norm
           + jnp.dot(w1c_ref[...], jnp.cos(ang), preferred_element_type=F32, precision=HIGHEST)
           + jnp.dot(w1s_ref[...], jnp.sin(ang), preferred_element_type=F32, precision=HIGHEST))
    h = jnp.sin(fr * (lin + b1_ref[...]))
    h = jnp.sin(fr * (jnp.dot(w2_ref[...], h, preferred_element_type=F32, precision=HIGHEST) + b2_ref[...]))
    h = jnp.dot(w3_ref[...], h, preferred_element_type=F32, precision=HIGHEST) + b3_ref[...]
    window = jnp.exp(-t_norm * dl_ref[...])
    first = (j * tl + lax.broadcasted_iota(jnp.int32, (HY_WIDTH, tl), 1)) == 0

    @pl.when(j == 0)
    def _():
        asum_ref[...] = jnp.zeros_like(asum_ref)

    for blk in range(2 * HY_ORDER):
        rs = slice(blk * HY_WIDTH, (blk + 1) * HY_WIDTH)
        hb = h[rs, :] * window
        if blk >= HY_ORDER:
            hb = jnp.where(first, 0.0, hb)
        h_ref[rs, :] = hb
        asum_ref[rs, :] += jnp.sum(jnp.abs(hb), axis=1, keepdims=True)


def _hy_filters(lp, l):
    tl = min(l, 512)
    hid = HY_FILT_HID
    w1 = lp['hy_w1']
    col = lambda v: v.reshape(-1, 1)
    deltas = np.abs(np.linspace(HY_MIN_DECAY, HY_MAX_DECAY, HY_WIDTH, dtype=np.float32)).reshape(-1, 1)
    full = lambda shape: pl.BlockSpec(shape, lambda j: (0, 0))
    n_out = 2 * HY_ORDER * HY_WIDTH
    return pl.pallas_call(
        functools.partial(_hy_filter_kernel, l=l, tl=tl),
        out_shape=(jax.ShapeDtypeStruct((n_out, l), F32), jax.ShapeDtypeStruct((n_out, 1), F32)),
        grid=(l // tl,),
        in_specs=[full((hid, 1)), full((hid, HY_BANDS)), full((hid, HY_BANDS)), full((hid, 1)),
                  full((hid, hid)), full((hid, 1)), full((n_out, hid)), full((n_out, 1)),
                  full((hid, 1)), full((HY_WIDTH, 1))],
        out_specs=(pl.BlockSpec((n_out, tl), lambda j: (0, j)), pl.BlockSpec((n_out, 1), lambda j: (0, 0))),
        compiler_params=_cp(("arbitrary",)),
        name="hyena_filters",
    )(w1[0:1].T, w1[1:1 + HY_BANDS].T, w1[1 + HY_BANDS:].T, col(lp['hy_b1']), lp['hy_w2'].T, col(lp['hy_b2']),
      lp['hy_w3'].T, col(lp['hy_b3']), col(lp['hy_freq']), jnp.asarray(deltas))


def _hy_short_conv_kernel(x_ref, w_ref, b_ref, z_ref, zbf_ref, *, l):
    x = x_ref[0]
    t = lax.broadcasted_iota(jnp.int32, x.shape, 1)
    left = HY_SHORT // 2
    z = jnp.zeros(x.shape, F32) + b_ref[...]
    for k in range(HY_SHORT):
        off = k - left
        if off == 0:
            sh = x
        else:
            sh = pltpu.roll(x, (-off) % l, 1)
            sh = jnp.where((t + off >= 0) & (t + off < l), sh, 0.0)
        z = z + sh * w_ref[:, k:k + 1]
    z_ref[0] = z
    zbf_ref[0] = z.astype(zbf_ref.dtype)


def _hy_conv_kernel(y_ref, fc_ref, fs_ref, ic_ref, is_ref, kfc_ref, kfs_ref, kbc_ref, kbs_ref, n_ref,
                    o_ref, *, nb):
    j = pl.program_id(1)

    @pl.when(j == 0)
    def _():
        o_ref[...] = jnp.zeros_like(o_ref)

    y = y_ref[...]
    zr = jnp.dot(y, fc_ref[...], preferred_element_type=F32)
    zi = jnp.dot(y, fs_ref[...], preferred_element_type=F32)
    inv_n = 1.0 / (n_ref[...] + 1e-6)
    kr = (kfc_ref[...] + kbc_ref[...]) * inv_n
    ki = (kfs_ref[...] - kbs_ref[...]) * inv_n
    tn = kr.shape[1]
    f0 = (j * tn + lax.broadcasted_iota(jnp.int32, kr.shape, 1)) == 0
    ki = jnp.where(f0, (kfs_ref[...] + kbs_ref[...]) * inv_n, ki)
    prs, pis = [], []
    for bb in range(nb):
        rs = slice(bb * HY_WIDTH, (bb + 1) * HY_WIDTH)
        a, b = zr[rs], zi[rs]
        prs.append(a * kr - jnp.where(f0, 0.0, b * ki))
        pis.append(jnp.where(f0, b * ki, a * ki + b * kr))
    pr = jnp.concatenate(prs, axis=0).astype(BF16)
    pi = jnp.concatenate(pis, axis=0).astype(BF16)
    o_ref[...] += (jnp.dot(pr, ic_ref[...], preferred_element_type=F32)
                   + jnp.dot(pi, is_ref[...], preferred_element_type=F32))


def _hy_long_conv(ybf, n_tiles, nb, row_stride, fwd, inv, kf, asum, order, l):
    tmh = nb * HY_WIDTH
    m = n_tiles * tmh
    tn = min(l, 256)
    jn = l // tn
    o_f = order
    o_b = HY_ORDER + order
    return pl.pallas_call(
        functools.partial(_hy_conv_kernel, nb=nb),
        out_shape=jax.ShapeDtypeStruct((m, l), F32),
        grid=(n_tiles, jn),
        in_specs=[pl.BlockSpec((tmh, l), lambda i, j: (i * row_stride, 0)),
                  pl.BlockSpec((l, tn), lambda i, j: (0, j)),
                  pl.BlockSpec((l, tn), lambda i, j: (0, jn + j)),
                  pl.BlockSpec((tn, l), lambda i, j: (j, 0)),
                  pl.BlockSpec((tn, l), lambda i, j: (jn + j, 0)),
                  pl.BlockSpec((HY_WIDTH, tn), lambda i, j: (o_f, j)),
                  pl.BlockSpec((HY_WIDTH, tn), lambda i, j: (o_f, jn + j)),
                  pl.BlockSpec((HY_WIDTH, tn), lambda i, j: (o_b, j)),
                  pl.BlockSpec((HY_WIDTH, tn), lambda i, j: (o_b, jn + j)),
                  pl.BlockSpec((HY_WIDTH, 1), lambda i, j: (order, 0))],
        out_specs=pl.BlockSpec((tmh, l), lambda i, j: (i, 0)),
        compiler_params=_cp(("parallel", "arbitrary")),
        name="hyena_long_conv",
    )(ybf, fwd, fwd, inv, inv, kf, kf, kf, kf, asum)


def _hy_gate_kernel(g_ref, c_ref, y_ref, bias_ref, o_ref, obf_ref):
    o = g_ref[0] * (c_ref[0] + y_ref[0] * bias_ref[...])
    o_ref[0] = o
    obf_ref[0] = o.astype(obf_ref.dtype)


def _hy_gate(z, conv, y, bias_col, gate_blk, y_blk):
    b, _, l = z.shape
    tl = min(l, 1024)
    spec = lambda blk: pl.BlockSpec((1, HY_WIDTH, tl), lambda i, j: (i, blk, j))
    return pl.pallas_call(
        _hy_gate_kernel,
        out_shape=(jax.ShapeDtypeStruct((b, HY_WIDTH, l), F32), jax.ShapeDtypeStruct((b, HY_WIDTH, l), BF16)),
        grid=(b, l // tl),
        in_specs=[spec(gate_blk), spec(0), spec(y_blk), pl.BlockSpec((HY_WIDTH, 1), lambda i, j: (0, 0))],
        out_specs=(spec(0), spec(0)),
        compiler_params=_cp(("parallel", "parallel")),
        name="hyena_gate",
    )(z, conv, y, bias_col)


def _hyena_sequence(p_bt, lp, dft):
    b, c3, l = p_bt.shape
    fwd, inv = dft
    h, asum = _hy_filters(lp, l)
    kf = _matmul(h.astype(BF16), fwd, F32, name="hyena_filter_dft")
    asum2 = asum.reshape(2, HY_ORDER * HY_WIDTH).sum(axis=0).reshape(HY_ORDER * HY_WIDTH, 1)
    blk = pl.BlockSpec((1, LANES, l), lambda i, g: (i, g, 0))
    z, z_bf = pl.pallas_call(
        functools.partial(_hy_short_conv_kernel, l=l),
        out_shape=(jax.ShapeDtypeStruct((b, c3, l), F32), jax.ShapeDtypeStruct((b, c3, l), BF16)),
        grid=(b, c3 // LANES),
        in_specs=[blk,
                  pl.BlockSpec((LANES, HY_SHORT), lambda i, g: (g, 0)),
                  pl.BlockSpec((LANES, 1), lambda i, g: (g, 0))],
        out_specs=(blk, blk),
        compiler_params=_cp(("parallel", "parallel")),
        name="hyena_short_conv",
    )(p_bt, lp['hy_conv_w'].T, lp['hy_conv_b'].reshape(c3, 1))
    n_blk = c3 // HY_WIDTH
    nb2 = 2 if b % 2 == 0 else 1
    conv1 = _hy_long_conv(z_bf.reshape(b * c3, l), b, 1, n_blk, fwd, inv, kf, asum2, 0, l).reshape(b, HY_WIDTH, l)
    y1, y1_bf = _hy_gate(z, conv1, z, lp['hy_bias'][0].reshape(HY_WIDTH, 1), 1, 0)
    conv2 = _hy_long_conv(y1_bf.reshape(b * HY_WIDTH, l), b // nb2, nb2, 1, fwd, inv, kf, asum2, 1, l)
    _, y2_bf = _hy_gate(z, conv2.reshape(b, HY_WIDTH, l), y1, lp['hy_bias'][1].reshape(HY_WIDTH, 1), 2, 0)
    return y2_bf


FFT_N2 = 128
FFT_KB = 4
FFT_NB = 8
FFT_PITCH = 136


def _fft_stage1(xa_ref, xb_ref, f1_ref, xpa, xpb, za, zb, n_in, n1):
    for blk in range(n_in):
        xpa[blk * FFT_PITCH:blk * FFT_PITCH + FFT_N2, :] = xa_ref[0, blk * FFT_N2:(blk + 1) * FFT_N2, :]
        xpb[blk * FFT_PITCH:blk * FFT_PITCH + FFT_N2, :] = xb_ref[0, blk * FFT_N2:(blk + 1) * FFT_N2, :]
    f1 = f1_ref[...]
    for c0 in range(0, FFT_N2, FFT_NB):
        cols = []
        for q in range(FFT_NB):
            cols.append(xpa[pl.ds(c0 + q, n_in, stride=FFT_PITCH), :])
            cols.append(xpb[pl.ds(c0 + q, n_in, stride=FFT_PITCH), :])
        z = jnp.dot(f1, jnp.concatenate(cols, axis=1).astype(BF16), preferred_element_type=F32)
        for q in range(FFT_NB):
            za[pl.ds(c0 + q, 2 * n1, stride=FFT_PITCH), :] = z[:, (2 * q) * LANES:(2 * q + 1) * LANES]
            zb[pl.ds(c0 + q, 2 * n1, stride=FFT_PITCH), :] = z[:, (2 * q + 1) * LANES:(2 * q + 2) * LANES]


def _fft_load_z(za, zb, k1, n1):
    r_re = pl.multiple_of(k1 * FFT_PITCH, SUBLANES)
    r_im = pl.multiple_of((n1 + k1) * FFT_PITCH, SUBLANES)
    z = jnp.concatenate(
        [jnp.concatenate([za[pl.ds(r_re, FFT_N2), :], zb[pl.ds(r_re, FFT_N2), :]], axis=1),
         jnp.concatenate([za[pl.ds(r_im, FFT_N2), :], zb[pl.ds(r_im, FFT_N2), :]], axis=1)], axis=0)
    return z.astype(BF16), r_re, r_im


def _hy_fft_filter_kernel(xa_ref, xb_ref, f1_ref, ef_ref, n_ref, y_ref, xpa, xpb, za, zb, *, n1):
    j = pl.program_id(1)

    @pl.when(j == 0)
    def _():
        _fft_stage1(xa_ref, xb_ref, f1_ref, xpa, xpb, za, zb, n1, n1)

    inv_n = 1.0 / (n_ref[0] + 1e-6)
    for q in range(FFT_KB):
        z, _, _ = _fft_load_z(za, zb, j * FFT_KB + q, n1)
        y = jnp.dot(ef_ref[q], z, preferred_element_type=F32) * inv_n
        y_ref[0, 0, q] = y[:FFT_N2]
        y_ref[0, 1, q] = y[FFT_N2:]


def _hy_fft_conv_kernel(xa_ref, xb_ref, ga_ref, gb_ref, bias_ref, f1_ref, f1i_ref, ef_ref, ei_ref, kf_ref,
                        oa_ref, ob_ref, xpa, xpb, za, zb, *, n1):
    n_in = n1 // 2
    j = pl.program_id(1)

    @pl.when(j == 0)
    def _():
        _fft_stage1(xa_ref, xb_ref, f1_ref, xpa, xpb, za, zb, n_in, n1)

    for q in range(FFT_KB):
        z, r_re, r_im = _fft_load_z(za, zb, j * FFT_KB + q, n1)
        y = jnp.dot(ef_ref[q], z, preferred_element_type=F32)
        kr = kf_ref[0, 0, q]
        ki = kf_ref[0, 1, q]
        yr, yi = y[:FFT_N2], y[FFT_N2:]
        p = jnp.concatenate([yr * kr - yi * ki, yr * ki + yi * kr], axis=0).astype(BF16)
        u = jnp.dot(ei_ref[q], p, preferred_element_type=F32)
        za[pl.ds(r_re, FFT_N2), :] = u[:FFT_N2, :LANES]
        zb[pl.ds(r_re, FFT_N2), :] = u[:FFT_N2, LANES:]
        za[pl.ds(r_im, FFT_N2), :] = u[FFT_N2:, :LANES]
        zb[pl.ds(r_im, FFT_N2), :] = u[FFT_N2:, LANES:]

    @pl.when(j == pl.num_programs(1) - 1)
    def _():
        f1i = f1i_ref[...]
        bias = bias_ref[0]
        for c0 in range(0, FFT_N2, FFT_NB):
            cols = []
            for q in range(FFT_NB):
                cols.append(za[pl.ds(c0 + q, 2 * n1, stride=FFT_PITCH), :])
                cols.append(zb[pl.ds(c0 + q, 2 * n1, stride=FFT_PITCH), :])
            yv = jnp.dot(f1i, jnp.concatenate(cols, axis=1).astype(BF16), preferred_element_type=F32)
            for q in range(FFT_NB):
                za[pl.ds(c0 + q, n_in, stride=FFT_PITCH), :] = yv[:, (2 * q) * LANES:(2 * q + 1) * LANES]
                zb[pl.ds(c0 + q, n_in, stride=FFT_PITCH), :] = yv[:, (2 * q + 1) * LANES:(2 * q + 2) * LANES]
        for blk in range(n_in):
            rows = slice(blk * FFT_N2, (blk + 1) * FFT_N2)
            prow = slice(blk * FFT_PITCH, blk * FFT_PITCH + FFT_N2)
            oa_ref[0, rows, :] = ga_ref[0, rows, :] * (za[prow, :] + xa_ref[0, rows, :] * bias[:, :LANES])
            ob_ref[0, rows, :] = gb_ref[0, rows, :] * (zb[prow, :] + xb_ref[0, rows, :] * bias[:, LANES:])


def _fft_tables(l):
    n = 2 * l
    n1 = n // FFT_N2
    n_in = n1 // 2
    k1 = np.arange(n1)
    phi = 2.0 * np.pi * np.outer(k1, np.arange(n1)) / n1
    f1_full = np.concatenate([np.cos(phi), -np.sin(phi)], axis=0).astype(np.float32)
    f1 = f1_full[:, :n_in]
    f1i = (np.concatenate([np.cos(phi[:, :n_in]), -np.sin(phi[:, :n_in])], axis=0).T / n).astype(np.float32)
    n2 = np.arange(FFT_N2)
    alpha = 2.0 * np.pi * np.outer(k1, n2) / n
    beta = 2.0 * np.pi * np.outer(np.arange(FFT_N2), n2) / FFT_N2
    ca, sa = jnp.asarray(np.cos(alpha), F32)[:, None, :], jnp.asarray(np.sin(alpha), F32)[:, None, :]
    cb, sb = jnp.asarray(np.cos(beta), F32)[None], jnp.asarray(np.sin(beta), F32)[None]
    er = ca * cb - sa * sb
    ei = -(sa * cb + ca * sb)
    ef = jnp.concatenate([jnp.concatenate([er, -ei], axis=2), jnp.concatenate([ei, er], axis=2)], axis=1)
    ert, eit = jnp.swapaxes(er, 1, 2), jnp.swapaxes(ei, 1, 2)
    einv = jnp.concatenate([jnp.concatenate([ert, eit], axis=2), jnp.concatenate([-eit, ert], axis=2)], axis=1)
    return {'n1': n1, 'f1': jnp.asarray(f1, BF16), 'f1_full': jnp.asarray(f1_full, BF16), 'f1i': jnp.asarray(f1i, BF16),
            'ef': ef.astype(BF16), 'ei': einv.astype(BF16)}


def _hy_filter_tm_kernel(w1t_ref, w1c_ref, w1s_ref, b1_ref, w2_ref, b2_ref, w3_ref, b3_ref, fr_ref, dl_ref,
                         k_ref, asum_ref, *, l, tl):
    j = pl.program_id(0)
    n = j * tl + lax.broadcasted_iota(jnp.int32, (1, tl), 1)
    t = jnp.where(n < l, n, 2 * l - n).astype(F32)
    t_norm = t / l
    bands = (1 + lax.broadcasted_iota(jnp.int32, (HY_BANDS, 1), 0)).astype(F32)
    ang = (2.0 * math.pi / l) * t * bands
    fr = fr_ref[...]
    lin = (w1t_ref[...] * t_norm
           + jnp.dot(w1c_ref[...], jnp.cos(ang), preferred_element_type=F32, precision=HIGHEST)
           + jnp.dot(w1s_ref[...], jnp.sin(ang), preferred_element_type=F32, precision=HIGHEST))
    h = jnp.sin(fr * (lin + b1_ref[...]))
    h = jnp.sin(fr * (jnp.dot(w2_ref[...], h, preferred_element_type=F32, precision=HIGHEST) + b2_ref[...]))
    h = jnp.dot(w3_ref[...], h, preferred_element_type=F32, precision=HIGHEST) + b3_ref[...]
    window = jnp.exp(-t_norm * dl_ref[...])

    @pl.when(j == 0)
    def _():
        asum_ref[...] = jnp.zeros_like(asum_ref)

    for o in range(HY_ORDER):
        hf = h[o * HY_WIDTH:(o + 1) * HY_WIDTH, :]
        hb = h[(HY_ORDER + o) * HY_WIDTH:(HY_ORDER + o + 1) * HY_WIDTH, :]
        k = jnp.where(n > l, hb, hf) * window
        k = jnp.where(n == l, 0.0, k).T
        k_ref[o] = k
        asum_ref[o] += jnp.sum(jnp.abs(k), axis=0, keepdims=True)


def _hyena_latent(p_b, lp, tabs, ctx_len):
    b, s, c3 = p_b.shape
    l = s - ctx_len
    n = 2 * l
    n1 = tabs['n1']
    n_in = n1 // 2
    hid = HY_FILT_HID
    w = HY_WIDTH
    full = lambda shape: pl.BlockSpec(shape, lambda *idx: (0,) * len(shape))
    tl = 512
    w1 = lp['hy_w1']
    col = lambda v: v.reshape(-1, 1)
    deltas = np.abs(np.linspace(HY_MIN_DECAY, HY_MAX_DECAY, w, dtype=np.float32)).reshape(-1, 1)
    n_out = 2 * HY_ORDER * w
    k_tm, asum = pl.pallas_call(
        functools.partial(_hy_filter_tm_kernel, l=l, tl=tl),
        out_shape=(jax.ShapeDtypeStruct((HY_ORDER, n, w), F32), jax.ShapeDtypeStruct((HY_ORDER, 1, w), F32)),
        grid=(n // tl,),
        in_specs=[full((hid, 1)), full((hid, HY_BANDS)), full((hid, HY_BANDS)), full((hid, 1)), full((hid, hid)),
                  full((hid, 1)), full((n_out, hid)), full((n_out, 1)), full((hid, 1)), full((w, 1))],
        out_specs=(pl.BlockSpec((HY_ORDER, tl, w), lambda j: (0, j, 0)),
                   pl.BlockSpec((HY_ORDER, 1, w), lambda j: (0, 0, 0))),
        compiler_params=_cp(("arbitrary",)),
        name="hyena_filters_tm",
    )(w1[0:1].T, w1[1:1 + HY_BANDS].T, w1[1 + HY_BANDS:].T, col(lp['hy_b1']), lp['hy_w2'].T, col(lp['hy_b2']),
      lp['hy_w3'].T, col(lp['hy_b3']), col(lp['hy_freq']), jnp.asarray(deltas))
    nj = n1 // FFT_KB
    scr = pltpu.VMEM((2 * n1 * FFT_PITCH, LANES), F32)
    xp_full = pltpu.VMEM((n1 * FFT_PITCH, LANES), F32)
    xp_half = pltpu.VMEM((n_in * FFT_PITCH, LANES), F32)
    half = lambda blk: pl.BlockSpec((1, n, LANES), lambda i, j: (i, 0, blk))
    kf = pl.pallas_call(
        functools.partial(_hy_fft_filter_kernel, n1=n1),
        out_shape=jax.ShapeDtypeStruct((HY_ORDER, 2, n1, FFT_N2, w), F32),
        grid=(HY_ORDER, nj),
        in_specs=[half(0), half(1), full((2 * n1, n1)),
                  pl.BlockSpec((FFT_KB, 2 * FFT_N2, 2 * FFT_N2), lambda i, j: (j, 0, 0)),
                  pl.BlockSpec((1, 1, w), lambda i, j: (i, 0, 0))],
        out_specs=pl.BlockSpec((1, 2, FFT_KB, FFT_N2, w), lambda i, j: (i, 0, j, 0, 0)),
        scratch_shapes=[xp_full, xp_full, scr, scr],
        compiler_params=_cp(("parallel", "arbitrary")),
        name="hyena_filter_fft",
    )(k_tm, k_tm, tabs['f1_full'], tabs['ef'], asum)
    z = pl.pallas_call(
        functools.partial(_hy_short_conv_tm_kernel, l=l, ctx_len=ctx_len),
        out_shape=jax.ShapeDtypeStruct((b, l, c3), F32),
        grid=(b, c3 // LANES),
        in_specs=[pl.BlockSpec((1, s, LANES), lambda i, g: (i, 0, g)),
                  pl.BlockSpec((HY_SHORT, LANES), lambda i, g: (0, g)),
                  pl.BlockSpec((1, LANES), lambda i, g: (0, g))],
        out_specs=pl.BlockSpec((1, l, LANES), lambda i, g: (i, 0, g)),
        scratch_shapes=[pltpu.VMEM((l + 2 * SUBLANES, LANES), F32)],
        compiler_params=_cp(("parallel", "parallel")),
        name="hyena_short_conv_tm",
    )(p_b, lp['hy_conv_w'], lp['hy_conv_b'].reshape(1, c3))

    def conv(xa, xb, xblk, gate_blk, order):
        lane = lambda arr, blk: pl.BlockSpec((1, l, LANES), lambda i, j: (i, 0, blk))
        half_out = jax.ShapeDtypeStruct((b, l, LANES), F32)
        xb_blk = xblk + 1 if xa is xb else xblk
        return pl.pallas_call(
            functools.partial(_hy_fft_conv_kernel, n1=n1),
            out_shape=(half_out, half_out),
            grid=(b, nj),
            in_specs=[lane(xa, xblk), lane(xb, xb_blk), lane(z, gate_blk), lane(z, gate_blk + 1),
                      pl.BlockSpec((1, 1, w), lambda i, j: (order, 0, 0)),
                      full((2 * n1, n_in)), full((n_in, 2 * n1)),
                      pl.BlockSpec((FFT_KB, 2 * FFT_N2, 2 * FFT_N2), lambda i, j: (j, 0, 0)),
                      pl.BlockSpec((FFT_KB, 2 * FFT_N2, 2 * FFT_N2), lambda i, j: (j, 0, 0)),
                      pl.BlockSpec((1, 2, FFT_KB, FFT_N2, w), lambda i, j: (order, 0, j, 0, 0))],
            out_specs=(pl.BlockSpec((1, l, LANES), lambda i, j: (i, 0, 0)),
                       pl.BlockSpec((1, l, LANES), lambda i, j: (i, 0, 0))),
            scratch_shapes=[xp_half, xp_half, scr, scr],
            compiler_params=_cp(("parallel", "arbitrary")),
            name="hyena_fft_conv",
        )(xa, xb, z, z, lp['hy_bias'].reshape(HY_ORDER, 1, w), tabs['f1'], tabs['f1i'], tabs['ef'], tabs['ei'], kf)

    y1a, y1b = conv(z, z, 0, 2, 0)
    return conv(y1a, y1b, 0, 4, 1)


def _hy_short_conv_tm_kernel(x_ref, w_ref, b_ref, z_ref, xpad, *, l, ctx_len):
    ch = LANES
    zeros8 = jnp.zeros((SUBLANES, LANES), F32)
    xpad[0:SUBLANES, :] = zeros8
    xpad[l + SUBLANES:l + 2 * SUBLANES, :] = zeros8

    def copy_body(i, carry):
        r = pl.multiple_of(i * ch, ch)
        xpad[pl.ds(r + SUBLANES, ch), :] = x_ref[0, pl.ds(r + ctx_len, ch), :]
        return carry

    lax.fori_loop(0, l // ch, copy_body, 0)

    def conv_body(i, carry):
        r = pl.multiple_of(i * ch, ch)
        win = xpad[pl.ds(r, ch + 2 * SUBLANES), :]
        acc = jnp.zeros((ch, LANES), F32) + b_ref[...]
        for k in range(HY_SHORT):
            off = k - HY_SHORT // 2
            acc = acc + win[SUBLANES + off:SUBLANES + off + ch, :] * w_ref[k:k + 1, :]
        z_ref[0, pl.ds(r, ch), :] = acc
        return carry

    lax.fori_loop(0, l // ch, conv_body, 0)


def _top2_route(logits):
    lane = lax.broadcasted_iota(jnp.int32, logits.shape, 1)
    lg = jnp.where(lane < N_EXPERTS, logits, -jnp.inf)
    v1 = lg.max(axis=-1, keepdims=True)
    i1 = jnp.min(jnp.where(lg == v1, lane, LANES), axis=-1, keepdims=True)
    lg2 = jnp.where(lane == i1, -jnp.inf, lg)
    v2 = lg2.max(axis=-1, keepdims=True)
    i2 = jnp.min(jnp.where(lg2 == v2, lane, LANES), axis=-1, keepdims=True)
    e2 = jnp.exp(v2 - v1)
    w1 = 1.0 / (1.0 + e2)
    w2 = e2 / (1.0 + e2)
    out = jnp.where(lane == 0, i1.astype(F32), 0.0)
    out = jnp.where(lane == 1, i2.astype(F32), out)
    out = jnp.where(lane == 2, w1, out)
    return jnp.where(lane == 3, w2, out)


def _merge_kernel(ya_ref, yba_ref, ybb_ref, ybc_ref, ycl_ref, ycc_ref, yd_ref, pg_ref, xc_ref, x_ref, mod_ref, wa_ref,
                  wb_ref, wc_ref, wd_ref, wo_ref, lng_ref, lnb_ref, *rest, alpha, first_tile, with_router, u_tiles):
    if with_router:
        wr_ref, x_out_ref, u_out_ref, route_ref = rest
    else:
        x_out_ref, u_out_ref = rest
    j = pl.program_id(1) + first_tile
    d = x_ref.shape[-1]
    yb = jnp.concatenate([yba_ref[0], ybb_ref[0]], axis=1)
    yc = ycl_ref[0]
    if first_tile == 0:
        yb = jnp.where(j == 0, ybc_ref[0].astype(F32).T, yb)
        yc = jnp.where(j == 0, ycc_ref[0], yc)
    yb = yb.astype(BF16)
    projs = [jnp.dot(ya_ref[0], wa_ref[...], preferred_element_type=F32),
             jnp.dot(yb, wb_ref[...], preferred_element_type=F32),
             jnp.dot(yc, wc_ref[...], preferred_element_type=F32),
             jnp.dot(yd_ref[0], wd_ref[...], preferred_element_type=F32)]
    merged = None
    for i, pr in enumerate(projs):
        term = _sigmoid(pg_ref[0, :, i * d:(i + 1) * d]).astype(F32) * pr
        merged = term if merged is None else merged + term
    m = jnp.dot(merged.astype(BF16), wo_ref[...], preferred_element_type=F32)
    mod = mod_ref[0, 0]
    x_res = x_ref[0]
    if first_tile == 0:
        x_res = jnp.where(j == 0, xc_ref[0], x_res)
    xn = _layer_norm_rows(alpha * x_res + mod[2:3, :] * m) * lng_ref[...] + lnb_ref[...]
    x_out_ref[0] = xn
    u = _layer_norm_rows(xn) * (1.0 + mod[4:5, :]) + mod[3:4, :]
    if u_tiles:
        for sub in range(d // LANES):
            u_out_ref[0, :, sub, :] = u[:, sub * LANES:(sub + 1) * LANES]
    else:
        u_out_ref[0] = u.astype(u_out_ref.dtype)
    if with_router:
        u_hi = u.astype(BF16)
        u_lo = (u - u_hi.astype(F32)).astype(BF16)
        wr = wr_ref[...]
        w_hi = wr.astype(BF16)
        w_lo = (wr - w_hi.astype(F32)).astype(BF16)
        logits = (jnp.dot(u_hi, w_hi, preferred_element_type=F32) + jnp.dot(u_lo, w_hi, preferred_element_type=F32)
                  + jnp.dot(u_hi, w_lo, preferred_element_type=F32))
        route_ref[0] = _top2_route(logits)


def _merge(ya, yb_lat_a, yb_lat_b, yb_ctx, yc_lat, yc_ctx, yd, pg, resid, mod_sel, lp, ln_g, ln_b, alpha, ctx_len,
           with_ctx, u_dtype, w_router=None, u_tiles=False):
    if isinstance(resid, tuple):
        x_ctx, x_lat = resid
        lat_tile = lambda j, first: jnp.maximum(j + first - 1, 0)
    else:
        x_ctx = x_lat = resid
        lat_tile = lambda j, first: j + first
    b, d = x_lat.shape[0], x_lat.shape[2]
    s = ya.shape[1]
    tm = ctx_len
    first = 0 if with_ctx else 1
    nt = s // tm - first
    tok = lambda w: pl.BlockSpec((1, tm, w), lambda i, j: (i, j + first, 0))
    lat = lambda j: jnp.maximum(j + first - 1, 0)
    out = lambda w: pl.BlockSpec((1, tm, w), lambda i, j: (i, j, 0))
    full = lambda shape: pl.BlockSpec(shape, lambda i, j: (0,) * len(shape))
    in_specs = [tok(LRU_WIDTH),
                pl.BlockSpec((1, tm, LANES), lambda i, j: (i, lat(j), 0)),
                pl.BlockSpec((1, tm, LANES), lambda i, j: (i, lat(j), 0)),
                pl.BlockSpec((1, HY_WIDTH, tm), lambda i, j: (i, 0, 0)),
                pl.BlockSpec((1, tm, NA_WIDTH), lambda i, j: (i, lat(j), 0)),
                pl.BlockSpec((1, tm, NA_WIDTH), lambda i, j: (i, 0, 0)),
                tok(S5_WIDTH), tok(N_BRANCH * d),
                pl.BlockSpec((1, tm, d), lambda i, j: (i, 0, 0)),
                pl.BlockSpec((1, tm, d), lambda i, j: (i, lat_tile(j, first), 0)),
                pl.BlockSpec((1, 1, 6, d), lambda i, j: (i, jnp.minimum(j + first, 1), 0, 0)),
                full((LRU_WIDTH, d)), full((HY_WIDTH, d)), full((NA_WIDTH, d)), full((S5_WIDTH, d)),
                full((d, d)), full((1, d)), full((1, d))]
    args = [ya, yb_lat_a, yb_lat_b, yb_ctx, yc_lat, yc_ctx, yd, pg, x_ctx, x_lat, mod_sel,
            lp['w_br_a'].astype(BF16), lp['w_br_b'].astype(BF16), lp['w_br_c'].astype(BF16),
            lp['w_br_d'].astype(BF16), lp['w_out'].astype(BF16), ln_g.reshape(1, d), ln_b.reshape(1, d)]
    if u_tiles:
        u_shape = jax.ShapeDtypeStruct((b, nt * tm, d // LANES, LANES), u_dtype)
        u_spec = pl.BlockSpec((1, tm, d // LANES, LANES), lambda i, j: (i, j, 0, 0))
    else:
        u_shape = jax.ShapeDtypeStruct((b, nt * tm, d), u_dtype)
        u_spec = out(d)
    out_shape = [jax.ShapeDtypeStruct((b, nt * tm, d), F32), u_shape]
    out_specs = [out(d), u_spec]
    if w_router is not None:
        in_specs.append(full((d, LANES)))
        args.append(jnp.zeros((d, LANES), F32).at[:, :N_EXPERTS].set(w_router))
        out_shape.append(jax.ShapeDtypeStruct((b, nt * tm, LANES), F32))
        out_specs.append(out(LANES))
    return pl.pallas_call(
        functools.partial(_merge_kernel, alpha=alpha, first_tile=first, with_router=w_router is not None,
                          u_tiles=u_tiles),
        out_shape=tuple(out_shape),
        grid=(b, nt),
        in_specs=in_specs,
        out_specs=tuple(out_specs),
        compiler_params=_cp(("parallel", "parallel")),
        name="merge",
    )(*args)


FFN_TF = 1408
FFN_TILES_PER_SEQ = 8


def _dense_ffn_kernel(u_ref, x_ref, mod_ref, modn_ref, wg_ref, wu_ref, wd_ref, lng_ref, lnb_ref,
                      xs_ref, un_ref, acc_ref, *, alpha, ctx_len, tiles_per_seq):
    i = pl.program_id(0)
    j = pl.program_id(1)

    @pl.when(j == 0)
    def _():
        acc_ref[...] = jnp.zeros_like(acc_ref)

    u = u_ref[...]
    g = jnp.dot(u, wg_ref[...], preferred_element_type=F32)
    v = jnp.dot(u, wu_ref[...], preferred_element_type=F32)
    h = (g * _sigmoid(g)) * v
    acc_ref[...] += jnp.dot(h.astype(BF16), wd_ref[...], preferred_element_type=F32)

    @pl.when(j == pl.num_programs(1) - 1)
    def _():
        row = lax.broadcasted_iota(jnp.int32, (acc_ref.shape[0], 1), 0)
        is_ctx = (row < ctx_len) & ((i % tiles_per_seq) == 0)
        pick = lambda m, k: jnp.where(is_ctx, m[0, 0, k:k + 1, :], m[0, 1, k:k + 1, :])
        x2 = (_layer_norm_rows(alpha * x_ref[...] + pick(mod_ref, 5) * acc_ref[...]) * lng_ref[...]
              + lnb_ref[...])
        xs_ref[...] = x2
        un_ref[...] = (_layer_norm_rows(x2) * (1.0 + pick(modn_ref, 1)) + pick(modn_ref, 0)).astype(un_ref.dtype)


def _dense_ffn(u2, x1, mod_sel, mod_next, w_gate, w_up, w_down, ln_g, ln_b, alpha, ctx_len):
    b, s, d = x1.shape
    ff = w_gate.shape[1]
    tiles_per_seq = FFN_TILES_PER_SEQ
    tm = s // tiles_per_seq
    assert tm % 16 == 0 and tm >= ctx_len
    tf = FFN_TF
    rows = pl.BlockSpec((tm, d), lambda i, j: (i, 0))
    modspec = pl.BlockSpec((1, 2, 6, d), lambda i, j: (i // tiles_per_seq, 0, 0, 0))
    vec = pl.BlockSpec((1, d), lambda i, j: (0, 0))
    xs, un = pl.pallas_call(
        functools.partial(_dense_ffn_kernel, alpha=alpha, ctx_len=ctx_len, tiles_per_seq=tiles_per_seq),
        out_shape=(jax.ShapeDtypeStruct((b * s, d), F32), jax.ShapeDtypeStruct((b * s, d), BF16)),
        grid=(b * tiles_per_seq, ff // tf),
        in_specs=[rows, rows, modspec, modspec,
                  pl.BlockSpec((d, tf), lambda i, j: (0, j)), pl.BlockSpec((d, tf), lambda i, j: (0, j)),
                  pl.BlockSpec((tf, d), lambda i, j: (j, 0)), vec, vec],
        out_specs=(rows, rows),
        scratch_shapes=[pltpu.VMEM((tm, d), F32)],
        compiler_params=_cp(("parallel", "arbitrary")),
        name="dense_swiglu_ln",
    )(u2.reshape(b * s, d), x1.reshape(b * s, d), mod_sel, mod_next, w_gate, w_up, w_down,
      ln_g.reshape(1, d), ln_b.reshape(1, d))
    return xs.reshape(b, s, d), un.reshape(b, s, d)


MOE_TM = 512
MOE_TF = 896


def _moe_kernel(te_ref, nu_ref, dst_ref, u_hbm, wg_ref, wu_ref, wd_ref, out_hbm,
                xbuf, xbf, acc2, acc, gsem, ssem, *, n_rows):
    tm = MOE_TM
    n_sub = acc.shape[2]
    n_tok = n_rows // TOP_K
    i = pl.program_id(0)
    j = pl.program_id(1)
    nt = pl.num_programs(0)
    nj = pl.num_programs(1)
    n_used = nu_ref[0]
    slot = i % 2

    def gather_start(tile, sl):
        base = tile * tm

        def body(r, carry):
            p = dst_ref[base + r]
            row = jnp.where(p >= n_tok, p - n_tok, jnp.maximum(p, 0))
            pltpu.make_async_copy(u_hbm.at[pl.ds(row, 1)], xbuf.at[sl, pl.ds(r, 1)], gsem.at[sl]).start()
            return carry

        lax.fori_loop(0, tm, body, 0, unroll=8)

    def gather_wait(sl):
        pltpu.make_async_copy(u_hbm.at[pl.ds(0, tm)], xbuf.at[sl], gsem.at[sl]).wait()

    def scatter_wait():
        pltpu.make_async_copy(acc.at[0], out_hbm.at[pl.ds(0, tm)], ssem.at[0]).wait()

    used = i < n_used

    @pl.when(used & (j == 0))
    def _():
        @pl.when(i == 0)
        def _():
            gather_start(0, 0)
            acc[1] = jnp.zeros(acc.shape[1:], F32)
            dump = pltpu.make_async_copy(acc.at[1], out_hbm.at[pl.ds(n_rows, tm)], ssem.at[0])
            dump.start()
            dump.wait()

        gather_wait(slot)

        @pl.when(i + 1 < n_used)
        def _():
            gather_start(i + 1, 1 - slot)

        xbf[...] = jnp.concatenate([xbuf[slot, :, sub, :] for sub in range(n_sub)], axis=1).astype(BF16)
        acc2[...] = jnp.zeros_like(acc2)

    @pl.when(used)
    def _():
        x = xbf[...]
        g = jnp.dot(x, wg_ref[0], preferred_element_type=F32)
        u = jnp.dot(x, wu_ref[0], preferred_element_type=F32)
        h = (g * _sigmoid(g)) * u
        acc2[...] += jnp.dot(h.astype(BF16), wd_ref[0].astype(BF16), preferred_element_type=F32)

    @pl.when(used & (j == nj - 1))
    def _():
        @pl.when(i > 0)
        def _():
            scatter_wait()

        for sub in range(n_sub):
            acc[slot, :, sub, :] = acc2[:, sub * LANES:(sub + 1) * LANES]
        base = i * tm

        def body(r, carry):
            p = dst_ref[base + r]
            row = jnp.where(p >= 0, p, n_rows + r)
            pltpu.make_async_copy(acc.at[slot, pl.ds(r, 1)], out_hbm.at[pl.ds(row, 1)], ssem.at[0]).start()
            return carry

        lax.fori_loop(0, tm, body, 0, unroll=8)

    @pl.when((i == nt - 1) & (j == nj - 1))
    def _():
        scatter_wait()


def _moe_experts(u_rows, dst, tile_expert, n_used, w_gate, w_up, w_down):
    r, n_sub, _ = u_rows.shape
    d = n_sub * LANES
    ff = w_gate.shape[2]
    tm, tf = MOE_TM, MOE_TF
    nt = dst.shape[0] // tm
    nj = ff // tf
    n_rows = TOP_K * r

    def jmap(i, j, nu):
        return jnp.where(i < nu[0], j, nj - 1)

    grid_spec = pltpu.PrefetchScalarGridSpec(
        num_scalar_prefetch=3,
        grid=(nt, nj),
        in_specs=[pl.BlockSpec(memory_space=pl.ANY),
                  pl.BlockSpec((1, d, tf), lambda i, j, te, nu, ds: (te[i], 0, jmap(i, j, nu))),
                  pl.BlockSpec((1, d, tf), lambda i, j, te, nu, ds: (te[i], 0, jmap(i, j, nu))),
                  pl.BlockSpec((1, tf, d), lambda i, j, te, nu, ds: (te[i], jmap(i, j, nu), 0))],
        out_specs=pl.BlockSpec(memory_space=pl.ANY),
        scratch_shapes=[pltpu.VMEM((2, tm, n_sub, LANES), F32), pltpu.VMEM((tm, d), BF16), pltpu.VMEM((tm, d), F32),
                        pltpu.VMEM((2, tm, n_sub, LANES), F32),
                        pltpu.SemaphoreType.DMA((2,)), pltpu.SemaphoreType.DMA((1,))],
    )
    return pl.pallas_call(
        functools.partial(_moe_kernel, n_rows=n_rows),
        out_shape=jax.ShapeDtypeStruct((n_rows + tm, n_sub, LANES), F32),
        grid_spec=grid_spec,
        compiler_params=_cp(("arbitrary", "arbitrary")),
        name="moe_experts",
    )(tile_expert, n_used, dst, u_rows, w_gate, w_up, w_down)


def _moe_combine_kernel(x_ref, y1_ref, y2_ref, w_ref, mod_ref, lng_ref, lnb_ref, o_ref, *, alpha):
    mod = mod_ref[0, 0]
    w = w_ref[0]
    untile = lambda r: jnp.concatenate([r[:, sub, :] for sub in range(r.shape[1])], axis=1)
    f = w[:, 2:3] * untile(y1_ref) + w[:, 3:4] * untile(y2_ref)
    o_ref[0] = _layer_norm_rows(alpha * x_ref[0] + mod[5:6, :] * f) * lng_ref[...] + lnb_ref[...]


def _moe_ffn(x_lat, u_lat, route, mod_sel, lp_moe, ln_g, ln_b, alpha):
    b, l, d = x_lat.shape
    n_sub = d // LANES
    t = b * l
    tm = MOE_TM
    ids = route[..., 0:TOP_K].astype(jnp.int32).reshape(t * TOP_K)
    onehot = (ids[:, None] == jnp.arange(N_EXPERTS)[None, :]).astype(jnp.int32)
    csum = jnp.cumsum(onehot, axis=0)
    rank = jnp.take_along_axis(csum, ids[:, None], axis=1)[:, 0] - 1
    counts = csum[-1]
    padded = ((counts + tm - 1) // tm) * tm
    ends = jnp.cumsum(padded)
    starts = ends - padded
    slot = starts[ids] + rank
    n_slots = t * TOP_K + N_EXPERTS * tm
    nt = n_slots // tm
    pair = jnp.arange(t * TOP_K, dtype=jnp.int32)
    dst = jnp.full((n_slots,), -1, jnp.int32).at[slot].set((pair % TOP_K) * t + pair // TOP_K)
    tile_start = jnp.arange(nt, dtype=jnp.int32) * tm
    tile_expert = jnp.minimum(jnp.sum(tile_start[:, None] >= ends[None, :], axis=1), N_EXPERTS - 1).astype(jnp.int32)
    n_used = (ends[-1] // tm).astype(jnp.int32).reshape(1)
    last_e = tile_expert[jnp.maximum(n_used[0] - 1, 0)]
    tile_expert = jnp.where(jnp.arange(nt) < n_used[0], tile_expert, last_e)
    y = _moe_experts(u_lat.reshape(t, n_sub, LANES), dst, tile_expert, n_used, lp_moe['w_gate'], lp_moe['w_up'],
                     lp_moe['w_down'])
    tmc = 256
    nl = l // tmc
    vec = pl.BlockSpec((1, d), lambda i, j: (0, 0))
    return pl.pallas_call(
        functools.partial(_moe_combine_kernel, alpha=alpha),
        out_shape=jax.ShapeDtypeStruct((b, l, d), F32),
        grid=(b, nl),
        in_specs=[pl.BlockSpec((1, tmc, d), lambda i, j: (i, j, 0)),
                  pl.BlockSpec((tmc, n_sub, LANES), lambda i, j: (i * nl + j, 0, 0)),
                  pl.BlockSpec((tmc, n_sub, LANES), lambda i, j: (b * nl + i * nl + j, 0, 0)),
                  pl.BlockSpec((1, tmc, LANES), lambda i, j: (i, j, 0)),
                  pl.BlockSpec((1, 1, 6, d), lambda i, j: (i, 1, 0, 0)), vec, vec],
        out_specs=pl.BlockSpec((1, tmc, d), lambda i, j: (i, j, 0)),
        compiler_params=_cp(("parallel", "parallel")),
        name="moe_combine_ln",
    )(x_lat, y, y, route, mod_sel, ln_g.reshape(1, d), ln_b.reshape(1, d))


def kernel(x, c, ctx, c_ctx, w_mod, b_mod, w_in, lru_conv_w, lru_conv_b, lru_w_r, lru_b_r, lru_w_i, lru_b_i, lru_lambda, hy_conv_w, hy_conv_b, hy_w1, hy_b1, hy_w2, hy_b2, hy_w3, hy_b3, hy_freq, hy_bias, na_rpb, s5_a_re, s5_a_im, s5_log_dt, s5_b_re, s5_b_im, s5_c_re, s5_c_im, s5_d, s5_w_glu, s5_b_glu, w_br_a, w_br_b, w_br_c, w_br_d, w_out, ln1_g, ln1_b, ln2_g, ln2_b, ff_w_gate, ff_w_up, ff_w_down, moe_router, moe_w_gate, moe_w_up, moe_w_down):
    bsz, l, d = x.shape
    ctx_len = ctx.shape[1]
    depth = w_in.shape[0]
    s = ctx_len + l
    alpha = (2.0 * depth) ** 0.25
    xs = (ctx, x)
    c_rows = jnp.zeros((SUBLANES, d), F32).at[0:bsz].set(c).at[bsz].set(c_ctx)
    fft_lat = _fft_tables(l)
    dft_ctx = _dft_matrices(ctx_len) if depth > 1 else None
    assert depth == 2
    mod_all = _mod_vectors(c_rows, w_mod, b_mod).reshape(depth, SUBLANES, 6, d)
    mods = [jnp.stack([jnp.broadcast_to(mod_all[li, bsz], (bsz, 6, d)), mod_all[li, 0:bsz]], axis=1)
            for li in range(depth)]
    mods.append(mods[-1])
    u1 = _ln_mod(ctx, x, mods[0], 0, 1)

    for li in range(depth):
        with_ctx = li < depth - 1
        lp = {
            'lru_conv_w': lru_conv_w[li], 'lru_conv_b': lru_conv_b[li], 'lru_w_r': lru_w_r[li],
            'lru_b_r': lru_b_r[li], 'lru_w_i': lru_w_i[li], 'lru_b_i': lru_b_i[li], 'lru_lambda': lru_lambda[li],
            'hy_conv_w': hy_conv_w[li], 'hy_conv_b': hy_conv_b[li], 'hy_w1': hy_w1[li], 'hy_b1': hy_b1[li],
            'hy_w2': hy_w2[li], 'hy_b2': hy_b2[li], 'hy_w3': hy_w3[li], 'hy_b3': hy_b3[li],
            'hy_freq': hy_freq[li], 'hy_bias': hy_bias[li],
            's5_a_re': s5_a_re[li], 's5_a_im': s5_a_im[li], 's5_log_dt': s5_log_dt[li], 's5_b_re': s5_b_re[li],
            's5_b_im': s5_b_im[li], 's5_c_re': s5_c_re[li], 's5_c_im': s5_c_im[li], 's5_d': s5_d[li],
            's5_w_glu': s5_w_glu[li], 's5_b_glu': s5_b_glu[li], 'w_br_a': w_br_a[li], 'w_br_b': w_br_b[li],
            'w_br_c': w_br_c[li], 'w_br_d': w_br_d[li], 'w_out': w_out[li],
        }
        mod_sel = mods[li]

        u1f = u1.reshape(bsz * s, d)
        wi = w_in[li].astype(BF16)
        p_a = _matmul(u1f, wi[:, OFF_A:OFF_B], F32, name="proj_lru").reshape(bsz, s, OFF_B - OFF_A)
        p_c = _matmul(u1f, wi[:, OFF_C:OFF_D], BF16, name="proj_natten").reshape(bsz, s, OFF_D - OFF_C)
        p_d = _matmul(u1f, wi[:, OFF_D:OFF_G], F32, name="proj_s5").reshape(bsz, s, OFF_G - OFF_D)
        p_g = _matmul(u1f, wi[:, OFF_G:], BF16, name="proj_gates").reshape(bsz, s, N_BRANCH * d)
        p_b = _matmul(u1f, wi[:, OFF_B:OFF_C], F32, name="proj_hyena").reshape(bsz, s, OFF_C - OFF_B)

        ya = _lru_mixer(p_a, lp, ctx_len)
        yb_a, yb_b = _hyena_latent(p_b, lp, fft_lat, ctx_len)
        if with_ctx:
            yb_ctx = _hyena_sequence(_matmul_nt(wi[:, OFF_B:OFF_C].T, u1, F32, 0, ctx_len), lp, dft_ctx)
        else:
            yb_ctx = jnp.zeros((bsz, HY_WIDTH, ctx_len), BF16)
        yc_l, yc_c = _natten_mixer(p_c, na_rpb[li], ctx_len, with_ctx)
        if yc_c is None:
            yc_c = jnp.zeros((bsz, ctx_len, NA_WIDTH), BF16)
        yd = _s5_mixer(p_d, lp, ctx_len)
        e = li // 2
        if li % 2 == 0:
            assert with_ctx
            x1, u2 = _merge(ya, yb_a, yb_b, yb_ctx, yc_l, yc_c, yd, p_g, xs, mod_sel, lp, ln1_g[li], ln1_b[li],
                            alpha, ctx_len, True, BF16)
            xs, u1 = _dense_ffn(u2, x1, mod_sel, mods[li + 1], ff_w_gate[e].astype(BF16), ff_w_up[e].astype(BF16),
                                ff_w_down[e].astype(BF16), ln2_g[li], ln2_b[li], alpha, ctx_len)
        else:
            assert not with_ctx
            x1, u2, route = _merge(ya, yb_a, yb_b, yb_ctx, yc_l, yc_c, yd, p_g, xs, mod_sel, lp, ln1_g[li],
                                   ln1_b[li], alpha, ctx_len, False, F32, moe_router[e], u_tiles=True)
            lp_moe = {'w_gate': moe_w_gate[e].astype(BF16), 'w_up': moe_w_up[e].astype(BF16),
                      'w_down': moe_w_down[e]}
            return _moe_ffn(x1, u2, route, mod_sel, lp_moe, ln2_g[li], ln2_b[li], alpha)
```

```python
import functools
import math

import numpy as np
import jax
import jax.numpy as jnp
from jax import lax
from jax.experimental import pallas as pl
from jax.experimental.pallas import tpu as pltpu

F32 = jnp.float32
BF16 = jnp.bfloat16
HIGHEST = lax.Precision.HIGHEST

LRU_WIDTH = 384
LRU_BLOCK = 64
LRU_CONV = 4
LRU_C = 8.0
HY_WIDTH = 256
HY_ORDER = 2
HY_SHORT = 3
HY_BANDS = 16
HY_FILT_HID = 64
HY_MAX_DECAY = math.log(1e-2) / 0.3
HY_MIN_DECAY = math.log(1e-2) / 1.5
NA_HEADS = 6
NA_HEAD_DIM = 64
NA_WIDTH = NA_HEADS * NA_HEAD_DIM
NA_WIN_R = 8
NA_WIN_C = 16
GRID_W = 64
S5_WIDTH = 256
S5_GROUP = 16
S5_GROUPS = 16
S5_STATE = 64
N_BRANCH = 4
OFF_A = 0
OFF_B = OFF_A + 2 * LRU_WIDTH
OFF_C = OFF_B + 3 * HY_WIDTH
OFF_D = OFF_C + 3 * NA_WIDTH
OFF_G = OFF_D + S5_WIDTH
N_EXPERTS = 8
TOP_K = 2
LN_EPS = 1e-5
MASK_VALUE = -1e30

LANES = 128
SUBLANES = 8
VMEM_LIMIT = 56 * 1024 * 1024


def _cp(sem, vmem=VMEM_LIMIT):
    return pltpu.CompilerParams(dimension_semantics=sem, vmem_limit_bytes=vmem)


def _gelu(x):
    return 0.5 * x * (1.0 + jnp.tanh(math.sqrt(2.0 / math.pi) * (x + 0.044715 * (x * x * x))))


def _sigmoid(x):
    return 0.5 + 0.5 * jnp.tanh(0.5 * x)


def _layer_norm_rows(x):
    mu = jnp.mean(x, axis=-1, keepdims=True)
    xc = x - mu
    var = jnp.mean(xc * xc, axis=-1, keepdims=True)
    return xc * lax.rsqrt(var + LN_EPS)


def _mod_kernel(c_ref, w_ref, b_ref, o_ref):
    c = c_ref[...]
    a = c * _sigmoid(c)
    o_ref[0] = jnp.dot(a, w_ref[0], preferred_element_type=F32, precision=HIGHEST) + b_ref[0]


def _mod_vectors(c_rows, w_mod, b_mod):
    d = c_rows.shape[1]
    depth, _, n = w_mod.shape
    tn = 1536
    return pl.pallas_call(
        _mod_kernel,
        out_shape=jax.ShapeDtypeStruct((depth, SUBLANES, n), F32),
        grid=(depth, n // tn),
        in_specs=[pl.BlockSpec((SUBLANES, d), lambda l, j: (0, 0)),
                  pl.BlockSpec((1, d, tn), lambda l, j: (l, 0, j)),
                  pl.BlockSpec((1, 1, tn), lambda l, j: (l, 0, j))],
        out_specs=pl.BlockSpec((1, SUBLANES, tn), lambda l, j: (l, 0, j)),
        compiler_params=_cp(("arbitrary", "arbitrary")),
        name="mod_vectors",
    )(c_rows, w_mod, b_mod.reshape(depth, 1, n))


def _ln_mod_kernel(xc_ref, x_ref, mod_ref, o_ref, *, shift_idx, scale_idx):
    x = jnp.where(pl.program_id(1) == 0, xc_ref[0], x_ref[0])
    y = _layer_norm_rows(x)
    m = mod_ref[0, 0]
    o = y * (1.0 + m[scale_idx:scale_idx + 1, :]) + m[shift_idx:shift_idx + 1, :]
    o_ref[0] = o.astype(o_ref.dtype)


def _ln_mod(ctx, x, mod_sel, shift_idx, scale_idx):
    b, ctx_len, d = ctx.shape
    s = ctx_len + x.shape[1]
    tm = ctx_len
    return pl.pallas_call(
        functools.partial(_ln_mod_kernel, shift_idx=shift_idx, scale_idx=scale_idx),
        out_shape=jax.ShapeDtypeStruct((b, s, d), BF16),
        grid=(b, s // tm),
        in_specs=[pl.BlockSpec((1, tm, d), lambda i, j: (i, 0, 0)),
                  pl.BlockSpec((1, tm, d), lambda i, j: (i, jnp.maximum(j - 1, 0), 0)),
                  pl.BlockSpec((1, 1, 6, d), lambda i, j: (i, jnp.minimum(j, 1), 0, 0))],
        out_specs=pl.BlockSpec((1, tm, d), lambda i, j: (i, j, 0)),
        compiler_params=_cp(("parallel", "parallel")),
        name="ln_mod",
    )(ctx, x, mod_sel)


def _mm_kernel(a_ref, w_ref, o_ref):
    o_ref[...] = jnp.dot(a_ref[...], w_ref[...], preferred_element_type=F32).astype(o_ref.dtype)


def _pick_tile(n, prefs):
    for t in prefs:
        if n % t == 0:
            return t
    return n


def _matmul(a, w, out_dtype, tm=None, tn=None, name="matmul"):
    m, k = a.shape
    n = w.shape[1]
    tm = tm or _pick_tile(m, (1024, 512, 256, 128))
    tn = tn or (n if n <= 1536 else _pick_tile(n, (1024, 768, 512, 384, 256, 128)))
    return pl.pallas_call(
        _mm_kernel,
        out_shape=jax.ShapeDtypeStruct((m, n), out_dtype),
        grid=(m // tm, n // tn),
        in_specs=[pl.BlockSpec((tm, k), lambda i, j: (i, 0)),
                  pl.BlockSpec((k, tn), lambda i, j: (0, j))],
        out_specs=pl.BlockSpec((tm, tn), lambda i, j: (i, j)),
        compiler_params=_cp(("parallel", "parallel")),
        name=name,
    )(a, w)


def _mm_nt_kernel(w_ref, u_ref, o_ref):
    o_ref[0] = lax.dot_general(w_ref[...], u_ref[0], (((1,), (1,)), ((), ())),
                               preferred_element_type=F32).astype(o_ref.dtype)


def _matmul_nt(w_t, u, out_dtype, tok0, ntok, tn=256):
    c, k = w_t.shape
    b = u.shape[0]
    j0 = tok0 // tn
    return pl.pallas_call(
        _mm_nt_kernel,
        out_shape=jax.ShapeDtypeStruct((b, c, ntok), out_dtype),
        grid=(b, ntok // tn),
        in_specs=[pl.BlockSpec((c, k), lambda i, j: (0, 0)),
                  pl.BlockSpec((1, tn, k), lambda i, j: (i, j + j0, 0))],
        out_specs=pl.BlockSpec((1, c, tn), lambda i, j: (i, 0, j)),
        compiler_params=_cp(("parallel", "parallel")),
        name="matmul_nt",
    )(w_t, u)


LRU_CHUNK = 128


def _tile_scan(a, b, row, reverse):
    for s in (1, 2, 4):
        if reverse:
            keep = row < SUBLANES - s
            shift = SUBLANES - s
        else:
            keep = row >= s
            shift = s
        a_sh = pltpu.roll(a, shift, 0)
        b_sh = pltpu.roll(b, shift, 0)
        b = jnp.where(keep, a * b_sh, 0.0) + b
        a = jnp.where(keep, a * a_sh, a)
    return a, b


def _lru_kernel(pg_ref, px_ref, cw_ref, cb_ref, wg_ref, bg_ref, lam_ref, y_ref,
                xpad, a_f, b_f, a_b, b_b, *, s_len, ctx_len):
    ch = LRU_CHUNK
    n_chunks = s_len // ch
    zeros8 = jnp.zeros((SUBLANES, LANES), F32)
    xpad[0:SUBLANES, :] = zeros8
    xpad[ctx_len + SUBLANES:ctx_len + 2 * SUBLANES, :] = zeros8
    xpad[s_len + 2 * SUBLANES:s_len + 3 * SUBLANES, :] = zeros8

    def pad_row(r):
        return pl.multiple_of(r + jnp.where(r >= ctx_len, 2 * SUBLANES, SUBLANES), SUBLANES)

    def copy_body(i, carry):
        r = pl.multiple_of(i * ch, ch)
        xpad[pl.ds(pad_row(r), ch), :] = px_ref[0, pl.ds(r, ch), :]
        return carry

    lax.fori_loop(0, n_chunks, copy_body, 0)

    lam = lam_ref[...]
    sp = jnp.log(1.0 + jnp.exp(-lam))

    def gates_body(i, carry):
        r = pl.multiple_of(i * ch, ch)
        win = xpad[pl.ds(pad_row(r) - SUBLANES, ch + 2 * SUBLANES), :]
        xc = jnp.zeros((ch, LANES), F32) + cb_ref[...]
        for k in range(LRU_CONV):
            off = k - LRU_CONV // 2
            xc = xc + win[SUBLANES + off:SUBLANES + off + ch, :] * cw_ref[k:k + 1, :]
        gl = jnp.dot(xc.astype(BF16), wg_ref[0], preferred_element_type=F32) + bg_ref[0]
        for d, (a_s, b_s) in enumerate(((a_f, b_f), (a_b, b_b))):
            g_r = _sigmoid(gl[:, d * 2 * LANES:d * 2 * LANES + LANES])
            g_i = _sigmoid(gl[:, d * 2 * LANES + LANES:(d + 1) * 2 * LANES])
            log_a = (-LRU_C) * g_r * sp[d:d + 1, :]
            a = jnp.exp(log_a)
            bb = jnp.sqrt(1.0 - a * a) * g_i * xc
            a_s[pl.ds(r, ch), :] = a
            b_s[pl.ds(r, ch), :] = bb
        return carry

    lax.fori_loop(0, n_chunks, gates_body, 0)

    row = lax.broadcasted_iota(jnp.int32, (SUBLANES, LANES), 0)

    n_ctx_tiles = ctx_len // SUBLANES
    n_tiles = s_len // SUBLANES

    def scan_body(i, carry):
        h_f, h_b = carry
        r = pl.multiple_of(i * SUBLANES, SUBLANES)
        a, b = _tile_scan(a_f[pl.ds(r, SUBLANES), :], b_f[pl.ds(r, SUBLANES), :], row, False)
        hf = b + a * h_f
        b_f[pl.ds(r, SUBLANES), :] = hf
        t = jnp.where(i < n_ctx_tiles, n_ctx_tiles - 1 - i, n_tiles + n_ctx_tiles - 1 - i)
        rb = pl.multiple_of(t * SUBLANES, SUBLANES)
        a, b = _tile_scan(a_b[pl.ds(rb, SUBLANES), :], b_b[pl.ds(rb, SUBLANES), :], row, True)
        hb = b + a * h_b
        b_b[pl.ds(rb, SUBLANES), :] = hb
        return (jnp.broadcast_to(hf[SUBLANES - 1:SUBLANES, :], (SUBLANES, LANES)),
                jnp.broadcast_to(hb[0:1, :], (SUBLANES, LANES)))

    lax.fori_loop(0, n_tiles, scan_body, (zeros8, zeros8), unroll=2)

    def out_body(i, carry):
        r = pl.multiple_of(i * ch, ch)
        g = _gelu(pg_ref[0, pl.ds(r, ch), :])
        y = g * (b_f[pl.ds(r, ch), :] + b_b[pl.ds(r, ch), :])
        y_ref[0, pl.ds(r, ch), :] = y.astype(y_ref.dtype)
        return carry

    lax.fori_loop(0, n_chunks, out_body, 0)


def _lru_gate_weights(w_r, w_i, b_r, b_i):
    n_grp = LRU_WIDTH // LANES
    per = LANES // LRU_BLOCK

    def bd(w):
        w = w.reshape(n_grp, per, LRU_BLOCK, LRU_BLOCK)
        z = jnp.zeros((n_grp, LRU_BLOCK, LRU_BLOCK), w.dtype)
        top = jnp.concatenate([w[:, 0], z], axis=2)
        bot = jnp.concatenate([z, w[:, 1]], axis=2)
        return jnp.concatenate([top, bot], axis=1)

    wg = jnp.concatenate([bd(w_r[0]), bd(w_i[0]), bd(w_r[1]), bd(w_i[1])], axis=2).astype(BF16)
    bg = jnp.stack([b_r[0], b_i[0], b_r[1], b_i[1]], axis=0).reshape(4, n_grp, LANES)
    bg = jnp.transpose(bg, (1, 0, 2)).reshape(n_grp, 1, 4 * LANES)
    return wg, bg


def _lru_mixer(p_a, lp, ctx_len):
    b, s, _ = p_a.shape
    n_grp = LRU_WIDTH // LANES
    wg, bg = _lru_gate_weights(lp['lru_w_r'], lp['lru_w_i'], lp['lru_b_r'], lp['lru_b_i'])
    scr = pltpu.VMEM((s, LANES), F32)
    return pl.pallas_call(
        functools.partial(_lru_kernel, s_len=s, ctx_len=ctx_len),
        out_shape=jax.ShapeDtypeStruct((b, s, LRU_WIDTH), BF16),
        grid=(b, n_grp),
        in_specs=[pl.BlockSpec((1, s, LANES), lambda i, g: (i, 0, g)),
                  pl.BlockSpec((1, s, LANES), lambda i, g: (i, 0, n_grp + g)),
                  pl.BlockSpec((LRU_CONV, LANES), lambda i, g: (0, g)),
                  pl.BlockSpec((1, LANES), lambda i, g: (0, g)),
                  pl.BlockSpec((1, LANES, 4 * LANES), lambda i, g: (g, 0, 0)),
                  pl.BlockSpec((1, 1, 4 * LANES), lambda i, g: (g, 0, 0)),
                  pl.BlockSpec((2, LANES), lambda i, g: (0, g))],
        out_specs=pl.BlockSpec((1, s, LANES), lambda i, g: (i, 0, g)),
        scratch_shapes=[pltpu.VMEM((s + 3 * SUBLANES, LANES), F32), scr, scr, scr, scr],
        compiler_params=_cp(("parallel", "parallel")),
        name="rglru",
    )(p_a, p_a, lp['lru_conv_w'], lp['lru_conv_b'].reshape(1, LRU_WIDTH), wg, bg, lp['lru_lambda'])


S5_R = 4
S5_NSTATE = S5_GROUPS * S5_STATE


def _s5_kernel(xa_ref, xb_ref, winj_ref, wloc_ref, wro_ref, ap_ref, ya_ref, yb_ref, g_ref, *, reverse, n_ctx_tiles):
    n = S5_NSTATE
    nr = g_ref.shape[0]
    x = jnp.concatenate([h[0, pl.ds(i, nr, stride=S5_R), :] for i in range(S5_R) for h in (xa_ref, xb_ref)],
                        axis=1).astype(BF16)
    g_ref[...] = jnp.dot(x, winj_ref[...], preferred_element_type=F32)
    n_tiles = g_ref.shape[0] // SUBLANES
    row = lax.broadcasted_iota(jnp.int32, (SUBLANES, n), 0)
    zeros = jnp.zeros((SUBLANES, n), F32)
    if reverse:
        shift1, e_in, e_out = SUBLANES - 1, SUBLANES - 1, 0
    else:
        shift1, e_in, e_out = 1, 0, SUBLANES - 1

    def make_body(first_tile):
        def body(i, carry):
            hr, hi = carry
            t = (first_tile - i) if reverse else (first_tile + i)
            r = pl.multiple_of(t * SUBLANES, SUBLANES)
            br = g_ref[pl.ds(r, SUBLANES), 0:n]
            bi = g_ref[pl.ds(r, SUBLANES), n:2 * n]
            for k, s in enumerate((1, 2, 4)):
                ar = ap_ref[SUBLANES + k:SUBLANES + k + 1, 0:n]
                ai = ap_ref[SUBLANES + k:SUBLANES + k + 1, n:2 * n]
                if reverse:
                    keep = row < SUBLANES - s
                    shift = SUBLANES - s
                else:
                    keep = row >= s
                    shift = s
                brs = pltpu.roll(br, shift, 0)
                bis = pltpu.roll(bi, shift, 0)
                nr = ar * brs - ai * bis
                ni = ar * bis + ai * brs
                br = br + jnp.where(keep, nr, 0.0)
                bi = bi + jnp.where(keep, ni, 0.0)
            cr = ap_ref[0:SUBLANES, 0:n]
            ci = ap_ref[0:SUBLANES, n:2 * n]
            out_r = br + (cr * hr - ci * hi)
            out_i = bi + (cr * hi + ci * hr)
            g_ref[pl.ds(r, SUBLANES), 0:n] = jnp.where(row == e_in, hr, pltpu.roll(out_r, shift1, 0))
            g_ref[pl.ds(r, SUBLANES), n:2 * n] = jnp.where(row == e_in, hi, pltpu.roll(out_i, shift1, 0))
            return (jnp.broadcast_to(out_r[e_out:e_out + 1, :], (SUBLANES, n)),
                    jnp.broadcast_to(out_i[e_out:e_out + 1, :], (SUBLANES, n)))
        return body

    if reverse:
        carry = lax.fori_loop(0, n_ctx_tiles, make_body(n_ctx_tiles - 1), (zeros, zeros))
        lax.fori_loop(0, n_tiles - n_ctx_tiles, make_body(n_tiles - 1), carry)
    else:
        lax.fori_loop(0, n_tiles, make_body(0), (zeros, zeros))
    y = (jnp.dot(x, wloc_ref[...], preferred_element_type=F32)
         + jnp.dot(g_ref[...].astype(BF16), wro_ref[...], preferred_element_type=F32))
    for i in range(S5_R):
        for k, h in enumerate((ya_ref, yb_ref)):
            h[0, pl.ds(i, nr, stride=S5_R), :] = y[:, (2 * i + k) * LANES:(2 * i + k + 1) * LANES]


def _s5_params(a_re, a_im, log_dt, b_re, b_im, c_re, c_im, reverse):
    rr = S5_R
    dt = jnp.exp(log_dt)[:, None]
    den = a_re * a_re + a_im * a_im
    mag = jnp.exp(dt * a_re)
    ab_re = mag * jnp.cos(dt * a_im)
    ab_im = mag * jnp.sin(dt * a_im)
    f_re = ((ab_re - 1.0) * a_re + ab_im * a_im) / den
    f_im = (ab_im * a_re - (ab_re - 1.0) * a_im) / den
    bb_re = f_re[..., None] * b_re - f_im[..., None] * b_im
    bb_im = f_re[..., None] * b_im + f_im[..., None] * b_re
    grp_tok = (np.arange(rr * S5_WIDTH) // S5_GROUP) % S5_GROUPS
    grp_state = np.arange(S5_NSTATE) // S5_STATE

    def block_diag(t, lead, grp_rows, grp_cols):
        n_lead, minor, ncols = t.shape
        full = jnp.broadcast_to(t[:, None], (n_lead, S5_GROUPS, minor, ncols)).reshape(-1, ncols)
        return jnp.where(jnp.asarray(grp_rows[:, None] == grp_cols[None, :]), full, 0.0)

    def apow(k):
        k = k.astype(F32)[:, None, None]
        m = jnp.exp(k * dt[None] * a_re[None])
        return m * jnp.cos(k * dt[None] * a_im[None]), m * jnp.sin(k * dt[None] * a_im[None])

    steps = jnp.arange(rr)
    rows = rr * S5_WIDTH
    er, ei = apow(steps if reverse else (rr - 1 - steps))
    inj_re = er[..., None] * bb_re[None] - ei[..., None] * bb_im[None]
    inj_im = er[..., None] * bb_im[None] + ei[..., None] * bb_re[None]
    def inj_map(t):
        t = jnp.transpose(t, (0, 3, 1, 2)).reshape(rr, S5_GROUP, S5_NSTATE)
        return block_diag(t, rr, grp_tok, grp_state)

    winj = jnp.concatenate([inj_map(inj_re), inj_map(inj_im)], axis=1)
    fr, fi = apow((rr - steps) if reverse else (steps + 1))
    ro_re = c_re[None] * fr[:, :, None, :] - c_im[None] * fi[:, :, None, :]
    ro_im = c_re[None] * fi[:, :, None, :] + c_im[None] * fr[:, :, None, :]
    def ro_map(t):
        t = jnp.transpose(t, (3, 0, 1, 2)).reshape(1, S5_STATE, rows)
        return block_diag(t, 1, grp_state, grp_tok)

    wro = jnp.concatenate([ro_map(ro_re), -ro_map(ro_im)], axis=0)
    kr, ki = apow(steps)
    ab_r = kr[..., None] * bb_re[None] - ki[..., None] * bb_im[None]
    ab_i = kr[..., None] * bb_im[None] + ki[..., None] * bb_re[None]
    kk = jnp.einsum('gop,kgpc->kgoc', c_re, ab_r) - jnp.einsum('gop,kgpc->kgoc', c_im, ab_i)
    src = jnp.arange(rr)[:, None]
    tgt = jnp.arange(rr)[None, :]
    lag = (src - tgt) if reverse else (tgt - src)
    kmat = jnp.where((lag >= 0)[:, :, None, None, None], kk[jnp.clip(lag, 0, rr - 1)], 0.0)
    wloc = block_diag(jnp.transpose(kmat, (0, 4, 1, 2, 3)).reshape(rr, S5_GROUP, rows), rr, grp_tok, grp_tok)
    i8 = jnp.arange(SUBLANES)
    dist = (SUBLANES - i8) if reverse else (i8 + 1)
    ks = jnp.concatenate([dist, jnp.array([1, 2, 4]), jnp.zeros((5,), dist.dtype)]) * rr
    pr, pi = apow(ks)
    ap = jnp.concatenate([pr.reshape(16, S5_NSTATE), pi.reshape(16, S5_NSTATE)], axis=1)
    return winj.astype(BF16), wloc.astype(BF16), wro.astype(BF16), ap


def _s5_scan(p_d, lp, d, ctx_len):
    b, s, w = p_d.shape
    nr = s // S5_R
    wr = S5_R * w
    reverse = d == 1
    n_ctx_tiles = ctx_len // (S5_R * SUBLANES)
    winj, wloc, wro, ap = _s5_params(lp['s5_a_re'][d], lp['s5_a_im'][d], lp['s5_log_dt'][d], lp['s5_b_re'][d],
                                     lp['s5_b_im'][d], lp['s5_c_re'][d], lp['s5_c_im'][d], reverse)
    assert w == 2 * LANES
    full = lambda shape: pl.BlockSpec(shape, lambda i: (0, 0))
    half = jax.ShapeDtypeStruct((b, s, LANES), F32)
    return pl.pallas_call(
        functools.partial(_s5_kernel, reverse=reverse, n_ctx_tiles=n_ctx_tiles),
        out_shape=(half, half),
        grid=(b,),
        in_specs=[pl.BlockSpec((1, s, LANES), lambda i: (i, 0, 0)), pl.BlockSpec((1, s, LANES), lambda i: (i, 0, 1)),
                  full((wr, 2 * S5_NSTATE)), full((wr, wr)), full((2 * S5_NSTATE, wr)), full((16, 2 * S5_NSTATE))],
        out_specs=(pl.BlockSpec((1, s, LANES), lambda i: (i, 0, 0)), pl.BlockSpec((1, s, LANES), lambda i: (i, 0, 0))),
        scratch_shapes=[pltpu.VMEM((nr, 2 * S5_NSTATE), F32)],
        compiler_params=_cp(("parallel",)),
        name="s5_scan_bwd" if reverse else "s5_scan_fwd",
    )(p_d, p_d, winj, wloc, wro, ap)


def _s5_out_kernel(yfa_ref, yfb_ref, yba_ref, ybb_ref, u_ref, d_ref, w_ref, b_ref, o_ref):
    y = (jnp.concatenate([yfa_ref[...] + yba_ref[...], yfb_ref[...] + ybb_ref[...]], axis=1)
         + d_ref[...] * u_ref[...])
    g = _gelu(y)
    z = jnp.dot(g.astype(BF16), w_ref[...], preferred_element_type=F32) + b_ref[...]
    o_ref[...] = (g * _sigmoid(z)).astype(o_ref.dtype)


def _s5_mixer(p_d, lp, ctx_len):
    b, s, w = p_d.shape
    yfa, yfb = _s5_scan(p_d, lp, 0, ctx_len)
    yba, ybb = _s5_scan(p_d, lp, 1, ctx_len)
    m = b * s
    tm = _pick_tile(m, (1024, 512, 256))
    row = pl.BlockSpec((tm, w), lambda i: (i, 0))
    hrow = pl.BlockSpec((tm, LANES), lambda i: (i, 0))
    vec = pl.BlockSpec((1, w), lambda i: (0, 0))
    flat = lambda a: a.reshape(m, LANES)
    out = pl.pallas_call(
        _s5_out_kernel,
        out_shape=jax.ShapeDtypeStruct((m, w), BF16),
        grid=(m // tm,),
        in_specs=[hrow, hrow, hrow, hrow, row, vec, pl.BlockSpec((w, w), lambda i: (0, 0)), vec],
        out_specs=row,
        compiler_params=_cp(("parallel",)),
        name="s5_out",
    )(flat(yfa), flat(yfb), flat(yba), flat(ybb), p_d.reshape(m, w), lp['s5_d'].reshape(1, w),
      lp['s5_w_glu'].astype(BF16), lp['s5_b_glu'].reshape(1, w))
    return out.reshape(b, s, w)


NA_QROWS = 4


def _natten_plan(rows):
    kr = min(NA_WIN_R, rows)
    span = kr + NA_QROWS - 1
    variants, index, blk_var = [], {}, []
    for blk in range(rows // NA_QROWS):
        r0 = blk * NA_QROWS
        ws = int(np.clip(r0 - kr // 2, 0, rows - span))
        dr = np.zeros((NA_QROWS, span), np.int32)
        ok = np.zeros((NA_QROWS, span), bool)
        for q in range(NA_QROWS):
            r = r0 + q
            rs = int(np.clip(r - kr // 2, 0, rows - kr))
            for i in range(span):
                ok[q, i] = rs <= ws + i < rs + kr
                dr[q, i] = (ws + i - r + (NA_WIN_R - 1)) if ok[q, i] else 0
        key = dr.tobytes() + ok.tobytes()
        if key not in index:
            index[key] = len(variants)
            variants.append((dr, ok))
        blk_var.append(index[key])
    return (np.stack([v[0] for v in variants]), np.stack([v[1] for v in variants]),
            np.asarray(blk_var, np.int32))


def _natten_bias(rpb, dr, ok):
    nv, nq, span = dr.shape
    n_dr, n_dc = 2 * NA_WIN_R - 1, 2 * NA_WIN_C - 1
    w = np.arange(GRID_W)
    cs = np.clip(w - NA_WIN_C // 2, 0, GRID_W - NA_WIN_C)
    ok_col = (w[None, :] >= cs[:, None]) & (w[None, :] < cs[:, None] + NA_WIN_C)
    dc = w[None, :] - w[:, None] + (NA_WIN_C - 1)
    e_dc = (dc[None] == np.arange(n_dc)[:, None, None]).astype(np.float32)
    e_dr = ((dr[..., None] == np.arange(n_dr)) & ok[..., None]).astype(np.float32)
    g = jnp.einsum('vqir,hrk,kwc->vhqwic', e_dr, rpb, e_dc, precision=HIGHEST)
    ok_all = ok[:, None, :, None, :, None] & ok_col[None, None, None, :, None, :]
    g = jnp.where(jnp.asarray(ok_all), g, MASK_VALUE)
    return g.reshape(nv, NA_HEADS, nq * GRID_W, span * GRID_W).astype(F32)


def _attend(q2, keys, vals, biases, lane):
    nq = q2.shape[0]
    sels = [(lane >= hh * NA_HEAD_DIM) & (lane < (hh + 1) * NA_HEAD_DIM) for hh in range(2)]
    qs = jnp.concatenate([jnp.where(sel, q2, jnp.zeros_like(q2)) for sel in sels], axis=0)
    ss = []
    for k_i, b0, b1 in zip(keys, biases[0], biases[1]):
        s_i = lax.dot_general(qs, k_i, (((1,), (1,)), ((), ())), preferred_element_type=F32)
        if b0 is not None:
            s_i = jnp.concatenate([s_i[:nq] + b0, s_i[nq:] + b1], axis=0)
        ss.append(s_i)
    m = ss[0].max(axis=-1, keepdims=True)
    for s_i in ss[1:]:
        m = jnp.maximum(m, s_i.max(axis=-1, keepdims=True))
    ps = [jnp.exp(s_i - m) for s_i in ss]
    den = ps[0].sum(axis=-1, keepdims=True)
    for p_i in ps[1:]:
        den = den + p_i.sum(axis=-1, keepdims=True)
    o = jnp.dot(ps[0].astype(BF16), vals[0], preferred_element_type=F32)
    for p_i, v_i in zip(ps[1:], vals[1:]):
        o = o + jnp.dot(p_i.astype(BF16), v_i, preferred_element_type=F32)
    o = o / den
    return jnp.where(sels[0], o[:nq], o[nq:])


def _natten_kernel(var_ref, q_ref, k_ref, v_ref, bias_ref, o_ref, *, rows, ctx_len):
    kr = min(NA_WIN_R, rows)
    span = kr + NA_QROWS - 1
    r0 = pl.program_id(1) * NA_QROWS
    ws = jnp.clip(r0 - kr // 2, 0, rows - span)
    base = pl.multiple_of(ctx_len + ws * GRID_W, GRID_W)
    lane = lax.broadcasted_iota(jnp.int32, (NA_QROWS * GRID_W, LANES), 1)
    scale = NA_HEAD_DIM ** -0.5
    for hp in range(NA_HEADS // 2):
        ls = slice(hp * LANES, (hp + 1) * LANES)
        q2 = q_ref[0, :, ls] * scale
        kw = k_ref[0, pl.ds(base, span * GRID_W), ls]
        vw = v_ref[0, pl.ds(base, span * GRID_W), ls]
        kc = k_ref[0, 0:ctx_len, ls]
        vc = v_ref[0, 0:ctx_len, ls]
        biases = [[bias_ref[0, 2 * hp + hh], None] for hh in range(2)]
        out = _attend(q2, [kw, kc], [vw, vc], biases, lane)
        o_ref[0, :, ls] = out.astype(o_ref.dtype)


def _ctx_attn_kernel(q_ref, k_ref, v_ref, o_ref, *, ctx_len):
    lane = lax.broadcasted_iota(jnp.int32, (ctx_len, LANES), 1)
    scale = NA_HEAD_DIM ** -0.5
    for hp in range(NA_HEADS // 2):
        ls = slice(hp * LANES, (hp + 1) * LANES)
        out = _attend(q_ref[0, :, ls] * scale, [k_ref[0, :, ls]], [v_ref[0, :, ls]], [[None], [None]], lane)
        o_ref[0, :, ls] = out.astype(o_ref.dtype)


def _natten_mixer(p_c, rpb, ctx_len, with_ctx):
    b, s, _ = p_c.shape
    l = s - ctx_len
    rows = l // GRID_W
    kr = min(NA_WIN_R, rows)
    span = kr + NA_QROWS - 1
    nq = NA_QROWS * GRID_W
    dr, ok, blk_var = _natten_plan(rows)
    bias = _natten_bias(rpb, dr, ok)
    cb = ctx_len // nq
    grid_spec = pltpu.PrefetchScalarGridSpec(
        num_scalar_prefetch=1,
        grid=(b, rows // NA_QROWS),
        in_specs=[pl.BlockSpec((1, nq, NA_WIDTH), lambda i, r, var: (i, cb + r, 0)),
                  pl.BlockSpec((1, s, NA_WIDTH), lambda i, r, var: (i, 0, 1)),
                  pl.BlockSpec((1, s, NA_WIDTH), lambda i, r, var: (i, 0, 2)),
                  pl.BlockSpec((1, NA_HEADS, nq, span * GRID_W), lambda i, r, var: (var[r], 0, 0, 0))],
        out_specs=pl.BlockSpec((1, nq, NA_WIDTH), lambda i, r, var: (i, r, 0)),
    )
    y_l = pl.pallas_call(
        functools.partial(_natten_kernel, rows=rows, ctx_len=ctx_len),
        out_shape=jax.ShapeDtypeStruct((b, l, NA_WIDTH), BF16),
        grid_spec=grid_spec,
        compiler_params=_cp(("parallel", "arbitrary")),
        name="natten",
    )(jnp.asarray(blk_var), p_c, p_c, p_c, bias)
    if not with_ctx:
        return y_l, None
    y_c = pl.pallas_call(
        functools.partial(_ctx_attn_kernel, ctx_len=ctx_len),
        out_shape=jax.ShapeDtypeStruct((b, ctx_len, NA_WIDTH), BF16),
        grid=(b,),
        in_specs=[pl.BlockSpec((1, ctx_len, NA_WIDTH), lambda i: (i, 0, 0)),
                  pl.BlockSpec((1, ctx_len, NA_WIDTH), lambda i: (i, 0, 1)),
                  pl.BlockSpec((1, ctx_len, NA_WIDTH), lambda i: (i, 0, 2))],
        out_specs=pl.BlockSpec((1, ctx_len, NA_WIDTH), lambda i: (i, 0, 0)),
        compiler_params=_cp(("parallel",)),
        name="ctx_attn",
    )(p_c, p_c, p_c)
    return y_l, y_c


DFT_ROWS = 64


def _dft_gen_kernel(ca_ref, sa_ref, cb_ref, sb_ref, fwd_ref, inv_ref, *, l):
    i = pl.program_id(0)
    ca = ca_ref[0]
    sa = sa_ref[0]
    cb = cb_ref[...]
    sb = sb_ref[...]
    gc = ca * cb - sa * sb
    gs = sa * cb + ca * sb
    x = i * DFT_ROWS + lax.broadcasted_iota(jnp.int32, (DFT_ROWS, l), 0)
    y = lax.broadcasted_iota(jnp.int32, (DFT_ROWS, l), 1)
    n = 2.0 * l
    nyq_x = jnp.where((x & 1) == 0, 1.0, -1.0)
    fwd_ref[:, 0:l] = gc.astype(fwd_ref.dtype)
    fwd_ref[:, l:2 * l] = jnp.where(y == 0, nyq_x, -gs).astype(fwd_ref.dtype)
    scale = jnp.where(x == 0, 1.0 / n, 2.0 / n)
    nyq_y = jnp.where((y & 1) == 0, 1.0, -1.0)
    inv_ref[0] = (scale * gc).astype(inv_ref.dtype)
    inv_ref[1] = (scale * jnp.where(x == 0, nyq_y, -gs)).astype(inv_ref.dtype)


def _dft_matrices(l):
    n = 2 * l
    k1 = l // DFT_ROWS
    y = np.arange(l, dtype=np.int64)
    xa = (DFT_ROWS * np.arange(k1, dtype=np.int64))[:, None]
    xb = np.arange(DFT_ROWS, dtype=np.int64)[:, None]
    ang_a = jnp.asarray(((xa * y[None, :]) % n).astype(np.float32)) * F32(2.0 * math.pi / n)
    ang_b = jnp.asarray(((xb * y[None, :]) % n).astype(np.float32)) * F32(2.0 * math.pi / n)
    ca, sa = jnp.cos(ang_a).reshape(k1, 1, l), jnp.sin(ang_a).reshape(k1, 1, l)
    cb, sb = jnp.cos(ang_b), jnp.sin(ang_b)
    row = pl.BlockSpec((1, 1, l), lambda i: (i, 0, 0))
    tab = pl.BlockSpec((DFT_ROWS, l), lambda i: (0, 0))
    fwd, inv = pl.pallas_call(
        functools.partial(_dft_gen_kernel, l=l),
        out_shape=(jax.ShapeDtypeStruct((l, 2 * l), BF16), jax.ShapeDtypeStruct((2, l, l), BF16)),
        grid=(k1,),
        in_specs=[row, row, tab, tab],
        out_specs=(pl.BlockSpec((DFT_ROWS, 2 * l), lambda i: (i, 0)),
                   pl.BlockSpec((2, DFT_ROWS, l), lambda i: (0, i, 0))),
        compiler_params=_cp(("parallel",)),
        name="dft_gen",
    )(ca, sa, cb, sb)
    return fwd, inv.reshape(2 * l, l)


def _hy_filter_kernel(w1t_ref, w1c_ref, w1s_ref, b1_ref, w2_ref, b2_ref, w3_ref, b3_ref, fr_ref, dl_ref,
                      h_ref, asum_ref, *, l, tl):
    j = pl.program_id(0)
    t = (j * tl + lax.broadcasted_iota(jnp.int32, (1, tl), 1)).astype(F32)
    t_norm = t / l
    bands = (1 + lax.broadcasted_iota(jnp.int32, (HY_BANDS, 1), 0)).astype(F32)
    ang = (2.0 * math.pi / l) * t * bands
    fr = fr_ref[...]
    lin = (w1t_ref[...] * t_norm
           + jnp.dot(w1c_ref[...], jnp.cos(ang), preferred_element_type=F32, precision=HIGHEST)
           + jnp.dot(w1s_ref[...], jnp.sin(ang), preferred_element_type=F32, precision=HIGHEST))
    h = jnp.sin(fr * (lin + b1_ref[...]))
    h = jnp.sin(fr * (jnp.dot(w2_ref[...], h, preferred_element_type=F32, precision=HIGHEST) + b2_ref[...]))
    h = jnp.dot(w3_ref[...], h, preferred_element_type=F32, precision=HIGHEST) + b3_ref[...]
    window = jnp.exp(-t_norm * dl_ref[...])
    first = (j * tl + lax.broadcasted_iota(jnp.int32, (HY_WIDTH, tl), 1)) == 0

    @pl.when(j == 0)
    def _():
        asum_ref[...] = jnp.zeros_like(asum_ref)

    for blk in range(2 * HY_ORDER):
        rs = slice(blk * HY_WIDTH, (blk + 1) * HY_WIDTH)
        hb = h[rs, :] * window
        if blk >= HY_ORDER:
            hb = jnp.where(first, 0.0, hb)
        h_ref[rs, :] = hb
        asum_ref[rs, :] += jnp.sum(jnp.abs(hb), axis=1, keepdims=True)


def _hy_filters(lp, l):
    tl = min(l, 512)
    hid = HY_FILT_HID
    w1 = lp['hy_w1']
    col = lambda v: v.reshape(-1, 1)
    deltas = np.abs(np.linspace(HY_MIN_DECAY, HY_MAX_DECAY, HY_WIDTH, dtype=np.float32)).reshape(-1, 1)
    full = lambda shape: pl.BlockSpec(shape, lambda j: (0, 0))
    n_out = 2 * HY_ORDER * HY_WIDTH
    return pl.pallas_call(
        functools.partial(_hy_filter_kernel, l=l, tl=tl),
        out_shape=(jax.ShapeDtypeStruct((n_out, l), F32), jax.ShapeDtypeStruct((n_out, 1), F32)),
        grid=(l // tl,),
        in_specs=[full((hid, 1)), full((hid, HY_BANDS)), full((hid, HY_BANDS)), full((hid, 1)),
                  full((hid, hid)), full((hid, 1)), full((n_out, hid)), full((n_out, 1)),
                  full((hid, 1)), full((HY_WIDTH, 1))],
        out_specs=(pl.BlockSpec((n_out, tl), lambda j: (0, j)), pl.BlockSpec((n_out, 1), lambda j: (0, 0))),
        compiler_params=_cp(("arbitrary",)),
        name="hyena_filters",
    )(w1[0:1].T, w1[1:1 + HY_BANDS].T, w1[1 + HY_BANDS:].T, col(lp['hy_b1']), lp['hy_w2'].T, col(lp['hy_b2']),
      lp['hy_w3'].T, col(lp['hy_b3']), col(lp['hy_freq']), jnp.asarray(deltas))


def _hy_short_conv_kernel(x_ref, w_ref, b_ref, z_ref, zbf_ref, *, l):
    x = x_ref[0]
    t = lax.broadcasted_iota(jnp.int32, x.shape, 1)
    left = HY_SHORT // 2
    z = jnp.zeros(x.shape, F32) + b_ref[...]
    for k in range(HY_SHORT):
        off = k - left
        if off == 0:
            sh = x
        else:
            sh = pltpu.roll(x, (-off) % l, 1)
            sh = jnp.where((t + off >= 0) & (t + off < l), sh, 0.0)
        z = z + sh * w_ref[:, k:k + 1]
    z_ref[0] = z
    zbf_ref[0] = z.astype(zbf_ref.dtype)


def _hy_conv_kernel(y_ref, fc_ref, fs_ref, ic_ref, is_ref, kfc_ref, kfs_ref, kbc_ref, kbs_ref, n_ref,
                    o_ref, *, nb):
    j = pl.program_id(1)

    @pl.when(j == 0)
    def _():
        o_ref[...] = jnp.zeros_like(o_ref)

    y = y_ref[...]
    zr = jnp.dot(y, fc_ref[...], preferred_element_type=F32)
    zi = jnp.dot(y, fs_ref[...], preferred_element_type=F32)
    inv_n = 1.0 / (n_ref[...] + 1e-6)
    kr = (kfc_ref[...] + kbc_ref[...]) * inv_n
    ki = (kfs_ref[...] - kbs_ref[...]) * inv_n
    tn = kr.shape[1]
    f0 = (j * tn + lax.broadcasted_iota(jnp.int32, kr.shape, 1)) == 0
    ki = jnp.where(f0, (kfs_ref[...] + kbs_ref[...]) * inv_n, ki)
    prs, pis = [], []
    for bb in range(nb):
        rs = slice(bb * HY_WIDTH, (bb + 1) * HY_WIDTH)
        a, b = zr[rs], zi[rs]
        prs.append(a * kr - jnp.where(f0, 0.0, b * ki))
        pis.append(jnp.where(f0, b * ki, a * ki + b * kr))
    pr = jnp.concatenate(prs, axis=0).astype(BF16)
    pi = jnp.concatenate(pis, axis=0).astype(BF16)
    o_ref[...] += (jnp.dot(pr, ic_ref[...], preferred_element_type=F32)
                   + jnp.dot(pi, is_ref[...], preferred_element_type=F32))


def _hy_long_conv(ybf, n_tiles, nb, row_stride, fwd, inv, kf, asum, order, l):
    tmh = nb * HY_WIDTH
    m = n_tiles * tmh
    tn = min(l, 256)
    jn = l // tn
    o_f = order
    o_b = HY_ORDER + order
    return pl.pallas_call(
        functools.partial(_hy_conv_kernel, nb=nb),
        out_shape=jax.ShapeDtypeStruct((m, l), F32),
        grid=(n_tiles, jn),
        in_specs=[pl.BlockSpec((tmh, l), lambda i, j: (i * row_stride, 0)),
                  pl.BlockSpec((l, tn), lambda i, j: (0, j)),
                  pl.BlockSpec((l, tn), lambda i, j: (0, jn + j)),
                  pl.BlockSpec((tn, l), lambda i, j: (j, 0)),
                  pl.BlockSpec((tn, l), lambda i, j: (jn + j, 0)),
                  pl.BlockSpec((HY_WIDTH, tn), lambda i, j: (o_f, j)),
                  pl.BlockSpec((HY_WIDTH, tn), lambda i, j: (o_f, jn + j)),
                  pl.BlockSpec((HY_WIDTH, tn), lambda i, j: (o_b, j)),
                  pl.BlockSpec((HY_WIDTH, tn), lambda i, j: (o_b, jn + j)),
                  pl.BlockSpec((HY_WIDTH, 1), lambda i, j: (order, 0))],
        out_specs=pl.BlockSpec((tmh, l), lambda i, j: (i, 0)),
        compiler_params=_cp(("parallel", "arbitrary")),
        name="hyena_long_conv",
    )(ybf, fwd, fwd, inv, inv, kf, kf, kf, kf, asum)


def _hy_gate_kernel(g_ref, c_ref, y_ref, bias_ref, o_ref, obf_ref):
    o = g_ref[0] * (c_ref[0] + y_ref[0] * bias_ref[...])
    o_ref[0] = o
    obf_ref[0] = o.astype(obf_ref.dtype)


def _hy_gate(z, conv, y, bias_col, gate_blk, y_blk):
    b, _, l = z.shape
    tl = min(l, 1024)
    spec = lambda blk: pl.BlockSpec((1, HY_WIDTH, tl), lambda i, j: (i, blk, j))
    return pl.pallas_call(
        _hy_gate_kernel,
        out_shape=(jax.ShapeDtypeStruct((b, HY_WIDTH, l), F32), jax.ShapeDtypeStruct((b, HY_WIDTH, l), BF16)),
        grid=(b, l // tl),
        in_specs=[spec(gate_blk), spec(0), spec(y_blk), pl.BlockSpec((HY_WIDTH, 1), lambda i, j: (0, 0))],
        out_specs=(spec(0), spec(0)),
        compiler_params=_cp(("parallel", "parallel")),
        name="hyena_gate",
    )(z, conv, y, bias_col)


def _hyena_sequence(p_bt, lp, dft):
    b, c3, l = p_bt.shape
    fwd, inv = dft
    h, asum = _hy_filters(lp, l)
    kf = _matmul(h.astype(BF16), fwd, F32, name="hyena_filter_dft")
    asum2 = asum.reshape(2, HY_ORDER * HY_WIDTH).sum(axis=0).reshape(HY_ORDER * HY_WIDTH, 1)
    blk = pl.BlockSpec((1, LANES, l), lambda i, g: (i, g, 0))
    z, z_bf = pl.pallas_call(
        functools.partial(_hy_short_conv_kernel, l=l),
        out_shape=(jax.ShapeDtypeStruct((b, c3, l), F32), jax.ShapeDtypeStruct((b, c3, l), BF16)),
        grid=(b, c3 // LANES),
        in_specs=[blk,
                  pl.BlockSpec((LANES, HY_SHORT), lambda i, g: (g, 0)),
                  pl.BlockSpec((LANES, 1), lambda i, g: (g, 0))],
        out_specs=(blk, blk),
        compiler_params=_cp(("parallel", "parallel")),
        name="hyena_short_conv",
    )(p_bt, lp['hy_conv_w'].T, lp['hy_conv_b'].reshape(c3, 1))
    n_blk = c3 // HY_WIDTH
    nb2 = 2 if b % 2 == 0 else 1
    conv1 = _hy_long_conv(z_bf.reshape(b * c3, l), b, 1, n_blk, fwd, inv, kf, asum2, 0, l).reshape(b, HY_WIDTH, l)
    y1, y1_bf = _hy_gate(z, conv1, z, lp['hy_bias'][0].reshape(HY_WIDTH, 1), 1, 0)
    conv2 = _hy_long_conv(y1_bf.reshape(b * HY_WIDTH, l), b // nb2, nb2, 1, fwd, inv, kf, asum2, 1, l)
    _, y2_bf = _hy_gate(z, conv2.reshape(b, HY_WIDTH, l), y1, lp['hy_bias'][1].reshape(HY_WIDTH, 1), 2, 0)
    return y2_bf


FFT_N2 = 128
FFT_KB = 8
FFT_NB = 8
FFT_PITCH = 136


def _fft_stage1(xa_ref, xb_ref, f1_ref, xpa, xpb, za, zb, n_in, n1):
    for blk in range(n_in):
        xpa[blk * FFT_PITCH:blk * FFT_PITCH + FFT_N2, :] = xa_ref[0, blk * FFT_N2:(blk + 1) * FFT_N2, :]
        xpb[blk * FFT_PITCH:blk * FFT_PITCH + FFT_N2, :] = xb_ref[0, blk * FFT_N2:(blk + 1) * FFT_N2, :]
    f1 = f1_ref[...]
    for c0 in range(0, FFT_N2, FFT_NB):
        cols = []
        for q in range(FFT_NB):
            cols.append(xpa[pl.ds(c0 + q, n_in, stride=FFT_PITCH), :])
            cols.append(xpb[pl.ds(c0 + q, n_in, stride=FFT_PITCH), :])
        z = jnp.dot(f1, jnp.concatenate(cols, axis=1).astype(BF16), preferred_element_type=F32)
        for q in range(FFT_NB):
            za[pl.ds(c0 + q, 2 * n1, stride=FFT_PITCH), :] = z[:, (2 * q) * LANES:(2 * q + 1) * LANES]
            zb[pl.ds(c0 + q, 2 * n1, stride=FFT_PITCH), :] = z[:, (2 * q + 1) * LANES:(2 * q + 2) * LANES]


def _fft_load_z(za, zb, k1, n1):
    r_re = pl.multiple_of(k1 * FFT_PITCH, SUBLANES)
    r_im = pl.multiple_of((n1 + k1) * FFT_PITCH, SUBLANES)
    z = jnp.concatenate(
        [jnp.concatenate([za[pl.ds(r_re, FFT_N2), :], zb[pl.ds(r_re, FFT_N2), :]], axis=1),
         jnp.concatenate([za[pl.ds(r_im, FFT_N2), :], zb[pl.ds(r_im, FFT_N2), :]], axis=1)], axis=0)
    return z.astype(BF16), r_re, r_im


def _hy_fft_filter_kernel(xa_ref, xb_ref, f1_ref, ef_ref, n_ref, y_ref, xpa, xpb, za, zb, *, n1):
    j = pl.program_id(1)

    @pl.when(j == 0)
    def _():
        _fft_stage1(xa_ref, xb_ref, f1_ref, xpa, xpb, za, zb, n1, n1)

    inv_n = 1.0 / (n_ref[0] + 1e-6)
    for q in range(FFT_KB):
        z, _, _ = _fft_load_z(za, zb, j * FFT_KB + q, n1)
        y = jnp.dot(ef_ref[q], z, preferred_element_type=F32) * inv_n
        y_ref[0, 0, q] = y[:FFT_N2].astype(y_ref.dtype)
        y_ref[0, 1, q] = y[FFT_N2:].astype(y_ref.dtype)


def _hy_fft_conv_kernel(xa_ref, xb_ref, ga_ref, gb_ref, bias_ref, f1_ref, f1i_ref, ef_ref, ei_ref, kf_ref,
                        oa_ref, ob_ref, xpa, xpb, za, zb, *, n1):
    n_in = n1 // 2
    j = pl.program_id(1)

    @pl.when(j == 0)
    def _():
        _fft_stage1(xa_ref, xb_ref, f1_ref, xpa, xpb, za, zb, n_in, n1)

    for q in range(FFT_KB):
        z, r_re, r_im = _fft_load_z(za, zb, j * FFT_KB + q, n1)
        y = jnp.dot(ef_ref[q], z, preferred_element_type=F32)
        kr = kf_ref[0, 0, q].astype(F32)
        ki = kf_ref[0, 1, q].astype(F32)
        yr, yi = y[:FFT_N2], y[FFT_N2:]
        p = jnp.concatenate([yr * kr - yi * ki, yr * ki + yi * kr], axis=0).astype(BF16)
        u = jnp.dot(ei_ref[q], p, preferred_element_type=F32)
        za[pl.ds(r_re, FFT_N2), :] = u[:FFT_N2, :LANES]
        zb[pl.ds(r_re, FFT_N2), :] = u[:FFT_N2, LANES:]
        za[pl.ds(r_im, FFT_N2), :] = u[FFT_N2:, :LANES]
        zb[pl.ds(r_im, FFT_N2), :] = u[FFT_N2:, LANES:]

    @pl.when(j == pl.num_programs(1) - 1)
    def _():
        f1i = f1i_ref[...]
        bias = bias_ref[0]
        for c0 in range(0, FFT_N2, FFT_NB):
            cols = []
            for q in range(FFT_NB):
                cols.append(za[pl.ds(c0 + q, 2 * n1, stride=FFT_PITCH), :])
                cols.append(zb[pl.ds(c0 + q, 2 * n1, stride=FFT_PITCH), :])
            yv = jnp.dot(f1i, jnp.concatenate(cols, axis=1).astype(BF16), preferred_element_type=F32)
            for q in range(FFT_NB):
                za[pl.ds(c0 + q, n_in, stride=FFT_PITCH), :] = yv[:, (2 * q) * LANES:(2 * q + 1) * LANES]
                zb[pl.ds(c0 + q, n_in, stride=FFT_PITCH), :] = yv[:, (2 * q + 1) * LANES:(2 * q + 2) * LANES]
        for blk in range(n_in):
            rows = slice(blk * FFT_N2, (blk + 1) * FFT_N2)
            prow = slice(blk * FFT_PITCH, blk * FFT_PITCH + FFT_N2)
            oa_ref[0, rows, :] = ga_ref[0, rows, :] * (za[prow, :] + xa_ref[0, rows, :] * bias[:, :LANES])
            ob_ref[0, rows, :] = gb_ref[0, rows, :] * (zb[prow, :] + xb_ref[0, rows, :] * bias[:, LANES:])


def _fft_tables(l):
    n = 2 * l
    n1 = n // FFT_N2
    n_in = n1 // 2
    k1 = np.arange(n1)
    phi = 2.0 * np.pi * np.outer(k1, np.arange(n1)) / n1
    f1_full = np.concatenate([np.cos(phi), -np.sin(phi)], axis=0).astype(np.float32)
    f1 = f1_full[:, :n_in]
    f1i = (np.concatenate([np.cos(phi[:, :n_in]), -np.sin(phi[:, :n_in])], axis=0).T / n).astype(np.float32)
    n2 = np.arange(FFT_N2)
    alpha = 2.0 * np.pi * np.outer(k1, n2) / n
    beta = 2.0 * np.pi * np.outer(np.arange(FFT_N2), n2) / FFT_N2
    ca, sa = jnp.asarray(np.cos(alpha), F32)[:, None, :], jnp.asarray(np.sin(alpha), F32)[:, None, :]
    cb, sb = jnp.asarray(np.cos(beta), F32)[None], jnp.asarray(np.sin(beta), F32)[None]
    er = ca * cb - sa * sb
    ei = -(sa * cb + ca * sb)
    ef = jnp.concatenate([jnp.concatenate([er, -ei], axis=2), jnp.concatenate([ei, er], axis=2)], axis=1)
    ert, eit = jnp.swapaxes(er, 1, 2), jnp.swapaxes(ei, 1, 2)
    einv = jnp.concatenate([jnp.concatenate([ert, eit], axis=2), jnp.concatenate([-eit, ert], axis=2)], axis=1)
    return {'n1': n1, 'f1': jnp.asarray(f1, BF16), 'f1_full': jnp.asarray(f1_full, BF16), 'f1i': jnp.asarray(f1i, BF16),
            'ef': ef.astype(BF16), 'ei': einv.astype(BF16)}


def _hy_filter_tm_kernel(w1t_ref, w1c_ref, w1s_ref, b1_ref, w2_ref, b2_ref, w3_ref, b3_ref, fr_ref, dl_ref,
                         k_ref, asum_ref, *, l, tl):
    j = pl.program_id(0)
    n = j * tl + lax.broadcasted_iota(jnp.int32, (1, tl), 1)
    t = jnp.where(n < l, n, 2 * l - n).astype(F32)
    t_norm = t / l
    bands = (1 + lax.broadcasted_iota(jnp.int32, (HY_BANDS, 1), 0)).astype(F32)
    ang = (2.0 * math.pi / l) * t * bands
    fr = fr_ref[...]
    lin = (w1t_ref[...] * t_norm
           + jnp.dot(w1c_ref[...], jnp.cos(ang), preferred_element_type=F32, precision=HIGHEST)
           + jnp.dot(w1s_ref[...], jnp.sin(ang), preferred_element_type=F32, precision=HIGHEST))
    h = jnp.sin(fr * (lin + b1_ref[...]))
    h = jnp.sin(fr * (jnp.dot(w2_ref[...], h, preferred_element_type=F32, precision=HIGHEST) + b2_ref[...]))
    h = jnp.dot(w3_ref[...], h, preferred_element_type=F32, precision=HIGHEST) + b3_ref[...]
    window = jnp.exp(-t_norm * dl_ref[...])

    @pl.when(j == 0)
    def _():
        asum_ref[...] = jnp.zeros_like(asum_ref)

    for o in range(HY_ORDER):
        hf = h[o * HY_WIDTH:(o + 1) * HY_WIDTH, :]
        hb = h[(HY_ORDER + o) * HY_WIDTH:(HY_ORDER + o + 1) * HY_WIDTH, :]
        k = jnp.where(n > l, hb, hf) * window
        k = jnp.where(n == l, 0.0, k).T
        k_ref[o] = k
        asum_ref[o] += jnp.sum(jnp.abs(k), axis=0, keepdims=True)


def _hyena_latent(p_b, lp, tabs, ctx_len):
    b, s, c3 = p_b.shape
    l = s - ctx_len
    n = 2 * l
    n1 = tabs['n1']
    n_in = n1 // 2
    hid = HY_FILT_HID
    w = HY_WIDTH
    full = lambda shape: pl.BlockSpec(shape, lambda *idx: (0,) * len(shape))
    tl = 512
    w1 = lp['hy_w1']
    col = lambda v: v.reshape(-1, 1)
    deltas = np.abs(np.linspace(HY_MIN_DECAY, HY_MAX_DECAY, w, dtype=np.float32)).reshape(-1, 1)
    n_out = 2 * HY_ORDER * w
    k_tm, asum = pl.pallas_call(
        functools.partial(_hy_filter_tm_kernel, l=l, tl=tl),
        out_shape=(jax.ShapeDtypeStruct((HY_ORDER, n, w), F32), jax.ShapeDtypeStruct((HY_ORDER, 1, w), F32)),
        grid=(n // tl,),
        in_specs=[full((hid, 1)), full((hid, HY_BANDS)), full((hid, HY_BANDS)), full((hid, 1)), full((hid, hid)),
                  full((hid, 1)), full((n_out, hid)), full((n_out, 1)), full((hid, 1)), full((w, 1))],
        out_specs=(pl.BlockSpec((HY_ORDER, tl, w), lambda j: (0, j, 0)),
                   pl.BlockSpec((HY_ORDER, 1, w), lambda j: (0, 0, 0))),
        compiler_params=_cp(("arbitrary",)),
        name="hyena_filters_tm",
    )(w1[0:1].T, w1[1:1 + HY_BANDS].T, w1[1 + HY_BANDS:].T, col(lp['hy_b1']), lp['hy_w2'].T, col(lp['hy_b2']),
      lp['hy_w3'].T, col(lp['hy_b3']), col(lp['hy_freq']), jnp.asarray(deltas))
    nj = n1 // FFT_KB
    scr = pltpu.VMEM((2 * n1 * FFT_PITCH, LANES), F32)
    xp_full = pltpu.VMEM((n1 * FFT_PITCH, LANES), F32)
    xp_half = pltpu.VMEM((n_in * FFT_PITCH, LANES), F32)
    half = lambda blk: pl.BlockSpec((1, n, LANES), lambda i, j: (i, 0, blk))
    kf = pl.pallas_call(
        functools.partial(_hy_fft_filter_kernel, n1=n1),
        out_shape=jax.ShapeDtypeStruct((HY_ORDER, 2, n1, FFT_N2, w), BF16),
        grid=(HY_ORDER, nj),
        in_specs=[half(0), half(1), full((2 * n1, n1)),
                  pl.BlockSpec((FFT_KB, 2 * FFT_N2, 2 * FFT_N2), lambda i, j: (j, 0, 0)),
                  pl.BlockSpec((1, 1, w), lambda i, j: (i, 0, 0))],
        out_specs=pl.BlockSpec((1, 2, FFT_KB, FFT_N2, w), lambda i, j: (i, 0, j, 0, 0)),
        scratch_shapes=[xp_full, xp_full, scr, scr],
        compiler_params=_cp(("parallel", "arbitrary")),
        name="hyena_filter_fft",
    )(k_tm, k_tm, tabs['f1_full'], tabs['ef'], asum)
    z = pl.pallas_call(
        functools.partial(_hy_short_conv_tm_kernel, l=l, ctx_len=ctx_len),
        out_shape=jax.ShapeDtypeStruct((b, l, c3), F32),
        grid=(b, c3 // LANES),
        in_specs=[pl.BlockSpec((1, s, LANES), lambda i, g: (i, 0, g)),
                  pl.BlockSpec((HY_SHORT, LANES), lambda i, g: (0, g)),
                  pl.BlockSpec((1, LANES), lambda i, g: (0, g))],
        out_specs=pl.BlockSpec((1, l, LANES), lambda i, g: (i, 0, g)),
        scratch_shapes=[pltpu.VMEM((l + 2 * SUBLANES, LANES), F32)],
        compiler_params=_cp(("parallel", "parallel")),
        name="hyena_short_conv_tm",
    )(p_b, lp['hy_conv_w'], lp['hy_conv_b'].reshape(1, c3))

    def conv(xa, xb, xblk, gate_blk, order):
        lane = lambda arr, blk: pl.BlockSpec((1, l, LANES), lambda i, j: (i, 0, blk))
        half_out = jax.ShapeDtypeStruct((b, l, LANES), F32)
        xb_blk = xblk + 1 if xa is xb else xblk
        return pl.pallas_call(
            functools.partial(_hy_fft_conv_kernel, n1=n1),
            out_shape=(half_out, half_out),
            grid=(b, nj),
            in_specs=[lane(xa, xblk), lane(xb, xb_blk), lane(z, gate_blk), lane(z, gate_blk + 1),
                      pl.BlockSpec((1, 1, w), lambda i, j: (order, 0, 0)),
                      full((2 * n1, n_in)), full((n_in, 2 * n1)),
                      pl.BlockSpec((FFT_KB, 2 * FFT_N2, 2 * FFT_N2), lambda i, j: (j, 0, 0)),
                      pl.BlockSpec((FFT_KB, 2 * FFT_N2, 2 * FFT_N2), lambda i, j: (j, 0, 0)),
                      pl.BlockSpec((1, 2, FFT_KB, FFT_N2, w), lambda i, j: (order, 0, j, 0, 0))],
            out_specs=(pl.BlockSpec((1, l, LANES), lambda i, j: (i, 0, 0)),
                       pl.BlockSpec((1, l, LANES), lambda i, j: (i, 0, 0))),
            scratch_shapes=[xp_half, xp_half, scr, scr],
            compiler_params=_cp(("parallel", "arbitrary")),
            name="hyena_fft_conv",
        )(xa, xb, z, z, lp['hy_bias'].reshape(HY_ORDER, 1, w), tabs['f1'], tabs['f1i'], tabs['ef'], tabs['ei'], kf)

    y1a, y1b = conv(z, z, 0, 2, 0)
    return conv(y1a, y1b, 0, 4, 1)


def _hy_short_conv_tm_kernel(x_ref, w_ref, b_ref, z_ref, xpad, *, l, ctx_len):
    ch = LANES
    zeros8 = jnp.zeros((SUBLANES, LANES), F32)
    xpad[0:SUBLANES, :] = zeros8
    xpad[l + SUBLANES:l + 2 * SUBLANES, :] = zeros8

    def copy_body(i, carry):
        r = pl.multiple_of(i * ch, ch)
        xpad[pl.ds(r + SUBLANES, ch), :] = x_ref[0, pl.ds(r + ctx_len, ch), :]
        return carry

    lax.fori_loop(0, l // ch, copy_body, 0)

    def conv_body(i, carry):
        r = pl.multiple_of(i * ch, ch)
        win = xpad[pl.ds(r, ch + 2 * SUBLANES), :]
        acc = jnp.zeros((ch, LANES), F32) + b_ref[...]
        for k in range(HY_SHORT):
            off = k - HY_SHORT // 2
            acc = acc + win[SUBLANES + off:SUBLANES + off + ch, :] * w_ref[k:k + 1, :]
        z_ref[0, pl.ds(r, ch), :] = acc
        return carry

    lax.fori_loop(0, l // ch, conv_body, 0)


def _top2_route(logits):
    lane = lax.broadcasted_iota(jnp.int32, logits.shape, 1)
    lg = jnp.where(lane < N_EXPERTS, logits, -jnp.inf)
    v1 = lg.max(axis=-1, keepdims=True)
    i1 = jnp.min(jnp.where(lg == v1, lane, LANES), axis=-1, keepdims=True)
    lg2 = jnp.where(lane == i1, -jnp.inf, lg)
    v2 = lg2.max(axis=-1, keepdims=True)
    i2 = jnp.min(jnp.where(lg2 == v2, lane, LANES), axis=-1, keepdims=True)
    e2 = jnp.exp(v2 - v1)
    w1 = 1.0 / (1.0 + e2)
    w2 = e2 / (1.0 + e2)
    out = jnp.where(lane == 0, i1.astype(F32), 0.0)
    out = jnp.where(lane == 1, i2.astype(F32), out)
    out = jnp.where(lane == 2, w1, out)
    return jnp.where(lane == 3, w2, out)


def _merge_kernel(ya_ref, yba_ref, ybb_ref, ybc_ref, ycl_ref, ycc_ref, yd_ref, pg_ref, xc_ref, x_ref, mod_ref, wa_ref,
                  wb_ref, wc_ref, wd_ref, wo_ref, lng_ref, lnb_ref, *rest, alpha, first_tile, with_router, u_tiles):
    if with_router:
        wr_ref, x_out_ref, u_out_ref, route_ref = rest
    else:
        x_out_ref, u_out_ref = rest
    j = pl.program_id(1) + first_tile
    d = x_ref.shape[-1]
    yb = jnp.concatenate([yba_ref[0], ybb_ref[0]], axis=1)
    yc = ycl_ref[0]
    if first_tile == 0:
        yb = jnp.where(j == 0, ybc_ref[0].astype(F32).T, yb)
        yc = jnp.where(j == 0, ycc_ref[0], yc)
    yb = yb.astype(BF16)
    projs = [jnp.dot(ya_ref[0], wa_ref[...], preferred_element_type=F32),
             jnp.dot(yb, wb_ref[...], preferred_element_type=F32),
             jnp.dot(yc, wc_ref[...], preferred_element_type=F32),
             jnp.dot(yd_ref[0], wd_ref[...], preferred_element_type=F32)]
    merged = None
    for i, pr in enumerate(projs):
        term = _sigmoid(pg_ref[0, :, i * d:(i + 1) * d]).astype(F32) * pr
        merged = term if merged is None else merged + term
    m = jnp.dot(merged.astype(BF16), wo_ref[...], preferred_element_type=F32)
    mod = mod_ref[0, 0]
    x_res = x_ref[0]
    if first_tile == 0:
        x_res = jnp.where(j == 0, xc_ref[0], x_res)
    xn = _layer_norm_rows(alpha * x_res + mod[2:3, :] * m) * lng_ref[...] + lnb_ref[...]
    x_out_ref[0] = xn
    u = _layer_norm_rows(xn) * (1.0 + mod[4:5, :]) + mod[3:4, :]
    if u_tiles:
        for sub in range(d // LANES):
            u_out_ref[0, :, sub, :] = u[:, sub * LANES:(sub + 1) * LANES]
    else:
        u_out_ref[0] = u.astype(u_out_ref.dtype)
    if with_router:
        u_hi = u.astype(BF16)
        u_lo = (u - u_hi.astype(F32)).astype(BF16)
        wr = wr_ref[...]
        w_hi = wr.astype(BF16)
        w_lo = (wr - w_hi.astype(F32)).astype(BF16)
        logits = (jnp.dot(u_hi, w_hi, preferred_element_type=F32) + jnp.dot(u_lo, w_hi, preferred_element_type=F32)
                  + jnp.dot(u_hi, w_lo, preferred_element_type=F32))
        route_ref[0] = _top2_route(logits)


def _merge(ya, yb_lat_a, yb_lat_b, yb_ctx, yc_lat, yc_ctx, yd, pg, resid, mod_sel, lp, ln_g, ln_b, alpha, ctx_len,
           with_ctx, u_dtype, w_router=None, u_tiles=False):
    if isinstance(resid, tuple):
        x_ctx, x_lat = resid
        lat_tile = lambda j, first: jnp.maximum(j + first - 1, 0)
    else:
        x_ctx = x_lat = resid
        lat_tile = lambda j, first: j + first
    b, d = x_lat.shape[0], x_lat.shape[2]
    s = ya.shape[1]
    tm = ctx_len
    first = 0 if with_ctx else 1
    nt = s // tm - first
    tok = lambda w: pl.BlockSpec((1, tm, w), lambda i, j: (i, j + first, 0))
    lat = lambda j: jnp.maximum(j + first - 1, 0)
    out = lambda w: pl.BlockSpec((1, tm, w), lambda i, j: (i, j, 0))
    full = lambda shape: pl.BlockSpec(shape, lambda i, j: (0,) * len(shape))
    in_specs = [tok(LRU_WIDTH),
                pl.BlockSpec((1, tm, LANES), lambda i, j: (i, lat(j), 0)),
                pl.BlockSpec((1, tm, LANES), lambda i, j: (i, lat(j), 0)),
                pl.BlockSpec((1, HY_WIDTH, tm), lambda i, j: (i, 0, 0)),
                pl.BlockSpec((1, tm, NA_WIDTH), lambda i, j: (i, lat(j), 0)),
                pl.BlockSpec((1, tm, NA_WIDTH), lambda i, j: (i, 0, 0)),
                tok(S5_WIDTH), tok(N_BRANCH * d),
                pl.BlockSpec((1, tm, d), lambda i, j: (i, 0, 0)),
                pl.BlockSpec((1, tm, d), lambda i, j: (i, lat_tile(j, first), 0)),
                pl.BlockSpec((1, 1, 6, d), lambda i, j: (i, jnp.minimum(j + first, 1), 0, 0)),
                full((LRU_WIDTH, d)), full((HY_WIDTH, d)), full((NA_WIDTH, d)), full((S5_WIDTH, d)),
                full((d, d)), full((1, d)), full((1, d))]
    args = [ya, yb_lat_a, yb_lat_b, yb_ctx, yc_lat, yc_ctx, yd, pg, x_ctx, x_lat, mod_sel,
            lp['w_br_a'].astype(BF16), lp['w_br_b'].astype(BF16), lp['w_br_c'].astype(BF16),
            lp['w_br_d'].astype(BF16), lp['w_out'].astype(BF16), ln_g.reshape(1, d), ln_b.reshape(1, d)]
    if u_tiles:
        u_shape = jax.ShapeDtypeStruct((b, nt * tm, d // LANES, LANES), u_dtype)
        u_spec = pl.BlockSpec((1, tm, d // LANES, LANES), lambda i, j: (i, j, 0, 0))
    else:
        u_shape = jax.ShapeDtypeStruct((b, nt * tm, d), u_dtype)
        u_spec = out(d)
    out_shape = [jax.ShapeDtypeStruct((b, nt * tm, d), F32), u_shape]
    out_specs = [out(d), u_spec]
    if w_router is not None:
        in_specs.append(full((d, LANES)))
        args.append(jnp.zeros((d, LANES), F32).at[:, :N_EXPERTS].set(w_router))
        out_shape.append(jax.ShapeDtypeStruct((b, nt * tm, LANES), F32))
        out_specs.append(out(LANES))
    return pl.pallas_call(
        functools.partial(_merge_kernel, alpha=alpha, first_tile=first, with_router=w_router is not None,
                          u_tiles=u_tiles),
        out_shape=tuple(out_shape),
        grid=(b, nt),
        in_specs=in_specs,
        out_specs=tuple(out_specs),
        compiler_params=_cp(("parallel", "parallel")),
        name="merge",
    )(*args)


FFN_TF = 1408
FFN_TILES_PER_SEQ = 8


def _dense_ffn_kernel(u_ref, x_ref, mod_ref, modn_ref, wg_ref, wu_ref, wd_ref, lng_ref, lnb_ref,
                      xs_ref, un_ref, acc_ref, *, alpha, ctx_len, tiles_per_seq):
    i = pl.program_id(0)
    j = pl.program_id(1)

    @pl.when(j == 0)
    def _():
        acc_ref[...] = jnp.zeros_like(acc_ref)

    u = u_ref[...]
    g = jnp.dot(u, wg_ref[...], preferred_element_type=F32)
    v = jnp.dot(u, wu_ref[...], preferred_element_type=F32)
    h = (g * _sigmoid(g)) * v
    acc_ref[...] += jnp.dot(h.astype(BF16), wd_ref[...], preferred_element_type=F32)

    @pl.when(j == pl.num_programs(1) - 1)
    def _():
        row = lax.broadcasted_iota(jnp.int32, (acc_ref.shape[0], 1), 0)
        is_ctx = (row < ctx_len) & ((i % tiles_per_seq) == 0)
        pick = lambda m, k: jnp.where(is_ctx, m[0, 0, k:k + 1, :], m[0, 1, k:k + 1, :])
        x2 = (_layer_norm_rows(alpha * x_ref[...] + pick(mod_ref, 5) * acc_ref[...]) * lng_ref[...]
              + lnb_ref[...])
        xs_ref[...] = x2
        un_ref[...] = (_layer_norm_rows(x2) * (1.0 + pick(modn_ref, 1)) + pick(modn_ref, 0)).astype(un_ref.dtype)


def _dense_ffn(u2, x1, mod_sel, mod_next, w_gate, w_up, w_down, ln_g, ln_b, alpha, ctx_len):
    b, s, d = x1.shape
    ff = w_gate.shape[1]
    tiles_per_seq = FFN_TILES_PER_SEQ
    tm = s // tiles_per_seq
    assert tm % 16 == 0 and tm >= ctx_len
    tf = FFN_TF
    rows = pl.BlockSpec((tm, d), lambda i, j: (i, 0))
    modspec = pl.BlockSpec((1, 2, 6, d), lambda i, j: (i // tiles_per_seq, 0, 0, 0))
    vec = pl.BlockSpec((1, d), lambda i, j: (0, 0))
    xs, un = pl.pallas_call(
        functools.partial(_dense_ffn_kernel, alpha=alpha, ctx_len=ctx_len, tiles_per_seq=tiles_per_seq),
        out_shape=(jax.ShapeDtypeStruct((b * s, d), F32), jax.ShapeDtypeStruct((b * s, d), BF16)),
        grid=(b * tiles_per_seq, ff // tf),
        in_specs=[rows, rows, modspec, modspec,
                  pl.BlockSpec((d, tf), lambda i, j: (0, j)), pl.BlockSpec((d, tf), lambda i, j: (0, j)),
                  pl.BlockSpec((tf, d), lambda i, j: (j, 0)), vec, vec],
        out_specs=(rows, rows),
        scratch_shapes=[pltpu.VMEM((tm, d), F32)],
        compiler_params=_cp(("parallel", "arbitrary")),
        name="dense_swiglu_ln",
    )(u2.reshape(b * s, d), x1.reshape(b * s, d), mod_sel, mod_next, w_gate, w_up, w_down,
      ln_g.reshape(1, d), ln_b.reshape(1, d))
    return xs.reshape(b, s, d), un.reshape(b, s, d)


MOE_TM = 512
MOE_TF = 896


def _moe_kernel(te_ref, nu_ref, dst_ref, u_hbm, wg_ref, wu_ref, wd_ref, out_hbm,
                xbuf, xbf, acc2, acc, gsem, ssem, *, n_rows):
    tm = MOE_TM
    n_sub = acc.shape[2]
    n_tok = n_rows // TOP_K
    i = pl.program_id(0)
    j = pl.program_id(1)
    nt = pl.num_programs(0)
    nj = pl.num_programs(1)
    n_used = nu_ref[0]
    slot = i % 2

    def gather_start(tile, sl):
        base = tile * tm

        def body(r, carry):
            p = dst_ref[base + r]
            row = jnp.where(p >= n_tok, p - n_tok, jnp.maximum(p, 0))
            pltpu.make_async_copy(u_hbm.at[pl.ds(row, 1)], xbuf.at[sl, pl.ds(r, 1)], gsem.at[sl]).start()
            return carry

        lax.fori_loop(0, tm, body, 0, unroll=8)

    def gather_wait(sl):
        pltpu.make_async_copy(u_hbm.at[pl.ds(0, tm)], xbuf.at[sl], gsem.at[sl]).wait()

    def scatter_wait():
        pltpu.make_async_copy(acc.at[0], out_hbm.at[pl.ds(0, tm)], ssem.at[0]).wait()

    used = i < n_used

    @pl.when(used & (j == 0))
    def _():
        @pl.when(i == 0)
        def _():
            gather_start(0, 0)
            acc[1] = jnp.zeros(acc.shape[1:], F32)
            dump = pltpu.make_async_copy(acc.at[1], out_hbm.at[pl.ds(n_rows, tm)], ssem.at[0])
            dump.start()
            dump.wait()

        gather_wait(slot)

        @pl.when(i + 1 < n_used)
        def _():
            gather_start(i + 1, 1 - slot)

        xbf[...] = jnp.concatenate([xbuf[slot, :, sub, :] for sub in range(n_sub)], axis=1).astype(BF16)
        acc2[...] = jnp.zeros_like(acc2)

    @pl.when(used)
    def _():
        x = xbf[...]
        g = jnp.dot(x, wg_ref[0], preferred_element_type=F32)
        u = jnp.dot(x, wu_ref[0], preferred_element_type=F32)
        h = (g * _sigmoid(g)) * u
        acc2[...] += jnp.dot(h.astype(BF16), wd_ref[0].astype(BF16), preferred_element_type=F32)

    @pl.when(used & (j == nj - 1))
    def _():
        @pl.when(i > 0)
        def _():
            scatter_wait()

        for sub in range(n_sub):
            acc[slot, :, sub, :] = acc2[:, sub * LANES:(sub + 1) * LANES]
        base = i * tm

        def body(r, carry):
            p = dst_ref[base + r]
            row = jnp.where(p >= 0, p, n_rows + r)
            pltpu.make_async_copy(acc.at[slot, pl.ds(r, 1)], out_hbm.at[pl.ds(row, 1)], ssem.at[0]).start()
            return carry

        lax.fori_loop(0, tm, body, 0, unroll=8)

    @pl.when((i == nt - 1) & (j == nj - 1))
    def _():
        scatter_wait()


def _moe_experts(u_rows, dst, tile_expert, n_used, w_gate, w_up, w_down):
    r, n_sub, _ = u_rows.shape
    d = n_sub * LANES
    ff = w_gate.shape[2]
    tm, tf = MOE_TM, MOE_TF
    nt = dst.shape[0] // tm
    nj = ff // tf
    n_rows = TOP_K * r

    def jmap(i, j, nu):
        return jnp.where(i < nu[0], j, nj - 1)

    grid_spec = pltpu.PrefetchScalarGridSpec(
        num_scalar_prefetch=3,
        grid=(nt, nj),
        in_specs=[pl.BlockSpec(memory_space=pl.ANY),
                  pl.BlockSpec((1, d, tf), lambda i, j, te, nu, ds: (te[i], 0, jmap(i, j, nu))),
                  pl.BlockSpec((1, d, tf), lambda i, j, te, nu, ds: (te[i], 0, jmap(i, j, nu))),
                  pl.BlockSpec((1, tf, d), lambda i, j, te, nu, ds: (te[i], jmap(i, j, nu), 0))],
        out_specs=pl.BlockSpec(memory_space=pl.ANY),
        scratch_shapes=[pltpu.VMEM((2, tm, n_sub, LANES), F32), pltpu.VMEM((tm, d), BF16), pltpu.VMEM((tm, d), F32),
                        pltpu.VMEM((2, tm, n_sub, LANES), F32),
                        pltpu.SemaphoreType.DMA((2,)), pltpu.SemaphoreType.DMA((1,))],
    )
    return pl.pallas_call(
        functools.partial(_moe_kernel, n_rows=n_rows),
        out_shape=jax.ShapeDtypeStruct((n_rows + tm, n_sub, LANES), F32),
        grid_spec=grid_spec,
        compiler_params=_cp(("arbitrary", "arbitrary")),
        name="moe_experts",
    )(tile_expert, n_used, dst, u_rows, w_gate, w_up, w_down)


def _moe_combine_kernel(x_ref, y1_ref, y2_ref, w_ref, mod_ref, lng_ref, lnb_ref, o_ref, *, alpha):
    mod = mod_ref[0, 0]
    w = w_ref[0]
    untile = lambda r: jnp.concatenate([r[:, sub, :] for sub in range(r.shape[1])], axis=1)
    f = w[:, 2:3] * untile(y1_ref) + w[:, 3:4] * untile(y2_ref)
    o_ref[0] = _layer_norm_rows(alpha * x_ref[0] + mod[5:6, :] * f) * lng_ref[...] + lnb_ref[...]


def _moe_ffn(x_lat, u_lat, route, mod_sel, lp_moe, ln_g, ln_b, alpha):
    b, l, d = x_lat.shape
    n_sub = d // LANES
    t = b * l
    tm = MOE_TM
    ids = route[..., 0:TOP_K].astype(jnp.int32).reshape(t * TOP_K)
    onehot = (ids[:, None] == jnp.arange(N_EXPERTS)[None, :]).astype(jnp.int32)
    csum = jnp.cumsum(onehot, axis=0)
    rank = jnp.take_along_axis(csum, ids[:, None], axis=1)[:, 0] - 1
    counts = csum[-1]
    padded = ((counts + tm - 1) // tm) * tm
    ends = jnp.cumsum(padded)
    starts = ends - padded
    slot = starts[ids] + rank
    n_slots = t * TOP_K + N_EXPERTS * tm
    nt = n_slots // tm
    pair = jnp.arange(t * TOP_K, dtype=jnp.int32)
    dst = jnp.full((n_slots,), -1, jnp.int32).at[slot].set((pair % TOP_K) * t + pair // TOP_K)
    tile_start = jnp.arange(nt, dtype=jnp.int32) * tm
    tile_expert = jnp.minimum(jnp.sum(tile_start[:, None] >= ends[None, :], axis=1), N_EXPERTS - 1).astype(jnp.int32)
    n_used = (ends[-1] // tm).astype(jnp.int32).reshape(1)
    last_e = tile_expert[jnp.maximum(n_used[0] - 1, 0)]
    tile_expert = jnp.where(jnp.arange(nt) < n_used[0], tile_expert, last_e)
    y = _moe_experts(u_lat.reshape(t, n_sub, LANES), dst, tile_expert, n_used, lp_moe['w_gate'], lp_moe['w_up'],
                     lp_moe['w_down'])
    tmc = 256
    nl = l // tmc
    vec = pl.BlockSpec((1, d), lambda i, j: (0, 0))
    return pl.pallas_call(
        functools.partial(_moe_combine_kernel, alpha=alpha),
        out_shape=jax.ShapeDtypeStruct((b, l, d), F32),
        grid=(b, nl),
        in_specs=[pl.BlockSpec((1, tmc, d), lambda i, j: (i, j, 0)),
                  pl.BlockSpec((tmc, n_sub, LANES), lambda i, j: (i * nl + j, 0, 0)),
                  pl.BlockSpec((tmc, n_sub, LANES), lambda i, j: (b * nl + i * nl + j, 0, 0)),
                  pl.BlockSpec((1, tmc, LANES), lambda i, j: (i, j, 0)),
                  pl.BlockSpec((1, 1, 6, d), lambda i, j: (i, 1, 0, 0)), vec, vec],
        out_specs=pl.BlockSpec((1, tmc, d), lambda i, j: (i, j, 0)),
        compiler_params=_cp(("parallel", "parallel")),
        name="moe_combine_ln",
    )(x_lat, y, y, route, mod_sel, ln_g.reshape(1, d), ln_b.reshape(1, d))


def kernel(x, c, ctx, c_ctx, w_mod, b_mod, w_in, lru_conv_w, lru_conv_b, lru_w_r, lru_b_r, lru_w_i, lru_b_i, lru_lambda, hy_conv_w, hy_conv_b, hy_w1, hy_b1, hy_w2, hy_b2, hy_w3, hy_b3, hy_freq, hy_bias, na_rpb, s5_a_re, s5_a_im, s5_log_dt, s5_b_re, s5_b_im, s5_c_re, s5_c_im, s5_d, s5_w_glu, s5_b_glu, w_br_a, w_br_b, w_br_c, w_br_d, w_out, ln1_g, ln1_b, ln2_g, ln2_b, ff_w_gate, ff_w_up, ff_w_down, moe_router, moe_w_gate, moe_w_up, moe_w_down):
    bsz, l, d = x.shape
    ctx_len = ctx.shape[1]
    depth = w_in.shape[0]
    s = ctx_len + l
    alpha = (2.0 * depth) ** 0.25
    xs = (ctx, x)
    c_rows = jnp.zeros((SUBLANES, d), F32).at[0:bsz].set(c).at[bsz].set(c_ctx)
    fft_lat = _fft_tables(l)
    dft_ctx = _dft_matrices(ctx_len) if depth > 1 else None
    assert depth == 2
    mod_all = _mod_vectors(c_rows, w_mod, b_mod).reshape(depth, SUBLANES, 6, d)
    mods = [jnp.stack([jnp.broadcast_to(mod_all[li, bsz], (bsz, 6, d)), mod_all[li, 0:bsz]], axis=1)
            for li in range(depth)]
    mods.append(mods[-1])
    u1 = _ln_mod(ctx, x, mods[0], 0, 1)

    for li in range(depth):
        with_ctx = li < depth - 1
        lp = {
            'lru_conv_w': lru_conv_w[li], 'lru_conv_b': lru_conv_b[li], 'lru_w_r': lru_w_r[li],
            'lru_b_r': lru_b_r[li], 'lru_w_i': lru_w_i[li], 'lru_b_i': lru_b_i[li], 'lru_lambda': lru_lambda[li],
            'hy_conv_w': hy_conv_w[li], 'hy_conv_b': hy_conv_b[li], 'hy_w1': hy_w1[li], 'hy_b1': hy_b1[li],
            'hy_w2': hy_w2[li], 'hy_b2': hy_b2[li], 'hy_w3': hy_w3[li], 'hy_b3': hy_b3[li],
            'hy_freq': hy_freq[li], 'hy_bias': hy_bias[li],
            's5_a_re': s5_a_re[li], 's5_a_im': s5_a_im[li], 's5_log_dt': s5_log_dt[li], 's5_b_re': s5_b_re[li],
            's5_b_im': s5_b_im[li], 's5_c_re': s5_c_re[li], 's5_c_im': s5_c_im[li], 's5_d': s5_d[li],
            's5_w_glu': s5_w_glu[li], 's5_b_glu': s5_b_glu[li], 'w_br_a': w_br_a[li], 'w_br_b': w_br_b[li],
            'w_br_c': w_br_c[li], 'w_br_d': w_br_d[li], 'w_out': w_out[li],
        }
        mod_sel = mods[li]

        u1f = u1.reshape(bsz * s, d)
        wi = w_in[li].astype(BF16)
        p_a = _matmul(u1f, wi[:, OFF_A:OFF_B], F32, name="proj_lru").reshape(bsz, s, OFF_B - OFF_A)
        p_c = _matmul(u1f, wi[:, OFF_C:OFF_D], BF16, name="proj_natten").reshape(bsz, s, OFF_D - OFF_C)
        p_d = _matmul(u1f, wi[:, OFF_D:OFF_G], F32, name="proj_s5").reshape(bsz, s, OFF_G - OFF_D)
        p_g = _matmul(u1f, wi[:, OFF_G:], BF16, name="proj_gates").reshape(bsz, s, N_BRANCH * d)
        p_b = _matmul(u1f, wi[:, OFF_B:OFF_C], F32, name="proj_hyena").reshape(bsz, s, OFF_C - OFF_B)

        ya = _lru_mixer(p_a, lp, ctx_len)
        yb_a, yb_b = _hyena_latent(p_b, lp, fft_lat, ctx_len)
        if with_ctx:
            yb_ctx = _hyena_sequence(_matmul_nt(wi[:, OFF_B:OFF_C].T, u1, F32, 0, ctx_len), lp, dft_ctx)
        else:
            yb_ctx = jnp.zeros((bsz, HY_WIDTH, ctx_len), BF16)
        yc_l, yc_c = _natten_mixer(p_c, na_rpb[li], ctx_len, with_ctx)
        if yc_c is None:
            yc_c = jnp.zeros((bsz, ctx_len, NA_WIDTH), BF16)
        yd = _s5_mixer(p_d, lp, ctx_len)
        e = li // 2
        if li % 2 == 0:
            assert with_ctx
            x1, u2 = _merge(ya, yb_a, yb_b, yb_ctx, yc_l, yc_c, yd, p_g, xs, mod_sel, lp, ln1_g[li], ln1_b[li],
                            alpha, ctx_len, True, BF16)
            xs, u1 = _dense_ffn(u2, x1, mod_sel, mods[li + 1], ff_w_gate[e].astype(BF16), ff_w_up[e].astype(BF16),
                                ff_w_down[e].astype(BF16), ln2_g[li], ln2_b[li], alpha, ctx_len)
        else:
            assert not with_ctx
            x1, u2, route = _merge(ya, yb_a, yb_b, yb_ctx, yc_l, yc_c, yd, p_g, xs, mod_sel, lp, ln1_g[li],
                                   ln1_b[li], alpha, ctx_len, False, F32, moe_router[e], u_tiles=True)
            lp_moe = {'w_gate': moe_w_gate[e].astype(BF16), 'w_up': moe_w_up[e].astype(BF16),
                      'w_down': moe_w_down[e]}
            return _moe_ffn(x1, u2, route, mod_sel, lp_moe, ln2_g[li], ln2_b[li], alpha)
```
